```python
import jax, jax.numpy as jnp
from jax import lax
import numpy as np

D_MODEL = 1024
BATCH = 16
SEQ = 2048
DEPTH = 2

CHUNK = 64
QBLK = 128
HEAD_DIM = 64
SB_HEADS = 4
FOX_HEADS = 4
MLA_HEADS = 8
MLA_Q_RANK = 256
MLA_KV_RANK = 128
MLA_NOPE_DIM = 64
MLA_ROPE_DIM = 32
MLA_V_DIM = 64
ROPE_BASE = 10000.0
N_BRANCHES = 3
N_EXPERTS = 256
TOP_K = 8
N_GROUPS = 8
TOPK_GROUPS = 4
EXPERT_FF = 256
SHARED_FF = 256
ROUTED_SCALE = 2.5
MOE_BLOCK = 128
LN_EPS = 1e-5
RMS_EPS = 1e-6
DN_ALPHA = (2 * DEPTH) ** 0.25
DN_BETA = (8 * DEPTH) ** -0.25

SB_W = SB_HEADS * HEAD_DIM
FOX_W = FOX_HEADS * HEAD_DIM
MLA_QK_DIM = MLA_NOPE_DIM + MLA_ROPE_DIM
OFF_SB = 0
OFF_FOX = OFF_SB + 3 * SB_W
OFF_FGATE = OFF_FOX + 3 * FOX_W
OFF_DQ = OFF_FGATE + FOX_HEADS
OFF_DKV = OFF_DQ + MLA_Q_RANK
OFF_KR = OFF_DKV + MLA_KV_RANK
OFF_GATE = OFF_KR + MLA_ROPE_DIM
IN_WIDTH = OFF_GATE + N_BRANCHES * D_MODEL

kernel_name = "hybrid_sb_fox_mla_moe_deepnorm"


def _layer_norm(x, g, b):
    xf = x.astype(jnp.float32)
    xc = xf - jnp.mean(xf, axis=-1, keepdims=True)
    var = jnp.mean(xc * xc, axis=-1, keepdims=True)
    y = xc * lax.rsqrt(var + LN_EPS) * g.astype(jnp.float32) + b.astype(jnp.float32)
    return y.astype(x.dtype)


def _rms_norm(x, g):
    xf = x.astype(jnp.float32)
    y = xf * lax.rsqrt(jnp.mean(xf * xf, axis=-1, keepdims=True) + RMS_EPS) * g.astype(jnp.float32)
    return y.astype(x.dtype)


def _rope(x, positions):
    half = MLA_ROPE_DIM // 2
    inv_freq = jnp.power(ROPE_BASE, -jnp.arange(half, dtype=jnp.float32) / half)
    ang = positions.astype(jnp.float32)[:, None] * inv_freq[None, :]
    cos, sin = jnp.cos(ang), jnp.sin(ang)
    x1 = x[..., :half].astype(jnp.float32)
    x2 = x[..., half:].astype(jnp.float32)
    return jnp.concatenate([x1 * cos - x2 * sin, x1 * sin + x2 * cos], axis=-1).astype(x.dtype)


def _heads(t, n_heads):
    b, s, _ = t.shape
    return t.reshape(b, s, n_heads, -1).transpose(0, 2, 1, 3)


def _merge_heads(t):
    b, h, s, d = t.shape
    return t.transpose(0, 2, 1, 3).reshape(b, s, h * d)


def _stick_breaking(q, k, v, t_pos, s_pos):
    z = jnp.einsum('bhtd,bhsd->bhts', q, k).astype(jnp.float32) * (HEAD_DIM ** -0.5)
    strict = s_pos[None, :] < t_pos[:, None]
    log_keep = jnp.where(strict, jax.nn.log_sigmoid(-z), 0.0)
    between = lax.cumsum(log_keep, axis=3, reverse=True) - log_keep
    w = jnp.where(strict, jnp.exp(jax.nn.log_sigmoid(z) + between), 0.0)
    return jnp.einsum('bhts,bhsd->bhtd', w.astype(v.dtype), v)


def _forgetting(q, k, v, cum_t, cum_s, t_pos, s_pos):
    logits = jnp.einsum('bhtd,bhsd->bhts', q, k).astype(jnp.float32) * (HEAD_DIM ** -0.5)
    logits = logits + (cum_t[..., :, None] - cum_s[..., None, :])
    causal = s_pos[None, :] <= t_pos[:, None]
    p = jax.nn.softmax(jnp.where(causal, logits, -jnp.inf), axis=-1)
    return jnp.einsum('bhts,bhsd->bhtd', p.astype(v.dtype), v)


def _mla_block(q_nope, q_rope, k_nope, k_rope, v, t_pos, s_pos):
    logits = (jnp.einsum('bhtd,bhsd->bhts', q_nope, k_nope)
              + jnp.einsum('bhtr,bsr->bhts', q_rope, k_rope)).astype(jnp.float32) * (MLA_QK_DIM ** -0.5)
    chunk_causal = (s_pos[None, :] // CHUNK) <= (t_pos[:, None] // CHUNK)
    p = jax.nn.softmax(jnp.where(chunk_causal, logits, -jnp.inf), axis=-1)
    return jnp.einsum('bhts,bhsd->bhtd', p.astype(v.dtype), v)


def _token_mixers(x, w_in, b_gate, b_forget, mla_q_norm, w_uq, mla_kv_norm, w_ukv,
                  w_proj_sb, w_proj_fox, w_proj_mla, w_out):
    b, s, _ = x.shape
    p = x @ w_in
    q_sb = _heads(p[..., OFF_SB:OFF_SB + SB_W], SB_HEADS)
    k_sb = _heads(p[..., OFF_SB + SB_W:OFF_SB + 2 * SB_W], SB_HEADS)
    v_sb = _heads(p[..., OFF_SB + 2 * SB_W:OFF_FOX], SB_HEADS)
    q_fox = _heads(p[..., OFF_FOX:OFF_FOX + FOX_W], FOX_HEADS)
    k_fox = _heads(p[..., OFF_FOX + FOX_W:OFF_FOX + 2 * FOX_W], FOX_HEADS)
    v_fox = _heads(p[..., OFF_FOX + 2 * FOX_W:OFF_FGATE], FOX_HEADS)
    log_f = jax.nn.log_sigmoid((p[..., OFF_FGATE:OFF_DQ] + b_forget).astype(jnp.float32))
    cum_f = jnp.swapaxes(lax.cumsum(log_f, axis=1), 1, 2)

    pos = jnp.arange(s)
    c_q = _rms_norm(p[..., OFF_DQ:OFF_DKV], mla_q_norm)
    q_mla = _heads(c_q @ w_uq, MLA_HEADS)
    q_nope = q_mla[..., :MLA_NOPE_DIM]
    q_rope = _rope(q_mla[..., MLA_NOPE_DIM:], pos)
    c_kv = _rms_norm(p[..., OFF_DKV:OFF_KR], mla_kv_norm)
    kv = _heads(c_kv @ w_ukv, MLA_HEADS)
    k_nope = kv[..., :MLA_NOPE_DIM]
    v_mla = kv[..., MLA_NOPE_DIM:]
    k_rope = _rope(p[..., OFF_KR:OFF_GATE], pos)

    gates = jax.nn.sigmoid(p[..., OFF_GATE:] + b_gate).reshape(b, s, N_BRANCHES, D_MODEL)

    out_sb, out_fox, out_mla = [], [], []
    for i in range(s // QBLK):
        q0, q1 = i * QBLK, (i + 1) * QBLK
        t_pos = jnp.arange(q0, q1)
        s_pos = jnp.arange(q1)
        out_sb.append(_stick_breaking(q_sb[:, :, q0:q1], k_sb[:, :, :q1], v_sb[:, :, :q1], t_pos, s_pos))
        out_fox.append(_forgetting(q_fox[:, :, q0:q1], k_fox[:, :, :q1], v_fox[:, :, :q1],
                                   cum_f[:, :, q0:q1], cum_f[:, :, :q1], t_pos, s_pos))
        out_mla.append(_mla_block(q_nope[:, :, q0:q1], q_rope[:, :, q0:q1], k_nope[:, :, :q1],
                                  k_rope[:, :q1], v_mla[:, :, :q1], t_pos, s_pos))
    o_sb = _merge_heads(jnp.concatenate(out_sb, axis=2))
    o_fox = _merge_heads(jnp.concatenate(out_fox, axis=2))
    o_mla = _merge_heads(jnp.concatenate(out_mla, axis=2))

    mixed = (gates[..., 0, :] * (o_sb @ w_proj_sb)
             + gates[..., 1, :] * (o_fox @ w_proj_fox)
             + gates[..., 2, :] * (o_mla @ w_proj_mla))
    return mixed @ w_out


def _swiglu(x, wg, wu, wd):
    return (jax.nn.silu(x @ wg) * (x @ wu)) @ wd


def _route(xf, w_router, router_bias):
    scores = jax.nn.sigmoid((xf @ w_router).astype(jnp.float32))
    biased = scores + router_bias.astype(jnp.float32)
    grouped = biased.reshape(-1, N_GROUPS, N_EXPERTS // N_GROUPS)
    group_score = lax.top_k(grouped, 2)[0].sum(-1)
    _, top_groups = lax.top_k(group_score, TOPK_GROUPS)
    group_keep = jnp.any(top_groups[:, :, None] == jnp.arange(N_GROUPS)[None, None, :], axis=1)
    expert_keep = jnp.repeat(group_keep, N_EXPERTS // N_GROUPS, axis=1)
    _, top_idx = lax.top_k(jnp.where(expert_keep, biased, -jnp.inf), TOP_K)
    w = jnp.take_along_axis(scores, top_idx, axis=1)
    w = w / jnp.sum(w, axis=-1, keepdims=True) * ROUTED_SCALE
    return top_idx, w.astype(xf.dtype)


def _routed_experts(xf, top_idx, top_w, w_gate, w_up, w_down):
    n_tok, d = xf.shape
    n_assign = n_tok * TOP_K
    n_blocks = -(-n_assign // MOE_BLOCK) + N_EXPERTS
    flat_e = top_idx.reshape(-1)
    order = jnp.argsort(flat_e)
    sorted_e = flat_e[order]
    counts = jnp.bincount(flat_e, length=N_EXPERTS)
    padded = (counts + MOE_BLOCK - 1) // MOE_BLOCK * MOE_BLOCK
    pad_end = jnp.cumsum(padded)
    pad_start = pad_end - padded
    grp_start = jnp.cumsum(counts) - counts
    dest = pad_start[sorted_e] + (jnp.arange(n_assign) - grp_start[sorted_e])
    n_rows = n_blocks * MOE_BLOCK
    row_tok = jnp.full((n_rows,), n_tok, jnp.int32).at[dest].set((order // TOP_K).astype(jnp.int32))
    row_w = jnp.zeros((n_rows,), xf.dtype).at[dest].set(top_w.reshape(-1)[order])
    blk_e = jnp.minimum(jnp.searchsorted(pad_end, jnp.arange(n_blocks) * MOE_BLOCK, side='right'),
                        N_EXPERTS - 1)
    x_pad = jnp.concatenate([xf, jnp.zeros((1, d), xf.dtype)], axis=0)

    def step(acc, blk):
        tok, wts, e = blk
        y = _swiglu(x_pad[tok], w_gate[e], w_up[e], w_down[e]) * wts[:, None]
        return acc.at[tok].add(y), None

    acc, _ = lax.scan(step, jnp.zeros_like(x_pad),
                      (row_tok.reshape(n_blocks, MOE_BLOCK), row_w.reshape(n_blocks, MOE_BLOCK), blk_e))
    return acc[:n_tok]


def _moe(x, w_router, router_bias, w_exp_gate, w_exp_up, w_exp_down, w_sh_gate, w_sh_up, w_sh_down):
    b, s, d = x.shape
    xf = x.reshape(b * s, d)
    top_idx, top_w = _route(xf, w_router, router_bias)
    y = _swiglu(xf, w_sh_gate, w_sh_up, w_sh_down) + _routed_experts(xf, top_idx, top_w,
                                                                      w_exp_gate, w_exp_up, w_exp_down)
    return y.reshape(b, s, d)


def setup_inputs(seed: int = 0) -> dict:
    key = jax.random.key(seed)
    ks = jax.random.split(key, 32)
    f32 = jnp.float32

    def nrm(k, shape, scale):
        return jax.random.normal(k, shape, f32) * scale

    din = D_MODEL ** -0.5
    x = nrm(ks[0], (BATCH, SEQ, D_MODEL), 1.0)
    ln1_g = 1.0 + nrm(ks[1], (DEPTH, D_MODEL), 0.02)
    ln1_b = nrm(ks[2], (DEPTH, D_MODEL), 0.02)
    ln2_g = 1.0 + nrm(ks[3], (DEPTH, D_MODEL), 0.02)
    ln2_b = nrm(ks[4], (DEPTH, D_MODEL), 0.02)
    w_in = jnp.concatenate([
        nrm(ks[5], (DEPTH, D_MODEL, 2 * SB_W), din),
        nrm(ks[6], (DEPTH, D_MODEL, SB_W), din * DN_BETA),
        nrm(ks[7], (DEPTH, D_MODEL, 2 * FOX_W), din),
        nrm(ks[8], (DEPTH, D_MODEL, FOX_W), din * DN_BETA),
        nrm(ks[9], (DEPTH, D_MODEL, IN_WIDTH - OFF_FGATE), din),
    ], axis=-1)
    b_gate = nrm(ks[10], (DEPTH, N_BRANCHES * D_MODEL), 0.1)
    b_forget = 3.0 + nrm(ks[11], (DEPTH, FOX_HEADS), 0.1)
    mla_q_norm = 1.0 + nrm(ks[12], (DEPTH, MLA_Q_RANK), 0.02)
    w_uq = nrm(ks[13], (DEPTH, MLA_Q_RANK, MLA_HEADS * MLA_QK_DIM), MLA_Q_RANK ** -0.5)
    mla_kv_norm = 1.0 + nrm(ks[14], (DEPTH, MLA_KV_RANK), 0.02)
    w_uk = nrm(ks[15], (DEPTH, MLA_KV_RANK, MLA_HEADS, MLA_NOPE_DIM), MLA_KV_RANK ** -0.5)
    w_uv = nrm(ks[16], (DEPTH, MLA_KV_RANK, MLA_HEADS, MLA_V_DIM), MLA_KV_RANK ** -0.5 * DN_BETA)
    w_ukv = jnp.concatenate([w_uk, w_uv], axis=-1).reshape(DEPTH, MLA_KV_RANK, MLA_HEADS * (MLA_NOPE_DIM + MLA_V_DIM))
    w_proj_sb = nrm(ks[17], (DEPTH, SB_W, D_MODEL), SB_W ** -0.5 * DN_BETA)
    w_proj_fox = nrm(ks[18], (DEPTH, FOX_W, D_MODEL), FOX_W ** -0.5 * DN_BETA)
    w_proj_mla = nrm(ks[19], (DEPTH, MLA_HEADS * MLA_V_DIM, D_MODEL), (MLA_HEADS * MLA_V_DIM) ** -0.5 * DN_BETA)
    w_out = nrm(ks[20], (DEPTH, D_MODEL, D_MODEL), din * DN_BETA)
    w_router = nrm(ks[21], (DEPTH, D_MODEL, N_EXPERTS), din)
    router_bias = nrm(ks[22], (DEPTH, N_EXPERTS), 0.01)
    w_exp_gate = nrm(ks[23], (DEPTH, N_EXPERTS, D_MODEL, EXPERT_FF), din)
    w_exp_up = nrm(ks[24], (DEPTH, N_EXPERTS, D_MODEL, EXPERT_FF), din)
    w_exp_down = nrm(ks[25], (DEPTH, N_EXPERTS, EXPERT_FF, D_MODEL), EXPERT_FF ** -0.5 * DN_BETA)
    w_sh_gate = nrm(ks[26], (DEPTH, D_MODEL, SHARED_FF), din)
    w_sh_up = nrm(ks[27], (DEPTH, D_MODEL, SHARED_FF), din)
    w_sh_down = nrm(ks[28], (DEPTH, SHARED_FF, D_MODEL), SHARED_FF ** -0.5 * DN_BETA)
    return {"x": x, "ln1_g": ln1_g, "ln1_b": ln1_b, "ln2_g": ln2_g, "ln2_b": ln2_b,
            "w_in": w_in, "b_gate": b_gate, "b_forget": b_forget,
            "mla_q_norm": mla_q_norm, "w_uq": w_uq, "mla_kv_norm": mla_kv_norm, "w_ukv": w_ukv,
            "w_proj_sb": w_proj_sb, "w_proj_fox": w_proj_fox, "w_proj_mla": w_proj_mla, "w_out": w_out,
            "w_router": w_router, "router_bias": router_bias,
            "w_exp_gate": w_exp_gate, "w_exp_up": w_exp_up, "w_exp_down": w_exp_down,
            "w_sh_gate": w_sh_gate, "w_sh_up": w_sh_up, "w_sh_down": w_sh_down}


def reference(x, ln1_g, ln1_b, ln2_g, ln2_b, w_in, b_gate, b_forget, mla_q_norm, w_uq,
              mla_kv_norm, w_ukv, w_proj_sb, w_proj_fox, w_proj_mla, w_out, w_router, router_bias,
              w_exp_gate, w_exp_up, w_exp_down, w_sh_gate, w_sh_up, w_sh_down):
    for l in range(DEPTH):
        mix = _token_mixers(x, w_in[l], b_gate[l], b_forget[l], mla_q_norm[l], w_uq[l],
                            mla_kv_norm[l], w_ukv[l], w_proj_sb[l], w_proj_fox[l], w_proj_mla[l], w_out[l])
        x = _layer_norm(DN_ALPHA * x + mix, ln1_g[l], ln1_b[l])
        ffn = _moe(x, w_router[l], router_bias[l], w_exp_gate[l], w_exp_up[l], w_exp_down[l],
                   w_sh_gate[l], w_sh_up[l], w_sh_down[l])
        x = _layer_norm(DN_ALPHA * x + ffn, ln2_g[l], ln2_b[l])
    return x
```

```python
import functools

import jax
import jax.numpy as jnp
from jax import lax
from jax.experimental import pallas as pl
from jax.experimental.pallas import tpu as pltpu

F32 = jnp.float32
BF16 = jnp.bfloat16

D_MODEL = 1024
HEAD_DIM = 64
SB_HEADS = 4
FOX_HEADS = 4
MLA_HEADS = 8
MLA_Q_RANK = 256
MLA_KV_RANK = 128
MLA_NOPE_DIM = 64
MLA_ROPE_DIM = 32
MLA_V_DIM = 64
ROPE_BASE = 10000.0
N_BRANCHES = 3
N_EXPERTS = 256
TOP_K = 8
N_GROUPS = 8
TOPK_GROUPS = 4
GROUP_SIZE = N_EXPERTS // N_GROUPS
EXPERT_FF = 256
SHARED_FF = 256
ROUTED_SCALE = 2.5
CHUNK = 64
LN_EPS = 1e-5
RMS_EPS = 1e-6
DEPTH = 2
DN_ALPHA = (2 * DEPTH) ** 0.25

SB_W = SB_HEADS * HEAD_DIM
FOX_W = FOX_HEADS * HEAD_DIM
MLA_QK_DIM = MLA_NOPE_DIM + MLA_ROPE_DIM
OFF_SB = 0
OFF_FOX = OFF_SB + 3 * SB_W
OFF_FGATE = OFF_FOX + 3 * FOX_W
OFF_DQ = OFF_FGATE + FOX_HEADS
OFF_DKV = OFF_DQ + MLA_Q_RANK
OFF_KR = OFF_DKV + MLA_KV_RANK
OFF_GATE = OFF_KR + MLA_ROPE_DIM

LANES = 128
SUBLANES = 8
ROW_TILES = D_MODEL // LANES
QKV_W = 3 * SB_W + 3 * FOX_W
SMALL_W = MLA_Q_RANK + MLA_KV_RANK + 3 * LANES
SB_CUTOFF = -104.0

VMEM_LIMIT = 48 * 1024 * 1024


def _cp(sem, vmem=VMEM_LIMIT):
    return pltpu.CompilerParams(dimension_semantics=sem, vmem_limit_bytes=vmem)


def _mm_kernel(x_ref, w_ref, b_ref, o_ref, *, act):
    acc = jnp.dot(x_ref[...], w_ref[...], preferred_element_type=F32)
    if act == "sigmoid":
        acc = jax.nn.sigmoid(acc + b_ref[...])
    o_ref[...] = acc.astype(o_ref.dtype)


def _matmul(x, w, bias, out_dtype, act=None, tm=1024, tn=512):
    n, k = x.shape
    c = w.shape[1]
    tm = min(tm, n)
    tn = min(tn, c)
    if bias is None:
        bias = jnp.zeros((1, c), F32)
    return pl.pallas_call(
        functools.partial(_mm_kernel, act=act),
        grid=(n // tm, c // tn),
        in_specs=[pl.BlockSpec((tm, k), lambda i, j: (i, 0)),
                  pl.BlockSpec((k, tn), lambda i, j: (0, j)),
                  pl.BlockSpec((1, tn), lambda i, j: (0, j))],
        out_specs=pl.BlockSpec((tm, tn), lambda i, j: (i, j)),
        out_shape=jax.ShapeDtypeStruct((n, c), out_dtype),
        compiler_params=_cp(("arbitrary", "arbitrary")),
        name="proj_matmul",
    )(x, w, bias)


def _split_bf16(x, parts):
    out = []
    for _ in range(parts):
        h = x.astype(BF16)
        out.append(h)
        x = x - h.astype(F32)
    return out


def _mla_prep_kernel(sm_ref, qn_ref, kvn_ref, wqa_ref, wqb_ref, wk_ref, wv_ref,
                     cos_ref, sin_ref, bf_ref, q_ref, k_ref, v_ref, lf_ref):
    sm = sm_ref[...]
    dq = sm[:, :MLA_Q_RANK]
    dkv = sm[:, MLA_Q_RANK:MLA_Q_RANK + MLA_KV_RANK]
    o = MLA_Q_RANK + MLA_KV_RANK
    kr = sm[:, o:o + LANES]
    kr_rot = sm[:, o + LANES:o + 2 * LANES]
    fg = sm[:, o + 2 * LANES:o + 3 * LANES]

    cq = dq * lax.rsqrt(jnp.mean(dq * dq, axis=-1, keepdims=True) + RMS_EPS) * qn_ref[...]
    ckv = dkv * lax.rsqrt(jnp.mean(dkv * dkv, axis=-1, keepdims=True) + RMS_EPS) * kvn_ref[...]
    cq = cq.astype(BF16)
    ckv = ckv.astype(BF16)

    cosk = cos_ref[...]
    sink = sin_ref[...]
    lane = lax.broadcasted_iota(jnp.int32, (1, LANES), 1)
    nope = (lane < MLA_NOPE_DIM).astype(F32)
    scale = MLA_QK_DIM ** -0.5
    cq_tab = jnp.concatenate([(cosk + nope) * scale] * MLA_HEADS, axis=1)
    sq_tab = jnp.concatenate([sink * scale] * MLA_HEADS, axis=1)

    qa = jnp.dot(cq, wqa_ref[...], preferred_element_type=F32)
    qb = jnp.dot(cq, wqb_ref[...], preferred_element_type=F32)
    q_ref[...] = (qa * cq_tab + qb * sq_tab).astype(q_ref.dtype)

    k_rope = kr * cosk + kr_rot * sink
    ka = jnp.dot(ckv, wk_ref[...], preferred_element_type=F32)
    k_ref[...] = (ka + jnp.concatenate([k_rope] * MLA_HEADS, axis=1)).astype(k_ref.dtype)
    v_ref[...] = jnp.dot(ckv, wv_ref[...], preferred_element_type=F32).astype(v_ref.dtype)
    lf_ref[...] = jax.nn.log_sigmoid(fg + bf_ref[...])


def _mla_prep(small, qn, kvn, wqa, wqb, wk, wv, cos_t, sin_t, bf, seq, tm=512):
    n = small.shape[0]
    tm = min(tm, seq)
    sblocks = seq // tm
    hw = MLA_HEADS * LANES
    full = lambda a: pl.BlockSpec(a.shape, lambda i: (0,) * a.ndim)
    return pl.pallas_call(
        _mla_prep_kernel,
        grid=(n // tm,),
        in_specs=[pl.BlockSpec((tm, SMALL_W), lambda i: (i, 0)),
                  full(qn), full(kvn), full(wqa), full(wqb), full(wk), full(wv),
                  pl.BlockSpec((tm, LANES), lambda i: (i % sblocks, 0)),
                  pl.BlockSpec((tm, LANES), lambda i: (i % sblocks, 0)),
                  full(bf)],
        out_specs=[pl.BlockSpec((tm, hw), lambda i: (i, 0)),
                   pl.BlockSpec((tm, hw), lambda i: (i, 0)),
                   pl.BlockSpec((tm, MLA_HEADS * MLA_V_DIM), lambda i: (i, 0)),
                   pl.BlockSpec((tm, LANES), lambda i: (i, 0))],
        out_shape=[jax.ShapeDtypeStruct((n, hw), BF16),
                   jax.ShapeDtypeStruct((n, hw), BF16),
                   jax.ShapeDtypeStruct((n, MLA_HEADS * MLA_V_DIM), BF16),
                   jax.ShapeDtypeStruct((n, LANES), F32)],
        compiler_params=_cp(("arbitrary",)),
        name="mla_prep",
    )(small, qn, kvn, wqa, wqb, wk, wv, cos_t, sin_t, bf)


def _cumsum_kernel(lf_ref, cum_ref, cumt_ref, *, seq):
    r = lax.broadcasted_iota(jnp.int32, (LANES, LANES), 0)
    c = lax.broadcasted_iota(jnp.int32, (LANES, LANES), 1)
    lower = (c <= r).astype(BF16)

    def body(j, carry):
        start = pl.multiple_of(j * LANES, LANES)
        blk = lf_ref[0, pl.ds(start, LANES), :]
        acc = carry
        for part in _split_bf16(blk, 3):
            acc = acc + jnp.dot(lower, part, preferred_element_type=F32)
        cum_ref[0, pl.ds(start, LANES), :] = acc
        cumt_ref[0, :, pl.ds(start, LANES)] = acc.T[:SUBLANES, :]
        return jnp.broadcast_to(acc[LANES - 1:LANES, :], (LANES, LANES))

    lax.fori_loop(0, seq // LANES, body, jnp.zeros((LANES, LANES), F32))


def _cumsum(lf3):
    b, seq, _ = lf3.shape
    return pl.pallas_call(
        functools.partial(_cumsum_kernel, seq=seq),
        grid=(b,),
        in_specs=[pl.BlockSpec((1, seq, LANES), lambda i: (i, 0, 0))],
        out_specs=[pl.BlockSpec((1, seq, LANES), lambda i: (i, 0, 0)),
                   pl.BlockSpec((1, SUBLANES, seq), lambda i: (i, 0, 0))],
        out_shape=[jax.ShapeDtypeStruct((b, seq, LANES), F32),
                   jax.ShapeDtypeStruct((b, SUBLANES, seq), F32)],
        compiler_params=_cp(("arbitrary",)),
        name="forget_cumsum",
    )(lf3)


_NT = (((1,), (1,)), ((), ()))
NEG_INIT = -1e30


def _softmax_step(s, m, l, acc, v):
    m_new = jnp.maximum(m, jnp.max(s, axis=1, keepdims=True))
    alpha = jnp.exp(m - m_new)
    p = jnp.exp(s - m_new)
    l = alpha * l + jnp.sum(p, axis=1, keepdims=True)
    acc = alpha * acc + jnp.dot(p.astype(BF16), v, preferred_element_type=F32)
    return m_new, l, acc


def _pair_out(accs, ls, o_ref):
    lane = lax.broadcasted_iota(jnp.int32, (1, LANES), 1)
    o0 = accs[0] / ls[0]
    o1 = accs[1] / ls[1]
    o_ref[0] = jnp.where(lane < HEAD_DIM, o0, o1).astype(o_ref.dtype)


def _fox_kernel(q_ref, k_ref, v_ref, cc_ref, cr_ref, o_ref, *, tq):
    hp = pl.program_id(1)
    i = pl.program_id(2)
    lane = lax.broadcasted_iota(jnp.int32, (1, LANES), 1)
    q = q_ref[0]
    zero = jnp.zeros_like(q)
    qs = [jnp.where(lane < HEAD_DIM, q, zero), jnp.where(lane >= HEAD_DIM, q, zero)]
    cc = cc_ref[0]
    cts = [jnp.sum(jnp.where(lane == 2 * hp + hh, cc, 0.0), axis=1, keepdims=True)
           for hh in range(2)]

    def chunk(j, carry, diag):
        start = pl.multiple_of(j * tq, tq)
        k = k_ref[0, pl.ds(start, tq), :]
        v = v_ref[0, pl.ds(start, tq), :]
        out = []
        for hh in range(2):
            m, l, acc = carry[hh]
            s = lax.dot_general(qs[hh], k, _NT, preferred_element_type=F32)
            cs = cr_ref[0, hh, :, pl.ds(start, tq)]
            s = s + (cts[hh] - cs)
            if diag:
                row = lax.broadcasted_iota(jnp.int32, (tq, tq), 0)
                col = lax.broadcasted_iota(jnp.int32, (tq, tq), 1)
                s = jnp.where(col <= row, s, -jnp.inf)
            out.append(_softmax_step(s, m, l, acc, v))
        return tuple(out)

    init = tuple((jnp.full((tq, 1), NEG_INIT, F32), jnp.zeros((tq, 1), F32),
                  jnp.zeros((tq, LANES), F32)) for _ in range(2))
    carry = lax.fori_loop(0, i, lambda j, c: chunk(j, c, False), init)
    carry = chunk(i, carry, True)
    _pair_out([carry[0][2], carry[1][2]], [carry[0][1], carry[1][1]], o_ref)


def _fox_attention(qkv3, cum, cumt4, tq=256):
    b, seq, _ = qkv3.shape
    tq = min(tq, seq)
    hp = FOX_HEADS // 2
    qoff = OFF_FOX // LANES
    return pl.pallas_call(
        functools.partial(_fox_kernel, tq=tq),
        grid=(b, hp, seq // tq),
        in_specs=[pl.BlockSpec((1, tq, LANES), lambda bi, h, i: (bi, i, qoff + h)),
                  pl.BlockSpec((1, seq, LANES), lambda bi, h, i: (bi, 0, qoff + hp + h)),
                  pl.BlockSpec((1, seq, LANES), lambda bi, h, i: (bi, 0, qoff + 2 * hp + h)),
                  pl.BlockSpec((1, tq, LANES), lambda bi, h, i: (bi, i, 0)),
                  pl.BlockSpec((1, 2, 1, seq), lambda bi, h, i: (bi, h, 0, 0))],
        out_specs=pl.BlockSpec((1, tq, LANES), lambda bi, h, i: (bi, i, h)),
        out_shape=jax.ShapeDtypeStruct((b, seq, FOX_W), BF16),
        compiler_params=_cp(("arbitrary", "arbitrary", "arbitrary")),
        name="fox_attention",
    )(qkv3, qkv3, qkv3, cum, cumt4)


def _mla_kernel(q_ref, k_ref, v_ref, o_ref, *, tq):
    i = pl.program_id(2)
    q = q_ref[0]
    qs = [q[:, :LANES], q[:, LANES:]]

    def chunk(j, carry, diag):
        start = pl.multiple_of(j * tq, tq)
        k = k_ref[0, pl.ds(start, tq), :]
        v = v_ref[0, pl.ds(start, tq), :]
        out = []
        for hh in range(2):
            m, l, acc = carry[hh]
            s = lax.dot_general(qs[hh], k[:, hh * LANES:(hh + 1) * LANES], _NT,
                                preferred_element_type=F32)
            if diag:
                row = lax.broadcasted_iota(jnp.int32, (tq, tq), 0)
                col = lax.broadcasted_iota(jnp.int32, (tq, tq), 1)
                shift = CHUNK.bit_length() - 1
                s = jnp.where((col >> shift) <= (row >> shift), s, -jnp.inf)
            out.append(_softmax_step(s, m, l, acc, v))
        return tuple(out)

    init = tuple((jnp.full((tq, 1), NEG_INIT, F32), jnp.zeros((tq, 1), F32),
                  jnp.zeros((tq, LANES), F32)) for _ in range(2))
    carry = lax.fori_loop(0, i, lambda j, c: chunk(j, c, False), init)
    carry = chunk(i, carry, True)
    _pair_out([carry[0][2], carry[1][2]], [carry[0][1], carry[1][1]], o_ref)


def _mla_attention(q3, k3, v3, tq=256):
    b, seq, _ = q3.shape
    tq = min(tq, seq)
    hp = MLA_HEADS // 2
    return pl.pallas_call(
        functools.partial(_mla_kernel, tq=tq),
        grid=(b, hp, seq // tq),
        in_specs=[pl.BlockSpec((1, tq, 2 * LANES), lambda bi, h, i: (bi, i, h)),
                  pl.BlockSpec((1, seq, 2 * LANES), lambda bi, h, i: (bi, 0, h)),
                  pl.BlockSpec((1, seq, LANES), lambda bi, h, i: (bi, 0, h))],
        out_specs=pl.BlockSpec((1, tq, LANES), lambda bi, h, i: (bi, i, h)),
        out_shape=jax.ShapeDtypeStruct((b, seq, MLA_HEADS * MLA_V_DIM), BF16),
        compiler_params=_cp(("arbitrary", "arbitrary", "arbitrary")),
        name="mla_attention",
    )(q3, k3, v3)


def _softplus(z):
    return jnp.maximum(z, 0.0) + jnp.log(1.0 + jnp.exp(-jnp.abs(z)))


def _sb_kernel(q_ref, k_ref, v_ref, o_ref, *, tq):
    i = pl.program_id(2)
    lane = lax.broadcasted_iota(jnp.int32, (1, LANES), 1)
    q = q_ref[0]
    zero = jnp.zeros_like(q)
    qs = [jnp.where(lane < HEAD_DIM, q, zero), jnp.where(lane >= HEAD_DIM, q, zero)]
    row = lax.broadcasted_iota(jnp.int32, (tq, tq), 0)
    col = lax.broadcasted_iota(jnp.int32, (tq, tq), 1)
    later = (row > col).astype(BF16)
    strict = col < row

    def chunk(j, carry, diag):
        start = pl.multiple_of(j * tq, tq)
        k = k_ref[0, pl.ds(start, tq), :]
        v = v_ref[0, pl.ds(start, tq), :]
        out = []
        for hh in range(2):
            run, acc = carry[hh]
            z = lax.dot_general(qs[hh], k, _NT, preferred_element_type=F32)
            sp = _softplus(z)
            log_keep = -sp
            if diag:
                log_keep = jnp.where(strict, log_keep, 0.0)
            between = run
            for part in _split_bf16(log_keep, 2):
                between = between + jnp.dot(part, later, preferred_element_type=F32)
            w = jnp.exp((z - sp) + between)
            if diag:
                w = jnp.where(strict, w, 0.0)
            acc = acc + jnp.dot(w.astype(BF16), v, preferred_element_type=F32)
            run = run + jnp.sum(log_keep, axis=1, keepdims=True)
            out.append((run, acc))
        return tuple(out)

    init = tuple((jnp.zeros((tq, 1), F32), jnp.zeros((tq, LANES), F32)) for _ in range(2))
    carry = chunk(i, init, True)
    carry = lax.fori_loop(0, i, lambda jj, c: chunk(i - 1 - jj, c, False), carry)
    o_ref[0] = jnp.where(lane < HEAD_DIM, carry[0][1], carry[1][1]).astype(o_ref.dtype)


def _sb_attention(qkv3, tq=256):
    b, seq, _ = qkv3.shape
    tq = min(tq, seq)
    hp = SB_HEADS // 2
    qoff = OFF_SB // LANES
    return pl.pallas_call(
        functools.partial(_sb_kernel, tq=tq),
        grid=(b, hp, seq // tq),
        in_specs=[pl.BlockSpec((1, tq, LANES), lambda bi, h, i: (bi, i, qoff + h)),
                  pl.BlockSpec((1, seq, LANES), lambda bi, h, i: (bi, 0, qoff + hp + h)),
                  pl.BlockSpec((1, seq, LANES), lambda bi, h, i: (bi, 0, qoff + 2 * hp + h))],
        out_specs=pl.BlockSpec((1, tq, LANES), lambda bi, h, i: (bi, i, h)),
        out_shape=jax.ShapeDtypeStruct((b, seq, SB_W), BF16),
        compiler_params=_cp(("arbitrary", "arbitrary", "arbitrary")),
        name="sb_attention",
    )(qkv3, qkv3, qkv3)


def _layer_norm(y, g, b):
    yc = y - jnp.mean(y, axis=-1, keepdims=True)
    var = jnp.mean(yc * yc, axis=-1, keepdims=True)
    return yc * lax.rsqrt(var + LN_EPS) * g + b


def _store_row_tiles(ref, y):
    for c in range(ROW_TILES):
        ref[:, c, :] = y[:, c * LANES:(c + 1) * LANES]


def _load_row_tiles(ref, *lead):
    return jnp.concatenate([ref[lead + (slice(None), c, slice(None))] for c in range(ROW_TILES)],
                           axis=1)


def _merge_kernel(osb_ref, ofox_ref, omla_ref, g_ref, x_ref, wsb_ref, wfox_ref, wmla_ref,
                  wout_ref, lg_ref, lb_ref, x1_ref, x1b_ref, x1t_ref):
    g = g_ref[...].astype(F32)
    a = jnp.dot(osb_ref[...], wsb_ref[...], preferred_element_type=F32)
    b = jnp.dot(ofox_ref[...], wfox_ref[...], preferred_element_type=F32)
    c = jnp.dot(omla_ref[...], wmla_ref[...], preferred_element_type=F32)
    mixed = (g[:, :D_MODEL] * a + g[:, D_MODEL:2 * D_MODEL] * b + g[:, 2 * D_MODEL:] * c)
    mix = jnp.dot(mixed.astype(BF16), wout_ref[...], preferred_element_type=F32)
    y = _layer_norm(DN_ALPHA * x_ref[...] + mix, lg_ref[...], lb_ref[...])
    x1_ref[...] = y
    x1b_ref[...] = y.astype(BF16)
    _store_row_tiles(x1t_ref, y)


def _merge(o_sb, o_fox, o_mla, gates, x, wsb, wfox, wmla, wout, lg, lb, tm=512):
    n = x.shape[0]
    tm = min(tm, n)
    full = lambda a: pl.BlockSpec(a.shape, lambda i: (0,) * a.ndim)
    rows = lambda w: pl.BlockSpec((tm, w), lambda i: (i, 0))
    return pl.pallas_call(
        _merge_kernel,
        grid=(n // tm,),
        in_specs=[rows(SB_W), rows(FOX_W), rows(MLA_HEADS * MLA_V_DIM), rows(N_BRANCHES * D_MODEL),
                  rows(D_MODEL), full(wsb), full(wfox), full(wmla), full(wout), full(lg), full(lb)],
        out_specs=[rows(D_MODEL), rows(D_MODEL),
                   pl.BlockSpec((tm, ROW_TILES, LANES), lambda i: (i, 0, 0))],
        out_shape=[jax.ShapeDtypeStruct((n, D_MODEL), F32),
                   jax.ShapeDtypeStruct((n, D_MODEL), BF16),
                   jax.ShapeDtypeStruct((n, ROW_TILES, LANES), F32)],
        compiler_params=_cp(("arbitrary",)),
        name="merge_outproj_ln",
    )(o_sb, o_fox, o_mla, gates, x, wsb, wfox, wmla, wout, lg, lb)


def _route_kernel(x_ref, wr_ref, rb_ref, idx_ref, w_ref, rank_ref, cnt_ref, run_ref, *, tm):
    step = pl.program_id(0)

    @pl.when(step == 0)
    def _():
        run_ref[...] = jnp.zeros_like(run_ref)

    logits = lax.dot_general(wr_ref[...], x_ref[...], _NT, preferred_element_type=F32)
    scores = jax.nn.sigmoid(logits)
    biased = scores + rb_ref[...]
    e_iota = lax.broadcasted_iota(jnp.int32, (N_EXPERTS, tm), 0)
    big = jnp.int32(1 << 20)

    g_iota = lax.broadcasted_iota(jnp.int32, (GROUP_SIZE, tm), 0)
    gs_rows = []
    for g in range(N_GROUPS):
        blk = biased[g * GROUP_SIZE:(g + 1) * GROUP_SIZE, :]
        m1 = jnp.max(blk, axis=0, keepdims=True)
        first = jnp.min(jnp.where(blk == m1, g_iota, big), axis=0, keepdims=True)
        m2 = jnp.max(jnp.where(g_iota == first, -jnp.inf, blk), axis=0, keepdims=True)
        gs_rows.append(m1 + m2)
    gs = jnp.concatenate(gs_rows, axis=0)
    n_iota = lax.broadcasted_iota(jnp.int32, (N_GROUPS, tm), 0)
    keep = jnp.zeros((N_GROUPS, tm), jnp.bool_)
    for _ in range(TOPK_GROUPS):
        m = jnp.max(gs, axis=0, keepdims=True)
        first = jnp.min(jnp.where(gs == m, n_iota, big), axis=0, keepdims=True)
        hit = n_iota == first
        keep = jnp.logical_or(keep, hit)
        gs = jnp.where(hit, -jnp.inf, gs)
    keep_f = keep.astype(F32)
    expert_keep = jnp.concatenate(
        [jnp.broadcast_to(keep_f[g:g + 1, :], (GROUP_SIZE, tm)) for g in range(N_GROUPS)], axis=0)
    masked = jnp.where(expert_keep > 0.5, biased, -jnp.inf)

    idx_rows, w_rows, hits = [], [], []
    sel = jnp.zeros((N_EXPERTS, tm), F32)
    for _ in range(TOP_K):
        m = jnp.max(masked, axis=0, keepdims=True)
        first = jnp.min(jnp.where(masked == m, e_iota, big), axis=0, keepdims=True)
        hit = e_iota == first
        idx_rows.append(first)
        w_rows.append(jnp.sum(jnp.where(hit, scores, 0.0), axis=0, keepdims=True))
        hits.append(hit)
        sel = sel + hit.astype(F32)
        masked = jnp.where(hit, -jnp.inf, masked)
    w = jnp.concatenate(w_rows, axis=0)
    w = w / jnp.sum(w, axis=0, keepdims=True) * ROUTED_SCALE
    idx_ref[...] = jnp.concatenate(idx_rows, axis=0)
    w_ref[...] = w

    r = lax.broadcasted_iota(jnp.int32, (tm, tm), 0)
    c = lax.broadcasted_iota(jnp.int32, (tm, tm), 1)
    earlier = (r < c).astype(BF16)
    prefix = jnp.dot(sel.astype(BF16), earlier, preferred_element_type=F32) + run_ref[...]
    rank_rows = [jnp.sum(jnp.where(h, prefix, 0.0), axis=0, keepdims=True) for h in hits]
    rank_ref[...] = jnp.concatenate(rank_rows, axis=0).astype(jnp.int32)
    total = run_ref[...] + jnp.sum(sel, axis=1, keepdims=True)
    run_ref[...] = total
    cnt_ref[...] = jnp.broadcast_to(total, (N_EXPERTS, LANES))


def _route(x1b, wr_t, rb, tm=256):
    n = x1b.shape[0]
    tm = min(tm, n)
    return pl.pallas_call(
        functools.partial(_route_kernel, tm=tm),
        grid=(n // tm,),
        in_specs=[pl.BlockSpec((tm, D_MODEL), lambda i: (i, 0)),
                  pl.BlockSpec((N_EXPERTS, D_MODEL), lambda i: (0, 0)),
                  pl.BlockSpec((N_EXPERTS, 1), lambda i: (0, 0))],
        out_specs=[pl.BlockSpec((TOP_K, tm), lambda i: (0, i)),
                   pl.BlockSpec((TOP_K, tm), lambda i: (0, i)),
                   pl.BlockSpec((TOP_K, tm), lambda i: (0, i)),
                   pl.BlockSpec((N_EXPERTS, LANES), lambda i: (0, 0))],
        out_shape=[jax.ShapeDtypeStruct((TOP_K, n), jnp.int32),
                   jax.ShapeDtypeStruct((TOP_K, n), F32),
                   jax.ShapeDtypeStruct((TOP_K, n), jnp.int32),
                   jax.ShapeDtypeStruct((N_EXPERTS, LANES), F32)],
        scratch_shapes=[pltpu.VMEM((N_EXPERTS, 1), F32)],
        compiler_params=_cp(("arbitrary",)),
        name="router_topk",
    )(x1b, wr_t, rb)


def _row_copy(src_hbm, dst_hbm, src_row, dst_row, sem):
    return pltpu.make_async_copy(src_hbm.at[src_row], dst_hbm.at[dst_row], sem)


def _dispatch_kernel(dest_ref, x_hbm, xs_hbm, sem, *, tt):
    base = pl.program_id(0) * tt

    def issue(t, _):
        for k in range(TOP_K):
            _row_copy(x_hbm, xs_hbm, base + t, dest_ref[0, 0, t * TOP_K + k], sem).start()
        return 0

    lax.fori_loop(0, tt, issue, 0)

    def drain(t, _):
        for k in range(TOP_K):
            _row_copy(x_hbm, xs_hbm, base + t, dest_ref[0, 0, t * TOP_K + k], sem).wait()
        return 0

    lax.fori_loop(0, tt, drain, 0)


def _dispatch(dest_tok, x1t, tt=256):
    n = x1t.shape[0]
    tt = min(tt, n)
    dest3 = dest_tok.reshape(n // tt, 1, tt * TOP_K)
    return pl.pallas_call(
        functools.partial(_dispatch_kernel, tt=tt),
        grid=(n // tt,),
        in_specs=[pl.BlockSpec((1, 1, tt * TOP_K), lambda i: (i, 0, 0),
                               memory_space=pltpu.SMEM),
                  pl.BlockSpec(memory_space=pl.ANY)],
        out_specs=pl.BlockSpec(memory_space=pl.ANY),
        out_shape=jax.ShapeDtypeStruct((n * TOP_K, ROW_TILES, LANES), F32),
        scratch_shapes=[pltpu.SemaphoreType.DMA(())],
        compiler_params=_cp(("arbitrary",)),
        name="moe_dispatch",
    )(dest3, x1t)


def _expert_kernel(tile_ref, exp_ref, lo_ref, hi_ref, xs_ref, wg_ref, wu_ref, wd_ref, ys_ref,
                   *, bm):
    w = pl.program_id(0)
    lo = lo_ref[w]
    hi = hi_ref[w]

    @pl.when(hi > lo)
    def _():
        x = _load_row_tiles(xs_ref).astype(BF16)
        g = jnp.dot(x, wg_ref[0].astype(BF16), preferred_element_type=F32)
        u = jnp.dot(x, wu_ref[0].astype(BF16), preferred_element_type=F32)
        h = (g * jax.nn.sigmoid(g) * u).astype(BF16)
        y = jnp.dot(h, wd_ref[0].astype(BF16), preferred_element_type=F32)

        @pl.when(lo == 0)
        def _():
            _store_row_tiles(ys_ref, y)

        @pl.when(lo > 0)
        def _():
            rows = lax.broadcasted_iota(jnp.int32, (bm, 1), 0)
            mine = rows >= lo
            for c in range(ROW_TILES):
                ys_ref[:, c, :] = jnp.where(mine, y[:, c * LANES:(c + 1) * LANES], ys_ref[:, c, :])


def _experts(item_tile, item_exp, item_lo, item_hi, xs, wg, wu, wd, bm):
    r = xs.shape[0]
    n_items = item_tile.shape[0]
    grid_spec = pltpu.PrefetchScalarGridSpec(
        num_scalar_prefetch=4,
        grid=(n_items,),
        in_specs=[pl.BlockSpec((bm, ROW_TILES, LANES), lambda w, t, e, lo, hi: (t[w], 0, 0)),
                  pl.BlockSpec((1, D_MODEL, EXPERT_FF), lambda w, t, e, lo, hi: (e[w], 0, 0)),
                  pl.BlockSpec((1, D_MODEL, EXPERT_FF), lambda w, t, e, lo, hi: (e[w], 0, 0)),
                  pl.BlockSpec((1, EXPERT_FF, D_MODEL), lambda w, t, e, lo, hi: (e[w], 0, 0))],
        out_specs=pl.BlockSpec((bm, ROW_TILES, LANES), lambda w, t, e, lo, hi: (t[w], 0, 0)),
    )
    return pl.pallas_call(
        functools.partial(_expert_kernel, bm=bm),
        grid_spec=grid_spec,
        out_shape=jax.ShapeDtypeStruct((r, ROW_TILES, LANES), F32),
        compiler_params=_cp(("arbitrary",)),
        name="moe_experts",
    )(item_tile, item_exp, item_lo, item_hi, xs, wg, wu, wd)


def _expert_items(counts, n_rows, bm):
    n_tiles = n_rows // bm
    ends = jnp.cumsum(counts)
    starts = ends - counts
    bounds = jnp.sort(jnp.concatenate([jnp.arange(n_tiles, dtype=jnp.int32) * bm,
                                       starts.astype(jnp.int32)]))
    nxt = jnp.concatenate([bounds[1:], jnp.array([n_rows], jnp.int32)])
    tile = jnp.minimum(bounds // bm, n_tiles - 1)
    exp = jnp.minimum(jnp.searchsorted(ends, bounds, side="right"), N_EXPERTS - 1).astype(jnp.int32)
    lo = bounds - tile * bm
    hi = nxt - tile * bm
    return tile.astype(jnp.int32), exp, lo.astype(jnp.int32), hi.astype(jnp.int32), starts


def _combine_kernel(dcur_ref, dnext_ref, ys_hbm, tw_ref, x1_ref, x1b_ref, wsg_ref, wsu_ref,
                    wsd_ref, lg_ref, lb_ref, o_ref, ob_ref, buf, sem, *, tc, n_steps):
    i = pl.program_id(0)
    slot = lax.rem(i, 2)

    def issue(dref, s):
        def body(t, _):
            for k in range(TOP_K):
                pltpu.make_async_copy(ys_hbm.at[dref[0, 0, t * TOP_K + k]],
                                      buf.at[s, k, t], sem.at[s]).start()
            return 0
        lax.fori_loop(0, tc, body, 0)

    @pl.when(i == 0)
    def _():
        issue(dcur_ref, 0)

    @pl.when(i + 1 < n_steps)
    def _():
        issue(dnext_ref, 1 - slot)

    def drain(t, _):
        for k in range(TOP_K):
            pltpu.make_async_copy(ys_hbm.at[0], buf.at[slot, k, t], sem.at[slot]).wait()
        return 0

    lax.fori_loop(0, tc, drain, 0)

    tw = tw_ref[...]
    routed = jnp.zeros((tc, D_MODEL), F32)
    for k in range(TOP_K):
        routed = routed + tw[:, k:k + 1] * _load_row_tiles(buf, slot, k)
    xb = x1b_ref[...]
    g = jnp.dot(xb, wsg_ref[...], preferred_element_type=F32)
    u = jnp.dot(xb, wsu_ref[...], preferred_element_type=F32)
    h = (g * jax.nn.sigmoid(g) * u).astype(BF16)
    shared = jnp.dot(h, wsd_ref[...], preferred_element_type=F32)
    y = _layer_norm(DN_ALPHA * x1_ref[...] + (shared + routed), lg_ref[...], lb_ref[...])
    o_ref[...] = y
    ob_ref[...] = y.astype(BF16)


def _combine(dest_tok, ys, tw, x1, x1b, wsg, wsu, wsd, lg, lb, tc=128):
    n = x1.shape[0]
    tc = min(tc, n)
    n_steps = n // tc
    dest3 = dest_tok.reshape(n_steps, 1, tc * TOP_K)
    full = lambda a: pl.BlockSpec(a.shape, lambda i: (0,) * a.ndim)
    rows = lambda w: pl.BlockSpec((tc, w), lambda i: (i, 0))
    return pl.pallas_call(
        functools.partial(_combine_kernel, tc=tc, n_steps=n_steps),
        grid=(n_steps,),
        in_specs=[pl.BlockSpec((1, 1, tc * TOP_K), lambda i: (i, 0, 0), memory_space=pltpu.SMEM),
                  pl.BlockSpec((1, 1, tc * TOP_K),
                               lambda i: (jnp.minimum(i + 1, n_steps - 1), 0, 0),
                               memory_space=pltpu.SMEM),
                  pl.BlockSpec(memory_space=pl.ANY),
                  rows(TOP_K), rows(D_MODEL), rows(D_MODEL),
                  full(wsg), full(wsu), full(wsd), full(lg), full(lb)],
        out_specs=[rows(D_MODEL), rows(D_MODEL)],
        out_shape=[jax.ShapeDtypeStruct((n, D_MODEL), F32),
                   jax.ShapeDtypeStruct((n, D_MODEL), BF16)],
        scratch_shapes=[pltpu.VMEM((2, TOP_K, tc, ROW_TILES, LANES), F32),
                        pltpu.SemaphoreType.DMA((2,))],
        compiler_params=_cp(("arbitrary",)),
        name="moe_combine_ln",
    )(dest3, dest3, ys, tw, x1, x1b, wsg, wsu, wsd, lg, lb)


def _head_cols(w, heads, width, lo, hi):
    return w.reshape(w.shape[0], heads, width)[:, :, lo:hi]


def _prep_layer(w_in, b_gate, b_forget, mla_q_norm, w_uq, mla_kv_norm, w_ukv):
    f = lambda a: a.astype(BF16)
    half = MLA_ROPE_DIM // 2
    d = w_in.shape[0]
    qscale = HEAD_DIM ** -0.5
    w_qkv = jnp.concatenate([
        w_in[:, OFF_SB:OFF_SB + SB_W] * qscale, w_in[:, OFF_SB + SB_W:OFF_FOX],
        w_in[:, OFF_FOX:OFF_FOX + FOX_W] * qscale, w_in[:, OFF_FOX + FOX_W:OFF_FGATE]], axis=1)
    w_kr = w_in[:, OFF_KR:OFF_GATE]
    w_kr_rot = jnp.concatenate([-w_kr[:, half:], w_kr[:, :half]], axis=1)
    z = lambda c: jnp.zeros((d, c), F32)
    pad_rope = LANES - MLA_QK_DIM
    w_small = jnp.concatenate([
        w_in[:, OFF_DQ:OFF_DKV], w_in[:, OFF_DKV:OFF_KR],
        z(MLA_NOPE_DIM), w_kr, z(pad_rope),
        z(MLA_NOPE_DIM), w_kr_rot, z(pad_rope),
        w_in[:, OFF_FGATE:OFF_DQ], z(LANES - FOX_HEADS)], axis=1)
    w_gate = w_in[:, OFF_GATE:]

    r = w_uq.shape[0]
    q_nope = _head_cols(w_uq, MLA_HEADS, MLA_QK_DIM, 0, MLA_NOPE_DIM)
    q_rope = _head_cols(w_uq, MLA_HEADS, MLA_QK_DIM, MLA_NOPE_DIM, MLA_QK_DIM)
    q_rope_rot = jnp.concatenate([-q_rope[:, :, half:], q_rope[:, :, :half]], axis=2)
    zq = lambda c: jnp.zeros((r, MLA_HEADS, c), F32)
    wqa = jnp.concatenate([q_nope, q_rope, zq(pad_rope)], axis=2).reshape(r, MLA_HEADS * LANES)
    wqb = jnp.concatenate([zq(MLA_NOPE_DIM), q_rope_rot, zq(pad_rope)], axis=2).reshape(
        r, MLA_HEADS * LANES)
    rk = w_ukv.shape[0]
    kvw = MLA_NOPE_DIM + MLA_V_DIM
    k_nope = _head_cols(w_ukv, MLA_HEADS, kvw, 0, MLA_NOPE_DIM)
    wk = jnp.concatenate([k_nope, jnp.zeros((rk, MLA_HEADS, LANES - MLA_NOPE_DIM), F32)],
                         axis=2).reshape(rk, MLA_HEADS * LANES)
    wv = _head_cols(w_ukv, MLA_HEADS, kvw, MLA_NOPE_DIM, kvw).reshape(rk, MLA_HEADS * MLA_V_DIM)
    bf = jnp.concatenate([b_forget, jnp.zeros((LANES - FOX_HEADS,), F32)]).reshape(1, LANES)
    return dict(w_qkv=f(w_qkv), w_small=f(w_small), w_gate=f(w_gate),
                b_gate=b_gate.reshape(1, -1), wqa=f(wqa), wqb=f(wqb), wk=f(wk), wv=f(wv),
                qn=mla_q_norm.reshape(1, -1), kvn=mla_kv_norm.reshape(1, -1), bf=bf)


def _rope_tables(seq):
    half = MLA_ROPE_DIM // 2
    inv_freq = jnp.power(ROPE_BASE, -jnp.arange(half, dtype=F32) / half)
    ang = jnp.arange(seq).astype(F32)[:, None] * inv_freq[None, :]
    cos = jnp.concatenate([jnp.cos(ang), jnp.cos(ang)], axis=1)
    sin = jnp.concatenate([jnp.sin(ang), jnp.sin(ang)], axis=1)
    pad = lambda t: jnp.concatenate([jnp.zeros((seq, MLA_NOPE_DIM), F32), t,
                                     jnp.zeros((seq, LANES - MLA_QK_DIM), F32)], axis=1)
    return pad(cos), pad(sin)


EXPERT_BM = 256


def kernel(x, ln1_g, ln1_b, ln2_g, ln2_b, w_in, b_gate, b_forget, mla_q_norm, w_uq, mla_kv_norm,
           w_ukv, w_proj_sb, w_proj_fox, w_proj_mla, w_out, w_router, router_bias,
           w_exp_gate, w_exp_up, w_exp_down, w_sh_gate, w_sh_up, w_sh_down):
    b, seq, d = x.shape
    n = b * seq
    depth = w_in.shape[0]
    cos_t, sin_t = _rope_tables(seq)
    xf = x.reshape(n, d)
    xb = xf.astype(BF16)
    f = lambda a: a.astype(BF16)
    bm = min(EXPERT_BM, n * TOP_K)
    for l in range(depth):
        p = _prep_layer(w_in[l], b_gate[l], b_forget[l], mla_q_norm[l], w_uq[l], mla_kv_norm[l],
                        w_ukv[l])
        qkv = _matmul(xb, p["w_qkv"], None, BF16)
        gates = _matmul(xb, p["w_gate"], p["b_gate"], BF16, act="sigmoid")
        small = _matmul(xb, p["w_small"], None, F32, tn=SMALL_W)
        q_mla, k_mla, v_mla, lf = _mla_prep(small, p["qn"], p["kvn"], p["wqa"], p["wqb"], p["wk"],
                                            p["wv"], cos_t, sin_t, p["bf"], seq)
        cum, cumt = _cumsum(lf.reshape(b, seq, LANES))
        cumt4 = cumt[:, :FOX_HEADS, :].reshape(b, FOX_HEADS, 1, seq)
        qkv3 = qkv.reshape(b, seq, QKV_W)
        o_sb = _sb_attention(qkv3).reshape(n, SB_W)
        o_fox = _fox_attention(qkv3, cum, cumt4).reshape(n, FOX_W)
        o_mla = _mla_attention(q_mla.reshape(b, seq, -1), k_mla.reshape(b, seq, -1),
                               v_mla.reshape(b, seq, -1)).reshape(n, -1)
        x1, x1b, x1t = _merge(o_sb, o_fox, o_mla, gates, xf, f(w_proj_sb[l]), f(w_proj_fox[l]),
                              f(w_proj_mla[l]), f(w_out[l]), ln1_g[l].reshape(1, d),
                              ln1_b[l].reshape(1, d))

        idx_t, tw_t, rank_t, cnt = _route(x1b, f(w_router[l].T), router_bias[l].reshape(-1, 1))
        counts = cnt[:, 0].astype(jnp.int32)
        item_tile, item_exp, item_lo, item_hi, starts = _expert_items(counts, n * TOP_K, bm)
        dest_tok = (starts.astype(jnp.int32)[idx_t] + rank_t).T.reshape(-1)
        xs = _dispatch(dest_tok, x1t)
        ys = _experts(item_tile, item_exp, item_lo, item_hi, xs, w_exp_gate[l], w_exp_up[l],
                      w_exp_down[l], bm)
        xf, xb = _combine(dest_tok, ys, tw_t.T, x1, x1b, f(w_sh_gate[l]), f(w_sh_up[l]),
                          f(w_sh_down[l]), ln2_g[l].reshape(1, d), ln2_b[l].reshape(1, d))
    return xf.reshape(b, seq, d)
```

```python
import functools

import jax
import jax.numpy as jnp
from jax import lax
from jax.experimental import pallas as pl
from jax.experimental.pallas import tpu as pltpu

F32 = jnp.float32
BF16 = jnp.bfloat16

D_MODEL = 1024
HEAD_DIM = 64
SB_HEADS = 4
FOX_HEADS = 4
MLA_HEADS = 8
MLA_Q_RANK = 256
MLA_KV_RANK = 128
MLA_NOPE_DIM = 64
MLA_ROPE_DIM = 32
MLA_V_DIM = 64
ROPE_BASE = 10000.0
N_BRANCHES = 3
N_EXPERTS = 256
TOP_K = 8
N_GROUPS = 8
TOPK_GROUPS = 4
GROUP_SIZE = N_EXPERTS // N_GROUPS
EXPERT_FF = 256
SHARED_FF = 256
ROUTED_SCALE = 2.5
CHUNK = 64
LN_EPS = 1e-5
RMS_EPS = 1e-6
DEPTH = 2
DN_ALPHA = (2 * DEPTH) ** 0.25

SB_W = SB_HEADS * HEAD_DIM
FOX_W = FOX_HEADS * HEAD_DIM
MLA_QK_DIM = MLA_NOPE_DIM + MLA_ROPE_DIM
OFF_SB = 0
OFF_FOX = OFF_SB + 3 * SB_W
OFF_FGATE = OFF_FOX + 3 * FOX_W
OFF_DQ = OFF_FGATE + FOX_HEADS
OFF_DKV = OFF_DQ + MLA_Q_RANK
OFF_KR = OFF_DKV + MLA_KV_RANK
OFF_GATE = OFF_KR + MLA_ROPE_DIM

LANES = 128
SUBLANES = 8
QKV_W = 3 * SB_W + 3 * FOX_W
SMALL_W = MLA_Q_RANK + MLA_KV_RANK + 3 * LANES
SB_CUTOFF = -104.0

VMEM_LIMIT = 48 * 1024 * 1024


def _cp(sem, vmem=VMEM_LIMIT):
    return pltpu.CompilerParams(dimension_semantics=sem, vmem_limit_bytes=vmem)


def _mm_kernel(x_ref, w_ref, b_ref, o_ref, *, act):
    acc = jnp.dot(x_ref[...], w_ref[...], preferred_element_type=F32)
    if act == "sigmoid":
        acc = jax.nn.sigmoid(acc + b_ref[...])
    o_ref[...] = acc.astype(o_ref.dtype)


def _matmul(x, w, bias, out_dtype, act=None, tm=1024, tn=512):
    n, k = x.shape
    c = w.shape[1]
    tm = min(tm, n)
    tn = min(tn, c)
    if bias is None:
        bias = jnp.zeros((1, c), F32)
    return pl.pallas_call(
        functools.partial(_mm_kernel, act=act),
        grid=(n // tm, c // tn),
        in_specs=[pl.BlockSpec((tm, k), lambda i, j: (i, 0)),
                  pl.BlockSpec((k, tn), lambda i, j: (0, j)),
                  pl.BlockSpec((1, tn), lambda i, j: (0, j))],
        out_specs=pl.BlockSpec((tm, tn), lambda i, j: (i, j)),
        out_shape=jax.ShapeDtypeStruct((n, c), out_dtype),
        compiler_params=_cp(("arbitrary", "arbitrary")),
        name="proj_matmul",
    )(x, w, bias)


def _split_bf16(x, parts):
    out = []
    for _ in range(parts):
        h = x.astype(BF16)
        out.append(h)
        x = x - h.astype(F32)
    return out


def _mla_prep_kernel(sm_ref, qn_ref, kvn_ref, wqa_ref, wqb_ref, wk_ref, wv_ref,
                     cos_ref, sin_ref, bf_ref, q_ref, k_ref, v_ref, lf_ref):
    sm = sm_ref[...]
    dq = sm[:, :MLA_Q_RANK]
    dkv = sm[:, MLA_Q_RANK:MLA_Q_RANK + MLA_KV_RANK]
    o = MLA_Q_RANK + MLA_KV_RANK
    kr = sm[:, o:o + LANES]
    kr_rot = sm[:, o + LANES:o + 2 * LANES]
    fg = sm[:, o + 2 * LANES:o + 3 * LANES]

    cq = dq * lax.rsqrt(jnp.mean(dq * dq, axis=-1, keepdims=True) + RMS_EPS) * qn_ref[...]
    ckv = dkv * lax.rsqrt(jnp.mean(dkv * dkv, axis=-1, keepdims=True) + RMS_EPS) * kvn_ref[...]
    cq = cq.astype(BF16)
    ckv = ckv.astype(BF16)

    cosk = cos_ref[...]
    sink = sin_ref[...]
    lane = lax.broadcasted_iota(jnp.int32, (1, LANES), 1)
    nope = (lane < MLA_NOPE_DIM).astype(F32)
    scale = MLA_QK_DIM ** -0.5
    cq_tab = jnp.concatenate([(cosk + nope) * scale] * MLA_HEADS, axis=1)
    sq_tab = jnp.concatenate([sink * scale] * MLA_HEADS, axis=1)

    qa = jnp.dot(cq, wqa_ref[...], preferred_element_type=F32)
    qb = jnp.dot(cq, wqb_ref[...], preferred_element_type=F32)
    q_ref[...] = (qa * cq_tab + qb * sq_tab).astype(q_ref.dtype)

    k_rope = kr * cosk + kr_rot * sink
    ka = jnp.dot(ckv, wk_ref[...], preferred_element_type=F32)
    k_ref[...] = (ka + jnp.concatenate([k_rope] * MLA_HEADS, axis=1)).astype(k_ref.dtype)
    v_ref[...] = jnp.dot(ckv, wv_ref[...], preferred_element_type=F32).astype(v_ref.dtype)
    lf_ref[...] = jax.nn.log_sigmoid(fg + bf_ref[...])


def _mla_prep(small, qn, kvn, wqa, wqb, wk, wv, cos_t, sin_t, bf, seq, tm=512):
    n = small.shape[0]
    tm = min(tm, seq)
    sblocks = seq // tm
    hw = MLA_HEADS * LANES
    full = lambda a: pl.BlockSpec(a.shape, lambda i: (0,) * a.ndim)
    return pl.pallas_call(
        _mla_prep_kernel,
        grid=(n // tm,),
        in_specs=[pl.BlockSpec((tm, SMALL_W), lambda i: (i, 0)),
                  full(qn), full(kvn), full(wqa), full(wqb), full(wk), full(wv),
                  pl.BlockSpec((tm, LANES), lambda i: (i % sblocks, 0)),
                  pl.BlockSpec((tm, LANES), lambda i: (i % sblocks, 0)),
                  full(bf)],
        out_specs=[pl.BlockSpec((tm, hw), lambda i: (i, 0)),
                   pl.BlockSpec((tm, hw), lambda i: (i, 0)),
                   pl.BlockSpec((tm, MLA_HEADS * MLA_V_DIM), lambda i: (i, 0)),
                   pl.BlockSpec((tm, LANES), lambda i: (i, 0))],
        out_shape=[jax.ShapeDtypeStruct((n, hw), BF16),
                   jax.ShapeDtypeStruct((n, hw), BF16),
                   jax.ShapeDtypeStruct((n, MLA_HEADS * MLA_V_DIM), BF16),
                   jax.ShapeDtypeStruct((n, LANES), F32)],
        compiler_params=_cp(("arbitrary",)),
        name="mla_prep",
    )(small, qn, kvn, wqa, wqb, wk, wv, cos_t, sin_t, bf)


def _cumsum_kernel(lf_ref, cum_ref, cumt_ref, *, seq):
    r = lax.broadcasted_iota(jnp.int32, (LANES, LANES), 0)
    c = lax.broadcasted_iota(jnp.int32, (LANES, LANES), 1)
    lower = (c <= r).astype(BF16)

    def body(j, carry):
        start = pl.multiple_of(j * LANES, LANES)
        blk = lf_ref[0, pl.ds(start, LANES), :]
        acc = carry
        for part in _split_bf16(blk, 3):
            acc = acc + jnp.dot(lower, part, preferred_element_type=F32)
        cum_ref[0, pl.ds(start, LANES), :] = acc
        cumt_ref[0, :, pl.ds(start, LANES)] = acc.T[:SUBLANES, :]
        return jnp.broadcast_to(acc[LANES - 1:LANES, :], (LANES, LANES))

    lax.fori_loop(0, seq // LANES, body, jnp.zeros((LANES, LANES), F32))


def _cumsum(lf3):
    b, seq, _ = lf3.shape
    return pl.pallas_call(
        functools.partial(_cumsum_kernel, seq=seq),
        grid=(b,),
        in_specs=[pl.BlockSpec((1, seq, LANES), lambda i: (i, 0, 0))],
        out_specs=[pl.BlockSpec((1, seq, LANES), lambda i: (i, 0, 0)),
                   pl.BlockSpec((1, SUBLANES, seq), lambda i: (i, 0, 0))],
        out_shape=[jax.ShapeDtypeStruct((b, seq, LANES), F32),
                   jax.ShapeDtypeStruct((b, SUBLANES, seq), F32)],
        compiler_params=_cp(("arbitrary",)),
        name="forget_cumsum",
    )(lf3)


_NT = (((1,), (1,)), ((), ()))
NEG_INIT = -1e30


def _softmax_step(s, m, l, acc, v):
    m_new = jnp.maximum(m, jnp.max(s, axis=1, keepdims=True))
    alpha = jnp.exp(m - m_new)
    p = jnp.exp(s - m_new)
    l = alpha * l + jnp.sum(p, axis=1, keepdims=True)
    acc = alpha * acc + jnp.dot(p.astype(BF16), v, preferred_element_type=F32)
    return m_new, l, acc


def _pair_out(accs, ls, o_ref):
    lane = lax.broadcasted_iota(jnp.int32, (1, LANES), 1)
    o0 = accs[0] / ls[0]
    o1 = accs[1] / ls[1]
    o_ref[0] = jnp.where(lane < HEAD_DIM, o0, o1).astype(o_ref.dtype)


def _fox_kernel(q_ref, k_ref, v_ref, cc_ref, cr_ref, o_ref, *, tq):
    hp = pl.program_id(1)
    i = pl.program_id(2)
    lane = lax.broadcasted_iota(jnp.int32, (1, LANES), 1)
    q = q_ref[0]
    zero = jnp.zeros_like(q)
    qs = [jnp.where(lane < HEAD_DIM, q, zero), jnp.where(lane >= HEAD_DIM, q, zero)]
    cc = cc_ref[0]
    cts = [jnp.sum(jnp.where(lane == 2 * hp + hh, cc, 0.0), axis=1, keepdims=True)
           for hh in range(2)]

    def chunk(j, carry, diag):
        start = pl.multiple_of(j * tq, tq)
        k = k_ref[0, pl.ds(start, tq), :]
        v = v_ref[0, pl.ds(start, tq), :]
        out = []
        for hh in range(2):
            m, l, acc = carry[hh]
            s = lax.dot_general(qs[hh], k, _NT, preferred_element_type=F32)
            cs = cr_ref[0, hh, :, pl.ds(start, tq)]
            s = s + (cts[hh] - cs)
            if diag:
                row = lax.broadcasted_iota(jnp.int32, (tq, tq), 0)
                col = lax.broadcasted_iota(jnp.int32, (tq, tq), 1)
                s = jnp.where(col <= row, s, -jnp.inf)
            out.append(_softmax_step(s, m, l, acc, v))
        return tuple(out)

    init = tuple((jnp.full((tq, 1), NEG_INIT, F32), jnp.zeros((tq, 1), F32),
                  jnp.zeros((tq, LANES), F32)) for _ in range(2))
    carry = lax.fori_loop(0, i, lambda j, c: chunk(j, c, False), init)
    carry = chunk(i, carry, True)
    _pair_out([carry[0][2], carry[1][2]], [carry[0][1], carry[1][1]], o_ref)


def _fox_attention(qkv3, cum, cumt4, tq=256):
    b, seq, _ = qkv3.shape
    tq = min(tq, seq)
    hp = FOX_HEADS // 2
    qoff = OFF_FOX // LANES
    return pl.pallas_call(
        functools.partial(_fox_kernel, tq=tq),
        grid=(b, hp, seq // tq),
        in_specs=[pl.BlockSpec((1, tq, LANES), lambda bi, h, i: (bi, i, qoff + h)),
                  pl.BlockSpec((1, seq, LANES), lambda bi, h, i: (bi, 0, qoff + hp + h)),
                  pl.BlockSpec((1, seq, LANES), lambda bi, h, i: (bi, 0, qoff + 2 * hp + h)),
                  pl.BlockSpec((1, tq, LANES), lambda bi, h, i: (bi, i, 0)),
                  pl.BlockSpec((1, 2, 1, seq), lambda bi, h, i: (bi, h, 0, 0))],
        out_specs=pl.BlockSpec((1, tq, LANES), lambda bi, h, i: (bi, i, h)),
        out_shape=jax.ShapeDtypeStruct((b, seq, FOX_W), BF16),
        compiler_params=_cp(("arbitrary", "arbitrary", "arbitrary")),
        name="fox_attention",
    )(qkv3, qkv3, qkv3, cum, cumt4)


def _mla_kernel(q_ref, k_ref, v_ref, o_ref, *, tq):
    i = pl.program_id(2)
    q = q_ref[0]
    qs = [q[:, :LANES], q[:, LANES:]]

    def chunk(j, carry, diag):
        start = pl.multiple_of(j * tq, tq)
        k = k_ref[0, pl.ds(start, tq), :]
        v = v_ref[0, pl.ds(start, tq), :]
        out = []
        for hh in range(2):
            m, l, acc = carry[hh]
            s = lax.dot_general(qs[hh], k[:, hh * LANES:(hh + 1) * LANES], _NT,
                                preferred_element_type=F32)
            if diag:
                row = lax.broadcasted_iota(jnp.int32, (tq, tq), 0)
                col = lax.broadcasted_iota(jnp.int32, (tq, tq), 1)
                shift = CHUNK.bit_length() - 1
                s = jnp.where((col >> shift) <= (row >> shift), s, -jnp.inf)
            out.append(_softmax_step(s, m, l, acc, v))
        return tuple(out)

    init = tuple((jnp.full((tq, 1), NEG_INIT, F32), jnp.zeros((tq, 1), F32),
                  jnp.zeros((tq, LANES), F32)) for _ in range(2))
    carry = lax.fori_loop(0, i, lambda j, c: chunk(j, c, False), init)
    carry = chunk(i, carry, True)
    _pair_out([carry[0][2], carry[1][2]], [carry[0][1], carry[1][1]], o_ref)


def _mla_attention(q3, k3, v3, tq=256):
    b, seq, _ = q3.shape
    tq = min(tq, seq)
    hp = MLA_HEADS // 2
    return pl.pallas_call(
        functools.partial(_mla_kernel, tq=tq),
        grid=(b, hp, seq // tq),
        in_specs=[pl.BlockSpec((1, tq, 2 * LANES), lambda bi, h, i: (bi, i, h)),
                  pl.BlockSpec((1, seq, 2 * LANES), lambda bi, h, i: (bi, 0, h)),
                  pl.BlockSpec((1, seq, LANES), lambda bi, h, i: (bi, 0, h))],
        out_specs=pl.BlockSpec((1, tq, LANES), lambda bi, h, i: (bi, i, h)),
        out_shape=jax.ShapeDtypeStruct((b, seq, MLA_HEADS * MLA_V_DIM), BF16),
        compiler_params=_cp(("arbitrary", "arbitrary", "arbitrary")),
        name="mla_attention",
    )(q3, k3, v3)


def _softplus(z):
    return jnp.maximum(z, 0.0) + jnp.log(1.0 + jnp.exp(-jnp.abs(z)))


def _sb_kernel(q_ref, k_ref, v_ref, o_ref, *, tq):
    i = pl.program_id(2)
    lane = lax.broadcasted_iota(jnp.int32, (1, LANES), 1)
    q = q_ref[0]
    zero = jnp.zeros_like(q)
    qs = [jnp.where(lane < HEAD_DIM, q, zero), jnp.where(lane >= HEAD_DIM, q, zero)]
    row = lax.broadcasted_iota(jnp.int32, (tq, tq), 0)
    col = lax.broadcasted_iota(jnp.int32, (tq, tq), 1)
    later = (row > col).astype(BF16)
    strict = col < row

    def chunk(j, carry, diag):
        start = pl.multiple_of(j * tq, tq)
        k = k_ref[0, pl.ds(start, tq), :]
        v = v_ref[0, pl.ds(start, tq), :]
        out = []
        for hh in range(2):
            run, acc = carry[hh]
            z = lax.dot_general(qs[hh], k, _NT, preferred_element_type=F32)
            sp = _softplus(z)
            log_keep = -sp
            if diag:
                log_keep = jnp.where(strict, log_keep, 0.0)
            between = run
            for part in _split_bf16(log_keep, 2):
                between = between + jnp.dot(part, later, preferred_element_type=F32)
            w = jnp.exp((z - sp) + between)
            if diag:
                w = jnp.where(strict, w, 0.0)
            acc = acc + jnp.dot(w.astype(BF16), v, preferred_element_type=F32)
            run = run + jnp.sum(log_keep, axis=1, keepdims=True)
            out.append((run, acc))
        return tuple(out)

    init = tuple((jnp.zeros((tq, 1), F32), jnp.zeros((tq, LANES), F32)) for _ in range(2))
    carry = chunk(i, init, True)
    carry = lax.fori_loop(0, i, lambda jj, c: chunk(i - 1 - jj, c, False), carry)
    o_ref[0] = jnp.where(lane < HEAD_DIM, carry[0][1], carry[1][1]).astype(o_ref.dtype)


def _sb_attention(qkv3, tq=256):
    b, seq, _ = qkv3.shape
    tq = min(tq, seq)
    hp = SB_HEADS // 2
    qoff = OFF_SB // LANES
    return pl.pallas_call(
        functools.partial(_sb_kernel, tq=tq),
        grid=(b, hp, seq // tq),
        in_specs=[pl.BlockSpec((1, tq, LANES), lambda bi, h, i: (bi, i, qoff + h)),
                  pl.BlockSpec((1, seq, LANES), lambda bi, h, i: (bi, 0, qoff + hp + h)),
                  pl.BlockSpec((1, seq, LANES), lambda bi, h, i: (bi, 0, qoff + 2 * hp + h))],
        out_specs=pl.BlockSpec((1, tq, LANES), lambda bi, h, i: (bi, i, h)),
        out_shape=jax.ShapeDtypeStruct((b, seq, SB_W), BF16),
        compiler_params=_cp(("arbitrary", "arbitrary", "arbitrary")),
        name="sb_attention",
    )(qkv3, qkv3, qkv3)


def _layer_norm(y, g, b):
    yc = y - jnp.mean(y, axis=-1, keepdims=True)
    var = jnp.mean(yc * yc, axis=-1, keepdims=True)
    return yc * lax.rsqrt(var + LN_EPS) * g + b


def _merge_kernel(osb_ref, ofox_ref, omla_ref, g_ref, x_ref, wsb_ref, wfox_ref, wmla_ref,
                  wout_ref, lg_ref, lb_ref, x1_ref, x1b_ref):
    g = g_ref[...].astype(F32)
    a = jnp.dot(osb_ref[...], wsb_ref[...], preferred_element_type=F32)
    b = jnp.dot(ofox_ref[...], wfox_ref[...], preferred_element_type=F32)
    c = jnp.dot(omla_ref[...], wmla_ref[...], preferred_element_type=F32)
    mixed = (g[:, :D_MODEL] * a + g[:, D_MODEL:2 * D_MODEL] * b + g[:, 2 * D_MODEL:] * c)
    mix = jnp.dot(mixed.astype(BF16), wout_ref[...], preferred_element_type=F32)
    y = _layer_norm(DN_ALPHA * x_ref[...] + mix, lg_ref[...], lb_ref[...])
    x1_ref[...] = y
    x1b_ref[...] = y.astype(BF16)


def _merge(o_sb, o_fox, o_mla, gates, x, wsb, wfox, wmla, wout, lg, lb, tm=512):
    n = x.shape[0]
    tm = min(tm, n)
    full = lambda a: pl.BlockSpec(a.shape, lambda i: (0,) * a.ndim)
    rows = lambda w: pl.BlockSpec((tm, w), lambda i: (i, 0))
    return pl.pallas_call(
        _merge_kernel,
        grid=(n // tm,),
        in_specs=[rows(SB_W), rows(FOX_W), rows(MLA_HEADS * MLA_V_DIM), rows(N_BRANCHES * D_MODEL),
                  rows(D_MODEL), full(wsb), full(wfox), full(wmla), full(wout), full(lg), full(lb)],
        out_specs=[rows(D_MODEL), rows(D_MODEL)],
        out_shape=[jax.ShapeDtypeStruct((n, D_MODEL), F32),
                   jax.ShapeDtypeStruct((n, D_MODEL), BF16)],
        compiler_params=_cp(("arbitrary",)),
        name="merge_outproj_ln",
    )(o_sb, o_fox, o_mla, gates, x, wsb, wfox, wmla, wout, lg, lb)


def _route_kernel(x_ref, wr_ref, rb_ref, idx_ref, w_ref, rank_ref, cnt_ref, run_ref, *, tm):
    step = pl.program_id(0)

    @pl.when(step == 0)
    def _():
        run_ref[...] = jnp.zeros_like(run_ref)

    logits = lax.dot_general(wr_ref[...], x_ref[...], _NT, preferred_element_type=F32)
    scores = jax.nn.sigmoid(logits)
    biased = scores + rb_ref[...]
    e_iota = lax.broadcasted_iota(jnp.int32, (N_EXPERTS, tm), 0)
    big = jnp.int32(1 << 20)

    g_iota = lax.broadcasted_iota(jnp.int32, (GROUP_SIZE, tm), 0)
    gs_rows = []
    for g in range(N_GROUPS):
        blk = biased[g * GROUP_SIZE:(g + 1) * GROUP_SIZE, :]
        m1 = jnp.max(blk, axis=0, keepdims=True)
        first = jnp.min(jnp.where(blk == m1, g_iota, big), axis=0, keepdims=True)
        m2 = jnp.max(jnp.where(g_iota == first, -jnp.inf, blk), axis=0, keepdims=True)
        gs_rows.append(m1 + m2)
    gs = jnp.concatenate(gs_rows, axis=0)
    n_iota = lax.broadcasted_iota(jnp.int32, (N_GROUPS, tm), 0)
    keep = jnp.zeros((N_GROUPS, tm), jnp.bool_)
    for _ in range(TOPK_GROUPS):
        m = jnp.max(gs, axis=0, keepdims=True)
        first = jnp.min(jnp.where(gs == m, n_iota, big), axis=0, keepdims=True)
        hit = n_iota == first
        keep = jnp.logical_or(keep, hit)
        gs = jnp.where(hit, -jnp.inf, gs)
    keep_f = keep.astype(F32)
    expert_keep = jnp.concatenate(
        [jnp.broadcast_to(keep_f[g:g + 1, :], (GROUP_SIZE, tm)) for g in range(N_GROUPS)], axis=0)
    masked = jnp.where(expert_keep > 0.5, biased, -jnp.inf)

    idx_rows, w_rows, hits = [], [], []
    sel = jnp.zeros((N_EXPERTS, tm), F32)
    for _ in range(TOP_K):
        m = jnp.max(masked, axis=0, keepdims=True)
        first = jnp.min(jnp.where(masked == m, e_iota, big), axis=0, keepdims=True)
        hit = e_iota == first
        idx_rows.append(first)
        w_rows.append(jnp.sum(jnp.where(hit, scores, 0.0), axis=0, keepdims=True))
        hits.append(hit)
        sel = sel + hit.astype(F32)
        masked = jnp.where(hit, -jnp.inf, masked)
    w = jnp.concatenate(w_rows, axis=0)
    w = w / jnp.sum(w, axis=0, keepdims=True) * ROUTED_SCALE
    idx_ref[...] = jnp.concatenate(idx_rows, axis=0)
    w_ref[...] = w

    r = lax.broadcasted_iota(jnp.int32, (tm, tm), 0)
    c = lax.broadcasted_iota(jnp.int32, (tm, tm), 1)
    earlier = (r < c).astype(BF16)
    prefix = jnp.dot(sel.astype(BF16), earlier, preferred_element_type=F32) + run_ref[...]
    rank_rows = [jnp.sum(jnp.where(h, prefix, 0.0), axis=0, keepdims=True) for h in hits]
    rank_ref[...] = jnp.concatenate(rank_rows, axis=0).astype(jnp.int32)
    total = run_ref[...] + jnp.sum(sel, axis=1, keepdims=True)
    run_ref[...] = total
    cnt_ref[...] = jnp.broadcast_to(total, (N_EXPERTS, LANES))


def _route(x1b, wr_t, rb, tm=256):
    n = x1b.shape[0]
    tm = min(tm, n)
    return pl.pallas_call(
        functools.partial(_route_kernel, tm=tm),
        grid=(n // tm,),
        in_specs=[pl.BlockSpec((tm, D_MODEL), lambda i: (i, 0)),
                  pl.BlockSpec((N_EXPERTS, D_MODEL), lambda i: (0, 0)),
                  pl.BlockSpec((N_EXPERTS, 1), lambda i: (0, 0))],
        out_specs=[pl.BlockSpec((TOP_K, tm), lambda i: (0, i)),
                   pl.BlockSpec((TOP_K, tm), lambda i: (0, i)),
                   pl.BlockSpec((TOP_K, tm), lambda i: (0, i)),
                   pl.BlockSpec((N_EXPERTS, LANES), lambda i: (0, 0))],
        out_shape=[jax.ShapeDtypeStruct((TOP_K, n), jnp.int32),
                   jax.ShapeDtypeStruct((TOP_K, n), F32),
                   jax.ShapeDtypeStruct((TOP_K, n), jnp.int32),
                   jax.ShapeDtypeStruct((N_EXPERTS, LANES), F32)],
        scratch_shapes=[pltpu.VMEM((N_EXPERTS, 1), F32)],
        compiler_params=_cp(("arbitrary",)),
        name="router_topk",
    )(x1b, wr_t, rb)


def _sorted_row(starts_ref, idx_ref, rank_ref, j):
    return starts_ref[idx_ref[0, 0, j]] + rank_ref[0, 0, j]


def _dispatch_kernel(starts_ref, idx_ref, rank_ref, x_ref, xs_hbm, sem, *, tt):
    def row_copy(t, k):
        dst = _sorted_row(starts_ref, idx_ref, rank_ref, t * TOP_K + k)
        return pltpu.make_async_copy(x_ref.at[pl.ds(t, 1)], xs_hbm.at[pl.ds(dst, 1)], sem)

    def issue(t, _):
        for k in range(TOP_K):
            row_copy(t, k).start()
        return 0

    lax.fori_loop(0, tt, issue, 0)

    def drain(t, _):
        for k in range(TOP_K):
            row_copy(t, k).wait()
        return 0

    lax.fori_loop(0, tt, drain, 0)


def _dispatch(starts, idx_tok, rank_tok, x1, tt=256):
    n = x1.shape[0]
    tt = min(tt, n)
    smem = lambda: pl.BlockSpec((1, 1, tt * TOP_K), lambda i, s: (i, 0, 0),
                                memory_space=pltpu.SMEM)
    grid_spec = pltpu.PrefetchScalarGridSpec(
        num_scalar_prefetch=1,
        grid=(n // tt,),
        in_specs=[smem(), smem(), pl.BlockSpec((tt, D_MODEL), lambda i, s: (i, 0))],
        out_specs=pl.BlockSpec(memory_space=pl.ANY),
        scratch_shapes=[pltpu.SemaphoreType.DMA(())],
    )
    return pl.pallas_call(
        functools.partial(_dispatch_kernel, tt=tt),
        grid_spec=grid_spec,
        out_shape=jax.ShapeDtypeStruct((n * TOP_K, D_MODEL), F32),
        compiler_params=_cp(("arbitrary",)),
        name="moe_dispatch",
    )(starts, idx_tok.reshape(n // tt, 1, tt * TOP_K), rank_tok.reshape(n // tt, 1, tt * TOP_K), x1)


def _expert_kernel(tile_ref, exp_ref, lo_ref, hi_ref, xs_ref, wg_ref, wu_ref, wd_ref, ys_ref,
                   *, bm):
    w = pl.program_id(0)
    lo = lo_ref[w]
    hi = hi_ref[w]

    @pl.when(hi > lo)
    def _():
        x = xs_ref[...].astype(BF16)
        g = jnp.dot(x, wg_ref[0, 0].astype(BF16), preferred_element_type=F32)
        u = jnp.dot(x, wu_ref[0, 0].astype(BF16), preferred_element_type=F32)
        h = (g * jax.nn.sigmoid(g) * u).astype(BF16)
        y = jnp.dot(h, wd_ref[0, 0].astype(BF16), preferred_element_type=F32)

        @pl.when(lo == 0)
        def _():
            ys_ref[...] = y

        @pl.when(lo > 0)
        def _():
            rows = lax.broadcasted_iota(jnp.int32, (bm, 1), 0)
            ys_ref[...] = jnp.where(rows >= lo, y, ys_ref[...])


def _experts(layer, item_tile, item_exp, item_lo, item_hi, xs, wg, wu, wd, bm):
    r = xs.shape[0]
    n_items = item_tile.shape[0]
    grid_spec = pltpu.PrefetchScalarGridSpec(
        num_scalar_prefetch=4,
        grid=(n_items,),
        in_specs=[pl.BlockSpec((bm, D_MODEL), lambda w, t, e, lo, hi: (t[w], 0)),
                  pl.BlockSpec((1, 1, D_MODEL, EXPERT_FF),
                               lambda w, t, e, lo, hi: (layer, e[w], 0, 0)),
                  pl.BlockSpec((1, 1, D_MODEL, EXPERT_FF),
                               lambda w, t, e, lo, hi: (layer, e[w], 0, 0)),
                  pl.BlockSpec((1, 1, EXPERT_FF, D_MODEL),
                               lambda w, t, e, lo, hi: (layer, e[w], 0, 0))],
        out_specs=pl.BlockSpec((bm, D_MODEL), lambda w, t, e, lo, hi: (t[w], 0)),
    )
    return pl.pallas_call(
        functools.partial(_expert_kernel, bm=bm),
        grid_spec=grid_spec,
        out_shape=jax.ShapeDtypeStruct((r, D_MODEL), F32),
        compiler_params=_cp(("arbitrary",)),
        name="moe_experts",
    )(item_tile, item_exp, item_lo, item_hi, xs, wg, wu, wd)


def _expert_items(counts, n_rows, bm):
    n_tiles = n_rows // bm
    ends = jnp.cumsum(counts)
    starts = ends - counts
    bounds = jnp.sort(jnp.concatenate([jnp.arange(n_tiles, dtype=jnp.int32) * bm,
                                       starts.astype(jnp.int32)]))
    nxt = jnp.concatenate([bounds[1:], jnp.array([n_rows], jnp.int32)])
    tile = jnp.minimum(bounds // bm, n_tiles - 1)
    exp = jnp.minimum(jnp.searchsorted(ends, bounds, side="right"), N_EXPERTS - 1).astype(jnp.int32)
    lo = bounds - tile * bm
    hi = nxt - tile * bm
    return tile.astype(jnp.int32), exp, lo.astype(jnp.int32), hi.astype(jnp.int32), starts


def _combine_kernel(starts_ref, icur_ref, rcur_ref, inext_ref, rnext_ref, ys_hbm, tw_ref, x1_ref,
                    x1b_ref, wsg_ref, wsu_ref, wsd_ref, lg_ref, lb_ref, o_ref, ob_ref, buf, sem,
                    *, tc, n_steps):
    i = pl.program_id(0)
    slot = lax.rem(i, 2)

    def row_copy(idx_ref, rank_ref, s, t, k):
        src = _sorted_row(starts_ref, idx_ref, rank_ref, t * TOP_K + k)
        return pltpu.make_async_copy(ys_hbm.at[pl.ds(src, 1)], buf.at[s, k, pl.ds(t, 1)],
                                     sem.at[s])

    def issue(idx_ref, rank_ref, s):
        def body(t, _):
            for k in range(TOP_K):
                row_copy(idx_ref, rank_ref, s, t, k).start()
            return 0
        lax.fori_loop(0, tc, body, 0)

    @pl.when(i == 0)
    def _():
        issue(icur_ref, rcur_ref, 0)

    @pl.when(i + 1 < n_steps)
    def _():
        issue(inext_ref, rnext_ref, 1 - slot)

    def drain(t, _):
        for k in range(TOP_K):
            row_copy(icur_ref, rcur_ref, slot, t, k).wait()
        return 0

    lax.fori_loop(0, tc, drain, 0)

    tw = tw_ref[...]
    routed = jnp.zeros((tc, D_MODEL), F32)
    for k in range(TOP_K):
        routed = routed + tw[:, k:k + 1] * buf[slot, k]
    xb = x1b_ref[...]
    g = jnp.dot(xb, wsg_ref[...], preferred_element_type=F32)
    u = jnp.dot(xb, wsu_ref[...], preferred_element_type=F32)
    h = (g * jax.nn.sigmoid(g) * u).astype(BF16)
    shared = jnp.dot(h, wsd_ref[...], preferred_element_type=F32)
    y = _layer_norm(DN_ALPHA * x1_ref[...] + (shared + routed), lg_ref[...], lb_ref[...])
    o_ref[...] = y
    ob_ref[...] = y.astype(BF16)


def _combine(starts, idx_tok, rank_tok, ys, tw, x1, x1b, wsg, wsu, wsd, lg, lb, tc=128):
    n = x1.shape[0]
    tc = min(tc, n)
    n_steps = n // tc
    idx3 = idx_tok.reshape(n_steps, 1, tc * TOP_K)
    rank3 = rank_tok.reshape(n_steps, 1, tc * TOP_K)
    full = lambda a: pl.BlockSpec(a.shape, lambda i, s: (0,) * a.ndim)
    rows = lambda w: pl.BlockSpec((tc, w), lambda i, s: (i, 0))
    cur = lambda: pl.BlockSpec((1, 1, tc * TOP_K), lambda i, s: (i, 0, 0),
                               memory_space=pltpu.SMEM)
    nxt = lambda: pl.BlockSpec((1, 1, tc * TOP_K),
                               lambda i, s: (jnp.minimum(i + 1, n_steps - 1), 0, 0),
                               memory_space=pltpu.SMEM)
    grid_spec = pltpu.PrefetchScalarGridSpec(
        num_scalar_prefetch=1,
        grid=(n_steps,),
        in_specs=[cur(), cur(), nxt(), nxt(), pl.BlockSpec(memory_space=pl.ANY),
                  rows(TOP_K), rows(D_MODEL), rows(D_MODEL),
                  full(wsg), full(wsu), full(wsd), full(lg), full(lb)],
        out_specs=[rows(D_MODEL), rows(D_MODEL)],
        scratch_shapes=[pltpu.VMEM((2, TOP_K, tc, D_MODEL), F32),
                        pltpu.SemaphoreType.DMA((2,))],
    )
    return pl.pallas_call(
        functools.partial(_combine_kernel, tc=tc, n_steps=n_steps),
        grid_spec=grid_spec,
        out_shape=[jax.ShapeDtypeStruct((n, D_MODEL), F32),
                   jax.ShapeDtypeStruct((n, D_MODEL), BF16)],
        compiler_params=_cp(("arbitrary",)),
        name="moe_combine_ln",
    )(starts, idx3, rank3, idx3, rank3, ys, tw, x1, x1b, wsg, wsu, wsd, lg, lb)


def _head_cols(w, heads, width, lo, hi):
    return w.reshape(w.shape[0], heads, width)[:, :, lo:hi]


def _prep_layer(w_in, b_gate, b_forget, mla_q_norm, w_uq, mla_kv_norm, w_ukv):
    f = lambda a: a.astype(BF16)
    half = MLA_ROPE_DIM // 2
    d = w_in.shape[0]
    qscale = HEAD_DIM ** -0.5
    w_qkv = jnp.concatenate([
        w_in[:, OFF_SB:OFF_SB + SB_W] * qscale, w_in[:, OFF_SB + SB_W:OFF_FOX],
        w_in[:, OFF_FOX:OFF_FOX + FOX_W] * qscale, w_in[:, OFF_FOX + FOX_W:OFF_FGATE]], axis=1)
    w_kr = w_in[:, OFF_KR:OFF_GATE]
    w_kr_rot = jnp.concatenate([-w_kr[:, half:], w_kr[:, :half]], axis=1)
    z = lambda c: jnp.zeros((d, c), F32)
    pad_rope = LANES - MLA_QK_DIM
    w_small = jnp.concatenate([
        w_in[:, OFF_DQ:OFF_DKV], w_in[:, OFF_DKV:OFF_KR],
        z(MLA_NOPE_DIM), w_kr, z(pad_rope),
        z(MLA_NOPE_DIM), w_kr_rot, z(pad_rope),
        w_in[:, OFF_FGATE:OFF_DQ], z(LANES - FOX_HEADS)], axis=1)
    w_gate = w_in[:, OFF_GATE:]

    r = w_uq.shape[0]
    q_nope = _head_cols(w_uq, MLA_HEADS, MLA_QK_DIM, 0, MLA_NOPE_DIM)
    q_rope = _head_cols(w_uq, MLA_HEADS, MLA_QK_DIM, MLA_NOPE_DIM, MLA_QK_DIM)
    q_rope_rot = jnp.concatenate([-q_rope[:, :, half:], q_rope[:, :, :half]], axis=2)
    zq = lambda c: jnp.zeros((r, MLA_HEADS, c), F32)
    wqa = jnp.concatenate([q_nope, q_rope, zq(pad_rope)], axis=2).reshape(r, MLA_HEADS * LANES)
    wqb = jnp.concatenate([zq(MLA_NOPE_DIM), q_rope_rot, zq(pad_rope)], axis=2).reshape(
        r, MLA_HEADS * LANES)
    rk = w_ukv.shape[0]
    kvw = MLA_NOPE_DIM + MLA_V_DIM
    k_nope = _head_cols(w_ukv, MLA_HEADS, kvw, 0, MLA_NOPE_DIM)
    wk = jnp.concatenate([k_nope, jnp.zeros((rk, MLA_HEADS, LANES - MLA_NOPE_DIM), F32)],
                         axis=2).reshape(rk, MLA_HEADS * LANES)
    wv = _head_cols(w_ukv, MLA_HEADS, kvw, MLA_NOPE_DIM, kvw).reshape(rk, MLA_HEADS * MLA_V_DIM)
    bf = jnp.concatenate([b_forget, jnp.zeros((LANES - FOX_HEADS,), F32)]).reshape(1, LANES)
    return dict(w_qkv=f(w_qkv), w_small=f(w_small), w_gate=f(w_gate),
                b_gate=b_gate.reshape(1, -1), wqa=f(wqa), wqb=f(wqb), wk=f(wk), wv=f(wv),
                qn=mla_q_norm.reshape(1, -1), kvn=mla_kv_norm.reshape(1, -1), bf=bf)


def _rope_tables(seq):
    half = MLA_ROPE_DIM // 2
    inv_freq = jnp.power(ROPE_BASE, -jnp.arange(half, dtype=F32) / half)
    ang = jnp.arange(seq).astype(F32)[:, None] * inv_freq[None, :]
    cos = jnp.concatenate([jnp.cos(ang), jnp.cos(ang)], axis=1)
    sin = jnp.concatenate([jnp.sin(ang), jnp.sin(ang)], axis=1)
    pad = lambda t: jnp.concatenate([jnp.zeros((seq, MLA_NOPE_DIM), F32), t,
                                     jnp.zeros((seq, LANES - MLA_QK_DIM), F32)], axis=1)
    return pad(cos), pad(sin)


EXPERT_BM = 256


def kernel(x, ln1_g, ln1_b, ln2_g, ln2_b, w_in, b_gate, b_forget, mla_q_norm, w_uq, mla_kv_norm,
           w_ukv, w_proj_sb, w_proj_fox, w_proj_mla, w_out, w_router, router_bias,
           w_exp_gate, w_exp_up, w_exp_down, w_sh_gate, w_sh_up, w_sh_down):
    b, seq, d = x.shape
    n = b * seq
    depth = w_in.shape[0]
    cos_t, sin_t = _rope_tables(seq)
    xf = x.reshape(n, d)
    xb = xf.astype(BF16)
    f = lambda a: a.astype(BF16)
    bm = min(EXPERT_BM, n * TOP_K)
    for l in range(depth):
        p = _prep_layer(w_in[l], b_gate[l], b_forget[l], mla_q_norm[l], w_uq[l], mla_kv_norm[l],
                        w_ukv[l])
        qkv = _matmul(xb, p["w_qkv"], None, BF16)
        gates = _matmul(xb, p["w_gate"], p["b_gate"], BF16, act="sigmoid")
        small = _matmul(xb, p["w_small"], None, F32, tn=SMALL_W)
        q_mla, k_mla, v_mla, lf = _mla_prep(small, p["qn"], p["kvn"], p["wqa"], p["wqb"], p["wk"],
                                            p["wv"], cos_t, sin_t, p["bf"], seq)
        cum, cumt = _cumsum(lf.reshape(b, seq, LANES))
        cumt4 = cumt[:, :FOX_HEADS, :].reshape(b, FOX_HEADS, 1, seq)
        qkv3 = qkv.reshape(b, seq, QKV_W)
        o_sb = _sb_attention(qkv3).reshape(n, SB_W)
        o_fox = _fox_attention(qkv3, cum, cumt4).reshape(n, FOX_W)
        o_mla = _mla_attention(q_mla.reshape(b, seq, -1), k_mla.reshape(b, seq, -1),
                               v_mla.reshape(b, seq, -1)).reshape(n, -1)
        x1, x1b = _merge(o_sb, o_fox, o_mla, gates, xf, f(w_proj_sb[l]), f(w_proj_fox[l]),
                              f(w_proj_mla[l]), f(w_out[l]), ln1_g[l].reshape(1, d),
                              ln1_b[l].reshape(1, d))

        idx_t, tw_t, rank_t, cnt = _route(x1b, f(w_router[l].T), router_bias[l].reshape(-1, 1))
        counts = cnt[:, 0].astype(jnp.int32)
        item_tile, item_exp, item_lo, item_hi, starts = _expert_items(counts, n * TOP_K, bm)
        starts = starts.astype(jnp.int32)
        idx_tok = idx_t.T.reshape(-1)
        rank_tok = rank_t.T.reshape(-1)
        xs = _dispatch(starts, idx_tok, rank_tok, x1)
        ys = _experts(l, item_tile, item_exp, item_lo, item_hi, xs, w_exp_gate, w_exp_up,
                      w_exp_down, bm)
        xf, xb = _combine(starts, idx_tok, rank_tok, ys, tw_t.T, x1, x1b, f(w_sh_gate[l]),
                          f(w_sh_up[l]), f(w_sh_down[l]), ln2_g[l].reshape(1, d),
                          ln2_b[l].reshape(1, d))
    return xf.reshape(b, seq, d)
```

```python
import functools

import jax
import jax.numpy as jnp
from jax import lax
from jax.experimental import pallas as pl
from jax.experimental.pallas import tpu as pltpu

F32 = jnp.float32
BF16 = jnp.bfloat16

D_MODEL = 1024
HEAD_DIM = 64
SB_HEADS = 4
FOX_HEADS = 4
MLA_HEADS = 8
MLA_Q_RANK = 256
MLA_KV_RANK = 128
MLA_NOPE_DIM = 64
MLA_ROPE_DIM = 32
MLA_V_DIM = 64
ROPE_BASE = 10000.0
N_BRANCHES = 3
N_EXPERTS = 256
TOP_K = 8
N_GROUPS = 8
TOPK_GROUPS = 4
GROUP_SIZE = N_EXPERTS // N_GROUPS
EXPERT_FF = 256
SHARED_FF = 256
ROUTED_SCALE = 2.5
CHUNK = 64
LN_EPS = 1e-5
RMS_EPS = 1e-6
DEPTH = 2
DN_ALPHA = (2 * DEPTH) ** 0.25

SB_W = SB_HEADS * HEAD_DIM
FOX_W = FOX_HEADS * HEAD_DIM
MLA_QK_DIM = MLA_NOPE_DIM + MLA_ROPE_DIM
OFF_SB = 0
OFF_FOX = OFF_SB + 3 * SB_W
OFF_FGATE = OFF_FOX + 3 * FOX_W
OFF_DQ = OFF_FGATE + FOX_HEADS
OFF_DKV = OFF_DQ + MLA_Q_RANK
OFF_KR = OFF_DKV + MLA_KV_RANK
OFF_GATE = OFF_KR + MLA_ROPE_DIM

LANES = 128
SUBLANES = 8
QKV_W = 3 * SB_W + 3 * FOX_W
SMALL_W = MLA_Q_RANK + MLA_KV_RANK + 3 * LANES
SB_CUTOFF = -104.0

VMEM_LIMIT = 48 * 1024 * 1024


def _cp(sem, vmem=VMEM_LIMIT):
    return pltpu.CompilerParams(dimension_semantics=sem, vmem_limit_bytes=vmem)


def _mm_kernel(x_ref, w_ref, b_ref, o_ref, *, act):
    acc = jnp.dot(x_ref[...], w_ref[...], preferred_element_type=F32)
    if act == "sigmoid":
        acc = jax.nn.sigmoid(acc + b_ref[...])
    o_ref[...] = acc.astype(o_ref.dtype)


def _matmul(x, w, bias, out_dtype, act=None, tm=1024, tn=512):
    n, k = x.shape
    c = w.shape[1]
    tm = min(tm, n)
    tn = min(tn, c)
    if bias is None:
        bias = jnp.zeros((1, c), F32)
    return pl.pallas_call(
        functools.partial(_mm_kernel, act=act),
        grid=(n // tm, c // tn),
        in_specs=[pl.BlockSpec((tm, k), lambda i, j: (i, 0)),
                  pl.BlockSpec((k, tn), lambda i, j: (0, j)),
                  pl.BlockSpec((1, tn), lambda i, j: (0, j))],
        out_specs=pl.BlockSpec((tm, tn), lambda i, j: (i, j)),
        out_shape=jax.ShapeDtypeStruct((n, c), out_dtype),
        compiler_params=_cp(("arbitrary", "arbitrary")),
        name="proj_matmul",
    )(x, w, bias)


def _split_bf16(x, parts):
    out = []
    for _ in range(parts):
        h = x.astype(BF16)
        out.append(h)
        x = x - h.astype(F32)
    return out


def _mla_prep_kernel(sm_ref, qn_ref, kvn_ref, wqa_ref, wqb_ref, wk_ref, wv_ref,
                     cos_ref, sin_ref, bf_ref, q_ref, k_ref, v_ref, lf_ref):
    sm = sm_ref[...]
    dq = sm[:, :MLA_Q_RANK]
    dkv = sm[:, MLA_Q_RANK:MLA_Q_RANK + MLA_KV_RANK]
    o = MLA_Q_RANK + MLA_KV_RANK
    kr = sm[:, o:o + LANES]
    kr_rot = sm[:, o + LANES:o + 2 * LANES]
    fg = sm[:, o + 2 * LANES:o + 3 * LANES]

    cq = dq * lax.rsqrt(jnp.mean(dq * dq, axis=-1, keepdims=True) + RMS_EPS) * qn_ref[...]
    ckv = dkv * lax.rsqrt(jnp.mean(dkv * dkv, axis=-1, keepdims=True) + RMS_EPS) * kvn_ref[...]
    cq = cq.astype(BF16)
    ckv = ckv.astype(BF16)

    cosk = cos_ref[...]
    sink = sin_ref[...]
    lane = lax.broadcasted_iota(jnp.int32, (1, LANES), 1)
    nope = (lane < MLA_NOPE_DIM).astype(F32)
    scale = MLA_QK_DIM ** -0.5
    cq_tab = jnp.concatenate([(cosk + nope) * scale] * MLA_HEADS, axis=1)
    sq_tab = jnp.concatenate([sink * scale] * MLA_HEADS, axis=1)

    qa = jnp.dot(cq, wqa_ref[...], preferred_element_type=F32)
    qb = jnp.dot(cq, wqb_ref[...], preferred_element_type=F32)
    q_ref[...] = (qa * cq_tab + qb * sq_tab).astype(q_ref.dtype)

    k_rope = kr * cosk + kr_rot * sink
    ka = jnp.dot(ckv, wk_ref[...], preferred_element_type=F32)
    k_ref[...] = (ka + jnp.concatenate([k_rope] * MLA_HEADS, axis=1)).astype(k_ref.dtype)
    v_ref[...] = jnp.dot(ckv, wv_ref[...], preferred_element_type=F32).astype(v_ref.dtype)
    lf_ref[...] = jax.nn.log_sigmoid(fg + bf_ref[...])


def _mla_prep(small, qn, kvn, wqa, wqb, wk, wv, cos_t, sin_t, bf, seq, tm=512):
    n = small.shape[0]
    tm = min(tm, seq)
    sblocks = seq // tm
    hw = MLA_HEADS * LANES
    full = lambda a: pl.BlockSpec(a.shape, lambda i: (0,) * a.ndim)
    return pl.pallas_call(
        _mla_prep_kernel,
        grid=(n // tm,),
        in_specs=[pl.BlockSpec((tm, SMALL_W), lambda i: (i, 0)),
                  full(qn), full(kvn), full(wqa), full(wqb), full(wk), full(wv),
                  pl.BlockSpec((tm, LANES), lambda i: (i % sblocks, 0)),
                  pl.BlockSpec((tm, LANES), lambda i: (i % sblocks, 0)),
                  full(bf)],
        out_specs=[pl.BlockSpec((tm, hw), lambda i: (i, 0)),
                   pl.BlockSpec((tm, hw), lambda i: (i, 0)),
                   pl.BlockSpec((tm, MLA_HEADS * MLA_V_DIM), lambda i: (i, 0)),
                   pl.BlockSpec((tm, LANES), lambda i: (i, 0))],
        out_shape=[jax.ShapeDtypeStruct((n, hw), BF16),
                   jax.ShapeDtypeStruct((n, hw), BF16),
                   jax.ShapeDtypeStruct((n, MLA_HEADS * MLA_V_DIM), BF16),
                   jax.ShapeDtypeStruct((n, LANES), F32)],
        compiler_params=_cp(("arbitrary",)),
        name="mla_prep",
    )(small, qn, kvn, wqa, wqb, wk, wv, cos_t, sin_t, bf)


def _cumsum_kernel(lf_ref, cum_ref, cumt_ref, *, seq):
    r = lax.broadcasted_iota(jnp.int32, (LANES, LANES), 0)
    c = lax.broadcasted_iota(jnp.int32, (LANES, LANES), 1)
    lower = (c <= r).astype(BF16)

    def body(j, carry):
        start = pl.multiple_of(j * LANES, LANES)
        blk = lf_ref[0, pl.ds(start, LANES), :]
        acc = carry
        for part in _split_bf16(blk, 3):
            acc = acc + jnp.dot(lower, part, preferred_element_type=F32)
        cum_ref[0, pl.ds(start, LANES), :] = acc
        cumt_ref[0, :, pl.ds(start, LANES)] = acc.T[:SUBLANES, :]
        return jnp.broadcast_to(acc[LANES - 1:LANES, :], (LANES, LANES))

    lax.fori_loop(0, seq // LANES, body, jnp.zeros((LANES, LANES), F32))


def _cumsum(lf3):
    b, seq, _ = lf3.shape
    return pl.pallas_call(
        functools.partial(_cumsum_kernel, seq=seq),
        grid=(b,),
        in_specs=[pl.BlockSpec((1, seq, LANES), lambda i: (i, 0, 0))],
        out_specs=[pl.BlockSpec((1, seq, LANES), lambda i: (i, 0, 0)),
                   pl.BlockSpec((1, SUBLANES, seq), lambda i: (i, 0, 0))],
        out_shape=[jax.ShapeDtypeStruct((b, seq, LANES), F32),
                   jax.ShapeDtypeStruct((b, SUBLANES, seq), F32)],
        compiler_params=_cp(("arbitrary",)),
        name="forget_cumsum",
    )(lf3)


_NT = (((1,), (1,)), ((), ()))
NEG_INIT = -1e30
_TN = (((0,), (0,)), ((), ()))
ATTN_TK = 256


def _head_halves(q):
    lane = lax.broadcasted_iota(jnp.int32, (1, LANES), 1)
    zero = jnp.zeros_like(q)
    return [jnp.where(lane < HEAD_DIM, q, zero), jnp.where(lane >= HEAD_DIM, q, zero)]


def _mask_first_block(x, keep, fill, tk):
    head = jnp.where(keep, x[:, :tk], fill)
    return head if x.shape[1] == tk else jnp.concatenate([head, x[:, tk:]], axis=1)


def _fox_kernel(q_ref, k_ref, v_ref, cc_ref, cr_ref, o_ref, *, seq, tk):
    hp = pl.program_id(1)
    lane = lax.broadcasted_iota(jnp.int32, (1, LANES), 1)
    qs = _head_halves(q_ref[0])
    key = lax.broadcasted_iota(jnp.int32, (tk, tk), 0)
    qry = lax.broadcasted_iota(jnp.int32, (tk, tk), 1)
    causal = key <= qry
    carry = [(jnp.full((1, seq), NEG_INIT, F32), jnp.zeros((1, seq), F32),
              jnp.zeros((LANES, seq), F32)) for _ in range(2)]
    for j in range(seq // tk):
        q0 = j * tk
        k = k_ref[0, q0:q0 + tk, :]
        v = v_ref[0, q0:q0 + tk, :]
        cc = cc_ref[0, q0:q0 + tk, :]
        for hh in range(2):
            m, l, acc = carry[hh]
            cs = jnp.sum(jnp.where(lane == 2 * hp + hh, cc, 0.0), axis=1, keepdims=True)
            ct = cr_ref[0, hh, :, q0:]
            st = lax.dot_general(k, qs[hh][q0:, :], _NT, preferred_element_type=F32)
            st = _mask_first_block((st + ct) - cs, causal, -jnp.inf, tk)
            mn, ln, an = _softmax_step_t(st, m[:, q0:], l[:, q0:], acc[:, q0:], v)
            if q0:
                mn = jnp.concatenate([m[:, :q0], mn], axis=1)
                ln = jnp.concatenate([l[:, :q0], ln], axis=1)
                an = jnp.concatenate([acc[:, :q0], an], axis=1)
            carry[hh] = (mn, ln, an)
    _pair_out_t([carry[0][2], carry[1][2]], [carry[0][1], carry[1][1]], o_ref)


def _fox_attention(qkv3, cum, cumt4):
    b, seq, _ = qkv3.shape
    tk = min(ATTN_TK, seq)
    hp = FOX_HEADS // 2
    qoff = OFF_FOX // LANES
    return pl.pallas_call(
        functools.partial(_fox_kernel, seq=seq, tk=tk),
        grid=(b, hp),
        in_specs=[pl.BlockSpec((1, seq, LANES), lambda bi, h: (bi, 0, qoff + h)),
                  pl.BlockSpec((1, seq, LANES), lambda bi, h: (bi, 0, qoff + hp + h)),
                  pl.BlockSpec((1, seq, LANES), lambda bi, h: (bi, 0, qoff + 2 * hp + h)),
                  pl.BlockSpec((1, seq, LANES), lambda bi, h: (bi, 0, 0)),
                  pl.BlockSpec((1, 2, 1, seq), lambda bi, h: (bi, h, 0, 0))],
        out_specs=pl.BlockSpec((1, seq, LANES), lambda bi, h: (bi, 0, h)),
        out_shape=jax.ShapeDtypeStruct((b, seq, FOX_W), BF16),
        compiler_params=_cp(("arbitrary", "arbitrary")),
        name="fox_attention",
    )(qkv3, qkv3, qkv3, cum, cumt4)


def _softmax_step_t(st, m, l, acc, v):
    m_new = jnp.maximum(m, jnp.max(st, axis=0, keepdims=True))
    alpha = jnp.exp(m - m_new)
    p = jnp.exp(st - m_new)
    l = alpha * l + jnp.sum(p, axis=0, keepdims=True)
    acc = alpha * acc + lax.dot_general(v, p.astype(BF16), _TN, preferred_element_type=F32)
    return m_new, l, acc


def _pair_out_t(accs, ls, o_ref):
    sub = lax.broadcasted_iota(jnp.int32, (LANES, 1), 0)
    ot = jnp.where(sub < HEAD_DIM, accs[0] / ls[0], accs[1] / ls[1])
    o_ref[0] = ot.T.astype(o_ref.dtype)


def _mla_kernel(q_ref, k_ref, v_ref, o_ref, *, seq, tk):
    q = q_ref[0]
    qs = [q[:, :LANES], q[:, LANES:]]
    key = lax.broadcasted_iota(jnp.int32, (tk, tk), 0)
    qry = lax.broadcasted_iota(jnp.int32, (tk, tk), 1)
    shift = CHUNK.bit_length() - 1
    chunk_causal = (key >> shift) <= (qry >> shift)
    carry = [(jnp.full((1, seq), NEG_INIT, F32), jnp.zeros((1, seq), F32),
              jnp.zeros((LANES, seq), F32)) for _ in range(2)]
    for j in range(seq // tk):
        q0 = j * tk
        k = k_ref[0, q0:q0 + tk, :]
        v = v_ref[0, q0:q0 + tk, :]
        for hh in range(2):
            m, l, acc = carry[hh]
            st = lax.dot_general(k[:, hh * LANES:(hh + 1) * LANES], qs[hh][q0:, :], _NT,
                                 preferred_element_type=F32)
            st = _mask_first_block(st, chunk_causal, -jnp.inf, tk)
            mn, ln, an = _softmax_step_t(st, m[:, q0:], l[:, q0:], acc[:, q0:], v)
            if q0:
                mn = jnp.concatenate([m[:, :q0], mn], axis=1)
                ln = jnp.concatenate([l[:, :q0], ln], axis=1)
                an = jnp.concatenate([acc[:, :q0], an], axis=1)
            carry[hh] = (mn, ln, an)
    _pair_out_t([carry[0][2], carry[1][2]], [carry[0][1], carry[1][1]], o_ref)


def _mla_attention(q3, k3, v3):
    b, seq, _ = q3.shape
    tk = min(ATTN_TK, seq)
    hp = MLA_HEADS // 2
    return pl.pallas_call(
        functools.partial(_mla_kernel, seq=seq, tk=tk),
        grid=(b, hp),
        in_specs=[pl.BlockSpec((1, seq, 2 * LANES), lambda bi, h: (bi, 0, h)),
                  pl.BlockSpec((1, seq, 2 * LANES), lambda bi, h: (bi, 0, h)),
                  pl.BlockSpec((1, seq, LANES), lambda bi, h: (bi, 0, h))],
        out_specs=pl.BlockSpec((1, seq, LANES), lambda bi, h: (bi, 0, h)),
        out_shape=jax.ShapeDtypeStruct((b, seq, MLA_HEADS * MLA_V_DIM), BF16),
        compiler_params=_cp(("arbitrary", "arbitrary")),
        name="mla_attention",
    )(q3, k3, v3)


def _softplus(z):
    return jnp.maximum(z, 0.0) + jnp.log(1.0 + jnp.exp(-jnp.abs(z)))


def _sb_kernel(q_ref, k_ref, v_ref, o_ref, run_ref, acc_ref, *, seq, tk):
    n = seq // tk
    qs = _head_halves(q_ref[0])
    key = lax.broadcasted_iota(jnp.int32, (tk, tk), 0)
    qry = lax.broadcasted_iota(jnp.int32, (tk, tk), 1)
    strict = key < qry
    later = (qry > key).astype(BF16)

    def unit(j, qlo, qhi, diag):
        k = k_ref[0, j * tk:(j + 1) * tk, :]
        v = v_ref[0, j * tk:(j + 1) * tk, :]
        for hh in range(2):
            z = lax.dot_general(k, qs[hh][qlo:qhi, :], _NT, preferred_element_type=F32)
            sp = _softplus(z)
            log_keep = -sp
            if diag:
                log_keep = _mask_first_block(log_keep, strict, 0.0, tk)
            between = run_ref[hh, :, qlo:qhi]
            for part in _split_bf16(log_keep, 2):
                between = between + jnp.dot(later, part, preferred_element_type=F32)
            w = jnp.exp((z - sp) + between)
            if diag:
                w = _mask_first_block(w, strict, 0.0, tk)
            acc_ref[hh, :, qlo:qhi] += lax.dot_general(v, w.astype(BF16), _TN,
                                                       preferred_element_type=F32)
            run_ref[hh, :, qlo:qhi] += jnp.sum(log_keep, axis=0, keepdims=True)

    run_ref[...] = jnp.zeros_like(run_ref)
    acc_ref[...] = jnp.zeros_like(acc_ref)
    for j in range(n - 1, -1, -1):
        unit(j, j * tk, min((j + 2) * tk, seq), True)

    for j in range(n - 3, -1, -1):
        qlo = (j + 2) * tk

        @pl.when(jnp.max(run_ref[:, :, qlo:]) >= SB_CUTOFF)
        def _():
            unit(j, qlo, seq, False)

    sub = lax.broadcasted_iota(jnp.int32, (LANES, 1), 0)
    o_ref[0] = jnp.where(sub < HEAD_DIM, acc_ref[0], acc_ref[1]).T.astype(o_ref.dtype)


def _sb_attention(qkv3):
    b, seq, _ = qkv3.shape
    tk = min(ATTN_TK, seq)
    hp = SB_HEADS // 2
    qoff = OFF_SB // LANES
    return pl.pallas_call(
        functools.partial(_sb_kernel, seq=seq, tk=tk),
        grid=(b, hp),
        in_specs=[pl.BlockSpec((1, seq, LANES), lambda bi, h: (bi, 0, qoff + h)),
                  pl.BlockSpec((1, seq, LANES), lambda bi, h: (bi, 0, qoff + hp + h)),
                  pl.BlockSpec((1, seq, LANES), lambda bi, h: (bi, 0, qoff + 2 * hp + h))],
        out_specs=pl.BlockSpec((1, seq, LANES), lambda bi, h: (bi, 0, h)),
        out_shape=jax.ShapeDtypeStruct((b, seq, SB_W), BF16),
        scratch_shapes=[pltpu.VMEM((2, 1, seq), F32), pltpu.VMEM((2, LANES, seq), F32)],
        compiler_params=_cp(("arbitrary", "arbitrary")),
        name="sb_attention",
    )(qkv3, qkv3, qkv3)


def _layer_norm(y, g, b):
    yc = y - jnp.mean(y, axis=-1, keepdims=True)
    var = jnp.mean(yc * yc, axis=-1, keepdims=True)
    return yc * lax.rsqrt(var + LN_EPS) * g + b


def _merge_kernel(osb_ref, ofox_ref, omla_ref, g_ref, x_ref, wsb_ref, wfox_ref, wmla_ref,
                  wout_ref, lg_ref, lb_ref, x1_ref, x1b_ref):
    g = g_ref[...].astype(F32)
    a = jnp.dot(osb_ref[...], wsb_ref[...], preferred_element_type=F32)
    b = jnp.dot(ofox_ref[...], wfox_ref[...], preferred_element_type=F32)
    c = jnp.dot(omla_ref[...], wmla_ref[...], preferred_element_type=F32)
    mixed = (g[:, :D_MODEL] * a + g[:, D_MODEL:2 * D_MODEL] * b + g[:, 2 * D_MODEL:] * c)
    mix = jnp.dot(mixed.astype(BF16), wout_ref[...], preferred_element_type=F32)
    y = _layer_norm(DN_ALPHA * x_ref[...] + mix, lg_ref[...], lb_ref[...])
    x1_ref[...] = y
    x1b_ref[...] = y.astype(BF16)


def _merge(o_sb, o_fox, o_mla, gates, x, wsb, wfox, wmla, wout, lg, lb, tm=512):
    n = x.shape[0]
    tm = min(tm, n)
    full = lambda a: pl.BlockSpec(a.shape, lambda i: (0,) * a.ndim)
    rows = lambda w: pl.BlockSpec((tm, w), lambda i: (i, 0))
    return pl.pallas_call(
        _merge_kernel,
        grid=(n // tm,),
        in_specs=[rows(SB_W), rows(FOX_W), rows(MLA_HEADS * MLA_V_DIM), rows(N_BRANCHES * D_MODEL),
                  rows(D_MODEL), full(wsb), full(wfox), full(wmla), full(wout), full(lg), full(lb)],
        out_specs=[rows(D_MODEL), rows(D_MODEL)],
        out_shape=[jax.ShapeDtypeStruct((n, D_MODEL), F32),
                   jax.ShapeDtypeStruct((n, D_MODEL), BF16)],
        compiler_params=_cp(("arbitrary",)),
        name="merge_outproj_ln",
    )(o_sb, o_fox, o_mla, gates, x, wsb, wfox, wmla, wout, lg, lb)


def _route_kernel(x_ref, wr_ref, rb_ref, idx_ref, w_ref, rank_ref, cnt_ref, run_ref, *, tm):
    step = pl.program_id(0)

    @pl.when(step == 0)
    def _():
        run_ref[...] = jnp.zeros_like(run_ref)

    logits = lax.dot_general(wr_ref[...], x_ref[...], _NT, preferred_element_type=F32)
    scores = jax.nn.sigmoid(logits)
    biased = scores + rb_ref[...]
    e_iota = lax.broadcasted_iota(jnp.int32, (N_EXPERTS, tm), 0)
    big = jnp.int32(1 << 20)

    g_iota = lax.broadcasted_iota(jnp.int32, (GROUP_SIZE, tm), 0)
    gs_rows = []
    for g in range(N_GROUPS):
        blk = biased[g * GROUP_SIZE:(g + 1) * GROUP_SIZE, :]
        m1 = jnp.max(blk, axis=0, keepdims=True)
        first = jnp.min(jnp.where(blk == m1, g_iota, big), axis=0, keepdims=True)
        m2 = jnp.max(jnp.where(g_iota == first, -jnp.inf, blk), axis=0, keepdims=True)
        gs_rows.append(m1 + m2)
    gs = jnp.concatenate(gs_rows, axis=0)
    n_iota = lax.broadcasted_iota(jnp.int32, (N_GROUPS, tm), 0)
    keep = jnp.zeros((N_GROUPS, tm), jnp.bool_)
    for _ in range(TOPK_GROUPS):
        m = jnp.max(gs, axis=0, keepdims=True)
        first = jnp.min(jnp.where(gs == m, n_iota, big), axis=0, keepdims=True)
        hit = n_iota == first
        keep = jnp.logical_or(keep, hit)
        gs = jnp.where(hit, -jnp.inf, gs)
    keep_f = keep.astype(F32)
    expert_keep = jnp.concatenate(
        [jnp.broadcast_to(keep_f[g:g + 1, :], (GROUP_SIZE, tm)) for g in range(N_GROUPS)], axis=0)
    masked = jnp.where(expert_keep > 0.5, biased, -jnp.inf)

    idx_rows, w_rows, hits = [], [], []
    sel = jnp.zeros((N_EXPERTS, tm), F32)
    for _ in range(TOP_K):
        m = jnp.max(masked, axis=0, keepdims=True)
        first = jnp.min(jnp.where(masked == m, e_iota, big), axis=0, keepdims=True)
        hit = e_iota == first
        idx_rows.append(first)
        w_rows.append(jnp.sum(jnp.where(hit, scores, 0.0), axis=0, keepdims=True))
        hits.append(hit)
        sel = sel + hit.astype(F32)
        masked = jnp.where(hit, -jnp.inf, masked)
    w = jnp.concatenate(w_rows, axis=0)
    w = w / jnp.sum(w, axis=0, keepdims=True) * ROUTED_SCALE
    idx_ref[...] = jnp.concatenate(idx_rows, axis=0)
    w_ref[...] = w

    r = lax.broadcasted_iota(jnp.int32, (tm, tm), 0)
    c = lax.broadcasted_iota(jnp.int32, (tm, tm), 1)
    earlier = (r < c).astype(BF16)
    prefix = jnp.dot(sel.astype(BF16), earlier, preferred_element_type=F32) + run_ref[...]
    rank_rows = [jnp.sum(jnp.where(h, prefix, 0.0), axis=0, keepdims=True) for h in hits]
    rank_ref[...] = jnp.concatenate(rank_rows, axis=0).astype(jnp.int32)
    total = run_ref[...] + jnp.sum(sel, axis=1, keepdims=True)
    run_ref[...] = total
    cnt_ref[...] = jnp.broadcast_to(total, (N_EXPERTS, LANES))


def _route(x1b, wr_t, rb, tm=256):
    n = x1b.shape[0]
    tm = min(tm, n)
    return pl.pallas_call(
        functools.partial(_route_kernel, tm=tm),
        grid=(n // tm,),
        in_specs=[pl.BlockSpec((tm, D_MODEL), lambda i: (i, 0)),
                  pl.BlockSpec((N_EXPERTS, D_MODEL), lambda i: (0, 0)),
                  pl.BlockSpec((N_EXPERTS, 1), lambda i: (0, 0))],
        out_specs=[pl.BlockSpec((TOP_K, tm), lambda i: (0, i)),
                   pl.BlockSpec((TOP_K, tm), lambda i: (0, i)),
                   pl.BlockSpec((TOP_K, tm), lambda i: (0, i)),
                   pl.BlockSpec((N_EXPERTS, LANES), lambda i: (0, 0))],
        out_shape=[jax.ShapeDtypeStruct((TOP_K, n), jnp.int32),
                   jax.ShapeDtypeStruct((TOP_K, n), F32),
                   jax.ShapeDtypeStruct((TOP_K, n), jnp.int32),
                   jax.ShapeDtypeStruct((N_EXPERTS, LANES), F32)],
        scratch_shapes=[pltpu.VMEM((N_EXPERTS, 1), F32)],
        compiler_params=_cp(("arbitrary",)),
        name="router_topk",
    )(x1b, wr_t, rb)


def _sorted_row(starts_ref, idx_ref, rank_ref, j):
    return starts_ref[idx_ref[0, 0, j]] + rank_ref[0, 0, j]


def _dispatch_kernel(starts_ref, idx_ref, rank_ref, x_ref, xs_hbm, sem, *, tt):
    def row_copy(t, k):
        dst = _sorted_row(starts_ref, idx_ref, rank_ref, t * TOP_K + k)
        return pltpu.make_async_copy(x_ref.at[pl.ds(t, 1)], xs_hbm.at[pl.ds(dst, 1)], sem)

    def issue(t, _):
        for k in range(TOP_K):
            row_copy(t, k).start()
        return 0

    lax.fori_loop(0, tt, issue, 0)

    def drain(t, _):
        for k in range(TOP_K):
            row_copy(t, k).wait()
        return 0

    lax.fori_loop(0, tt, drain, 0)


def _dispatch(starts, idx_tok, rank_tok, x1, tt=256):
    n = x1.shape[0]
    tt = min(tt, n)
    smem = lambda: pl.BlockSpec((1, 1, tt * TOP_K), lambda i, s: (i, 0, 0),
                                memory_space=pltpu.SMEM)
    grid_spec = pltpu.PrefetchScalarGridSpec(
        num_scalar_prefetch=1,
        grid=(n // tt,),
        in_specs=[smem(), smem(), pl.BlockSpec((tt, D_MODEL), lambda i, s: (i, 0))],
        out_specs=pl.BlockSpec(memory_space=pl.ANY),
        scratch_shapes=[pltpu.SemaphoreType.DMA(())],
    )
    return pl.pallas_call(
        functools.partial(_dispatch_kernel, tt=tt),
        grid_spec=grid_spec,
        out_shape=jax.ShapeDtypeStruct((n * TOP_K, D_MODEL), F32),
        compiler_params=_cp(("arbitrary",)),
        name="moe_dispatch",
    )(starts, idx_tok.reshape(n // tt, 1, tt * TOP_K), rank_tok.reshape(n // tt, 1, tt * TOP_K), x1)


def _expert_kernel(tile_ref, exp_ref, lo_ref, hi_ref, xs_ref, wg_ref, wu_ref, wd_ref, ys_ref,
                   *, bm):
    w = pl.program_id(0)
    lo = lo_ref[w]
    hi = hi_ref[w]

    @pl.when(hi > lo)
    def _():
        x = xs_ref[...].astype(BF16)
        g = jnp.dot(x, wg_ref[0, 0].astype(BF16), preferred_element_type=F32)
        u = jnp.dot(x, wu_ref[0, 0].astype(BF16), preferred_element_type=F32)
        h = (g * jax.nn.sigmoid(g) * u).astype(BF16)
        y = jnp.dot(h, wd_ref[0, 0].astype(BF16), preferred_element_type=F32)

        @pl.when(lo == 0)
        def _():
            ys_ref[...] = y

        @pl.when(lo > 0)
        def _():
            rows = lax.broadcasted_iota(jnp.int32, (bm, 1), 0)
            ys_ref[...] = jnp.where(rows >= lo, y, ys_ref[...])


def _experts(layer, item_tile, item_exp, item_lo, item_hi, xs, wg, wu, wd, bm):
    r = xs.shape[0]
    n_items = item_tile.shape[0]
    grid_spec = pltpu.PrefetchScalarGridSpec(
        num_scalar_prefetch=4,
        grid=(n_items,),
        in_specs=[pl.BlockSpec((bm, D_MODEL), lambda w, t, e, lo, hi: (t[w], 0)),
                  pl.BlockSpec((1, 1, D_MODEL, EXPERT_FF),
                               lambda w, t, e, lo, hi: (layer, e[w], 0, 0)),
                  pl.BlockSpec((1, 1, D_MODEL, EXPERT_FF),
                               lambda w, t, e, lo, hi: (layer, e[w], 0, 0)),
                  pl.BlockSpec((1, 1, EXPERT_FF, D_MODEL),
                               lambda w, t, e, lo, hi: (layer, e[w], 0, 0))],
        out_specs=pl.BlockSpec((bm, D_MODEL), lambda w, t, e, lo, hi: (t[w], 0)),
    )
    return pl.pallas_call(
        functools.partial(_expert_kernel, bm=bm),
        grid_spec=grid_spec,
        out_shape=jax.ShapeDtypeStruct((r, D_MODEL), F32),
        compiler_params=_cp(("arbitrary",)),
        name="moe_experts",
    )(item_tile, item_exp, item_lo, item_hi, xs, wg, wu, wd)


def _expert_items(counts, n_rows, bm):
    n_tiles = n_rows // bm
    ends = jnp.cumsum(counts)
    starts = ends - counts
    bounds = jnp.sort(jnp.concatenate([jnp.arange(n_tiles, dtype=jnp.int32) * bm,
                                       starts.astype(jnp.int32)]))
    nxt = jnp.concatenate([bounds[1:], jnp.array([n_rows], jnp.int32)])
    tile = jnp.minimum(bounds // bm, n_tiles - 1)
    exp = jnp.minimum(jnp.searchsorted(ends, bounds, side="right"), N_EXPERTS - 1).astype(jnp.int32)
    lo = bounds - tile * bm
    hi = nxt - tile * bm
    return tile.astype(jnp.int32), exp, lo.astype(jnp.int32), hi.astype(jnp.int32), starts


def _combine_kernel(starts_ref, icur_ref, rcur_ref, inext_ref, rnext_ref, ys_hbm, tw_ref, x1_ref,
                    x1b_ref, wsg_ref, wsu_ref, wsd_ref, lg_ref, lb_ref, o_ref, ob_ref, buf, sem,
                    *, tc, n_steps):
    i = pl.program_id(0)
    slot = lax.rem(i, 2)

    def row_copy(idx_ref, rank_ref, s, t, k):
        src = _sorted_row(starts_ref, idx_ref, rank_ref, t * TOP_K + k)
        return pltpu.make_async_copy(ys_hbm.at[pl.ds(src, 1)], buf.at[s, k, pl.ds(t, 1)],
                                     sem.at[s])

    def issue(idx_ref, rank_ref, s):
        def body(t, _):
            for k in range(TOP_K):
                row_copy(idx_ref, rank_ref, s, t, k).start()
            return 0
        lax.fori_loop(0, tc, body, 0)

    @pl.when(i == 0)
    def _():
        issue(icur_ref, rcur_ref, 0)

    @pl.when(i + 1 < n_steps)
    def _():
        issue(inext_ref, rnext_ref, 1 - slot)

    def drain(t, _):
        for k in range(TOP_K):
            row_copy(icur_ref, rcur_ref, slot, t, k).wait()
        return 0

    lax.fori_loop(0, tc, drain, 0)

    tw = tw_ref[...]
    routed = jnp.zeros((tc, D_MODEL), F32)
    for k in range(TOP_K):
        routed = routed + tw[:, k:k + 1] * buf[slot, k]
    xb = x1b_ref[...]
    g = jnp.dot(xb, wsg_ref[...], preferred_element_type=F32)
    u = jnp.dot(xb, wsu_ref[...], preferred_element_type=F32)
    h = (g * jax.nn.sigmoid(g) * u).astype(BF16)
    shared = jnp.dot(h, wsd_ref[...], preferred_element_type=F32)
    y = _layer_norm(DN_ALPHA * x1_ref[...] + (shared + routed), lg_ref[...], lb_ref[...])
    o_ref[...] = y
    ob_ref[...] = y.astype(BF16)


def _combine(starts, idx_tok, rank_tok, ys, tw, x1, x1b, wsg, wsu, wsd, lg, lb, tc=128):
    n = x1.shape[0]
    tc = min(tc, n)
    n_steps = n // tc
    idx3 = idx_tok.reshape(n_steps, 1, tc * TOP_K)
    rank3 = rank_tok.reshape(n_steps, 1, tc * TOP_K)
    full = lambda a: pl.BlockSpec(a.shape, lambda i, s: (0,) * a.ndim)
    rows = lambda w: pl.BlockSpec((tc, w), lambda i, s: (i, 0))
    cur = lambda: pl.BlockSpec((1, 1, tc * TOP_K), lambda i, s: (i, 0, 0),
                               memory_space=pltpu.SMEM)
    nxt = lambda: pl.BlockSpec((1, 1, tc * TOP_K),
                               lambda i, s: (jnp.minimum(i + 1, n_steps - 1), 0, 0),
                               memory_space=pltpu.SMEM)
    grid_spec = pltpu.PrefetchScalarGridSpec(
        num_scalar_prefetch=1,
        grid=(n_steps,),
        in_specs=[cur(), cur(), nxt(), nxt(), pl.BlockSpec(memory_space=pl.ANY),
                  rows(TOP_K), rows(D_MODEL), rows(D_MODEL),
                  full(wsg), full(wsu), full(wsd), full(lg), full(lb)],
        out_specs=[rows(D_MODEL), rows(D_MODEL)],
        scratch_shapes=[pltpu.VMEM((2, TOP_K, tc, D_MODEL), F32),
                        pltpu.SemaphoreType.DMA((2,))],
    )
    return pl.pallas_call(
        functools.partial(_combine_kernel, tc=tc, n_steps=n_steps),
        grid_spec=grid_spec,
        out_shape=[jax.ShapeDtypeStruct((n, D_MODEL), F32),
                   jax.ShapeDtypeStruct((n, D_MODEL), BF16)],
        compiler_params=_cp(("arbitrary",)),
        name="moe_combine_ln",
    )(starts, idx3, rank3, idx3, rank3, ys, tw, x1, x1b, wsg, wsu, wsd, lg, lb)


def _head_cols(w, heads, width, lo, hi):
    return w.reshape(w.shape[0], heads, width)[:, :, lo:hi]


def _prep_layer(w_in, b_gate, b_forget, mla_q_norm, w_uq, mla_kv_norm, w_ukv):
    f = lambda a: a.astype(BF16)
    half = MLA_ROPE_DIM // 2
    d = w_in.shape[0]
    qscale = HEAD_DIM ** -0.5
    w_qkv = jnp.concatenate([
        w_in[:, OFF_SB:OFF_SB + SB_W] * qscale, w_in[:, OFF_SB + SB_W:OFF_FOX],
        w_in[:, OFF_FOX:OFF_FOX + FOX_W] * qscale, w_in[:, OFF_FOX + FOX_W:OFF_FGATE]], axis=1)
    w_kr = w_in[:, OFF_KR:OFF_GATE]
    w_kr_rot = jnp.concatenate([-w_kr[:, half:], w_kr[:, :half]], axis=1)
    z = lambda c: jnp.zeros((d, c), F32)
    pad_rope = LANES - MLA_QK_DIM
    w_small = jnp.concatenate([
        w_in[:, OFF_DQ:OFF_DKV], w_in[:, OFF_DKV:OFF_KR],
        z(MLA_NOPE_DIM), w_kr, z(pad_rope),
        z(MLA_NOPE_DIM), w_kr_rot, z(pad_rope),
        w_in[:, OFF_FGATE:OFF_DQ], z(LANES - FOX_HEADS)], axis=1)
    w_gate = w_in[:, OFF_GATE:]

    r = w_uq.shape[0]
    q_nope = _head_cols(w_uq, MLA_HEADS, MLA_QK_DIM, 0, MLA_NOPE_DIM)
    q_rope = _head_cols(w_uq, MLA_HEADS, MLA_QK_DIM, MLA_NOPE_DIM, MLA_QK_DIM)
    q_rope_rot = jnp.concatenate([-q_rope[:, :, half:], q_rope[:, :, :half]], axis=2)
    zq = lambda c: jnp.zeros((r, MLA_HEADS, c), F32)
    wqa = jnp.concatenate([q_nope, q_rope, zq(pad_rope)], axis=2).reshape(r, MLA_HEADS * LANES)
    wqb = jnp.concatenate([zq(MLA_NOPE_DIM), q_rope_rot, zq(pad_rope)], axis=2).reshape(
        r, MLA_HEADS * LANES)
    rk = w_ukv.shape[0]
    kvw = MLA_NOPE_DIM + MLA_V_DIM
    k_nope = _head_cols(w_ukv, MLA_HEADS, kvw, 0, MLA_NOPE_DIM)
    wk = jnp.concatenate([k_nope, jnp.zeros((rk, MLA_HEADS, LANES - MLA_NOPE_DIM), F32)],
                         axis=2).reshape(rk, MLA_HEADS * LANES)
    wv = _head_cols(w_ukv, MLA_HEADS, kvw, MLA_NOPE_DIM, kvw).reshape(rk, MLA_HEADS * MLA_V_DIM)
    bf = jnp.concatenate([b_forget, jnp.zeros((LANES - FOX_HEADS,), F32)]).reshape(1, LANES)
    return dict(w_qkv=f(w_qkv), w_small=f(w_small), w_gate=f(w_gate),
                b_gate=b_gate.reshape(1, -1), wqa=f(wqa), wqb=f(wqb), wk=f(wk), wv=f(wv),
                qn=mla_q_norm.reshape(1, -1), kvn=mla_kv_norm.reshape(1, -1), bf=bf)


def _rope_tables(seq):
    half = MLA_ROPE_DIM // 2
    inv_freq = jnp.power(ROPE_BASE, -jnp.arange(half, dtype=F32) / half)
    ang = jnp.arange(seq).astype(F32)[:, None] * inv_freq[None, :]
    cos = jnp.concatenate([jnp.cos(ang), jnp.cos(ang)], axis=1)
    sin = jnp.concatenate([jnp.sin(ang), jnp.sin(ang)], axis=1)
    pad = lambda t: jnp.concatenate([jnp.zeros((seq, MLA_NOPE_DIM), F32), t,
                                     jnp.zeros((seq, LANES - MLA_QK_DIM), F32)], axis=1)
    return pad(cos), pad(sin)


EXPERT_BM = 256


def kernel(x, ln1_g, ln1_b, ln2_g, ln2_b, w_in, b_gate, b_forget, mla_q_norm, w_uq, mla_kv_norm,
           w_ukv, w_proj_sb, w_proj_fox, w_proj_mla, w_out, w_router, router_bias,
           w_exp_gate, w_exp_up, w_exp_down, w_sh_gate, w_sh_up, w_sh_down):
    b, seq, d = x.shape
    n = b * seq
    depth = w_in.shape[0]
    cos_t, sin_t = _rope_tables(seq)
    xf = x.reshape(n, d)
    xb = xf.astype(BF16)
    f = lambda a: a.astype(BF16)
    bm = min(EXPERT_BM, n * TOP_K)
    for l in range(depth):
        p = _prep_layer(w_in[l], b_gate[l], b_forget[l], mla_q_norm[l], w_uq[l], mla_kv_norm[l],
                        w_ukv[l])
        qkv = _matmul(xb, p["w_qkv"], None, BF16)
        gates = _matmul(xb, p["w_gate"], p["b_gate"], BF16, act="sigmoid")
        small = _matmul(xb, p["w_small"], None, F32, tn=SMALL_W)
        q_mla, k_mla, v_mla, lf = _mla_prep(small, p["qn"], p["kvn"], p["wqa"], p["wqb"], p["wk"],
                                            p["wv"], cos_t, sin_t, p["bf"], seq)
        cum, cumt = _cumsum(lf.reshape(b, seq, LANES))
        cumt4 = cumt[:, :FOX_HEADS, :].reshape(b, FOX_HEADS, 1, seq)
        qkv3 = qkv.reshape(b, seq, QKV_W)
        o_sb = _sb_attention(qkv3).reshape(n, SB_W)
        o_fox = _fox_attention(qkv3, cum, cumt4).reshape(n, FOX_W)
        o_mla = _mla_attention(q_mla.reshape(b, seq, -1), k_mla.reshape(b, seq, -1),
                               v_mla.reshape(b, seq, -1)).reshape(n, -1)
        x1, x1b = _merge(o_sb, o_fox, o_mla, gates, xf, f(w_proj_sb[l]), f(w_proj_fox[l]),
                              f(w_proj_mla[l]), f(w_out[l]), ln1_g[l].reshape(1, d),
                              ln1_b[l].reshape(1, d))

        idx_t, tw_t, rank_t, cnt = _route(x1b, f(w_router[l].T), router_bias[l].reshape(-1, 1))
        counts = cnt[:, 0].astype(jnp.int32)
        item_tile, item_exp, item_lo, item_hi, starts = _expert_items(counts, n * TOP_K, bm)
        starts = starts.astype(jnp.int32)
        idx_tok = idx_t.T.reshape(-1)
        rank_tok = rank_t.T.reshape(-1)
        xs = _dispatch(starts, idx_tok, rank_tok, x1)
        ys = _experts(l, item_tile, item_exp, item_lo, item_hi, xs, w_exp_gate, w_exp_up,
                      w_exp_down, bm)
        xf, xb = _combine(starts, idx_tok, rank_tok, ys, tw_t.T, x1, x1b, f(w_sh_gate[l]),
                          f(w_sh_up[l]), f(w_sh_down[l]), ln2_g[l].reshape(1, d),
                          ln2_b[l].reshape(1, d))
    return xf.reshape(b, seq, d)
```

```python
import functools

import jax
import jax.numpy as jnp
from jax import lax
from jax.experimental import pallas as pl
from jax.experimental.pallas import tpu as pltpu

F32 = jnp.float32
BF16 = jnp.bfloat16

D_MODEL = 1024
HEAD_DIM = 64
SB_HEADS = 4
FOX_HEADS = 4
MLA_HEADS = 8
MLA_Q_RANK = 256
MLA_KV_RANK = 128
MLA_NOPE_DIM = 64
MLA_ROPE_DIM = 32
MLA_V_DIM = 64
ROPE_BASE = 10000.0
N_BRANCHES = 3
N_EXPERTS = 256
TOP_K = 8
N_GROUPS = 8
TOPK_GROUPS = 4
GROUP_SIZE = N_EXPERTS // N_GROUPS
EXPERT_FF = 256
SHARED_FF = 256
ROUTED_SCALE = 2.5
CHUNK = 64
LN_EPS = 1e-5
RMS_EPS = 1e-6
DEPTH = 2
DN_ALPHA = (2 * DEPTH) ** 0.25

SB_W = SB_HEADS * HEAD_DIM
FOX_W = FOX_HEADS * HEAD_DIM
MLA_QK_DIM = MLA_NOPE_DIM + MLA_ROPE_DIM
OFF_SB = 0
OFF_FOX = OFF_SB + 3 * SB_W
OFF_FGATE = OFF_FOX + 3 * FOX_W
OFF_DQ = OFF_FGATE + FOX_HEADS
OFF_DKV = OFF_DQ + MLA_Q_RANK
OFF_KR = OFF_DKV + MLA_KV_RANK
OFF_GATE = OFF_KR + MLA_ROPE_DIM

LANES = 128
SUBLANES = 8
QKV_W = 3 * SB_W + 3 * FOX_W
SMALL_W = MLA_Q_RANK + MLA_KV_RANK + 3 * LANES
SB_CUTOFF = -104.0

VMEM_LIMIT = 48 * 1024 * 1024


def _cp(sem, vmem=VMEM_LIMIT):
    return pltpu.CompilerParams(dimension_semantics=sem, vmem_limit_bytes=vmem)


def _mm_kernel(x_ref, w_ref, b_ref, o_ref, *, act):
    acc = jnp.dot(x_ref[...], w_ref[...], preferred_element_type=F32)
    if act == "sigmoid":
        acc = jax.nn.sigmoid(acc + b_ref[...])
    o_ref[...] = acc.astype(o_ref.dtype)


def _matmul(x, w, bias, out_dtype, act=None, tm=1024, tn=512):
    n, k = x.shape
    c = w.shape[1]
    tm = min(tm, n)
    tn = min(tn, c)
    if bias is None:
        bias = jnp.zeros((1, c), F32)
    return pl.pallas_call(
        functools.partial(_mm_kernel, act=act),
        grid=(n // tm, c // tn),
        in_specs=[pl.BlockSpec((tm, k), lambda i, j: (i, 0)),
                  pl.BlockSpec((k, tn), lambda i, j: (0, j)),
                  pl.BlockSpec((1, tn), lambda i, j: (0, j))],
        out_specs=pl.BlockSpec((tm, tn), lambda i, j: (i, j)),
        out_shape=jax.ShapeDtypeStruct((n, c), out_dtype),
        compiler_params=_cp(("arbitrary", "arbitrary")),
        name="proj_matmul",
    )(x, w, bias)


def _split_bf16(x, parts):
    out = []
    for _ in range(parts):
        h = x.astype(BF16)
        out.append(h)
        x = x - h.astype(F32)
    return out


def _mla_prep_kernel(sm_ref, qn_ref, kvn_ref, wqa_ref, wqb_ref, wk_ref, wv_ref,
                     cos_ref, sin_ref, bf_ref, q_ref, k_ref, v_ref, lf_ref):
    sm = sm_ref[...]
    dq = sm[:, :MLA_Q_RANK]
    dkv = sm[:, MLA_Q_RANK:MLA_Q_RANK + MLA_KV_RANK]
    o = MLA_Q_RANK + MLA_KV_RANK
    kr = sm[:, o:o + LANES]
    kr_rot = sm[:, o + LANES:o + 2 * LANES]
    fg = sm[:, o + 2 * LANES:o + 3 * LANES]

    cq = dq * lax.rsqrt(jnp.mean(dq * dq, axis=-1, keepdims=True) + RMS_EPS) * qn_ref[...]
    ckv = dkv * lax.rsqrt(jnp.mean(dkv * dkv, axis=-1, keepdims=True) + RMS_EPS) * kvn_ref[...]
    cq = cq.astype(BF16)
    ckv = ckv.astype(BF16)

    cosk = cos_ref[...]
    sink = sin_ref[...]
    lane = lax.broadcasted_iota(jnp.int32, (1, LANES), 1)
    nope = (lane < MLA_NOPE_DIM).astype(F32)
    scale = MLA_QK_DIM ** -0.5
    cq_tab = jnp.concatenate([(cosk + nope) * scale] * MLA_HEADS, axis=1)
    sq_tab = jnp.concatenate([sink * scale] * MLA_HEADS, axis=1)

    qa = jnp.dot(cq, wqa_ref[...], preferred_element_type=F32)
    qb = jnp.dot(cq, wqb_ref[...], preferred_element_type=F32)
    q_ref[...] = (qa * cq_tab + qb * sq_tab).astype(q_ref.dtype)

    k_rope = kr * cosk + kr_rot * sink
    ka = jnp.dot(ckv, wk_ref[...], preferred_element_type=F32)
    k_ref[...] = (ka + jnp.concatenate([k_rope] * MLA_HEADS, axis=1)).astype(k_ref.dtype)
    v_ref[...] = jnp.dot(ckv, wv_ref[...], preferred_element_type=F32).astype(v_ref.dtype)
    lf_ref[...] = jax.nn.log_sigmoid(fg + bf_ref[...])


def _mla_prep(small, qn, kvn, wqa, wqb, wk, wv, cos_t, sin_t, bf, seq, tm=512):
    n = small.shape[0]
    tm = min(tm, seq)
    sblocks = seq // tm
    hw = MLA_HEADS * LANES
    full = lambda a: pl.BlockSpec(a.shape, lambda i: (0,) * a.ndim)
    return pl.pallas_call(
        _mla_prep_kernel,
        grid=(n // tm,),
        in_specs=[pl.BlockSpec((tm, SMALL_W), lambda i: (i, 0)),
                  full(qn), full(kvn), full(wqa), full(wqb), full(wk), full(wv),
                  pl.BlockSpec((tm, LANES), lambda i: (i % sblocks, 0)),
                  pl.BlockSpec((tm, LANES), lambda i: (i % sblocks, 0)),
                  full(bf)],
        out_specs=[pl.BlockSpec((tm, hw), lambda i: (i, 0)),
                   pl.BlockSpec((tm, hw), lambda i: (i, 0)),
                   pl.BlockSpec((tm, MLA_HEADS * MLA_V_DIM), lambda i: (i, 0)),
                   pl.BlockSpec((tm, LANES), lambda i: (i, 0))],
        out_shape=[jax.ShapeDtypeStruct((n, hw), BF16),
                   jax.ShapeDtypeStruct((n, hw), BF16),
                   jax.ShapeDtypeStruct((n, MLA_HEADS * MLA_V_DIM), BF16),
                   jax.ShapeDtypeStruct((n, LANES), F32)],
        compiler_params=_cp(("arbitrary",)),
        name="mla_prep",
    )(small, qn, kvn, wqa, wqb, wk, wv, cos_t, sin_t, bf)


def _cumsum_kernel(lf_ref, cum_ref, cumt_ref, *, seq):
    r = lax.broadcasted_iota(jnp.int32, (LANES, LANES), 0)
    c = lax.broadcasted_iota(jnp.int32, (LANES, LANES), 1)
    lower = (c <= r).astype(BF16)

    def body(j, carry):
        start = pl.multiple_of(j * LANES, LANES)
        blk = lf_ref[0, pl.ds(start, LANES), :]
        acc = carry
        for part in _split_bf16(blk, 3):
            acc = acc + jnp.dot(lower, part, preferred_element_type=F32)
        cum_ref[0, pl.ds(start, LANES), :] = acc
        cumt_ref[0, :, pl.ds(start, LANES)] = acc.T[:SUBLANES, :]
        return jnp.broadcast_to(acc[LANES - 1:LANES, :], (LANES, LANES))

    lax.fori_loop(0, seq // LANES, body, jnp.zeros((LANES, LANES), F32))


def _cumsum(lf3):
    b, seq, _ = lf3.shape
    return pl.pallas_call(
        functools.partial(_cumsum_kernel, seq=seq),
        grid=(b,),
        in_specs=[pl.BlockSpec((1, seq, LANES), lambda i: (i, 0, 0))],
        out_specs=[pl.BlockSpec((1, seq, LANES), lambda i: (i, 0, 0)),
                   pl.BlockSpec((1, SUBLANES, seq), lambda i: (i, 0, 0))],
        out_shape=[jax.ShapeDtypeStruct((b, seq, LANES), F32),
                   jax.ShapeDtypeStruct((b, SUBLANES, seq), F32)],
        compiler_params=_cp(("arbitrary",)),
        name="forget_cumsum",
    )(lf3)


_NT = (((1,), (1,)), ((), ()))
NEG_INIT = -1e30
_TN = (((0,), (0,)), ((), ()))
ATTN_TK = 256


def _head_halves(q):
    lane = lax.broadcasted_iota(jnp.int32, (1, LANES), 1)
    zero = jnp.zeros_like(q)
    return [jnp.where(lane < HEAD_DIM, q, zero), jnp.where(lane >= HEAD_DIM, q, zero)]


def _mask_first_block(x, keep, fill, tk):
    head = jnp.where(keep, x[:, :tk], fill)
    return head if x.shape[1] == tk else jnp.concatenate([head, x[:, tk:]], axis=1)


def _fox_kernel(q_ref, k_ref, v_ref, cc_ref, cr_ref, o_ref, *, seq, tk):
    hp = pl.program_id(1)
    lane = lax.broadcasted_iota(jnp.int32, (1, LANES), 1)
    qs = _head_halves(q_ref[0])
    key = lax.broadcasted_iota(jnp.int32, (tk, tk), 0)
    qry = lax.broadcasted_iota(jnp.int32, (tk, tk), 1)
    causal = key <= qry
    carry = [(jnp.full((1, seq), NEG_INIT, F32), jnp.zeros((1, seq), F32),
              jnp.zeros((LANES, seq), F32)) for _ in range(2)]
    for j in range(seq // tk):
        q0 = j * tk
        k = k_ref[0, q0:q0 + tk, :]
        v = v_ref[0, q0:q0 + tk, :]
        cc = cc_ref[0, q0:q0 + tk, :]
        for hh in range(2):
            m, l, acc = carry[hh]
            cs = jnp.sum(jnp.where(lane == 2 * hp + hh, cc, 0.0), axis=1, keepdims=True)
            ct = cr_ref[0, hh, :, q0:]
            st = lax.dot_general(k, qs[hh][q0:, :], _NT, preferred_element_type=F32)
            st = _mask_first_block((st + ct) - cs, causal, -jnp.inf, tk)
            mn, ln, an = _softmax_step_t(st, m[:, q0:], l[:, q0:], acc[:, q0:], v)
            if q0:
                mn = jnp.concatenate([m[:, :q0], mn], axis=1)
                ln = jnp.concatenate([l[:, :q0], ln], axis=1)
                an = jnp.concatenate([acc[:, :q0], an], axis=1)
            carry[hh] = (mn, ln, an)
    _pair_out_t([carry[0][2], carry[1][2]], [carry[0][1], carry[1][1]], o_ref)


def _fox_attention(qkv3, cum, cumt4):
    b, seq, _ = qkv3.shape
    tk = min(ATTN_TK, seq)
    hp = FOX_HEADS // 2
    qoff = OFF_FOX // LANES
    return pl.pallas_call(
        functools.partial(_fox_kernel, seq=seq, tk=tk),
        grid=(b, hp),
        in_specs=[pl.BlockSpec((1, seq, LANES), lambda bi, h: (bi, 0, qoff + h)),
                  pl.BlockSpec((1, seq, LANES), lambda bi, h: (bi, 0, qoff + hp + h)),
                  pl.BlockSpec((1, seq, LANES), lambda bi, h: (bi, 0, qoff + 2 * hp + h)),
                  pl.BlockSpec((1, seq, LANES), lambda bi, h: (bi, 0, 0)),
                  pl.BlockSpec((1, 2, 1, seq), lambda bi, h: (bi, h, 0, 0))],
        out_specs=pl.BlockSpec((1, seq, LANES), lambda bi, h: (bi, 0, h)),
        out_shape=jax.ShapeDtypeStruct((b, seq, FOX_W), BF16),
        compiler_params=_cp(("arbitrary", "arbitrary")),
        name="fox_attention",
    )(qkv3, qkv3, qkv3, cum, cumt4)


def _softmax_step_t(st, m, l, acc, v):
    m_new = jnp.maximum(m, jnp.max(st, axis=0, keepdims=True))
    alpha = jnp.exp(m - m_new)
    p = jnp.exp(st - m_new)
    l = alpha * l + jnp.sum(p, axis=0, keepdims=True)
    acc = alpha * acc + lax.dot_general(v, p.astype(BF16), _TN, preferred_element_type=F32)
    return m_new, l, acc


def _pair_out_t(accs, ls, o_ref):
    sub = lax.broadcasted_iota(jnp.int32, (LANES, 1), 0)
    ot = jnp.where(sub < HEAD_DIM, accs[0] / ls[0], accs[1] / ls[1])
    o_ref[0] = ot.T.astype(o_ref.dtype)


def _mla_kernel(q_ref, k_ref, v_ref, o_ref, *, seq, tk):
    q = q_ref[0]
    qs = [q[:, :LANES], q[:, LANES:]]
    key = lax.broadcasted_iota(jnp.int32, (tk, tk), 0)
    qry = lax.broadcasted_iota(jnp.int32, (tk, tk), 1)
    shift = CHUNK.bit_length() - 1
    chunk_causal = (key >> shift) <= (qry >> shift)
    carry = [(jnp.full((1, seq), NEG_INIT, F32), jnp.zeros((1, seq), F32),
              jnp.zeros((LANES, seq), F32)) for _ in range(2)]
    for j in range(seq // tk):
        q0 = j * tk
        k = k_ref[0, q0:q0 + tk, :]
        v = v_ref[0, q0:q0 + tk, :]
        for hh in range(2):
            m, l, acc = carry[hh]
            st = lax.dot_general(k[:, hh * LANES:(hh + 1) * LANES], qs[hh][q0:, :], _NT,
                                 preferred_element_type=F32)
            st = _mask_first_block(st, chunk_causal, -jnp.inf, tk)
            mn, ln, an = _softmax_step_t(st, m[:, q0:], l[:, q0:], acc[:, q0:], v)
            if q0:
                mn = jnp.concatenate([m[:, :q0], mn], axis=1)
                ln = jnp.concatenate([l[:, :q0], ln], axis=1)
                an = jnp.concatenate([acc[:, :q0], an], axis=1)
            carry[hh] = (mn, ln, an)
    _pair_out_t([carry[0][2], carry[1][2]], [carry[0][1], carry[1][1]], o_ref)


def _mla_attention(q3, k3, v3):
    b, seq, _ = q3.shape
    tk = min(ATTN_TK, seq)
    hp = MLA_HEADS // 2
    return pl.pallas_call(
        functools.partial(_mla_kernel, seq=seq, tk=tk),
        grid=(b, hp),
        in_specs=[pl.BlockSpec((1, seq, 2 * LANES), lambda bi, h: (bi, 0, h)),
                  pl.BlockSpec((1, seq, 2 * LANES), lambda bi, h: (bi, 0, h)),
                  pl.BlockSpec((1, seq, LANES), lambda bi, h: (bi, 0, h))],
        out_specs=pl.BlockSpec((1, seq, LANES), lambda bi, h: (bi, 0, h)),
        out_shape=jax.ShapeDtypeStruct((b, seq, MLA_HEADS * MLA_V_DIM), BF16),
        compiler_params=_cp(("arbitrary", "arbitrary")),
        name="mla_attention",
    )(q3, k3, v3)


def _softplus(z):
    return jnp.maximum(z, 0.0) + jnp.log(1.0 + jnp.exp(-jnp.abs(z)))


def _sb_kernel(q_ref, k_ref, v_ref, o_ref, run_ref, acc_ref, *, seq, tk):
    n = seq // tk
    qs = _head_halves(q_ref[0])
    key = lax.broadcasted_iota(jnp.int32, (tk, tk), 0)
    qry = lax.broadcasted_iota(jnp.int32, (tk, tk), 1)
    strict = key < qry
    later = (qry > key).astype(BF16)

    def unit(j, qlo, qhi, diag):
        k = k_ref[0, j * tk:(j + 1) * tk, :]
        v = v_ref[0, j * tk:(j + 1) * tk, :]
        for hh in range(2):
            z = lax.dot_general(k, qs[hh][qlo:qhi, :], _NT, preferred_element_type=F32)
            sp = _softplus(z)
            log_keep = -sp
            if diag:
                log_keep = _mask_first_block(log_keep, strict, 0.0, tk)
            between = run_ref[hh, :, qlo:qhi]
            for part in _split_bf16(log_keep, 2):
                between = between + jnp.dot(later, part, preferred_element_type=F32)
            w = jnp.exp((z - sp) + between)
            if diag:
                w = _mask_first_block(w, strict, 0.0, tk)
            acc_ref[hh, :, qlo:qhi] += lax.dot_general(v, w.astype(BF16), _TN,
                                                       preferred_element_type=F32)
            run_ref[hh, :, qlo:qhi] += jnp.sum(log_keep, axis=0, keepdims=True)

    run_ref[...] = jnp.zeros_like(run_ref)
    acc_ref[...] = jnp.zeros_like(acc_ref)
    for j in range(n - 1, -1, -1):
        unit(j, j * tk, min((j + 2) * tk, seq), True)

    for j in range(n - 3, -1, -1):
        qlo = (j + 2) * tk

        @pl.when(jnp.max(run_ref[:, :, qlo:]) >= SB_CUTOFF)
        def _():
            unit(j, qlo, seq, False)

    sub = lax.broadcasted_iota(jnp.int32, (LANES, 1), 0)
    o_ref[0] = jnp.where(sub < HEAD_DIM, acc_ref[0], acc_ref[1]).T.astype(o_ref.dtype)


def _sb_attention(qkv3):
    b, seq, _ = qkv3.shape
    tk = min(ATTN_TK, seq)
    hp = SB_HEADS // 2
    qoff = OFF_SB // LANES
    return pl.pallas_call(
        functools.partial(_sb_kernel, seq=seq, tk=tk),
        grid=(b, hp),
        in_specs=[pl.BlockSpec((1, seq, LANES), lambda bi, h: (bi, 0, qoff + h)),
                  pl.BlockSpec((1, seq, LANES), lambda bi, h: (bi, 0, qoff + hp + h)),
                  pl.BlockSpec((1, seq, LANES), lambda bi, h: (bi, 0, qoff + 2 * hp + h))],
        out_specs=pl.BlockSpec((1, seq, LANES), lambda bi, h: (bi, 0, h)),
        out_shape=jax.ShapeDtypeStruct((b, seq, SB_W), BF16),
        scratch_shapes=[pltpu.VMEM((2, 1, seq), F32), pltpu.VMEM((2, LANES, seq), F32)],
        compiler_params=_cp(("arbitrary", "arbitrary")),
        name="sb_attention",
    )(qkv3, qkv3, qkv3)


def _layer_norm(y, g, b):
    yc = y - jnp.mean(y, axis=-1, keepdims=True)
    var = jnp.mean(yc * yc, axis=-1, keepdims=True)
    return yc * lax.rsqrt(var + LN_EPS) * g + b


HALF = D_MODEL // 2
U32 = jnp.uint32


def _pack_halves(y):
    bits = lax.bitcast_convert_type(y.astype(BF16).astype(F32), U32)
    return (bits[:, :HALF] >> 16) | (bits[:, HALF:] & U32(0xFFFF0000))


def _unpack_halves(w):
    lo = lax.bitcast_convert_type(w << 16, F32)
    hi = lax.bitcast_convert_type(w & U32(0xFFFF0000), F32)
    return lo, hi


def _merge_kernel(osb_ref, ofox_ref, omla_ref, g_ref, x_ref, wsb_ref, wfox_ref, wmla_ref,
                  wout_ref, lg_ref, lb_ref, x1_ref, x1b_ref, x1p_ref):
    g = g_ref[...].astype(F32)
    a = jnp.dot(osb_ref[...], wsb_ref[...], preferred_element_type=F32)
    b = jnp.dot(ofox_ref[...], wfox_ref[...], preferred_element_type=F32)
    c = jnp.dot(omla_ref[...], wmla_ref[...], preferred_element_type=F32)
    mixed = (g[:, :D_MODEL] * a + g[:, D_MODEL:2 * D_MODEL] * b + g[:, 2 * D_MODEL:] * c)
    mix = jnp.dot(mixed.astype(BF16), wout_ref[...], preferred_element_type=F32)
    y = _layer_norm(DN_ALPHA * x_ref[...] + mix, lg_ref[...], lb_ref[...])
    x1_ref[...] = y
    x1b_ref[...] = y.astype(BF16)
    x1p_ref[...] = _pack_halves(y)


def _merge(o_sb, o_fox, o_mla, gates, x, wsb, wfox, wmla, wout, lg, lb, tm=512):
    n = x.shape[0]
    tm = min(tm, n)
    full = lambda a: pl.BlockSpec(a.shape, lambda i: (0,) * a.ndim)
    rows = lambda w: pl.BlockSpec((tm, w), lambda i: (i, 0))
    return pl.pallas_call(
        _merge_kernel,
        grid=(n // tm,),
        in_specs=[rows(SB_W), rows(FOX_W), rows(MLA_HEADS * MLA_V_DIM), rows(N_BRANCHES * D_MODEL),
                  rows(D_MODEL), full(wsb), full(wfox), full(wmla), full(wout), full(lg), full(lb)],
        out_specs=[rows(D_MODEL), rows(D_MODEL), rows(HALF)],
        out_shape=[jax.ShapeDtypeStruct((n, D_MODEL), F32),
                   jax.ShapeDtypeStruct((n, D_MODEL), BF16),
                   jax.ShapeDtypeStruct((n, HALF), U32)],
        compiler_params=_cp(("arbitrary",)),
        name="merge_outproj_ln",
    )(o_sb, o_fox, o_mla, gates, x, wsb, wfox, wmla, wout, lg, lb)


def _route_kernel(x_ref, wr_ref, rb_ref, idx_ref, w_ref, rank_ref, cnt_ref, run_ref, *, tm):
    step = pl.program_id(0)

    @pl.when(step == 0)
    def _():
        run_ref[...] = jnp.zeros_like(run_ref)

    logits = lax.dot_general(wr_ref[...], x_ref[...], _NT, preferred_element_type=F32)
    scores = jax.nn.sigmoid(logits)
    biased = scores + rb_ref[...]
    e_iota = lax.broadcasted_iota(jnp.int32, (N_EXPERTS, tm), 0)
    big = jnp.int32(1 << 20)

    g_iota = lax.broadcasted_iota(jnp.int32, (GROUP_SIZE, tm), 0)
    gs_rows = []
    for g in range(N_GROUPS):
        blk = biased[g * GROUP_SIZE:(g + 1) * GROUP_SIZE, :]
        m1 = jnp.max(blk, axis=0, keepdims=True)
        first = jnp.min(jnp.where(blk == m1, g_iota, big), axis=0, keepdims=True)
        m2 = jnp.max(jnp.where(g_iota == first, -jnp.inf, blk), axis=0, keepdims=True)
        gs_rows.append(m1 + m2)
    gs = jnp.concatenate(gs_rows, axis=0)
    n_iota = lax.broadcasted_iota(jnp.int32, (N_GROUPS, tm), 0)
    keep = jnp.zeros((N_GROUPS, tm), jnp.bool_)
    for _ in range(TOPK_GROUPS):
        m = jnp.max(gs, axis=0, keepdims=True)
        first = jnp.min(jnp.where(gs == m, n_iota, big), axis=0, keepdims=True)
        hit = n_iota == first
        keep = jnp.logical_or(keep, hit)
        gs = jnp.where(hit, -jnp.inf, gs)
    keep_f = keep.astype(F32)
    expert_keep = jnp.concatenate(
        [jnp.broadcast_to(keep_f[g:g + 1, :], (GROUP_SIZE, tm)) for g in range(N_GROUPS)], axis=0)
    masked = jnp.where(expert_keep > 0.5, biased, -jnp.inf)

    idx_rows, w_rows, hits = [], [], []
    sel = jnp.zeros((N_EXPERTS, tm), F32)
    for _ in range(TOP_K):
        m = jnp.max(masked, axis=0, keepdims=True)
        first = jnp.min(jnp.where(masked == m, e_iota, big), axis=0, keepdims=True)
        hit = e_iota == first
        idx_rows.append(first)
        w_rows.append(jnp.sum(jnp.where(hit, scores, 0.0), axis=0, keepdims=True))
        hits.append(hit)
        sel = sel + hit.astype(F32)
        masked = jnp.where(hit, -jnp.inf, masked)
    w = jnp.concatenate(w_rows, axis=0)
    w = w / jnp.sum(w, axis=0, keepdims=True) * ROUTED_SCALE
    idx_ref[...] = jnp.concatenate(idx_rows, axis=0)
    w_ref[...] = w

    r = lax.broadcasted_iota(jnp.int32, (tm, tm), 0)
    c = lax.broadcasted_iota(jnp.int32, (tm, tm), 1)
    earlier = (r < c).astype(BF16)
    prefix = jnp.dot(sel.astype(BF16), earlier, preferred_element_type=F32) + run_ref[...]
    rank_rows = [jnp.sum(jnp.where(h, prefix, 0.0), axis=0, keepdims=True) for h in hits]
    rank_ref[...] = jnp.concatenate(rank_rows, axis=0).astype(jnp.int32)
    total = run_ref[...] + jnp.sum(sel, axis=1, keepdims=True)
    run_ref[...] = total
    cnt_ref[...] = jnp.broadcast_to(total, (N_EXPERTS, LANES))


def _route(x1b, wr_t, rb, tm=256):
    n = x1b.shape[0]
    tm = min(tm, n)
    return pl.pallas_call(
        functools.partial(_route_kernel, tm=tm),
        grid=(n // tm,),
        in_specs=[pl.BlockSpec((tm, D_MODEL), lambda i: (i, 0)),
                  pl.BlockSpec((N_EXPERTS, D_MODEL), lambda i: (0, 0)),
                  pl.BlockSpec((N_EXPERTS, 1), lambda i: (0, 0))],
        out_specs=[pl.BlockSpec((TOP_K, tm), lambda i: (0, i)),
                   pl.BlockSpec((TOP_K, tm), lambda i: (0, i)),
                   pl.BlockSpec((TOP_K, tm), lambda i: (0, i)),
                   pl.BlockSpec((N_EXPERTS, LANES), lambda i: (0, 0))],
        out_shape=[jax.ShapeDtypeStruct((TOP_K, n), jnp.int32),
                   jax.ShapeDtypeStruct((TOP_K, n), F32),
                   jax.ShapeDtypeStruct((TOP_K, n), jnp.int32),
                   jax.ShapeDtypeStruct((N_EXPERTS, LANES), F32)],
        scratch_shapes=[pltpu.VMEM((N_EXPERTS, 1), F32)],
        compiler_params=_cp(("arbitrary",)),
        name="router_topk",
    )(x1b, wr_t, rb)


def _dest_kernel(idx_ref, rank_ref, st_ref, dest_ref, *, tm):
    e_iota = lax.broadcasted_iota(jnp.int32, (N_EXPERTS, tm), 0)
    starts = st_ref[...]
    rows = []
    for k in range(TOP_K):
        hit = e_iota == idx_ref[k:k + 1, :]
        rows.append(jnp.sum(jnp.where(hit, starts, 0.0), axis=0, keepdims=True))
    dest_ref[...] = jnp.concatenate(rows, axis=0).astype(jnp.int32) + rank_ref[...]


def _dest(idx_t, rank_t, starts, tm=512):
    n = idx_t.shape[1]
    tm = min(tm, n)
    blk = lambda: pl.BlockSpec((TOP_K, tm), lambda i: (0, i))
    return pl.pallas_call(
        functools.partial(_dest_kernel, tm=tm),
        grid=(n // tm,),
        in_specs=[blk(), blk(), pl.BlockSpec((N_EXPERTS, 1), lambda i: (0, 0))],
        out_specs=blk(),
        out_shape=jax.ShapeDtypeStruct((TOP_K, n), jnp.int32),
        compiler_params=_cp(("arbitrary",)),
        name="moe_dest",
    )(idx_t, rank_t, starts.astype(F32).reshape(N_EXPERTS, 1))


def _dispatch_kernel(dest_ref, x_ref, xs_hbm, sem, *, tt):
    def row_copy(t, k):
        dst = dest_ref[0, 0, t * TOP_K + k]
        return pltpu.make_async_copy(x_ref.at[pl.ds(t, 1)], xs_hbm.at[pl.ds(dst, 1)], sem)

    def issue(t, _):
        for k in range(TOP_K):
            row_copy(t, k).start()
        return 0

    lax.fori_loop(0, tt, issue, 0)

    def drain(t, _):
        for k in range(TOP_K):
            row_copy(t, k).wait()
        return 0

    lax.fori_loop(0, tt, drain, 0)


def _dispatch(dest_tok, x1p, tt=256):
    n = x1p.shape[0]
    tt = min(tt, n)
    return pl.pallas_call(
        functools.partial(_dispatch_kernel, tt=tt),
        grid=(n // tt,),
        in_specs=[pl.BlockSpec((1, 1, tt * TOP_K), lambda i: (i, 0, 0), memory_space=pltpu.SMEM),
                  pl.BlockSpec((tt, HALF), lambda i: (i, 0))],
        out_specs=pl.BlockSpec(memory_space=pl.ANY),
        out_shape=jax.ShapeDtypeStruct((n * TOP_K, HALF), U32),
        scratch_shapes=[pltpu.SemaphoreType.DMA(())],
        compiler_params=_cp(("arbitrary",)),
        name="moe_dispatch",
    )(dest_tok.reshape(n // tt, 1, tt * TOP_K), x1p)


def _expert_kernel(tile_ref, exp_ref, lo_ref, hi_ref, xs_ref, wg_ref, wu_ref, wd_ref, ys_ref,
                   *, bm):
    w = pl.program_id(0)
    lo = lo_ref[w]
    hi = hi_ref[w]

    @pl.when(hi > lo)
    def _():
        x_lo, x_hi = _unpack_halves(xs_ref[...])
        x = jnp.concatenate([x_lo, x_hi], axis=1).astype(BF16)
        g = jnp.dot(x, wg_ref[0, 0].astype(BF16), preferred_element_type=F32)
        u = jnp.dot(x, wu_ref[0, 0].astype(BF16), preferred_element_type=F32)
        h = (g * jax.nn.sigmoid(g) * u).astype(BF16)
        y = _pack_halves(jnp.dot(h, wd_ref[0, 0].astype(BF16), preferred_element_type=F32))

        @pl.when(lo == 0)
        def _():
            ys_ref[...] = y

        @pl.when(lo > 0)
        def _():
            rows = lax.broadcasted_iota(jnp.int32, (bm, 1), 0)
            ys_ref[...] = jnp.where(rows >= lo, y, ys_ref[...])


def _experts(layer, item_tile, item_exp, item_lo, item_hi, xs, wg, wu, wd, bm):
    r = xs.shape[0]
    n_items = item_tile.shape[0]
    grid_spec = pltpu.PrefetchScalarGridSpec(
        num_scalar_prefetch=4,
        grid=(n_items,),
        in_specs=[pl.BlockSpec((bm, HALF), lambda w, t, e, lo, hi: (t[w], 0)),
                  pl.BlockSpec((1, 1, D_MODEL, EXPERT_FF),
                               lambda w, t, e, lo, hi: (layer, e[w], 0, 0)),
                  pl.BlockSpec((1, 1, D_MODEL, EXPERT_FF),
                               lambda w, t, e, lo, hi: (layer, e[w], 0, 0)),
                  pl.BlockSpec((1, 1, EXPERT_FF, D_MODEL),
                               lambda w, t, e, lo, hi: (layer, e[w], 0, 0))],
        out_specs=pl.BlockSpec((bm, HALF), lambda w, t, e, lo, hi: (t[w], 0)),
    )
    return pl.pallas_call(
        functools.partial(_expert_kernel, bm=bm),
        grid_spec=grid_spec,
        out_shape=jax.ShapeDtypeStruct((r, HALF), U32),
        compiler_params=_cp(("arbitrary",)),
        name="moe_experts",
    )(item_tile, item_exp, item_lo, item_hi, xs, wg, wu, wd)


def _expert_items(counts, n_rows, bm):
    n_tiles = n_rows // bm
    ends = jnp.cumsum(counts)
    starts = ends - counts
    bounds = jnp.sort(jnp.concatenate([jnp.arange(n_tiles, dtype=jnp.int32) * bm,
                                       starts.astype(jnp.int32)]))
    nxt = jnp.concatenate([bounds[1:], jnp.array([n_rows], jnp.int32)])
    tile = jnp.minimum(bounds // bm, n_tiles - 1)
    exp = jnp.minimum(jnp.searchsorted(ends, bounds, side="right"), N_EXPERTS - 1).astype(jnp.int32)
    lo = bounds - tile * bm
    hi = nxt - tile * bm
    return tile.astype(jnp.int32), exp, lo.astype(jnp.int32), hi.astype(jnp.int32), starts


def _combine_kernel(dcur_ref, dnext_ref, ys_hbm, tw_ref, x1_ref, x1b_ref, wsg_ref, wsu_ref,
                    wsd_ref, lg_ref, lb_ref, o_ref, ob_ref, buf, sem, *, tc, n_steps):
    i = pl.program_id(0)
    slot = lax.rem(i, 2)

    def row_copy(dest_ref, s, t, k):
        src = dest_ref[0, 0, t * TOP_K + k]
        return pltpu.make_async_copy(ys_hbm.at[pl.ds(src, 1)], buf.at[s, k, pl.ds(t, 1)],
                                     sem.at[s])

    def issue(dest_ref, s):
        def body(t, _):
            for k in range(TOP_K):
                row_copy(dest_ref, s, t, k).start()
            return 0
        lax.fori_loop(0, tc, body, 0)

    @pl.when(i == 0)
    def _():
        issue(dcur_ref, 0)

    @pl.when(i + 1 < n_steps)
    def _():
        issue(dnext_ref, 1 - slot)

    def drain(t, _):
        for k in range(TOP_K):
            row_copy(dcur_ref, slot, t, k).wait()
        return 0

    lax.fori_loop(0, tc, drain, 0)

    tw = tw_ref[...]
    r_lo = jnp.zeros((tc, HALF), F32)
    r_hi = jnp.zeros((tc, HALF), F32)
    for k in range(TOP_K):
        y_lo, y_hi = _unpack_halves(buf[slot, k])
        r_lo = r_lo + tw[:, k:k + 1] * y_lo
        r_hi = r_hi + tw[:, k:k + 1] * y_hi
    routed = jnp.concatenate([r_lo, r_hi], axis=1)
    xb = x1b_ref[...]
    g = jnp.dot(xb, wsg_ref[...], preferred_element_type=F32)
    u = jnp.dot(xb, wsu_ref[...], preferred_element_type=F32)
    h = (g * jax.nn.sigmoid(g) * u).astype(BF16)
    shared = jnp.dot(h, wsd_ref[...], preferred_element_type=F32)
    y = _layer_norm(DN_ALPHA * x1_ref[...] + (shared + routed), lg_ref[...], lb_ref[...])
    o_ref[...] = y
    ob_ref[...] = y.astype(BF16)


def _combine(dest_tok, ys, tw, x1, x1b, wsg, wsu, wsd, lg, lb, tc=128):
    n = x1.shape[0]
    tc = min(tc, n)
    n_steps = n // tc
    dest3 = dest_tok.reshape(n_steps, 1, tc * TOP_K)
    full = lambda a: pl.BlockSpec(a.shape, lambda i: (0,) * a.ndim)
    rows = lambda w: pl.BlockSpec((tc, w), lambda i: (i, 0))
    return pl.pallas_call(
        functools.partial(_combine_kernel, tc=tc, n_steps=n_steps),
        grid=(n_steps,),
        in_specs=[pl.BlockSpec((1, 1, tc * TOP_K), lambda i: (i, 0, 0), memory_space=pltpu.SMEM),
                  pl.BlockSpec((1, 1, tc * TOP_K),
                               lambda i: (jnp.minimum(i + 1, n_steps - 1), 0, 0),
                               memory_space=pltpu.SMEM),
                  pl.BlockSpec(memory_space=pl.ANY),
                  rows(TOP_K), rows(D_MODEL), rows(D_MODEL),
                  full(wsg), full(wsu), full(wsd), full(lg), full(lb)],
        out_specs=[rows(D_MODEL), rows(D_MODEL)],
        out_shape=[jax.ShapeDtypeStruct((n, D_MODEL), F32),
                   jax.ShapeDtypeStruct((n, D_MODEL), BF16)],
        scratch_shapes=[pltpu.VMEM((2, TOP_K, tc, HALF), U32),
                        pltpu.SemaphoreType.DMA((2,))],
        compiler_params=_cp(("arbitrary",)),
        name="moe_combine_ln",
    )(dest3, dest3, ys, tw, x1, x1b, wsg, wsu, wsd, lg, lb)


def _head_cols(w, heads, width, lo, hi):
    return w.reshape(w.shape[0], heads, width)[:, :, lo:hi]


def _prep_layer(w_in, b_gate, b_forget, mla_q_norm, w_uq, mla_kv_norm, w_ukv):
    f = lambda a: a.astype(BF16)
    half = MLA_ROPE_DIM // 2
    d = w_in.shape[0]
    qscale = HEAD_DIM ** -0.5
    w_qkv = jnp.concatenate([
        w_in[:, OFF_SB:OFF_SB + SB_W] * qscale, w_in[:, OFF_SB + SB_W:OFF_FOX],
        w_in[:, OFF_FOX:OFF_FOX + FOX_W] * qscale, w_in[:, OFF_FOX + FOX_W:OFF_FGATE]], axis=1)
    w_kr = w_in[:, OFF_KR:OFF_GATE]
    w_kr_rot = jnp.concatenate([-w_kr[:, half:], w_kr[:, :half]], axis=1)
    z = lambda c: jnp.zeros((d, c), F32)
    pad_rope = LANES - MLA_QK_DIM
    w_small = jnp.concatenate([
        w_in[:, OFF_DQ:OFF_DKV], w_in[:, OFF_DKV:OFF_KR],
        z(MLA_NOPE_DIM), w_kr, z(pad_rope),
        z(MLA_NOPE_DIM), w_kr_rot, z(pad_rope),
        w_in[:, OFF_FGATE:OFF_DQ], z(LANES - FOX_HEADS)], axis=1)
    w_gate = w_in[:, OFF_GATE:]

    r = w_uq.shape[0]
    q_nope = _head_cols(w_uq, MLA_HEADS, MLA_QK_DIM, 0, MLA_NOPE_DIM)
    q_rope = _head_cols(w_uq, MLA_HEADS, MLA_QK_DIM, MLA_NOPE_DIM, MLA_QK_DIM)
    q_rope_rot = jnp.concatenate([-q_rope[:, :, half:], q_rope[:, :, :half]], axis=2)
    zq = lambda c: jnp.zeros((r, MLA_HEADS, c), F32)
    wqa = jnp.concatenate([q_nope, q_rope, zq(pad_rope)], axis=2).reshape(r, MLA_HEADS * LANES)
    wqb = jnp.concatenate([zq(MLA_NOPE_DIM), q_rope_rot, zq(pad_rope)], axis=2).reshape(
        r, MLA_HEADS * LANES)
    rk = w_ukv.shape[0]
    kvw = MLA_NOPE_DIM + MLA_V_DIM
    k_nope = _head_cols(w_ukv, MLA_HEADS, kvw, 0, MLA_NOPE_DIM)
    wk = jnp.concatenate([k_nope, jnp.zeros((rk, MLA_HEADS, LANES - MLA_NOPE_DIM), F32)],
                         axis=2).reshape(rk, MLA_HEADS * LANES)
    wv = _head_cols(w_ukv, MLA_HEADS, kvw, MLA_NOPE_DIM, kvw).reshape(rk, MLA_HEADS * MLA_V_DIM)
    bf = jnp.concatenate([b_forget, jnp.zeros((LANES - FOX_HEADS,), F32)]).reshape(1, LANES)
    return dict(w_qkv=f(w_qkv), w_small=f(w_small), w_gate=f(w_gate),
                b_gate=b_gate.reshape(1, -1), wqa=f(wqa), wqb=f(wqb), wk=f(wk), wv=f(wv),
                qn=mla_q_norm.reshape(1, -1), kvn=mla_kv_norm.reshape(1, -1), bf=bf)


def _rope_tables(seq):
    half = MLA_ROPE_DIM // 2
    inv_freq = jnp.power(ROPE_BASE, -jnp.arange(half, dtype=F32) / half)
    ang = jnp.arange(seq).astype(F32)[:, None] * inv_freq[None, :]
    cos = jnp.concatenate([jnp.cos(ang), jnp.cos(ang)], axis=1)
    sin = jnp.concatenate([jnp.sin(ang), jnp.sin(ang)], axis=1)
    pad = lambda t: jnp.concatenate([jnp.zeros((seq, MLA_NOPE_DIM), F32), t,
                                     jnp.zeros((seq, LANES - MLA_QK_DIM), F32)], axis=1)
    return pad(cos), pad(sin)


EXPERT_BM = 256


def kernel(x, ln1_g, ln1_b, ln2_g, ln2_b, w_in, b_gate, b_forget, mla_q_norm, w_uq, mla_kv_norm,
           w_ukv, w_proj_sb, w_proj_fox, w_proj_mla, w_out, w_router, router_bias,
           w_exp_gate, w_exp_up, w_exp_down, w_sh_gate, w_sh_up, w_sh_down):
    b, seq, d = x.shape
    n = b * seq
    depth = w_in.shape[0]
    cos_t, sin_t = _rope_tables(seq)
    xf = x.reshape(n, d)
    xb = xf.astype(BF16)
    f = lambda a: a.astype(BF16)
    bm = min(EXPERT_BM, n * TOP_K)
    for l in range(depth):
        p = _prep_layer(w_in[l], b_gate[l], b_forget[l], mla_q_norm[l], w_uq[l], mla_kv_norm[l],
                        w_ukv[l])
        qkv = _matmul(xb, p["w_qkv"], None, BF16)
        gates = _matmul(xb, p["w_gate"], p["b_gate"], BF16, act="sigmoid")
        small = _matmul(xb, p["w_small"], None, F32, tn=SMALL_W)
        q_mla, k_mla, v_mla, lf = _mla_prep(small, p["qn"], p["kvn"], p["wqa"], p["wqb"], p["wk"],
                                            p["wv"], cos_t, sin_t, p["bf"], seq)
        cum, cumt = _cumsum(lf.reshape(b, seq, LANES))
        cumt4 = cumt[:, :FOX_HEADS, :].reshape(b, FOX_HEADS, 1, seq)
        qkv3 = qkv.reshape(b, seq, QKV_W)
        o_sb = _sb_attention(qkv3).reshape(n, SB_W)
        o_fox = _fox_attention(qkv3, cum, cumt4).reshape(n, FOX_W)
        o_mla = _mla_attention(q_mla.reshape(b, seq, -1), k_mla.reshape(b, seq, -1),
                               v_mla.reshape(b, seq, -1)).reshape(n, -1)
        x1, x1b, x1p = _merge(o_sb, o_fox, o_mla, gates, xf, f(w_proj_sb[l]), f(w_proj_fox[l]),
                              f(w_proj_mla[l]), f(w_out[l]), ln1_g[l].reshape(1, d),
                              ln1_b[l].reshape(1, d))

        idx_t, tw_t, rank_t, cnt = _route(x1b, f(w_router[l].T), router_bias[l].reshape(-1, 1))
        counts = cnt[:, 0].astype(jnp.int32)
        item_tile, item_exp, item_lo, item_hi, starts = _expert_items(counts, n * TOP_K, bm)
        dest_tok = _dest(idx_t, rank_t, starts).T.reshape(-1)
        xs = _dispatch(dest_tok, x1p)
        ys = _experts(l, item_tile, item_exp, item_lo, item_hi, xs, w_exp_gate, w_exp_up,
                      w_exp_down, bm)
        xf, xb = _combine(dest_tok, ys, tw_t.T, x1, x1b, f(w_sh_gate[l]), f(w_sh_up[l]),
                          f(w_sh_down[l]), ln2_g[l].reshape(1, d), ln2_b[l].reshape(1, d))
    return xf.reshape(b, seq, d)
```

```python
import functools

import jax
import jax.numpy as jnp
from jax import lax
from jax.experimental import pallas as pl
from jax.experimental.pallas import tpu as pltpu
from jax.experimental.pallas import tpu_sc as plsc

F32 = jnp.float32
BF16 = jnp.bfloat16

D_MODEL = 1024
HEAD_DIM = 64
SB_HEADS = 4
FOX_HEADS = 4
MLA_HEADS = 8
MLA_Q_RANK = 256
MLA_KV_RANK = 128
MLA_NOPE_DIM = 64
MLA_ROPE_DIM = 32
MLA_V_DIM = 64
ROPE_BASE = 10000.0
N_BRANCHES = 3
N_EXPERTS = 256
TOP_K = 8
N_GROUPS = 8
TOPK_GROUPS = 4
GROUP_SIZE = N_EXPERTS // N_GROUPS
EXPERT_FF = 256
SHARED_FF = 256
ROUTED_SCALE = 2.5
CHUNK = 64
LN_EPS = 1e-5
RMS_EPS = 1e-6
DEPTH = 2
DN_ALPHA = (2 * DEPTH) ** 0.25

SB_W = SB_HEADS * HEAD_DIM
FOX_W = FOX_HEADS * HEAD_DIM
MLA_QK_DIM = MLA_NOPE_DIM + MLA_ROPE_DIM
OFF_SB = 0
OFF_FOX = OFF_SB + 3 * SB_W
OFF_FGATE = OFF_FOX + 3 * FOX_W
OFF_DQ = OFF_FGATE + FOX_HEADS
OFF_DKV = OFF_DQ + MLA_Q_RANK
OFF_KR = OFF_DKV + MLA_KV_RANK
OFF_GATE = OFF_KR + MLA_ROPE_DIM

LANES = 128
SUBLANES = 8
QKV_W = 3 * SB_W + 3 * FOX_W
SMALL_W = MLA_Q_RANK + MLA_KV_RANK + 3 * LANES
SB_CUTOFF = -104.0

VMEM_LIMIT = 48 * 1024 * 1024


def _cp(sem, vmem=VMEM_LIMIT):
    return pltpu.CompilerParams(dimension_semantics=sem, vmem_limit_bytes=vmem)


def _mm_kernel(x_ref, w_ref, b_ref, o_ref, *, act):
    acc = jnp.dot(x_ref[...], w_ref[...], preferred_element_type=F32)
    if act == "sigmoid":
        acc = jax.nn.sigmoid(acc + b_ref[...])
    o_ref[...] = acc.astype(o_ref.dtype)


def _matmul(x, w, bias, out_dtype, act=None, tm=1024, tn=512):
    n, k = x.shape
    c = w.shape[1]
    tm = min(tm, n)
    tn = min(tn, c)
    if bias is None:
        bias = jnp.zeros((1, c), F32)
    return pl.pallas_call(
        functools.partial(_mm_kernel, act=act),
        grid=(n // tm, c // tn),
        in_specs=[pl.BlockSpec((tm, k), lambda i, j: (i, 0)),
                  pl.BlockSpec((k, tn), lambda i, j: (0, j)),
                  pl.BlockSpec((1, tn), lambda i, j: (0, j))],
        out_specs=pl.BlockSpec((tm, tn), lambda i, j: (i, j)),
        out_shape=jax.ShapeDtypeStruct((n, c), out_dtype),
        compiler_params=_cp(("arbitrary", "arbitrary")),
        name="proj_matmul",
    )(x, w, bias)


def _split_bf16(x, parts):
    out = []
    for _ in range(parts):
        h = x.astype(BF16)
        out.append(h)
        x = x - h.astype(F32)
    return out


def _mla_prep_kernel(sm_ref, qn_ref, kvn_ref, wqa_ref, wqb_ref, wk_ref, wv_ref,
                     cos_ref, sin_ref, bf_ref, q_ref, k_ref, v_ref, lf_ref):
    sm = sm_ref[...]
    dq = sm[:, :MLA_Q_RANK]
    dkv = sm[:, MLA_Q_RANK:MLA_Q_RANK + MLA_KV_RANK]
    o = MLA_Q_RANK + MLA_KV_RANK
    kr = sm[:, o:o + LANES]
    kr_rot = sm[:, o + LANES:o + 2 * LANES]
    fg = sm[:, o + 2 * LANES:o + 3 * LANES]

    cq = dq * lax.rsqrt(jnp.mean(dq * dq, axis=-1, keepdims=True) + RMS_EPS) * qn_ref[...]
    ckv = dkv * lax.rsqrt(jnp.mean(dkv * dkv, axis=-1, keepdims=True) + RMS_EPS) * kvn_ref[...]
    cq = cq.astype(BF16)
    ckv = ckv.astype(BF16)

    cosk = cos_ref[...]
    sink = sin_ref[...]
    lane = lax.broadcasted_iota(jnp.int32, (1, LANES), 1)
    nope = (lane < MLA_NOPE_DIM).astype(F32)
    scale = MLA_QK_DIM ** -0.5
    cq_tab = jnp.concatenate([(cosk + nope) * scale] * MLA_HEADS, axis=1)
    sq_tab = jnp.concatenate([sink * scale] * MLA_HEADS, axis=1)

    qa = jnp.dot(cq, wqa_ref[...], preferred_element_type=F32)
    qb = jnp.dot(cq, wqb_ref[...], preferred_element_type=F32)
    q_ref[...] = (qa * cq_tab + qb * sq_tab).astype(q_ref.dtype)

    k_rope = kr * cosk + kr_rot * sink
    ka = jnp.dot(ckv, wk_ref[...], preferred_element_type=F32)
    k_ref[...] = (ka + jnp.concatenate([k_rope] * MLA_HEADS, axis=1)).astype(k_ref.dtype)
    v_ref[...] = jnp.dot(ckv, wv_ref[...], preferred_element_type=F32).astype(v_ref.dtype)
    lf_ref[...] = jax.nn.log_sigmoid(fg + bf_ref[...])


def _mla_prep(small, qn, kvn, wqa, wqb, wk, wv, cos_t, sin_t, bf, seq, tm=512):
    n = small.shape[0]
    tm = min(tm, seq)
    sblocks = seq // tm
    hw = MLA_HEADS * LANES
    full = lambda a: pl.BlockSpec(a.shape, lambda i: (0,) * a.ndim)
    return pl.pallas_call(
        _mla_prep_kernel,
        grid=(n // tm,),
        in_specs=[pl.BlockSpec((tm, SMALL_W), lambda i: (i, 0)),
                  full(qn), full(kvn), full(wqa), full(wqb), full(wk), full(wv),
                  pl.BlockSpec((tm, LANES), lambda i: (i % sblocks, 0)),
                  pl.BlockSpec((tm, LANES), lambda i: (i % sblocks, 0)),
                  full(bf)],
        out_specs=[pl.BlockSpec((tm, hw), lambda i: (i, 0)),
                   pl.BlockSpec((tm, hw), lambda i: (i, 0)),
                   pl.BlockSpec((tm, MLA_HEADS * MLA_V_DIM), lambda i: (i, 0)),
                   pl.BlockSpec((tm, LANES), lambda i: (i, 0))],
        out_shape=[jax.ShapeDtypeStruct((n, hw), BF16),
                   jax.ShapeDtypeStruct((n, hw), BF16),
                   jax.ShapeDtypeStruct((n, MLA_HEADS * MLA_V_DIM), BF16),
                   jax.ShapeDtypeStruct((n, LANES), F32)],
        compiler_params=_cp(("arbitrary",)),
        name="mla_prep",
    )(small, qn, kvn, wqa, wqb, wk, wv, cos_t, sin_t, bf)


def _cumsum_kernel(lf_ref, cum_ref, cumt_ref, *, seq):
    r = lax.broadcasted_iota(jnp.int32, (LANES, LANES), 0)
    c = lax.broadcasted_iota(jnp.int32, (LANES, LANES), 1)
    lower = (c <= r).astype(BF16)

    def body(j, carry):
        start = pl.multiple_of(j * LANES, LANES)
        blk = lf_ref[0, pl.ds(start, LANES), :]
        acc = carry
        for part in _split_bf16(blk, 3):
            acc = acc + jnp.dot(lower, part, preferred_element_type=F32)
        cum_ref[0, pl.ds(start, LANES), :] = acc
        cumt_ref[0, :, pl.ds(start, LANES)] = acc.T[:SUBLANES, :]
        return jnp.broadcast_to(acc[LANES - 1:LANES, :], (LANES, LANES))

    lax.fori_loop(0, seq // LANES, body, jnp.zeros((LANES, LANES), F32))


def _cumsum(lf3):
    b, seq, _ = lf3.shape
    return pl.pallas_call(
        functools.partial(_cumsum_kernel, seq=seq),
        grid=(b,),
        in_specs=[pl.BlockSpec((1, seq, LANES), lambda i: (i, 0, 0))],
        out_specs=[pl.BlockSpec((1, seq, LANES), lambda i: (i, 0, 0)),
                   pl.BlockSpec((1, SUBLANES, seq), lambda i: (i, 0, 0))],
        out_shape=[jax.ShapeDtypeStruct((b, seq, LANES), F32),
                   jax.ShapeDtypeStruct((b, SUBLANES, seq), F32)],
        compiler_params=_cp(("arbitrary",)),
        name="forget_cumsum",
    )(lf3)


_NT = (((1,), (1,)), ((), ()))
NEG_INIT = -1e30
_TN = (((0,), (0,)), ((), ()))
ATTN_TK = 256


def _head_halves(q):
    lane = lax.broadcasted_iota(jnp.int32, (1, LANES), 1)
    zero = jnp.zeros_like(q)
    return [jnp.where(lane < HEAD_DIM, q, zero), jnp.where(lane >= HEAD_DIM, q, zero)]


def _mask_first_block(x, keep, fill, tk):
    head = jnp.where(keep, x[:, :tk], fill)
    return head if x.shape[1] == tk else jnp.concatenate([head, x[:, tk:]], axis=1)


def _fox_kernel(q_ref, k_ref, v_ref, cc_ref, cr_ref, o_ref, *, seq, tk):
    hp = pl.program_id(1)
    lane = lax.broadcasted_iota(jnp.int32, (1, LANES), 1)
    qs = _head_halves(q_ref[0])
    key = lax.broadcasted_iota(jnp.int32, (tk, tk), 0)
    qry = lax.broadcasted_iota(jnp.int32, (tk, tk), 1)
    causal = key <= qry
    carry = [(jnp.full((1, seq), NEG_INIT, F32), jnp.zeros((1, seq), F32),
              jnp.zeros((LANES, seq), F32)) for _ in range(2)]
    for j in range(seq // tk):
        q0 = j * tk
        k = k_ref[0, q0:q0 + tk, :]
        v = v_ref[0, q0:q0 + tk, :]
        cc = cc_ref[0, q0:q0 + tk, :]
        for hh in range(2):
            m, l, acc = carry[hh]
            cs = jnp.sum(jnp.where(lane == 2 * hp + hh, cc, 0.0), axis=1, keepdims=True)
            ct = cr_ref[0, hh, :, q0:]
            st = lax.dot_general(k, qs[hh][q0:, :], _NT, preferred_element_type=F32)
            st = _mask_first_block((st + ct) - cs, causal, -jnp.inf, tk)
            mn, ln, an = _softmax_step_t(st, m[:, q0:], l[:, q0:], acc[:, q0:], v)
            if q0:
                mn = jnp.concatenate([m[:, :q0], mn], axis=1)
                ln = jnp.concatenate([l[:, :q0], ln], axis=1)
                an = jnp.concatenate([acc[:, :q0], an], axis=1)
            carry[hh] = (mn, ln, an)
    _pair_out_t([carry[0][2], carry[1][2]], [carry[0][1], carry[1][1]], o_ref)


def _fox_attention(qkv3, cum, cumt4):
    b, seq, _ = qkv3.shape
    tk = min(ATTN_TK, seq)
    hp = FOX_HEADS // 2
    qoff = OFF_FOX // LANES
    return pl.pallas_call(
        functools.partial(_fox_kernel, seq=seq, tk=tk),
        grid=(b, hp),
        in_specs=[pl.BlockSpec((1, seq, LANES), lambda bi, h: (bi, 0, qoff + h)),
                  pl.BlockSpec((1, seq, LANES), lambda bi, h: (bi, 0, qoff + hp + h)),
                  pl.BlockSpec((1, seq, LANES), lambda bi, h: (bi, 0, qoff + 2 * hp + h)),
                  pl.BlockSpec((1, seq, LANES), lambda bi, h: (bi, 0, 0)),
                  pl.BlockSpec((1, 2, 1, seq), lambda bi, h: (bi, h, 0, 0))],
        out_specs=pl.BlockSpec((1, seq, LANES), lambda bi, h: (bi, 0, h)),
        out_shape=jax.ShapeDtypeStruct((b, seq, FOX_W), BF16),
        compiler_params=_cp(("arbitrary", "arbitrary")),
        name="fox_attention",
    )(qkv3, qkv3, qkv3, cum, cumt4)


def _softmax_step_t(st, m, l, acc, v):
    m_new = jnp.maximum(m, jnp.max(st, axis=0, keepdims=True))
    alpha = jnp.exp(m - m_new)
    p = jnp.exp(st - m_new)
    l = alpha * l + jnp.sum(p, axis=0, keepdims=True)
    acc = alpha * acc + lax.dot_general(v, p.astype(BF16), _TN, preferred_element_type=F32)
    return m_new, l, acc


def _pair_out_t(accs, ls, o_ref):
    sub = lax.broadcasted_iota(jnp.int32, (LANES, 1), 0)
    ot = jnp.where(sub < HEAD_DIM, accs[0] / ls[0], accs[1] / ls[1])
    o_ref[0] = ot.T.astype(o_ref.dtype)


def _mla_kernel(q_ref, k_ref, v_ref, o_ref, *, seq, tk):
    q = q_ref[0]
    qs = [q[:, :LANES], q[:, LANES:]]
    key = lax.broadcasted_iota(jnp.int32, (tk, tk), 0)
    qry = lax.broadcasted_iota(jnp.int32, (tk, tk), 1)
    shift = CHUNK.bit_length() - 1
    chunk_causal = (key >> shift) <= (qry >> shift)
    carry = [(jnp.full((1, seq), NEG_INIT, F32), jnp.zeros((1, seq), F32),
              jnp.zeros((LANES, seq), F32)) for _ in range(2)]
    for j in range(seq // tk):
        q0 = j * tk
        k = k_ref[0, q0:q0 + tk, :]
        v = v_ref[0, q0:q0 + tk, :]
        for hh in range(2):
            m, l, acc = carry[hh]
            st = lax.dot_general(k[:, hh * LANES:(hh + 1) * LANES], qs[hh][q0:, :], _NT,
                                 preferred_element_type=F32)
            st = _mask_first_block(st, chunk_causal, -jnp.inf, tk)
            mn, ln, an = _softmax_step_t(st, m[:, q0:], l[:, q0:], acc[:, q0:], v)
            if q0:
                mn = jnp.concatenate([m[:, :q0], mn], axis=1)
                ln = jnp.concatenate([l[:, :q0], ln], axis=1)
                an = jnp.concatenate([acc[:, :q0], an], axis=1)
            carry[hh] = (mn, ln, an)
    _pair_out_t([carry[0][2], carry[1][2]], [carry[0][1], carry[1][1]], o_ref)


def _mla_attention(q3, k3, v3):
    b, seq, _ = q3.shape
    tk = min(ATTN_TK, seq)
    hp = MLA_HEADS // 2
    return pl.pallas_call(
        functools.partial(_mla_kernel, seq=seq, tk=tk),
        grid=(b, hp),
        in_specs=[pl.BlockSpec((1, seq, 2 * LANES), lambda bi, h: (bi, 0, h)),
                  pl.BlockSpec((1, seq, 2 * LANES), lambda bi, h: (bi, 0, h)),
                  pl.BlockSpec((1, seq, LANES), lambda bi, h: (bi, 0, h))],
        out_specs=pl.BlockSpec((1, seq, LANES), lambda bi, h: (bi, 0, h)),
        out_shape=jax.ShapeDtypeStruct((b, seq, MLA_HEADS * MLA_V_DIM), BF16),
        compiler_params=_cp(("arbitrary", "arbitrary")),
        name="mla_attention",
    )(q3, k3, v3)


def _softplus(z):
    return jnp.maximum(z, 0.0) + jnp.log(1.0 + jnp.exp(-jnp.abs(z)))


def _sb_kernel(q_ref, k_ref, v_ref, o_ref, run_ref, acc_ref, *, seq, tk):
    n = seq // tk
    qs = _head_halves(q_ref[0])
    key = lax.broadcasted_iota(jnp.int32, (tk, tk), 0)
    qry = lax.broadcasted_iota(jnp.int32, (tk, tk), 1)
    strict = key < qry
    later = (qry > key).astype(BF16)

    def unit(j, qlo, qhi, diag):
        k = k_ref[0, j * tk:(j + 1) * tk, :]
        v = v_ref[0, j * tk:(j + 1) * tk, :]
        for hh in range(2):
            z = lax.dot_general(k, qs[hh][qlo:qhi, :], _NT, preferred_element_type=F32)
            sp = _softplus(z)
            log_keep = -sp
            if diag:
                log_keep = _mask_first_block(log_keep, strict, 0.0, tk)
            between = run_ref[hh, :, qlo:qhi]
            for part in _split_bf16(log_keep, 2):
                between = between + jnp.dot(later, part, preferred_element_type=F32)
            w = jnp.exp((z - sp) + between)
            if diag:
                w = _mask_first_block(w, strict, 0.0, tk)
            acc_ref[hh, :, qlo:qhi] += lax.dot_general(v, w.astype(BF16), _TN,
                                                       preferred_element_type=F32)
            run_ref[hh, :, qlo:qhi] += jnp.sum(log_keep, axis=0, keepdims=True)

    run_ref[...] = jnp.zeros_like(run_ref)
    acc_ref[...] = jnp.zeros_like(acc_ref)
    for j in range(n - 1, -1, -1):
        unit(j, j * tk, min((j + 2) * tk, seq), True)

    for j in range(n - 3, -1, -1):
        qlo = (j + 2) * tk

        @pl.when(jnp.max(run_ref[:, :, qlo:]) >= SB_CUTOFF)
        def _():
            unit(j, qlo, seq, False)

    sub = lax.broadcasted_iota(jnp.int32, (LANES, 1), 0)
    o_ref[0] = jnp.where(sub < HEAD_DIM, acc_ref[0], acc_ref[1]).T.astype(o_ref.dtype)


def _sb_attention(qkv3):
    b, seq, _ = qkv3.shape
    tk = min(ATTN_TK, seq)
    hp = SB_HEADS // 2
    qoff = OFF_SB // LANES
    return pl.pallas_call(
        functools.partial(_sb_kernel, seq=seq, tk=tk),
        grid=(b, hp),
        in_specs=[pl.BlockSpec((1, seq, LANES), lambda bi, h: (bi, 0, qoff + h)),
                  pl.BlockSpec((1, seq, LANES), lambda bi, h: (bi, 0, qoff + hp + h)),
                  pl.BlockSpec((1, seq, LANES), lambda bi, h: (bi, 0, qoff + 2 * hp + h))],
        out_specs=pl.BlockSpec((1, seq, LANES), lambda bi, h: (bi, 0, h)),
        out_shape=jax.ShapeDtypeStruct((b, seq, SB_W), BF16),
        scratch_shapes=[pltpu.VMEM((2, 1, seq), F32), pltpu.VMEM((2, LANES, seq), F32)],
        compiler_params=_cp(("arbitrary", "arbitrary")),
        name="sb_attention",
    )(qkv3, qkv3, qkv3)


def _layer_norm(y, g, b):
    yc = y - jnp.mean(y, axis=-1, keepdims=True)
    var = jnp.mean(yc * yc, axis=-1, keepdims=True)
    return yc * lax.rsqrt(var + LN_EPS) * g + b


HALF = D_MODEL // 2
U32 = jnp.uint32


def _pack_halves(y):
    bits = lax.bitcast_convert_type(y.astype(BF16).astype(F32), U32)
    return (bits[:, :HALF] >> 16) | (bits[:, HALF:] & U32(0xFFFF0000))


def _unpack_halves(w):
    lo = lax.bitcast_convert_type(w << 16, F32)
    hi = lax.bitcast_convert_type(w & U32(0xFFFF0000), F32)
    return lo, hi


def _merge_kernel(osb_ref, ofox_ref, omla_ref, g_ref, x_ref, wsb_ref, wfox_ref, wmla_ref,
                  wout_ref, lg_ref, lb_ref, x1_ref, x1b_ref, x1p_ref):
    g = g_ref[...].astype(F32)
    a = jnp.dot(osb_ref[...], wsb_ref[...], preferred_element_type=F32)
    b = jnp.dot(ofox_ref[...], wfox_ref[...], preferred_element_type=F32)
    c = jnp.dot(omla_ref[...], wmla_ref[...], preferred_element_type=F32)
    mixed = (g[:, :D_MODEL] * a + g[:, D_MODEL:2 * D_MODEL] * b + g[:, 2 * D_MODEL:] * c)
    mix = jnp.dot(mixed.astype(BF16), wout_ref[...], preferred_element_type=F32)
    y = _layer_norm(DN_ALPHA * x_ref[...] + mix, lg_ref[...], lb_ref[...])
    x1_ref[...] = y
    x1b_ref[...] = y.astype(BF16)
    x1p_ref[...] = _pack_halves(y)


def _merge(o_sb, o_fox, o_mla, gates, x, wsb, wfox, wmla, wout, lg, lb, tm=512):
    n = x.shape[0]
    tm = min(tm, n)
    full = lambda a: pl.BlockSpec(a.shape, lambda i: (0,) * a.ndim)
    rows = lambda w: pl.BlockSpec((tm, w), lambda i: (i, 0))
    return pl.pallas_call(
        _merge_kernel,
        grid=(n // tm,),
        in_specs=[rows(SB_W), rows(FOX_W), rows(MLA_HEADS * MLA_V_DIM), rows(N_BRANCHES * D_MODEL),
                  rows(D_MODEL), full(wsb), full(wfox), full(wmla), full(wout), full(lg), full(lb)],
        out_specs=[rows(D_MODEL), rows(D_MODEL), rows(HALF)],
        out_shape=[jax.ShapeDtypeStruct((n, D_MODEL), F32),
                   jax.ShapeDtypeStruct((n, D_MODEL), BF16),
                   jax.ShapeDtypeStruct((n, HALF), U32)],
        compiler_params=_cp(("arbitrary",)),
        name="merge_outproj_ln",
    )(o_sb, o_fox, o_mla, gates, x, wsb, wfox, wmla, wout, lg, lb)


def _route_kernel(x_ref, wr_ref, rb_ref, idx_ref, w_ref, rank_ref, cnt_ref, run_ref, *, tm):
    step = pl.program_id(0)

    @pl.when(step == 0)
    def _():
        run_ref[...] = jnp.zeros_like(run_ref)

    logits = lax.dot_general(wr_ref[...], x_ref[...], _NT, preferred_element_type=F32)
    scores = jax.nn.sigmoid(logits)
    biased = scores + rb_ref[...]
    e_iota = lax.broadcasted_iota(jnp.int32, (N_EXPERTS, tm), 0)
    big = jnp.int32(1 << 20)

    g_iota = lax.broadcasted_iota(jnp.int32, (GROUP_SIZE, tm), 0)
    gs_rows = []
    for g in range(N_GROUPS):
        blk = biased[g * GROUP_SIZE:(g + 1) * GROUP_SIZE, :]
        m1 = jnp.max(blk, axis=0, keepdims=True)
        first = jnp.min(jnp.where(blk == m1, g_iota, big), axis=0, keepdims=True)
        m2 = jnp.max(jnp.where(g_iota == first, -jnp.inf, blk), axis=0, keepdims=True)
        gs_rows.append(m1 + m2)
    gs = jnp.concatenate(gs_rows, axis=0)
    n_iota = lax.broadcasted_iota(jnp.int32, (N_GROUPS, tm), 0)
    keep = jnp.zeros((N_GROUPS, tm), jnp.bool_)
    for _ in range(TOPK_GROUPS):
        m = jnp.max(gs, axis=0, keepdims=True)
        first = jnp.min(jnp.where(gs == m, n_iota, big), axis=0, keepdims=True)
        hit = n_iota == first
        keep = jnp.logical_or(keep, hit)
        gs = jnp.where(hit, -jnp.inf, gs)
    keep_f = keep.astype(F32)
    expert_keep = jnp.concatenate(
        [jnp.broadcast_to(keep_f[g:g + 1, :], (GROUP_SIZE, tm)) for g in range(N_GROUPS)], axis=0)
    masked = jnp.where(expert_keep > 0.5, biased, -jnp.inf)

    idx_rows, w_rows, hits = [], [], []
    sel = jnp.zeros((N_EXPERTS, tm), F32)
    for _ in range(TOP_K):
        m = jnp.max(masked, axis=0, keepdims=True)
        first = jnp.min(jnp.where(masked == m, e_iota, big), axis=0, keepdims=True)
        hit = e_iota == first
        idx_rows.append(first)
        w_rows.append(jnp.sum(jnp.where(hit, scores, 0.0), axis=0, keepdims=True))
        hits.append(hit)
        sel = sel + hit.astype(F32)
        masked = jnp.where(hit, -jnp.inf, masked)
    w = jnp.concatenate(w_rows, axis=0)
    w = w / jnp.sum(w, axis=0, keepdims=True) * ROUTED_SCALE
    idx_ref[...] = jnp.concatenate(idx_rows, axis=0)
    w_ref[...] = w

    r = lax.broadcasted_iota(jnp.int32, (tm, tm), 0)
    c = lax.broadcasted_iota(jnp.int32, (tm, tm), 1)
    earlier = (r < c).astype(BF16)
    prefix = jnp.dot(sel.astype(BF16), earlier, preferred_element_type=F32) + run_ref[...]
    rank_rows = [jnp.sum(jnp.where(h, prefix, 0.0), axis=0, keepdims=True) for h in hits]
    rank_ref[...] = jnp.concatenate(rank_rows, axis=0).astype(jnp.int32)
    total = run_ref[...] + jnp.sum(sel, axis=1, keepdims=True)
    run_ref[...] = total
    cnt_ref[...] = jnp.broadcast_to(total, (N_EXPERTS, LANES))


def _route(x1b, wr_t, rb, tm=256):
    n = x1b.shape[0]
    tm = min(tm, n)
    return pl.pallas_call(
        functools.partial(_route_kernel, tm=tm),
        grid=(n // tm,),
        in_specs=[pl.BlockSpec((tm, D_MODEL), lambda i: (i, 0)),
                  pl.BlockSpec((N_EXPERTS, D_MODEL), lambda i: (0, 0)),
                  pl.BlockSpec((N_EXPERTS, 1), lambda i: (0, 0))],
        out_specs=[pl.BlockSpec((TOP_K, tm), lambda i: (0, i)),
                   pl.BlockSpec((TOP_K, tm), lambda i: (0, i)),
                   pl.BlockSpec((TOP_K, tm), lambda i: (0, i)),
                   pl.BlockSpec((N_EXPERTS, LANES), lambda i: (0, 0))],
        out_shape=[jax.ShapeDtypeStruct((TOP_K, n), jnp.int32),
                   jax.ShapeDtypeStruct((TOP_K, n), F32),
                   jax.ShapeDtypeStruct((TOP_K, n), jnp.int32),
                   jax.ShapeDtypeStruct((N_EXPERTS, LANES), F32)],
        scratch_shapes=[pltpu.VMEM((N_EXPERTS, 1), F32)],
        compiler_params=_cp(("arbitrary",)),
        name="router_topk",
    )(x1b, wr_t, rb)


def _dest_kernel(idx_ref, rank_ref, st_ref, dest_ref, *, tm):
    e_iota = lax.broadcasted_iota(jnp.int32, (N_EXPERTS, tm), 0)
    starts = st_ref[...]
    rows = []
    for k in range(TOP_K):
        hit = e_iota == idx_ref[k:k + 1, :]
        rows.append(jnp.sum(jnp.where(hit, starts, 0.0), axis=0, keepdims=True))
    dest_ref[...] = jnp.concatenate(rows, axis=0).astype(jnp.int32) + rank_ref[...]


def _dest(idx_t, rank_t, starts, tm=512):
    n = idx_t.shape[1]
    tm = min(tm, n)
    blk = lambda: pl.BlockSpec((TOP_K, tm), lambda i: (0, i))
    return pl.pallas_call(
        functools.partial(_dest_kernel, tm=tm),
        grid=(n // tm,),
        in_specs=[blk(), blk(), pl.BlockSpec((N_EXPERTS, 1), lambda i: (0, 0))],
        out_specs=blk(),
        out_shape=jax.ShapeDtypeStruct((TOP_K, n), jnp.int32),
        compiler_params=_cp(("arbitrary",)),
        name="moe_dest",
    )(idx_t, rank_t, starts.astype(F32).reshape(N_EXPERTS, 1))


SC_CORES = 2
SC_SUBCORES = 16
SC_WINDOW = 128


def _sc_mesh():
    return plsc.VectorSubcoreMesh(core_axis_name="c", subcore_axis_name="s",
                                  num_cores=SC_CORES, num_subcores=SC_SUBCORES)


def _sc_windows(n):
    workers = SC_CORES * SC_SUBCORES
    assert n % (SC_WINDOW * workers) == 0, n
    return n // SC_WINDOW // workers


def _sc_dispatch(dest_win, x1p):
    n = x1p.shape[0]
    per_worker = _sc_windows(n)

    @functools.partial(
        pl.kernel, mesh=_sc_mesh(),
        out_type=jax.ShapeDtypeStruct((n * TOP_K, HALF), U32),
        scratch_types=[pltpu.VMEM((TOP_K, SC_WINDOW), jnp.int32),
                       pltpu.VMEM((SC_WINDOW, HALF), U32)],
        name="moe_dispatch_sc")
    def run(dest_hbm, x_hbm, xs_hbm, idx_v, rows_v):
        worker = lax.axis_index("s") * SC_CORES + lax.axis_index("c")

        @pl.loop(0, per_worker)
        def _(c):
            win = worker * per_worker + c
            pltpu.sync_copy(dest_hbm.at[win], idx_v)
            pltpu.sync_copy(x_hbm.at[pl.ds(pl.multiple_of(win * SC_WINDOW, SC_WINDOW),
                                           SC_WINDOW)], rows_v)
            for k in range(TOP_K):
                pltpu.sync_copy(rows_v, xs_hbm.at[idx_v.at[k]])

    return run(dest_win, x1p)


def _sc_gather(dest_win, ys, n):
    per_worker = _sc_windows(n)

    @functools.partial(
        pl.kernel, mesh=_sc_mesh(),
        out_type=jax.ShapeDtypeStruct((TOP_K, n, HALF), U32),
        scratch_types=[pltpu.VMEM((TOP_K, SC_WINDOW), jnp.int32),
                       pltpu.VMEM((SC_WINDOW, HALF), U32)],
        name="moe_gather_sc")
    def run(dest_hbm, ys_hbm, yt_hbm, idx_v, rows_v):
        worker = lax.axis_index("s") * SC_CORES + lax.axis_index("c")

        @pl.loop(0, per_worker)
        def _(c):
            win = worker * per_worker + c
            start = pl.multiple_of(win * SC_WINDOW, SC_WINDOW)
            pltpu.sync_copy(dest_hbm.at[win], idx_v)
            for k in range(TOP_K):
                pltpu.sync_copy(ys_hbm.at[idx_v.at[k]], rows_v)
                pltpu.sync_copy(rows_v, yt_hbm.at[k, pl.ds(start, SC_WINDOW)])

    return run(dest_win, ys)


def _expert_kernel(tile_ref, exp_ref, lo_ref, hi_ref, xs_ref, wg_ref, wu_ref, wd_ref, ys_ref,
                   *, bm):
    w = pl.program_id(0)
    lo = lo_ref[w]
    hi = hi_ref[w]

    @pl.when(hi > lo)
    def _():
        x_lo, x_hi = _unpack_halves(xs_ref[...])
        x = jnp.concatenate([x_lo, x_hi], axis=1).astype(BF16)
        g = jnp.dot(x, wg_ref[0, 0].astype(BF16), preferred_element_type=F32)
        u = jnp.dot(x, wu_ref[0, 0].astype(BF16), preferred_element_type=F32)
        h = (g * jax.nn.sigmoid(g) * u).astype(BF16)
        y = _pack_halves(jnp.dot(h, wd_ref[0, 0].astype(BF16), preferred_element_type=F32))

        @pl.when(lo == 0)
        def _():
            ys_ref[...] = y

        @pl.when(lo > 0)
        def _():
            rows = lax.broadcasted_iota(jnp.int32, (bm, 1), 0)
            ys_ref[...] = jnp.where(rows >= lo, y, ys_ref[...])


def _experts(layer, item_tile, item_exp, item_lo, item_hi, xs, wg, wu, wd, bm):
    r = xs.shape[0]
    n_items = item_tile.shape[0]
    grid_spec = pltpu.PrefetchScalarGridSpec(
        num_scalar_prefetch=4,
        grid=(n_items,),
        in_specs=[pl.BlockSpec((bm, HALF), lambda w, t, e, lo, hi: (t[w], 0)),
                  pl.BlockSpec((1, 1, D_MODEL, EXPERT_FF),
                               lambda w, t, e, lo, hi: (layer, e[w], 0, 0)),
                  pl.BlockSpec((1, 1, D_MODEL, EXPERT_FF),
                               lambda w, t, e, lo, hi: (layer, e[w], 0, 0)),
                  pl.BlockSpec((1, 1, EXPERT_FF, D_MODEL),
                               lambda w, t, e, lo, hi: (layer, e[w], 0, 0))],
        out_specs=pl.BlockSpec((bm, HALF), lambda w, t, e, lo, hi: (t[w], 0)),
    )
    return pl.pallas_call(
        functools.partial(_expert_kernel, bm=bm),
        grid_spec=grid_spec,
        out_shape=jax.ShapeDtypeStruct((r, HALF), U32),
        compiler_params=_cp(("arbitrary",)),
        name="moe_experts",
    )(item_tile, item_exp, item_lo, item_hi, xs, wg, wu, wd)


def _expert_items(counts, n_rows, bm):
    n_tiles = n_rows // bm
    ends = jnp.cumsum(counts)
    starts = ends - counts
    bounds = jnp.sort(jnp.concatenate([jnp.arange(n_tiles, dtype=jnp.int32) * bm,
                                       starts.astype(jnp.int32)]))
    nxt = jnp.concatenate([bounds[1:], jnp.array([n_rows], jnp.int32)])
    tile = jnp.minimum(bounds // bm, n_tiles - 1)
    exp = jnp.minimum(jnp.searchsorted(ends, bounds, side="right"), N_EXPERTS - 1).astype(jnp.int32)
    lo = bounds - tile * bm
    hi = nxt - tile * bm
    return tile.astype(jnp.int32), exp, lo.astype(jnp.int32), hi.astype(jnp.int32), starts


def _combine_rows_kernel(yt_ref, tw_ref, x1_ref, x1b_ref, wsg_ref, wsu_ref, wsd_ref, lg_ref, lb_ref,
                         o_ref, ob_ref):
    tw = tw_ref[...]
    r_lo = jnp.zeros((tw.shape[0], HALF), F32)
    r_hi = jnp.zeros((tw.shape[0], HALF), F32)
    for k in range(TOP_K):
        y_lo, y_hi = _unpack_halves(yt_ref[k])
        r_lo = r_lo + tw[:, k:k + 1] * y_lo
        r_hi = r_hi + tw[:, k:k + 1] * y_hi
    routed = jnp.concatenate([r_lo, r_hi], axis=1)
    xb = x1b_ref[...]
    g = jnp.dot(xb, wsg_ref[...], preferred_element_type=F32)
    u = jnp.dot(xb, wsu_ref[...], preferred_element_type=F32)
    h = (g * jax.nn.sigmoid(g) * u).astype(BF16)
    shared = jnp.dot(h, wsd_ref[...], preferred_element_type=F32)
    y = _layer_norm(DN_ALPHA * x1_ref[...] + (shared + routed), lg_ref[...], lb_ref[...])
    o_ref[...] = y
    ob_ref[...] = y.astype(BF16)


def _combine_rows(yt, tw, x1, x1b, wsg, wsu, wsd, lg, lb, tc=256):
    n = x1.shape[0]
    tc = min(tc, n)
    full = lambda a: pl.BlockSpec(a.shape, lambda i: (0,) * a.ndim)
    rows = lambda w: pl.BlockSpec((tc, w), lambda i: (i, 0))
    return pl.pallas_call(
        _combine_rows_kernel,
        grid=(n // tc,),
        in_specs=[pl.BlockSpec((TOP_K, tc, HALF), lambda i: (0, i, 0)),
                  rows(TOP_K), rows(D_MODEL), rows(D_MODEL),
                  full(wsg), full(wsu), full(wsd), full(lg), full(lb)],
        out_specs=[rows(D_MODEL), rows(D_MODEL)],
        out_shape=[jax.ShapeDtypeStruct((n, D_MODEL), F32),
                   jax.ShapeDtypeStruct((n, D_MODEL), BF16)],
        compiler_params=_cp(("arbitrary",)),
        name="moe_combine_ln",
    )(yt, tw, x1, x1b, wsg, wsu, wsd, lg, lb)


def _head_cols(w, heads, width, lo, hi):
    return w.reshape(w.shape[0], heads, width)[:, :, lo:hi]


def _prep_layer(w_in, b_gate, b_forget, mla_q_norm, w_uq, mla_kv_norm, w_ukv):
    f = lambda a: a.astype(BF16)
    half = MLA_ROPE_DIM // 2
    d = w_in.shape[0]
    qscale = HEAD_DIM ** -0.5
    w_qkv = jnp.concatenate([
        w_in[:, OFF_SB:OFF_SB + SB_W] * qscale, w_in[:, OFF_SB + SB_W:OFF_FOX],
        w_in[:, OFF_FOX:OFF_FOX + FOX_W] * qscale, w_in[:, OFF_FOX + FOX_W:OFF_FGATE]], axis=1)
    w_kr = w_in[:, OFF_KR:OFF_GATE]
    w_kr_rot = jnp.concatenate([-w_kr[:, half:], w_kr[:, :half]], axis=1)
    z = lambda c: jnp.zeros((d, c), F32)
    pad_rope = LANES - MLA_QK_DIM
    w_small = jnp.concatenate([
        w_in[:, OFF_DQ:OFF_DKV], w_in[:, OFF_DKV:OFF_KR],
        z(MLA_NOPE_DIM), w_kr, z(pad_rope),
        z(MLA_NOPE_DIM), w_kr_rot, z(pad_rope),
        w_in[:, OFF_FGATE:OFF_DQ], z(LANES - FOX_HEADS)], axis=1)
    w_gate = w_in[:, OFF_GATE:]

    r = w_uq.shape[0]
    q_nope = _head_cols(w_uq, MLA_HEADS, MLA_QK_DIM, 0, MLA_NOPE_DIM)
    q_rope = _head_cols(w_uq, MLA_HEADS, MLA_QK_DIM, MLA_NOPE_DIM, MLA_QK_DIM)
    q_rope_rot = jnp.concatenate([-q_rope[:, :, half:], q_rope[:, :, :half]], axis=2)
    zq = lambda c: jnp.zeros((r, MLA_HEADS, c), F32)
    wqa = jnp.concatenate([q_nope, q_rope, zq(pad_rope)], axis=2).reshape(r, MLA_HEADS * LANES)
    wqb = jnp.concatenate([zq(MLA_NOPE_DIM), q_rope_rot, zq(pad_rope)], axis=2).reshape(
        r, MLA_HEADS * LANES)
    rk = w_ukv.shape[0]
    kvw = MLA_NOPE_DIM + MLA_V_DIM
    k_nope = _head_cols(w_ukv, MLA_HEADS, kvw, 0, MLA_NOPE_DIM)
    wk = jnp.concatenate([k_nope, jnp.zeros((rk, MLA_HEADS, LANES - MLA_NOPE_DIM), F32)],
                         axis=2).reshape(rk, MLA_HEADS * LANES)
    wv = _head_cols(w_ukv, MLA_HEADS, kvw, MLA_NOPE_DIM, kvw).reshape(rk, MLA_HEADS * MLA_V_DIM)
    bf = jnp.concatenate([b_forget, jnp.zeros((LANES - FOX_HEADS,), F32)]).reshape(1, LANES)
    return dict(w_qkv=f(w_qkv), w_small=f(w_small), w_gate=f(w_gate),
                b_gate=b_gate.reshape(1, -1), wqa=f(wqa), wqb=f(wqb), wk=f(wk), wv=f(wv),
                qn=mla_q_norm.reshape(1, -1), kvn=mla_kv_norm.reshape(1, -1), bf=bf)


def _rope_tables(seq):
    half = MLA_ROPE_DIM // 2
    inv_freq = jnp.power(ROPE_BASE, -jnp.arange(half, dtype=F32) / half)
    ang = jnp.arange(seq).astype(F32)[:, None] * inv_freq[None, :]
    cos = jnp.concatenate([jnp.cos(ang), jnp.cos(ang)], axis=1)
    sin = jnp.concatenate([jnp.sin(ang), jnp.sin(ang)], axis=1)
    pad = lambda t: jnp.concatenate([jnp.zeros((seq, MLA_NOPE_DIM), F32), t,
                                     jnp.zeros((seq, LANES - MLA_QK_DIM), F32)], axis=1)
    return pad(cos), pad(sin)


EXPERT_BM = 256


def kernel(x, ln1_g, ln1_b, ln2_g, ln2_b, w_in, b_gate, b_forget, mla_q_norm, w_uq, mla_kv_norm,
           w_ukv, w_proj_sb, w_proj_fox, w_proj_mla, w_out, w_router, router_bias,
           w_exp_gate, w_exp_up, w_exp_down, w_sh_gate, w_sh_up, w_sh_down):
    b, seq, d = x.shape
    n = b * seq
    depth = w_in.shape[0]
    cos_t, sin_t = _rope_tables(seq)
    xf = x.reshape(n, d)
    xb = xf.astype(BF16)
    f = lambda a: a.astype(BF16)
    bm = min(EXPERT_BM, n * TOP_K)
    for l in range(depth):
        p = _prep_layer(w_in[l], b_gate[l], b_forget[l], mla_q_norm[l], w_uq[l], mla_kv_norm[l],
                        w_ukv[l])
        qkv = _matmul(xb, p["w_qkv"], None, BF16)
        gates = _matmul(xb, p["w_gate"], p["b_gate"], BF16, act="sigmoid")
        small = _matmul(xb, p["w_small"], None, F32, tn=SMALL_W)
        q_mla, k_mla, v_mla, lf = _mla_prep(small, p["qn"], p["kvn"], p["wqa"], p["wqb"], p["wk"],
                                            p["wv"], cos_t, sin_t, p["bf"], seq)
        cum, cumt = _cumsum(lf.reshape(b, seq, LANES))
        cumt4 = cumt[:, :FOX_HEADS, :].reshape(b, FOX_HEADS, 1, seq)
        qkv3 = qkv.reshape(b, seq, QKV_W)
        o_sb = _sb_attention(qkv3).reshape(n, SB_W)
        o_fox = _fox_attention(qkv3, cum, cumt4).reshape(n, FOX_W)
        o_mla = _mla_attention(q_mla.reshape(b, seq, -1), k_mla.reshape(b, seq, -1),
                               v_mla.reshape(b, seq, -1)).reshape(n, -1)
        x1, x1b, x1p = _merge(o_sb, o_fox, o_mla, gates, xf, f(w_proj_sb[l]), f(w_proj_fox[l]),
                              f(w_proj_mla[l]), f(w_out[l]), ln1_g[l].reshape(1, d),
                              ln1_b[l].reshape(1, d))

        idx_t, tw_t, rank_t, cnt = _route(x1b, f(w_router[l].T), router_bias[l].reshape(-1, 1))
        counts = cnt[:, 0].astype(jnp.int32)
        item_tile, item_exp, item_lo, item_hi, starts = _expert_items(counts, n * TOP_K, bm)
        dest_t = _dest(idx_t, rank_t, starts)
        dest_win = dest_t.reshape(TOP_K, n // SC_WINDOW, SC_WINDOW).transpose(1, 0, 2)
        xs = _sc_dispatch(dest_win, x1p)
        ys = _experts(l, item_tile, item_exp, item_lo, item_hi, xs, w_exp_gate, w_exp_up,
                      w_exp_down, bm)
        yt = _sc_gather(dest_win, ys, n)
        xf, xb = _combine_rows(yt, tw_t.T, x1, x1b, f(w_sh_gate[l]), f(w_sh_up[l]),
                               f(w_sh_down[l]), ln2_g[l].reshape(1, d), ln2_b[l].reshape(1, d))
    return xf.reshape(b, seq, d)
```

```python
import functools

import jax
import jax.numpy as jnp
from jax import lax
from jax.experimental import pallas as pl
from jax.experimental.pallas import tpu as pltpu
from jax.experimental.pallas import tpu_sc as plsc

F32 = jnp.float32
BF16 = jnp.bfloat16

D_MODEL = 1024
HEAD_DIM = 64
SB_HEADS = 4
FOX_HEADS = 4
MLA_HEADS = 8
MLA_Q_RANK = 256
MLA_KV_RANK = 128
MLA_NOPE_DIM = 64
MLA_ROPE_DIM = 32
MLA_V_DIM = 64
ROPE_BASE = 10000.0
N_BRANCHES = 3
N_EXPERTS = 256
TOP_K = 8
N_GROUPS = 8
TOPK_GROUPS = 4
GROUP_SIZE = N_EXPERTS // N_GROUPS
EXPERT_FF = 256
SHARED_FF = 256
ROUTED_SCALE = 2.5
CHUNK = 64
LN_EPS = 1e-5
RMS_EPS = 1e-6
DEPTH = 2
DN_ALPHA = (2 * DEPTH) ** 0.25

SB_W = SB_HEADS * HEAD_DIM
FOX_W = FOX_HEADS * HEAD_DIM
MLA_QK_DIM = MLA_NOPE_DIM + MLA_ROPE_DIM
OFF_SB = 0
OFF_FOX = OFF_SB + 3 * SB_W
OFF_FGATE = OFF_FOX + 3 * FOX_W
OFF_DQ = OFF_FGATE + FOX_HEADS
OFF_DKV = OFF_DQ + MLA_Q_RANK
OFF_KR = OFF_DKV + MLA_KV_RANK
OFF_GATE = OFF_KR + MLA_ROPE_DIM

LANES = 128
SUBLANES = 8
QKV_W = 3 * SB_W + 3 * FOX_W
SMALL_W = MLA_Q_RANK + MLA_KV_RANK + 3 * LANES
SB_CUTOFF = -104.0

VMEM_LIMIT = 48 * 1024 * 1024


def _cp(sem, vmem=VMEM_LIMIT):
    return pltpu.CompilerParams(dimension_semantics=sem, vmem_limit_bytes=vmem)


def _mm_kernel(x_ref, w_ref, b_ref, o_ref, *, act):
    acc = jnp.dot(x_ref[...], w_ref[...], preferred_element_type=F32)
    if act == "sigmoid":
        acc = jax.nn.sigmoid(acc + b_ref[...])
    o_ref[...] = acc.astype(o_ref.dtype)


def _matmul(x, w, bias, out_dtype, act=None, tm=1024, tn=512):
    n, k = x.shape
    c = w.shape[1]
    tm = min(tm, n)
    tn = min(tn, c)
    if bias is None:
        bias = jnp.zeros((1, c), F32)
    return pl.pallas_call(
        functools.partial(_mm_kernel, act=act),
        grid=(n // tm, c // tn),
        in_specs=[pl.BlockSpec((tm, k), lambda i, j: (i, 0)),
                  pl.BlockSpec((k, tn), lambda i, j: (0, j)),
                  pl.BlockSpec((1, tn), lambda i, j: (0, j))],
        out_specs=pl.BlockSpec((tm, tn), lambda i, j: (i, j)),
        out_shape=jax.ShapeDtypeStruct((n, c), out_dtype),
        compiler_params=_cp(("arbitrary", "arbitrary")),
        name="proj_matmul",
    )(x, w, bias)


def _split_bf16(x, parts):
    out = []
    for _ in range(parts):
        h = x.astype(BF16)
        out.append(h)
        x = x - h.astype(F32)
    return out


def _mla_prep_kernel(sm_ref, qn_ref, kvn_ref, wqa_ref, wqb_ref, wk_ref, wv_ref,
                     cos_ref, sin_ref, bf_ref, q_ref, k_ref, v_ref, lf_ref):
    sm = sm_ref[...]
    dq = sm[:, :MLA_Q_RANK]
    dkv = sm[:, MLA_Q_RANK:MLA_Q_RANK + MLA_KV_RANK]
    o = MLA_Q_RANK + MLA_KV_RANK
    kr = sm[:, o:o + LANES]
    kr_rot = sm[:, o + LANES:o + 2 * LANES]
    fg = sm[:, o + 2 * LANES:o + 3 * LANES]

    cq = dq * lax.rsqrt(jnp.mean(dq * dq, axis=-1, keepdims=True) + RMS_EPS) * qn_ref[...]
    ckv = dkv * lax.rsqrt(jnp.mean(dkv * dkv, axis=-1, keepdims=True) + RMS_EPS) * kvn_ref[...]
    cq = cq.astype(BF16)
    ckv = ckv.astype(BF16)

    cosk = cos_ref[...]
    sink = sin_ref[...]
    lane = lax.broadcasted_iota(jnp.int32, (1, LANES), 1)
    nope = (lane < MLA_NOPE_DIM).astype(F32)
    scale = MLA_QK_DIM ** -0.5
    cq_tab = jnp.concatenate([(cosk + nope) * scale] * MLA_HEADS, axis=1)
    sq_tab = jnp.concatenate([sink * scale] * MLA_HEADS, axis=1)

    qa = jnp.dot(cq, wqa_ref[...], preferred_element_type=F32)
    qb = jnp.dot(cq, wqb_ref[...], preferred_element_type=F32)
    q_ref[...] = (qa * cq_tab + qb * sq_tab).astype(q_ref.dtype)

    k_rope = kr * cosk + kr_rot * sink
    ka = jnp.dot(ckv, wk_ref[...], preferred_element_type=F32)
    k_ref[...] = (ka + jnp.concatenate([k_rope] * MLA_HEADS, axis=1)).astype(k_ref.dtype)
    v_ref[...] = jnp.dot(ckv, wv_ref[...], preferred_element_type=F32).astype(v_ref.dtype)
    lf_ref[...] = jax.nn.log_sigmoid(fg + bf_ref[...])


def _mla_prep(small, qn, kvn, wqa, wqb, wk, wv, cos_t, sin_t, bf, seq, tm=512):
    n = small.shape[0]
    tm = min(tm, seq)
    sblocks = seq // tm
    hw = MLA_HEADS * LANES
    full = lambda a: pl.BlockSpec(a.shape, lambda i: (0,) * a.ndim)
    return pl.pallas_call(
        _mla_prep_kernel,
        grid=(n // tm,),
        in_specs=[pl.BlockSpec((tm, SMALL_W), lambda i: (i, 0)),
                  full(qn), full(kvn), full(wqa), full(wqb), full(wk), full(wv),
                  pl.BlockSpec((tm, LANES), lambda i: (i % sblocks, 0)),
                  pl.BlockSpec((tm, LANES), lambda i: (i % sblocks, 0)),
                  full(bf)],
        out_specs=[pl.BlockSpec((tm, hw), lambda i: (i, 0)),
                   pl.BlockSpec((tm, hw), lambda i: (i, 0)),
                   pl.BlockSpec((tm, MLA_HEADS * MLA_V_DIM), lambda i: (i, 0)),
                   pl.BlockSpec((tm, LANES), lambda i: (i, 0))],
        out_shape=[jax.ShapeDtypeStruct((n, hw), BF16),
                   jax.ShapeDtypeStruct((n, hw), BF16),
                   jax.ShapeDtypeStruct((n, MLA_HEADS * MLA_V_DIM), BF16),
                   jax.ShapeDtypeStruct((n, LANES), F32)],
        compiler_params=_cp(("arbitrary",)),
        name="mla_prep",
    )(small, qn, kvn, wqa, wqb, wk, wv, cos_t, sin_t, bf)


def _cumsum_kernel(lf_ref, cum_ref, cumt_ref, *, seq):
    r = lax.broadcasted_iota(jnp.int32, (LANES, LANES), 0)
    c = lax.broadcasted_iota(jnp.int32, (LANES, LANES), 1)
    lower = (c <= r).astype(BF16)

    def body(j, carry):
        start = pl.multiple_of(j * LANES, LANES)
        blk = lf_ref[0, pl.ds(start, LANES), :]
        acc = carry
        for part in _split_bf16(blk, 3):
            acc = acc + jnp.dot(lower, part, preferred_element_type=F32)
        cum_ref[0, pl.ds(start, LANES), :] = acc
        cumt_ref[0, :, pl.ds(start, LANES)] = acc.T[:SUBLANES, :]
        return jnp.broadcast_to(acc[LANES - 1:LANES, :], (LANES, LANES))

    lax.fori_loop(0, seq // LANES, body, jnp.zeros((LANES, LANES), F32))


def _cumsum(lf3):
    b, seq, _ = lf3.shape
    return pl.pallas_call(
        functools.partial(_cumsum_kernel, seq=seq),
        grid=(b,),
        in_specs=[pl.BlockSpec((1, seq, LANES), lambda i: (i, 0, 0))],
        out_specs=[pl.BlockSpec((1, seq, LANES), lambda i: (i, 0, 0)),
                   pl.BlockSpec((1, SUBLANES, seq), lambda i: (i, 0, 0))],
        out_shape=[jax.ShapeDtypeStruct((b, seq, LANES), F32),
                   jax.ShapeDtypeStruct((b, SUBLANES, seq), F32)],
        compiler_params=_cp(("arbitrary",)),
        name="forget_cumsum",
    )(lf3)


_NT = (((1,), (1,)), ((), ()))
NEG_INIT = -1e30
_TN = (((0,), (0,)), ((), ()))
ATTN_TK = 256


def _head_halves(q):
    lane = lax.broadcasted_iota(jnp.int32, (1, LANES), 1)
    zero = jnp.zeros_like(q)
    return [jnp.where(lane < HEAD_DIM, q, zero), jnp.where(lane >= HEAD_DIM, q, zero)]


def _mask_first_block(x, keep, fill, tk):
    head = jnp.where(keep, x[:, :tk], fill)
    return head if x.shape[1] == tk else jnp.concatenate([head, x[:, tk:]], axis=1)


def _fox_kernel(q_ref, k_ref, v_ref, cc_ref, cr_ref, o_ref, *, seq, tk):
    hp = pl.program_id(1)
    lane = lax.broadcasted_iota(jnp.int32, (1, LANES), 1)
    qs = _head_halves(q_ref[0])
    key = lax.broadcasted_iota(jnp.int32, (tk, tk), 0)
    qry = lax.broadcasted_iota(jnp.int32, (tk, tk), 1)
    causal = key <= qry
    carry = [(jnp.full((1, seq), NEG_INIT, F32), jnp.zeros((1, seq), F32),
              jnp.zeros((LANES, seq), F32)) for _ in range(2)]
    for j in range(seq // tk):
        q0 = j * tk
        k = k_ref[0, q0:q0 + tk, :]
        v = v_ref[0, q0:q0 + tk, :]
        cc = cc_ref[0, q0:q0 + tk, :]
        for hh in range(2):
            m, l, acc = carry[hh]
            cs = jnp.sum(jnp.where(lane == 2 * hp + hh, cc, 0.0), axis=1, keepdims=True)
            ct = cr_ref[0, hh, :, q0:]
            st = lax.dot_general(k, qs[hh][q0:, :], _NT, preferred_element_type=F32)
            st = _mask_first_block((st + ct) - cs, causal, -jnp.inf, tk)
            mn, ln, an = _softmax_step_t(st, m[:, q0:], l[:, q0:], acc[:, q0:], v)
            if q0:
                mn = jnp.concatenate([m[:, :q0], mn], axis=1)
                ln = jnp.concatenate([l[:, :q0], ln], axis=1)
                an = jnp.concatenate([acc[:, :q0], an], axis=1)
            carry[hh] = (mn, ln, an)
    _pair_out_t([carry[0][2], carry[1][2]], [carry[0][1], carry[1][1]], o_ref)


def _fox_attention(qkv3, cum, cumt4):
    b, seq, _ = qkv3.shape
    tk = min(ATTN_TK, seq)
    hp = FOX_HEADS // 2
    qoff = OFF_FOX // LANES
    return pl.pallas_call(
        functools.partial(_fox_kernel, seq=seq, tk=tk),
        grid=(b, hp),
        in_specs=[pl.BlockSpec((1, seq, LANES), lambda bi, h: (bi, 0, qoff + h)),
                  pl.BlockSpec((1, seq, LANES), lambda bi, h: (bi, 0, qoff + hp + h)),
                  pl.BlockSpec((1, seq, LANES), lambda bi, h: (bi, 0, qoff + 2 * hp + h)),
                  pl.BlockSpec((1, seq, LANES), lambda bi, h: (bi, 0, 0)),
                  pl.BlockSpec((1, 2, 1, seq), lambda bi, h: (bi, h, 0, 0))],
        out_specs=pl.BlockSpec((1, seq, LANES), lambda bi, h: (bi, 0, h)),
        out_shape=jax.ShapeDtypeStruct((b, seq, FOX_W), BF16),
        compiler_params=_cp(("arbitrary", "arbitrary")),
        name="fox_attention",
    )(qkv3, qkv3, qkv3, cum, cumt4)


def _softmax_step_t(st, m, l, acc, v):
    m_new = jnp.maximum(m, jnp.max(st, axis=0, keepdims=True))
    alpha = jnp.exp(m - m_new)
    p = jnp.exp(st - m_new)
    l = alpha * l + jnp.sum(p, axis=0, keepdims=True)
    acc = alpha * acc + lax.dot_general(v, p.astype(BF16), _TN, preferred_element_type=F32)
    return m_new, l, acc


def _pair_out_t(accs, ls, o_ref):
    sub = lax.broadcasted_iota(jnp.int32, (LANES, 1), 0)
    ot = jnp.where(sub < HEAD_DIM, accs[0] / ls[0], accs[1] / ls[1])
    o_ref[0] = ot.T.astype(o_ref.dtype)


def _mla_kernel(q_ref, k_ref, v_ref, o_ref, *, seq, tk):
    q = q_ref[0]
    qs = [q[:, :LANES], q[:, LANES:]]
    key = lax.broadcasted_iota(jnp.int32, (tk, tk), 0)
    qry = lax.broadcasted_iota(jnp.int32, (tk, tk), 1)
    shift = CHUNK.bit_length() - 1
    chunk_causal = (key >> shift) <= (qry >> shift)
    carry = [(jnp.full((1, seq), NEG_INIT, F32), jnp.zeros((1, seq), F32),
              jnp.zeros((LANES, seq), F32)) for _ in range(2)]
    for j in range(seq // tk):
        q0 = j * tk
        k = k_ref[0, q0:q0 + tk, :]
        v = v_ref[0, q0:q0 + tk, :]
        for hh in range(2):
            m, l, acc = carry[hh]
            st = lax.dot_general(k[:, hh * LANES:(hh + 1) * LANES], qs[hh][q0:, :], _NT,
                                 preferred_element_type=F32)
            st = _mask_first_block(st, chunk_causal, -jnp.inf, tk)
            mn, ln, an = _softmax_step_t(st, m[:, q0:], l[:, q0:], acc[:, q0:], v)
            if q0:
                mn = jnp.concatenate([m[:, :q0], mn], axis=1)
                ln = jnp.concatenate([l[:, :q0], ln], axis=1)
                an = jnp.concatenate([acc[:, :q0], an], axis=1)
            carry[hh] = (mn, ln, an)
    _pair_out_t([carry[0][2], carry[1][2]], [carry[0][1], carry[1][1]], o_ref)


def _mla_attention(q3, k3, v3):
    b, seq, _ = q3.shape
    tk = min(ATTN_TK, seq)
    hp = MLA_HEADS // 2
    return pl.pallas_call(
        functools.partial(_mla_kernel, seq=seq, tk=tk),
        grid=(b, hp),
        in_specs=[pl.BlockSpec((1, seq, 2 * LANES), lambda bi, h: (bi, 0, h)),
                  pl.BlockSpec((1, seq, 2 * LANES), lambda bi, h: (bi, 0, h)),
                  pl.BlockSpec((1, seq, LANES), lambda bi, h: (bi, 0, h))],
        out_specs=pl.BlockSpec((1, seq, LANES), lambda bi, h: (bi, 0, h)),
        out_shape=jax.ShapeDtypeStruct((b, seq, MLA_HEADS * MLA_V_DIM), BF16),
        compiler_params=_cp(("arbitrary", "arbitrary")),
        name="mla_attention",
    )(q3, k3, v3)


def _softplus(z):
    return jnp.maximum(z, 0.0) + jnp.log(1.0 + jnp.exp(-jnp.abs(z)))


def _sb_kernel(q_ref, k_ref, v_ref, o_ref, run_ref, acc_ref, *, seq, tk):
    n = seq // tk
    qs = _head_halves(q_ref[0])
    key = lax.broadcasted_iota(jnp.int32, (tk, tk), 0)
    qry = lax.broadcasted_iota(jnp.int32, (tk, tk), 1)
    strict = key < qry
    later = (qry > key).astype(BF16)

    def unit(j, qlo, qhi, diag):
        k = k_ref[0, j * tk:(j + 1) * tk, :]
        v = v_ref[0, j * tk:(j + 1) * tk, :]
        for hh in range(2):
            z = lax.dot_general(k, qs[hh][qlo:qhi, :], _NT, preferred_element_type=F32)
            sp = _softplus(z)
            log_keep = -sp
            if diag:
                log_keep = _mask_first_block(log_keep, strict, 0.0, tk)
            between = run_ref[hh, :, qlo:qhi]
            for part in _split_bf16(log_keep, 2):
                between = between + jnp.dot(later, part, preferred_element_type=F32)
            w = jnp.exp((z - sp) + between)
            if diag:
                w = _mask_first_block(w, strict, 0.0, tk)
            acc_ref[hh, :, qlo:qhi] += lax.dot_general(v, w.astype(BF16), _TN,
                                                       preferred_element_type=F32)
            run_ref[hh, :, qlo:qhi] += jnp.sum(log_keep, axis=0, keepdims=True)

    run_ref[...] = jnp.zeros_like(run_ref)
    acc_ref[...] = jnp.zeros_like(acc_ref)
    for j in range(n - 1, -1, -1):
        unit(j, j * tk, min((j + 2) * tk, seq), True)

    for j in range(n - 3, -1, -1):
        qlo = (j + 2) * tk

        @pl.when(jnp.max(run_ref[:, :, qlo:]) >= SB_CUTOFF)
        def _():
            unit(j, qlo, seq, False)

    sub = lax.broadcasted_iota(jnp.int32, (LANES, 1), 0)
    o_ref[0] = jnp.where(sub < HEAD_DIM, acc_ref[0], acc_ref[1]).T.astype(o_ref.dtype)


def _sb_attention(qkv3):
    b, seq, _ = qkv3.shape
    tk = min(ATTN_TK, seq)
    hp = SB_HEADS // 2
    qoff = OFF_SB // LANES
    return pl.pallas_call(
        functools.partial(_sb_kernel, seq=seq, tk=tk),
        grid=(b, hp),
        in_specs=[pl.BlockSpec((1, seq, LANES), lambda bi, h: (bi, 0, qoff + h)),
                  pl.BlockSpec((1, seq, LANES), lambda bi, h: (bi, 0, qoff + hp + h)),
                  pl.BlockSpec((1, seq, LANES), lambda bi, h: (bi, 0, qoff + 2 * hp + h))],
        out_specs=pl.BlockSpec((1, seq, LANES), lambda bi, h: (bi, 0, h)),
        out_shape=jax.ShapeDtypeStruct((b, seq, SB_W), BF16),
        scratch_shapes=[pltpu.VMEM((2, 1, seq), F32), pltpu.VMEM((2, LANES, seq), F32)],
        compiler_params=_cp(("arbitrary", "arbitrary")),
        name="sb_attention",
    )(qkv3, qkv3, qkv3)


def _layer_norm(y, g, b):
    yc = y - jnp.mean(y, axis=-1, keepdims=True)
    var = jnp.mean(yc * yc, axis=-1, keepdims=True)
    return yc * lax.rsqrt(var + LN_EPS) * g + b


HALF = D_MODEL // 2
U32 = jnp.uint32


def _pack_halves(y):
    bits = lax.bitcast_convert_type(y.astype(BF16).astype(F32), U32)
    return (bits[:, :HALF] >> 16) | (bits[:, HALF:] & U32(0xFFFF0000))


def _unpack_halves(w):
    lo = lax.bitcast_convert_type(w << 16, F32)
    hi = lax.bitcast_convert_type(w & U32(0xFFFF0000), F32)
    return lo, hi


def _merge_kernel(osb_ref, ofox_ref, omla_ref, g_ref, x_ref, wsb_ref, wfox_ref, wmla_ref,
                  wout_ref, lg_ref, lb_ref, x1_ref, x1b_ref, x1p_ref):
    g = g_ref[...].astype(F32)
    a = jnp.dot(osb_ref[...], wsb_ref[...], preferred_element_type=F32)
    b = jnp.dot(ofox_ref[...], wfox_ref[...], preferred_element_type=F32)
    c = jnp.dot(omla_ref[...], wmla_ref[...], preferred_element_type=F32)
    mixed = (g[:, :D_MODEL] * a + g[:, D_MODEL:2 * D_MODEL] * b + g[:, 2 * D_MODEL:] * c)
    mix = jnp.dot(mixed.astype(BF16), wout_ref[...], preferred_element_type=F32)
    y = _layer_norm(DN_ALPHA * x_ref[...] + mix, lg_ref[...], lb_ref[...])
    x1_ref[...] = y
    x1b_ref[...] = y.astype(BF16)
    x1p_ref[...] = _pack_halves(y)


def _merge(o_sb, o_fox, o_mla, gates, x, wsb, wfox, wmla, wout, lg, lb, tm=512):
    n = x.shape[0]
    tm = min(tm, n)
    full = lambda a: pl.BlockSpec(a.shape, lambda i: (0,) * a.ndim)
    rows = lambda w: pl.BlockSpec((tm, w), lambda i: (i, 0))
    return pl.pallas_call(
        _merge_kernel,
        grid=(n // tm,),
        in_specs=[rows(SB_W), rows(FOX_W), rows(MLA_HEADS * MLA_V_DIM), rows(N_BRANCHES * D_MODEL),
                  rows(D_MODEL), full(wsb), full(wfox), full(wmla), full(wout), full(lg), full(lb)],
        out_specs=[rows(D_MODEL), rows(D_MODEL), rows(HALF)],
        out_shape=[jax.ShapeDtypeStruct((n, D_MODEL), F32),
                   jax.ShapeDtypeStruct((n, D_MODEL), BF16),
                   jax.ShapeDtypeStruct((n, HALF), U32)],
        compiler_params=_cp(("arbitrary",)),
        name="merge_outproj_ln",
    )(o_sb, o_fox, o_mla, gates, x, wsb, wfox, wmla, wout, lg, lb)


def _route_kernel(x_ref, wr_ref, rb_ref, idx_ref, w_ref, rank_ref, cnt_ref, run_ref, *, tm):
    step = pl.program_id(0)

    @pl.when(step == 0)
    def _():
        run_ref[...] = jnp.zeros_like(run_ref)

    logits = lax.dot_general(wr_ref[...], x_ref[...], _NT, preferred_element_type=F32)
    scores = jax.nn.sigmoid(logits)
    biased = scores + rb_ref[...]
    e_iota = lax.broadcasted_iota(jnp.int32, (N_EXPERTS, tm), 0)
    big = jnp.int32(1 << 20)

    g_iota = lax.broadcasted_iota(jnp.int32, (GROUP_SIZE, tm), 0)
    gs_rows = []
    for g in range(N_GROUPS):
        blk = biased[g * GROUP_SIZE:(g + 1) * GROUP_SIZE, :]
        m1 = jnp.max(blk, axis=0, keepdims=True)
        first = jnp.min(jnp.where(blk == m1, g_iota, big), axis=0, keepdims=True)
        m2 = jnp.max(jnp.where(g_iota == first, -jnp.inf, blk), axis=0, keepdims=True)
        gs_rows.append(m1 + m2)
    gs = jnp.concatenate(gs_rows, axis=0)
    n_iota = lax.broadcasted_iota(jnp.int32, (N_GROUPS, tm), 0)
    keep = jnp.zeros((N_GROUPS, tm), jnp.bool_)
    for _ in range(TOPK_GROUPS):
        m = jnp.max(gs, axis=0, keepdims=True)
        first = jnp.min(jnp.where(gs == m, n_iota, big), axis=0, keepdims=True)
        hit = n_iota == first
        keep = jnp.logical_or(keep, hit)
        gs = jnp.where(hit, -jnp.inf, gs)
    keep_f = keep.astype(F32)
    expert_keep = jnp.concatenate(
        [jnp.broadcast_to(keep_f[g:g + 1, :], (GROUP_SIZE, tm)) for g in range(N_GROUPS)], axis=0)
    masked = jnp.where(expert_keep > 0.5, biased, -jnp.inf)

    idx_rows, w_rows, hits = [], [], []
    sel = jnp.zeros((N_EXPERTS, tm), F32)
    for _ in range(TOP_K):
        m = jnp.max(masked, axis=0, keepdims=True)
        first = jnp.min(jnp.where(masked == m, e_iota, big), axis=0, keepdims=True)
        hit = e_iota == first
        idx_rows.append(first)
        w_rows.append(jnp.sum(jnp.where(hit, scores, 0.0), axis=0, keepdims=True))
        hits.append(hit)
        sel = sel + hit.astype(F32)
        masked = jnp.where(hit, -jnp.inf, masked)
    w = jnp.concatenate(w_rows, axis=0)
    w = w / jnp.sum(w, axis=0, keepdims=True) * ROUTED_SCALE
    idx_ref[...] = jnp.concatenate(idx_rows, axis=0)
    w_ref[...] = w

    r = lax.broadcasted_iota(jnp.int32, (tm, tm), 0)
    c = lax.broadcasted_iota(jnp.int32, (tm, tm), 1)
    earlier = (r < c).astype(BF16)
    prefix = jnp.dot(sel.astype(BF16), earlier, preferred_element_type=F32) + run_ref[...]
    rank_rows = [jnp.sum(jnp.where(h, prefix, 0.0), axis=0, keepdims=True) for h in hits]
    rank_ref[...] = jnp.concatenate(rank_rows, axis=0).astype(jnp.int32)
    total = run_ref[...] + jnp.sum(sel, axis=1, keepdims=True)
    run_ref[...] = total
    cnt_ref[...] = jnp.broadcast_to(total, (N_EXPERTS, LANES))


def _route(x1b, wr_t, rb, tm=256):
    n = x1b.shape[0]
    tm = min(tm, n)
    return pl.pallas_call(
        functools.partial(_route_kernel, tm=tm),
        grid=(n // tm,),
        in_specs=[pl.BlockSpec((tm, D_MODEL), lambda i: (i, 0)),
                  pl.BlockSpec((N_EXPERTS, D_MODEL), lambda i: (0, 0)),
                  pl.BlockSpec((N_EXPERTS, 1), lambda i: (0, 0))],
        out_specs=[pl.BlockSpec((TOP_K, tm), lambda i: (0, i)),
                   pl.BlockSpec((TOP_K, tm), lambda i: (0, i)),
                   pl.BlockSpec((TOP_K, tm), lambda i: (0, i)),
                   pl.BlockSpec((N_EXPERTS, LANES), lambda i: (0, 0))],
        out_shape=[jax.ShapeDtypeStruct((TOP_K, n), jnp.int32),
                   jax.ShapeDtypeStruct((TOP_K, n), F32),
                   jax.ShapeDtypeStruct((TOP_K, n), jnp.int32),
                   jax.ShapeDtypeStruct((N_EXPERTS, LANES), F32)],
        scratch_shapes=[pltpu.VMEM((N_EXPERTS, 1), F32)],
        compiler_params=_cp(("arbitrary",)),
        name="router_topk",
    )(x1b, wr_t, rb)


def _dest_kernel(idx_ref, rank_ref, st_ref, dest_ref, *, tm):
    e_iota = lax.broadcasted_iota(jnp.int32, (N_EXPERTS, tm), 0)
    starts = st_ref[...]
    rows = []
    for k in range(TOP_K):
        hit = e_iota == idx_ref[k:k + 1, :]
        rows.append(jnp.sum(jnp.where(hit, starts, 0.0), axis=0, keepdims=True))
    dest_ref[...] = jnp.concatenate(rows, axis=0).astype(jnp.int32) + rank_ref[...]


def _dest(idx_t, rank_t, starts, tm=512):
    n = idx_t.shape[1]
    tm = min(tm, n)
    blk = lambda: pl.BlockSpec((TOP_K, tm), lambda i: (0, i))
    return pl.pallas_call(
        functools.partial(_dest_kernel, tm=tm),
        grid=(n // tm,),
        in_specs=[blk(), blk(), pl.BlockSpec((N_EXPERTS, 1), lambda i: (0, 0))],
        out_specs=blk(),
        out_shape=jax.ShapeDtypeStruct((TOP_K, n), jnp.int32),
        compiler_params=_cp(("arbitrary",)),
        name="moe_dest",
    )(idx_t, rank_t, starts.astype(F32).reshape(N_EXPERTS, 1))


SC_CORES = 2
SC_SUBCORES = 16
SC_WINDOW = 128


def _sc_mesh():
    return plsc.VectorSubcoreMesh(core_axis_name="c", subcore_axis_name="s",
                                  num_cores=SC_CORES, num_subcores=SC_SUBCORES)


def _sc_windows(n):
    workers = SC_CORES * SC_SUBCORES
    assert n % (SC_WINDOW * workers) == 0, n
    return n // SC_WINDOW // workers


def _sc_dispatch(dest_win, x1p):
    n = x1p.shape[0]
    per_worker = _sc_windows(n)

    @functools.partial(
        pl.kernel, mesh=_sc_mesh(),
        out_type=jax.ShapeDtypeStruct((n * TOP_K, HALF), U32),
        scratch_types=[pltpu.VMEM((TOP_K, SC_WINDOW), jnp.int32),
                       pltpu.VMEM((SC_WINDOW, HALF), U32)],
        name="moe_dispatch_sc")
    def run(dest_hbm, x_hbm, xs_hbm, idx_v, rows_v):
        worker = lax.axis_index("s") * SC_CORES + lax.axis_index("c")

        @pl.loop(0, per_worker)
        def _(c):
            win = worker * per_worker + c
            pltpu.sync_copy(dest_hbm.at[win], idx_v)
            pltpu.sync_copy(x_hbm.at[pl.ds(pl.multiple_of(win * SC_WINDOW, SC_WINDOW),
                                           SC_WINDOW)], rows_v)
            for k in range(TOP_K):
                pltpu.sync_copy(rows_v, xs_hbm.at[idx_v.at[k]])

    return run(dest_win, x1p)


def _sc_gather(dest_win, ys, n):
    per_worker = _sc_windows(n)

    @functools.partial(
        pl.kernel, mesh=_sc_mesh(),
        out_type=jax.ShapeDtypeStruct((TOP_K, n, HALF), U32),
        scratch_types=[pltpu.VMEM((TOP_K, SC_WINDOW), jnp.int32),
                       pltpu.VMEM((SC_WINDOW, HALF), U32)],
        name="moe_gather_sc")
    def run(dest_hbm, ys_hbm, yt_hbm, idx_v, rows_v):
        worker = lax.axis_index("s") * SC_CORES + lax.axis_index("c")

        @pl.loop(0, per_worker)
        def _(c):
            win = worker * per_worker + c
            start = pl.multiple_of(win * SC_WINDOW, SC_WINDOW)
            pltpu.sync_copy(dest_hbm.at[win], idx_v)
            for k in range(TOP_K):
                pltpu.sync_copy(ys_hbm.at[idx_v.at[k]], rows_v)
                pltpu.sync_copy(rows_v, yt_hbm.at[k, pl.ds(start, SC_WINDOW)])

    return run(dest_win, ys)


def _expert_kernel(tile_ref, exp_ref, lo_ref, hi_ref, xs_ref, wg_ref, wu_ref, wd_ref, ys_ref,
                   *, bm):
    w = pl.program_id(0)
    lo = lo_ref[w]
    hi = hi_ref[w]

    @pl.when(hi > lo)
    def _():
        x_lo, x_hi = _unpack_halves(xs_ref[...])
        x = jnp.concatenate([x_lo, x_hi], axis=1).astype(BF16)
        g = jnp.dot(x, wg_ref[0, 0].astype(BF16), preferred_element_type=F32)
        u = jnp.dot(x, wu_ref[0, 0].astype(BF16), preferred_element_type=F32)
        h = (g * jax.nn.sigmoid(g) * u).astype(BF16)
        y = _pack_halves(jnp.dot(h, wd_ref[0, 0].astype(BF16), preferred_element_type=F32))

        @pl.when(lo == 0)
        def _():
            ys_ref[...] = y

        @pl.when(lo > 0)
        def _():
            rows = lax.broadcasted_iota(jnp.int32, (bm, 1), 0)
            ys_ref[...] = jnp.where(rows >= lo, y, ys_ref[...])


def _experts(layer, item_tile, item_exp, item_lo, item_hi, xs, wg, wu, wd, bm):
    r = xs.shape[0]
    n_items = item_tile.shape[0]
    grid_spec = pltpu.PrefetchScalarGridSpec(
        num_scalar_prefetch=4,
        grid=(n_items,),
        in_specs=[pl.BlockSpec((bm, HALF), lambda w, t, e, lo, hi: (t[w], 0)),
                  pl.BlockSpec((1, 1, D_MODEL, EXPERT_FF),
                               lambda w, t, e, lo, hi: (layer, e[w], 0, 0)),
                  pl.BlockSpec((1, 1, D_MODEL, EXPERT_FF),
                               lambda w, t, e, lo, hi: (layer, e[w], 0, 0)),
                  pl.BlockSpec((1, 1, EXPERT_FF, D_MODEL),
                               lambda w, t, e, lo, hi: (layer, e[w], 0, 0))],
        out_specs=pl.BlockSpec((bm, HALF), lambda w, t, e, lo, hi: (t[w], 0)),
    )
    return pl.pallas_call(
        functools.partial(_expert_kernel, bm=bm),
        grid_spec=grid_spec,
        out_shape=jax.ShapeDtypeStruct((r, HALF), U32),
        compiler_params=_cp(("arbitrary",)),
        name="moe_experts",
    )(item_tile, item_exp, item_lo, item_hi, xs, wg, wu, wd)


def _expert_items(counts, n_rows, bm):
    n_tiles = n_rows // bm
    ends = jnp.cumsum(counts)
    starts = ends - counts
    bounds = jnp.sort(jnp.concatenate([jnp.arange(n_tiles, dtype=jnp.int32) * bm,
                                       starts.astype(jnp.int32)]))
    nxt = jnp.concatenate([bounds[1:], jnp.array([n_rows], jnp.int32)])
    tile = jnp.minimum(bounds // bm, n_tiles - 1)
    exp = jnp.sum((ends[None, :] <= bounds[:, None]).astype(jnp.int32), axis=1)
    exp = jnp.minimum(exp, N_EXPERTS - 1)
    lo = bounds - tile * bm
    hi = nxt - tile * bm
    return tile.astype(jnp.int32), exp, lo.astype(jnp.int32), hi.astype(jnp.int32), starts


def _combine_rows_kernel(yt_ref, tw_ref, x1_ref, x1b_ref, wsg_ref, wsu_ref, wsd_ref, lg_ref, lb_ref,
                         o_ref, ob_ref):
    tw = tw_ref[...]
    r_lo = jnp.zeros((tw.shape[0], HALF), F32)
    r_hi = jnp.zeros((tw.shape[0], HALF), F32)
    for k in range(TOP_K):
        y_lo, y_hi = _unpack_halves(yt_ref[k])
        r_lo = r_lo + tw[:, k:k + 1] * y_lo
        r_hi = r_hi + tw[:, k:k + 1] * y_hi
    routed = jnp.concatenate([r_lo, r_hi], axis=1)
    xb = x1b_ref[...]
    g = jnp.dot(xb, wsg_ref[...], preferred_element_type=F32)
    u = jnp.dot(xb, wsu_ref[...], preferred_element_type=F32)
    h = (g * jax.nn.sigmoid(g) * u).astype(BF16)
    shared = jnp.dot(h, wsd_ref[...], preferred_element_type=F32)
    y = _layer_norm(DN_ALPHA * x1_ref[...] + (shared + routed), lg_ref[...], lb_ref[...])
    o_ref[...] = y
    ob_ref[...] = y.astype(BF16)


def _combine_rows(yt, tw, x1, x1b, wsg, wsu, wsd, lg, lb, tc=256):
    n = x1.shape[0]
    tc = min(tc, n)
    full = lambda a: pl.BlockSpec(a.shape, lambda i: (0,) * a.ndim)
    rows = lambda w: pl.BlockSpec((tc, w), lambda i: (i, 0))
    return pl.pallas_call(
        _combine_rows_kernel,
        grid=(n // tc,),
        in_specs=[pl.BlockSpec((TOP_K, tc, HALF), lambda i: (0, i, 0)),
                  rows(TOP_K), rows(D_MODEL), rows(D_MODEL),
                  full(wsg), full(wsu), full(wsd), full(lg), full(lb)],
        out_specs=[rows(D_MODEL), rows(D_MODEL)],
        out_shape=[jax.ShapeDtypeStruct((n, D_MODEL), F32),
                   jax.ShapeDtypeStruct((n, D_MODEL), BF16)],
        compiler_params=_cp(("arbitrary",)),
        name="moe_combine_ln",
    )(yt, tw, x1, x1b, wsg, wsu, wsd, lg, lb)


def _head_cols(w, heads, width, lo, hi):
    return w.reshape(w.shape[0], heads, width)[:, :, lo:hi]


def _prep_layer(w_in, b_gate, b_forget, mla_q_norm, w_uq, mla_kv_norm, w_ukv):
    f = lambda a: a.astype(BF16)
    half = MLA_ROPE_DIM // 2
    d = w_in.shape[0]
    qscale = HEAD_DIM ** -0.5
    w_qkv = jnp.concatenate([
        w_in[:, OFF_SB:OFF_SB + SB_W] * qscale, w_in[:, OFF_SB + SB_W:OFF_FOX],
        w_in[:, OFF_FOX:OFF_FOX + FOX_W] * qscale, w_in[:, OFF_FOX + FOX_W:OFF_FGATE]], axis=1)
    w_kr = w_in[:, OFF_KR:OFF_GATE]
    w_kr_rot = jnp.concatenate([-w_kr[:, half:], w_kr[:, :half]], axis=1)
    z = lambda c: jnp.zeros((d, c), F32)
    pad_rope = LANES - MLA_QK_DIM
    w_small = jnp.concatenate([
        w_in[:, OFF_DQ:OFF_DKV], w_in[:, OFF_DKV:OFF_KR],
        z(MLA_NOPE_DIM), w_kr, z(pad_rope),
        z(MLA_NOPE_DIM), w_kr_rot, z(pad_rope),
        w_in[:, OFF_FGATE:OFF_DQ], z(LANES - FOX_HEADS)], axis=1)
    w_gate = w_in[:, OFF_GATE:]

    r = w_uq.shape[0]
    q_nope = _head_cols(w_uq, MLA_HEADS, MLA_QK_DIM, 0, MLA_NOPE_DIM)
    q_rope = _head_cols(w_uq, MLA_HEADS, MLA_QK_DIM, MLA_NOPE_DIM, MLA_QK_DIM)
    q_rope_rot = jnp.concatenate([-q_rope[:, :, half:], q_rope[:, :, :half]], axis=2)
    zq = lambda c: jnp.zeros((r, MLA_HEADS, c), F32)
    wqa = jnp.concatenate([q_nope, q_rope, zq(pad_rope)], axis=2).reshape(r, MLA_HEADS * LANES)
    wqb = jnp.concatenate([zq(MLA_NOPE_DIM), q_rope_rot, zq(pad_rope)], axis=2).reshape(
        r, MLA_HEADS * LANES)
    rk = w_ukv.shape[0]
    kvw = MLA_NOPE_DIM + MLA_V_DIM
    k_nope = _head_cols(w_ukv, MLA_HEADS, kvw, 0, MLA_NOPE_DIM)
    wk = jnp.concatenate([k_nope, jnp.zeros((rk, MLA_HEADS, LANES - MLA_NOPE_DIM), F32)],
                         axis=2).reshape(rk, MLA_HEADS * LANES)
    wv = _head_cols(w_ukv, MLA_HEADS, kvw, MLA_NOPE_DIM, kvw).reshape(rk, MLA_HEADS * MLA_V_DIM)
    bf = jnp.concatenate([b_forget, jnp.zeros((LANES - FOX_HEADS,), F32)]).reshape(1, LANES)
    return dict(w_qkv=f(w_qkv), w_small=f(w_small), w_gate=f(w_gate),
                b_gate=b_gate.reshape(1, -1), wqa=f(wqa), wqb=f(wqb), wk=f(wk), wv=f(wv),
                qn=mla_q_norm.reshape(1, -1), kvn=mla_kv_norm.reshape(1, -1), bf=bf)


def _rope_tables(seq):
    half = MLA_ROPE_DIM // 2
    inv_freq = jnp.power(ROPE_BASE, -jnp.arange(half, dtype=F32) / half)
    ang = jnp.arange(seq).astype(F32)[:, None] * inv_freq[None, :]
    cos = jnp.concatenate([jnp.cos(ang), jnp.cos(ang)], axis=1)
    sin = jnp.concatenate([jnp.sin(ang), jnp.sin(ang)], axis=1)
    pad = lambda t: jnp.concatenate([jnp.zeros((seq, MLA_NOPE_DIM), F32), t,
                                     jnp.zeros((seq, LANES - MLA_QK_DIM), F32)], axis=1)
    return pad(cos), pad(sin)


EXPERT_BM = 512


def kernel(x, ln1_g, ln1_b, ln2_g, ln2_b, w_in, b_gate, b_forget, mla_q_norm, w_uq, mla_kv_norm,
           w_ukv, w_proj_sb, w_proj_fox, w_proj_mla, w_out, w_router, router_bias,
           w_exp_gate, w_exp_up, w_exp_down, w_sh_gate, w_sh_up, w_sh_down):
    b, seq, d = x.shape
    n = b * seq
    depth = w_in.shape[0]
    cos_t, sin_t = _rope_tables(seq)
    xf = x.reshape(n, d)
    xb = xf.astype(BF16)
    f = lambda a: a.astype(BF16)
    bm = min(EXPERT_BM, n * TOP_K)
    for l in range(depth):
        p = _prep_layer(w_in[l], b_gate[l], b_forget[l], mla_q_norm[l], w_uq[l], mla_kv_norm[l],
                        w_ukv[l])
        qkv = _matmul(xb, p["w_qkv"], None, BF16)
        gates = _matmul(xb, p["w_gate"], p["b_gate"], BF16, act="sigmoid")
        small = _matmul(xb, p["w_small"], None, F32, tn=SMALL_W)
        q_mla, k_mla, v_mla, lf = _mla_prep(small, p["qn"], p["kvn"], p["wqa"], p["wqb"], p["wk"],
                                            p["wv"], cos_t, sin_t, p["bf"], seq)
        cum, cumt = _cumsum(lf.reshape(b, seq, LANES))
        cumt4 = cumt[:, :FOX_HEADS, :].reshape(b, FOX_HEADS, 1, seq)
        qkv3 = qkv.reshape(b, seq, QKV_W)
        o_sb = _sb_attention(qkv3).reshape(n, SB_W)
        o_fox = _fox_attention(qkv3, cum, cumt4).reshape(n, FOX_W)
        o_mla = _mla_attention(q_mla.reshape(b, seq, -1), k_mla.reshape(b, seq, -1),
                               v_mla.reshape(b, seq, -1)).reshape(n, -1)
        x1, x1b, x1p = _merge(o_sb, o_fox, o_mla, gates, xf, f(w_proj_sb[l]), f(w_proj_fox[l]),
                              f(w_proj_mla[l]), f(w_out[l]), ln1_g[l].reshape(1, d),
                              ln1_b[l].reshape(1, d))

        idx_t, tw_t, rank_t, cnt = _route(x1b, f(w_router[l].T), router_bias[l].reshape(-1, 1))
        counts = cnt[:, 0].astype(jnp.int32)
        item_tile, item_exp, item_lo, item_hi, starts = _expert_items(counts, n * TOP_K, bm)
        dest_t = _dest(idx_t, rank_t, starts)
        dest_win = dest_t.reshape(TOP_K, n // SC_WINDOW, SC_WINDOW).transpose(1, 0, 2)
        xs = _sc_dispatch(dest_win, x1p)
        ys = _experts(l, item_tile, item_exp, item_lo, item_hi, xs, w_exp_gate, w_exp_up,
                      w_exp_down, bm)
        yt = _sc_gather(dest_win, ys, n)
        xf, xb = _combine_rows(yt, tw_t.T, x1, x1b, f(w_sh_gate[l]), f(w_sh_up[l]),
                               f(w_sh_down[l]), ln2_g[l].reshape(1, d), ln2_b[l].reshape(1, d))
    return xf.reshape(b, seq, d)
```

```python
import functools

import jax
import jax.numpy as jnp
from jax import lax
from jax.experimental import pallas as pl
from jax.experimental.pallas import tpu as pltpu
from jax.experimental.pallas import tpu_sc as plsc

F32 = jnp.float32
BF16 = jnp.bfloat16

D_MODEL = 1024
HEAD_DIM = 64
SB_HEADS = 4
FOX_HEADS = 4
MLA_HEADS = 8
MLA_Q_RANK = 256
MLA_KV_RANK = 128
MLA_NOPE_DIM = 64
MLA_ROPE_DIM = 32
MLA_V_DIM = 64
ROPE_BASE = 10000.0
N_BRANCHES = 3
N_EXPERTS = 256
TOP_K = 8
N_GROUPS = 8
TOPK_GROUPS = 4
GROUP_SIZE = N_EXPERTS // N_GROUPS
EXPERT_FF = 256
SHARED_FF = 256
ROUTED_SCALE = 2.5
CHUNK = 64
LN_EPS = 1e-5
RMS_EPS = 1e-6
DEPTH = 2
DN_ALPHA = (2 * DEPTH) ** 0.25

SB_W = SB_HEADS * HEAD_DIM
FOX_W = FOX_HEADS * HEAD_DIM
MLA_QK_DIM = MLA_NOPE_DIM + MLA_ROPE_DIM
OFF_SB = 0
OFF_FOX = OFF_SB + 3 * SB_W
OFF_FGATE = OFF_FOX + 3 * FOX_W
OFF_DQ = OFF_FGATE + FOX_HEADS
OFF_DKV = OFF_DQ + MLA_Q_RANK
OFF_KR = OFF_DKV + MLA_KV_RANK
OFF_GATE = OFF_KR + MLA_ROPE_DIM

LANES = 128
SUBLANES = 8
QKV_W = 3 * SB_W + 3 * FOX_W
SMALL_W = MLA_Q_RANK + MLA_KV_RANK + 3 * LANES
SB_CUTOFF = -104.0
LOG2E = 1.4426950408889634

VMEM_LIMIT = 48 * 1024 * 1024


def _cp(sem, vmem=VMEM_LIMIT):
    return pltpu.CompilerParams(dimension_semantics=sem, vmem_limit_bytes=vmem)


def _mm_kernel(x_ref, w_ref, b_ref, o_ref, *, act):
    acc = jnp.dot(x_ref[...], w_ref[...], preferred_element_type=F32)
    if act == "sigmoid":
        acc = jax.nn.sigmoid(acc + b_ref[...])
    o_ref[...] = acc.astype(o_ref.dtype)


def _matmul(x, w, bias, out_dtype, act=None, tm=1024, tn=512):
    n, k = x.shape
    c = w.shape[1]
    tm = min(tm, n)
    tn = min(tn, c)
    if bias is None:
        bias = jnp.zeros((1, c), F32)
    return pl.pallas_call(
        functools.partial(_mm_kernel, act=act),
        grid=(n // tm, c // tn),
        in_specs=[pl.BlockSpec((tm, k), lambda i, j: (i, 0)),
                  pl.BlockSpec((k, tn), lambda i, j: (0, j)),
                  pl.BlockSpec((1, tn), lambda i, j: (0, j))],
        out_specs=pl.BlockSpec((tm, tn), lambda i, j: (i, j)),
        out_shape=jax.ShapeDtypeStruct((n, c), out_dtype),
        compiler_params=_cp(("arbitrary", "arbitrary")),
        name="proj_matmul",
    )(x, w, bias)


def _split_bf16(x, parts):
    out = []
    for _ in range(parts):
        h = x.astype(BF16)
        out.append(h)
        x = x - h.astype(F32)
    return out


def _mla_prep_kernel(sm_ref, qn_ref, kvn_ref, wqa_ref, wqb_ref, wk_ref, wv_ref,
                     cos_ref, sin_ref, bf_ref, q_ref, k_ref, v_ref, lf_ref):
    sm = sm_ref[...]
    dq = sm[:, :MLA_Q_RANK]
    dkv = sm[:, MLA_Q_RANK:MLA_Q_RANK + MLA_KV_RANK]
    o = MLA_Q_RANK + MLA_KV_RANK
    kr = sm[:, o:o + LANES]
    kr_rot = sm[:, o + LANES:o + 2 * LANES]
    fg = sm[:, o + 2 * LANES:o + 3 * LANES]

    cq = dq * lax.rsqrt(jnp.mean(dq * dq, axis=-1, keepdims=True) + RMS_EPS) * qn_ref[...]
    ckv = dkv * lax.rsqrt(jnp.mean(dkv * dkv, axis=-1, keepdims=True) + RMS_EPS) * kvn_ref[...]
    cq = cq.astype(BF16)
    ckv = ckv.astype(BF16)

    cosk = cos_ref[...]
    sink = sin_ref[...]
    lane = lax.broadcasted_iota(jnp.int32, (1, LANES), 1)
    nope = (lane < MLA_NOPE_DIM).astype(F32)
    scale = MLA_QK_DIM ** -0.5 * LOG2E
    cq_tab = jnp.concatenate([(cosk + nope) * scale] * MLA_HEADS, axis=1)
    sq_tab = jnp.concatenate([sink * scale] * MLA_HEADS, axis=1)

    qa = jnp.dot(cq, wqa_ref[...], preferred_element_type=F32)
    qb = jnp.dot(cq, wqb_ref[...], preferred_element_type=F32)
    q_ref[...] = (qa * cq_tab + qb * sq_tab).astype(q_ref.dtype)

    k_rope = kr * cosk + kr_rot * sink
    ka = jnp.dot(ckv, wk_ref[...], preferred_element_type=F32)
    k_ref[...] = (ka + jnp.concatenate([k_rope] * MLA_HEADS, axis=1)).astype(k_ref.dtype)
    v_ref[...] = jnp.dot(ckv, wv_ref[...], preferred_element_type=F32).astype(v_ref.dtype)
    lf_ref[...] = jax.nn.log_sigmoid(fg + bf_ref[...])


def _mla_prep(small, qn, kvn, wqa, wqb, wk, wv, cos_t, sin_t, bf, seq, tm=512):
    n = small.shape[0]
    tm = min(tm, seq)
    sblocks = seq // tm
    hw = MLA_HEADS * LANES
    full = lambda a: pl.BlockSpec(a.shape, lambda i: (0,) * a.ndim)
    return pl.pallas_call(
        _mla_prep_kernel,
        grid=(n // tm,),
        in_specs=[pl.BlockSpec((tm, SMALL_W), lambda i: (i, 0)),
                  full(qn), full(kvn), full(wqa), full(wqb), full(wk), full(wv),
                  pl.BlockSpec((tm, LANES), lambda i: (i % sblocks, 0)),
                  pl.BlockSpec((tm, LANES), lambda i: (i % sblocks, 0)),
                  full(bf)],
        out_specs=[pl.BlockSpec((tm, hw), lambda i: (i, 0)),
                   pl.BlockSpec((tm, hw), lambda i: (i, 0)),
                   pl.BlockSpec((tm, MLA_HEADS * MLA_V_DIM), lambda i: (i, 0)),
                   pl.BlockSpec((tm, LANES), lambda i: (i, 0))],
        out_shape=[jax.ShapeDtypeStruct((n, hw), BF16),
                   jax.ShapeDtypeStruct((n, hw), BF16),
                   jax.ShapeDtypeStruct((n, MLA_HEADS * MLA_V_DIM), BF16),
                   jax.ShapeDtypeStruct((n, LANES), F32)],
        compiler_params=_cp(("arbitrary",)),
        name="mla_prep",
    )(small, qn, kvn, wqa, wqb, wk, wv, cos_t, sin_t, bf)


def _cumsum_kernel(lf_ref, cum_ref, cumt_ref, *, seq):
    r = lax.broadcasted_iota(jnp.int32, (LANES, LANES), 0)
    c = lax.broadcasted_iota(jnp.int32, (LANES, LANES), 1)
    lower = (c <= r).astype(BF16)

    def body(j, carry):
        start = pl.multiple_of(j * LANES, LANES)
        blk = lf_ref[0, pl.ds(start, LANES), :]
        acc = carry
        for part in _split_bf16(blk, 3):
            acc = acc + jnp.dot(lower, part, preferred_element_type=F32)
        scaled = acc * LOG2E
        cum_ref[0, pl.ds(start, LANES), :] = scaled
        cumt_ref[0, :, pl.ds(start, LANES)] = scaled.T[:SUBLANES, :]
        return jnp.broadcast_to(acc[LANES - 1:LANES, :], (LANES, LANES))

    lax.fori_loop(0, seq // LANES, body, jnp.zeros((LANES, LANES), F32))


def _cumsum(lf3):
    b, seq, _ = lf3.shape
    return pl.pallas_call(
        functools.partial(_cumsum_kernel, seq=seq),
        grid=(b,),
        in_specs=[pl.BlockSpec((1, seq, LANES), lambda i: (i, 0, 0))],
        out_specs=[pl.BlockSpec((1, seq, LANES), lambda i: (i, 0, 0)),
                   pl.BlockSpec((1, SUBLANES, seq), lambda i: (i, 0, 0))],
        out_shape=[jax.ShapeDtypeStruct((b, seq, LANES), F32),
                   jax.ShapeDtypeStruct((b, SUBLANES, seq), F32)],
        compiler_params=_cp(("arbitrary",)),
        name="forget_cumsum",
    )(lf3)


_NT = (((1,), (1,)), ((), ()))
NEG_INIT = -1e30
_TN = (((0,), (0,)), ((), ()))
ATTN_TK = 256


def _head_halves(q):
    lane = lax.broadcasted_iota(jnp.int32, (1, LANES), 1)
    zero = jnp.zeros_like(q)
    return [jnp.where(lane < HEAD_DIM, q, zero), jnp.where(lane >= HEAD_DIM, q, zero)]


def _mask_first_block(x, keep, fill, tk):
    head = jnp.where(keep, x[:, :tk], fill)
    return head if x.shape[1] == tk else jnp.concatenate([head, x[:, tk:]], axis=1)


def _fox_kernel(q_ref, k_ref, v_ref, cc_ref, cr_ref, o_ref, *, seq, tk):
    hp = pl.program_id(1)
    lane = lax.broadcasted_iota(jnp.int32, (1, LANES), 1)
    qs = _head_halves(q_ref[0])
    key = lax.broadcasted_iota(jnp.int32, (tk, tk), 0)
    qry = lax.broadcasted_iota(jnp.int32, (tk, tk), 1)
    causal = key <= qry
    carry = [(jnp.full((1, seq), NEG_INIT, F32), jnp.zeros((1, seq), F32),
              jnp.zeros((LANES, seq), F32)) for _ in range(2)]
    for j in range(seq // tk):
        q0 = j * tk
        k = k_ref[0, q0:q0 + tk, :]
        v = v_ref[0, q0:q0 + tk, :]
        cc = cc_ref[0, q0:q0 + tk, :]
        for hh in range(2):
            m, l, acc = carry[hh]
            cs = jnp.sum(jnp.where(lane == 2 * hp + hh, cc, 0.0), axis=1, keepdims=True)
            ct = cr_ref[0, hh, :, q0:]
            st = lax.dot_general(k, qs[hh][q0:, :], _NT, preferred_element_type=F32)
            st = _mask_first_block((st + ct) - cs, causal, -jnp.inf, tk)
            mn, ln, an = _softmax_step_t(st, m[:, q0:], l[:, q0:], acc[:, q0:], v)
            if q0:
                mn = jnp.concatenate([m[:, :q0], mn], axis=1)
                ln = jnp.concatenate([l[:, :q0], ln], axis=1)
                an = jnp.concatenate([acc[:, :q0], an], axis=1)
            carry[hh] = (mn, ln, an)
    _pair_out_t([carry[0][2], carry[1][2]], [carry[0][1], carry[1][1]], o_ref)


def _fox_attention(qkv3, cum, cumt4):
    b, seq, _ = qkv3.shape
    tk = min(ATTN_TK, seq)
    hp = FOX_HEADS // 2
    qoff = OFF_FOX // LANES
    return pl.pallas_call(
        functools.partial(_fox_kernel, seq=seq, tk=tk),
        grid=(b, hp),
        in_specs=[pl.BlockSpec((1, seq, LANES), lambda bi, h: (bi, 0, qoff + h)),
                  pl.BlockSpec((1, seq, LANES), lambda bi, h: (bi, 0, qoff + hp + h)),
                  pl.BlockSpec((1, seq, LANES), lambda bi, h: (bi, 0, qoff + 2 * hp + h)),
                  pl.BlockSpec((1, seq, LANES), lambda bi, h: (bi, 0, 0)),
                  pl.BlockSpec((1, 2, 1, seq), lambda bi, h: (bi, h, 0, 0))],
        out_specs=pl.BlockSpec((1, seq, LANES), lambda bi, h: (bi, 0, h)),
        out_shape=jax.ShapeDtypeStruct((b, seq, FOX_W), BF16),
        compiler_params=_cp(("arbitrary", "arbitrary")),
        name="fox_attention",
    )(qkv3, qkv3, qkv3, cum, cumt4)


def _softmax_step_t(st, m, l, acc, v):
    m_new = jnp.maximum(m, jnp.max(st, axis=0, keepdims=True))
    alpha = jnp.exp2(m - m_new)
    p = jnp.exp2(st - m_new)
    l = alpha * l + jnp.sum(p, axis=0, keepdims=True)
    acc = alpha * acc + lax.dot_general(v, p.astype(BF16), _TN, preferred_element_type=F32)
    return m_new, l, acc


def _pair_out_t(accs, ls, o_ref):
    sub = lax.broadcasted_iota(jnp.int32, (LANES, 1), 0)
    ot = jnp.where(sub < HEAD_DIM, accs[0] / ls[0], accs[1] / ls[1])
    o_ref[0] = ot.T.astype(o_ref.dtype)


def _mla_kernel(q_ref, k_ref, v_ref, o_ref, *, seq, tk):
    q = q_ref[0]
    qs = [q[:, :LANES], q[:, LANES:]]
    key = lax.broadcasted_iota(jnp.int32, (tk, tk), 0)
    qry = lax.broadcasted_iota(jnp.int32, (tk, tk), 1)
    shift = CHUNK.bit_length() - 1
    chunk_causal = (key >> shift) <= (qry >> shift)
    carry = [(jnp.full((1, seq), NEG_INIT, F32), jnp.zeros((1, seq), F32),
              jnp.zeros((LANES, seq), F32)) for _ in range(2)]
    for j in range(seq // tk):
        q0 = j * tk
        k = k_ref[0, q0:q0 + tk, :]
        v = v_ref[0, q0:q0 + tk, :]
        for hh in range(2):
            m, l, acc = carry[hh]
            st = lax.dot_general(k[:, hh * LANES:(hh + 1) * LANES], qs[hh][q0:, :], _NT,
                                 preferred_element_type=F32)
            st = _mask_first_block(st, chunk_causal, -jnp.inf, tk)
            mn, ln, an = _softmax_step_t(st, m[:, q0:], l[:, q0:], acc[:, q0:], v)
            if q0:
                mn = jnp.concatenate([m[:, :q0], mn], axis=1)
                ln = jnp.concatenate([l[:, :q0], ln], axis=1)
                an = jnp.concatenate([acc[:, :q0], an], axis=1)
            carry[hh] = (mn, ln, an)
    _pair_out_t([carry[0][2], carry[1][2]], [carry[0][1], carry[1][1]], o_ref)


def _mla_attention(q3, k3, v3):
    b, seq, _ = q3.shape
    tk = min(ATTN_TK, seq)
    hp = MLA_HEADS // 2
    return pl.pallas_call(
        functools.partial(_mla_kernel, seq=seq, tk=tk),
        grid=(b, hp),
        in_specs=[pl.BlockSpec((1, seq, 2 * LANES), lambda bi, h: (bi, 0, h)),
                  pl.BlockSpec((1, seq, 2 * LANES), lambda bi, h: (bi, 0, h)),
                  pl.BlockSpec((1, seq, LANES), lambda bi, h: (bi, 0, h))],
        out_specs=pl.BlockSpec((1, seq, LANES), lambda bi, h: (bi, 0, h)),
        out_shape=jax.ShapeDtypeStruct((b, seq, MLA_HEADS * MLA_V_DIM), BF16),
        compiler_params=_cp(("arbitrary", "arbitrary")),
        name="mla_attention",
    )(q3, k3, v3)


def _softplus(z):
    return jnp.maximum(z, 0.0) + jnp.log(1.0 + jnp.exp(-jnp.abs(z)))


def _sb_kernel(q_ref, k_ref, v_ref, o_ref, run_ref, acc_ref, *, seq, tk):
    n = seq // tk
    qs = _head_halves(q_ref[0])
    key = lax.broadcasted_iota(jnp.int32, (tk, tk), 0)
    qry = lax.broadcasted_iota(jnp.int32, (tk, tk), 1)
    strict = key < qry
    later = (qry > key).astype(BF16)

    def unit(j, qlo, qhi, diag):
        k = k_ref[0, j * tk:(j + 1) * tk, :]
        v = v_ref[0, j * tk:(j + 1) * tk, :]
        for hh in range(2):
            z = lax.dot_general(k, qs[hh][qlo:qhi, :], _NT, preferred_element_type=F32)
            sp = _softplus(z)
            log_keep = -sp
            if diag:
                log_keep = _mask_first_block(log_keep, strict, 0.0, tk)
            between = run_ref[hh, :, qlo:qhi]
            for part in _split_bf16(log_keep, 2):
                between = between + jnp.dot(later, part, preferred_element_type=F32)
            w = jnp.exp((z - sp) + between)
            if diag:
                w = _mask_first_block(w, strict, 0.0, tk)
            acc_ref[hh, :, qlo:qhi] += lax.dot_general(v, w.astype(BF16), _TN,
                                                       preferred_element_type=F32)
            run_ref[hh, :, qlo:qhi] += jnp.sum(log_keep, axis=0, keepdims=True)

    run_ref[...] = jnp.zeros_like(run_ref)
    acc_ref[...] = jnp.zeros_like(acc_ref)
    for j in range(n - 1, -1, -1):
        unit(j, j * tk, min((j + 2) * tk, seq), True)

    for j in range(n - 3, -1, -1):
        qlo = (j + 2) * tk

        @pl.when(jnp.max(run_ref[:, :, qlo:]) >= SB_CUTOFF)
        def _():
            unit(j, qlo, seq, False)

    sub = lax.broadcasted_iota(jnp.int32, (LANES, 1), 0)
    o_ref[0] = jnp.where(sub < HEAD_DIM, acc_ref[0], acc_ref[1]).T.astype(o_ref.dtype)


def _sb_attention(qkv3):
    b, seq, _ = qkv3.shape
    tk = min(ATTN_TK, seq)
    hp = SB_HEADS // 2
    qoff = OFF_SB // LANES
    return pl.pallas_call(
        functools.partial(_sb_kernel, seq=seq, tk=tk),
        grid=(b, hp),
        in_specs=[pl.BlockSpec((1, seq, LANES), lambda bi, h: (bi, 0, qoff + h)),
                  pl.BlockSpec((1, seq, LANES), lambda bi, h: (bi, 0, qoff + hp + h)),
                  pl.BlockSpec((1, seq, LANES), lambda bi, h: (bi, 0, qoff + 2 * hp + h))],
        out_specs=pl.BlockSpec((1, seq, LANES), lambda bi, h: (bi, 0, h)),
        out_shape=jax.ShapeDtypeStruct((b, seq, SB_W), BF16),
        scratch_shapes=[pltpu.VMEM((2, 1, seq), F32), pltpu.VMEM((2, LANES, seq), F32)],
        compiler_params=_cp(("arbitrary", "arbitrary")),
        name="sb_attention",
    )(qkv3, qkv3, qkv3)


def _layer_norm(y, g, b):
    yc = y - jnp.mean(y, axis=-1, keepdims=True)
    var = jnp.mean(yc * yc, axis=-1, keepdims=True)
    return yc * lax.rsqrt(var + LN_EPS) * g + b


HALF = D_MODEL // 2
U32 = jnp.uint32


def _pack_halves(y):
    bits = lax.bitcast_convert_type(y.astype(BF16).astype(F32), U32)
    return (bits[:, :HALF] >> 16) | (bits[:, HALF:] & U32(0xFFFF0000))


def _unpack_halves(w):
    lo = lax.bitcast_convert_type(w << 16, F32)
    hi = lax.bitcast_convert_type(w & U32(0xFFFF0000), F32)
    return lo, hi


def _merge_kernel(osb_ref, ofox_ref, omla_ref, g_ref, x_ref, wsb_ref, wfox_ref, wmla_ref,
                  wout_ref, lg_ref, lb_ref, x1_ref, x1b_ref, x1p_ref):
    g = g_ref[...].astype(F32)
    a = jnp.dot(osb_ref[...], wsb_ref[...], preferred_element_type=F32)
    b = jnp.dot(ofox_ref[...], wfox_ref[...], preferred_element_type=F32)
    c = jnp.dot(omla_ref[...], wmla_ref[...], preferred_element_type=F32)
    mixed = (g[:, :D_MODEL] * a + g[:, D_MODEL:2 * D_MODEL] * b + g[:, 2 * D_MODEL:] * c)
    mix = jnp.dot(mixed.astype(BF16), wout_ref[...], preferred_element_type=F32)
    y = _layer_norm(DN_ALPHA * x_ref[...] + mix, lg_ref[...], lb_ref[...])
    x1_ref[...] = y
    x1b_ref[...] = y.astype(BF16)
    x1p_ref[...] = _pack_halves(y)


def _merge(o_sb, o_fox, o_mla, gates, x, wsb, wfox, wmla, wout, lg, lb, tm=512):
    n = x.shape[0]
    tm = min(tm, n)
    full = lambda a: pl.BlockSpec(a.shape, lambda i: (0,) * a.ndim)
    rows = lambda w: pl.BlockSpec((tm, w), lambda i: (i, 0))
    return pl.pallas_call(
        _merge_kernel,
        grid=(n // tm,),
        in_specs=[rows(SB_W), rows(FOX_W), rows(MLA_HEADS * MLA_V_DIM), rows(N_BRANCHES * D_MODEL),
                  rows(D_MODEL), full(wsb), full(wfox), full(wmla), full(wout), full(lg), full(lb)],
        out_specs=[rows(D_MODEL), rows(D_MODEL), rows(HALF)],
        out_shape=[jax.ShapeDtypeStruct((n, D_MODEL), F32),
                   jax.ShapeDtypeStruct((n, D_MODEL), BF16),
                   jax.ShapeDtypeStruct((n, HALF), U32)],
        compiler_params=_cp(("arbitrary",)),
        name="merge_outproj_ln",
    )(o_sb, o_fox, o_mla, gates, x, wsb, wfox, wmla, wout, lg, lb)


def _route_kernel(x_ref, wr_ref, rb_ref, idx_ref, w_ref, rank_ref, cnt_ref, run_ref, *, tm):
    step = pl.program_id(0)

    @pl.when(step == 0)
    def _():
        run_ref[...] = jnp.zeros_like(run_ref)

    logits = lax.dot_general(wr_ref[...], x_ref[...], _NT, preferred_element_type=F32)
    scores = jax.nn.sigmoid(logits)
    biased = scores + rb_ref[...]
    e_iota = lax.broadcasted_iota(jnp.int32, (N_EXPERTS, tm), 0)
    big = jnp.int32(1 << 20)

    g_iota = lax.broadcasted_iota(jnp.int32, (GROUP_SIZE, tm), 0)
    gs_rows = []
    for g in range(N_GROUPS):
        blk = biased[g * GROUP_SIZE:(g + 1) * GROUP_SIZE, :]
        m1 = jnp.max(blk, axis=0, keepdims=True)
        first = jnp.min(jnp.where(blk == m1, g_iota, big), axis=0, keepdims=True)
        m2 = jnp.max(jnp.where(g_iota == first, -jnp.inf, blk), axis=0, keepdims=True)
        gs_rows.append(m1 + m2)
    gs = jnp.concatenate(gs_rows, axis=0)
    n_iota = lax.broadcasted_iota(jnp.int32, (N_GROUPS, tm), 0)
    keep = jnp.zeros((N_GROUPS, tm), jnp.bool_)
    for _ in range(TOPK_GROUPS):
        m = jnp.max(gs, axis=0, keepdims=True)
        first = jnp.min(jnp.where(gs == m, n_iota, big), axis=0, keepdims=True)
        hit = n_iota == first
        keep = jnp.logical_or(keep, hit)
        gs = jnp.where(hit, -jnp.inf, gs)
    keep_f = keep.astype(F32)
    expert_keep = jnp.concatenate(
        [jnp.broadcast_to(keep_f[g:g + 1, :], (GROUP_SIZE, tm)) for g in range(N_GROUPS)], axis=0)
    masked = jnp.where(expert_keep > 0.5, biased, -jnp.inf)

    idx_rows, w_rows, hits = [], [], []
    sel = jnp.zeros((N_EXPERTS, tm), F32)
    for _ in range(TOP_K):
        m = jnp.max(masked, axis=0, keepdims=True)
        first = jnp.min(jnp.where(masked == m, e_iota, big), axis=0, keepdims=True)
        hit = e_iota == first
        idx_rows.append(first)
        w_rows.append(jnp.sum(jnp.where(hit, scores, 0.0), axis=0, keepdims=True))
        hits.append(hit)
        sel = sel + hit.astype(F32)
        masked = jnp.where(hit, -jnp.inf, masked)
    w = jnp.concatenate(w_rows, axis=0)
    w = w / jnp.sum(w, axis=0, keepdims=True) * ROUTED_SCALE
    idx_ref[...] = jnp.concatenate(idx_rows, axis=0)
    w_ref[...] = w

    r = lax.broadcasted_iota(jnp.int32, (tm, tm), 0)
    c = lax.broadcasted_iota(jnp.int32, (tm, tm), 1)
    earlier = (r < c).astype(BF16)
    prefix = jnp.dot(sel.astype(BF16), earlier, preferred_element_type=F32) + run_ref[...]
    rank_rows = [jnp.sum(jnp.where(h, prefix, 0.0), axis=0, keepdims=True) for h in hits]
    rank_ref[...] = jnp.concatenate(rank_rows, axis=0).astype(jnp.int32)
    total = run_ref[...] + jnp.sum(sel, axis=1, keepdims=True)
    run_ref[...] = total
    cnt_ref[...] = jnp.broadcast_to(total, (N_EXPERTS, LANES))


def _route(x1b, wr_t, rb, tm=256):
    n = x1b.shape[0]
    tm = min(tm, n)
    return pl.pallas_call(
        functools.partial(_route_kernel, tm=tm),
        grid=(n // tm,),
        in_specs=[pl.BlockSpec((tm, D_MODEL), lambda i: (i, 0)),
                  pl.BlockSpec((N_EXPERTS, D_MODEL), lambda i: (0, 0)),
                  pl.BlockSpec((N_EXPERTS, 1), lambda i: (0, 0))],
        out_specs=[pl.BlockSpec((TOP_K, tm), lambda i: (0, i)),
                   pl.BlockSpec((TOP_K, tm), lambda i: (0, i)),
                   pl.BlockSpec((TOP_K, tm), lambda i: (0, i)),
                   pl.BlockSpec((N_EXPERTS, LANES), lambda i: (0, 0))],
        out_shape=[jax.ShapeDtypeStruct((TOP_K, n), jnp.int32),
                   jax.ShapeDtypeStruct((TOP_K, n), F32),
                   jax.ShapeDtypeStruct((TOP_K, n), jnp.int32),
                   jax.ShapeDtypeStruct((N_EXPERTS, LANES), F32)],
        scratch_shapes=[pltpu.VMEM((N_EXPERTS, 1), F32)],
        compiler_params=_cp(("arbitrary",)),
        name="router_topk",
    )(x1b, wr_t, rb)


def _dest_kernel(idx_ref, rank_ref, st_ref, dest_ref, *, tm):
    e_iota = lax.broadcasted_iota(jnp.int32, (N_EXPERTS, tm), 0)
    starts = st_ref[...]
    rows = []
    for k in range(TOP_K):
        hit = e_iota == idx_ref[k:k + 1, :]
        rows.append(jnp.sum(jnp.where(hit, starts, 0.0), axis=0, keepdims=True))
    dest_ref[...] = jnp.concatenate(rows, axis=0).astype(jnp.int32) + rank_ref[...]


def _dest(idx_t, rank_t, starts, tm=512):
    n = idx_t.shape[1]
    tm = min(tm, n)
    blk = lambda: pl.BlockSpec((TOP_K, tm), lambda i: (0, i))
    return pl.pallas_call(
        functools.partial(_dest_kernel, tm=tm),
        grid=(n // tm,),
        in_specs=[blk(), blk(), pl.BlockSpec((N_EXPERTS, 1), lambda i: (0, 0))],
        out_specs=blk(),
        out_shape=jax.ShapeDtypeStruct((TOP_K, n), jnp.int32),
        compiler_params=_cp(("arbitrary",)),
        name="moe_dest",
    )(idx_t, rank_t, starts.astype(F32).reshape(N_EXPERTS, 1))


SC_CORES = 2
SC_SUBCORES = 16
SC_WINDOW = 128
MOE_PARTS = 2


def _sc_mesh():
    return plsc.VectorSubcoreMesh(core_axis_name="c", subcore_axis_name="s",
                                  num_cores=SC_CORES, num_subcores=SC_SUBCORES)


def _sc_windows(n):
    workers = SC_CORES * SC_SUBCORES
    assert n % (SC_WINDOW * workers) == 0, n
    return n // SC_WINDOW // workers


def _sc_dispatch(dest_win, x1p):
    n = x1p.shape[0]
    per_worker = _sc_windows(n)

    @functools.partial(
        pl.kernel, mesh=_sc_mesh(),
        out_type=jax.ShapeDtypeStruct((n * TOP_K, HALF), U32),
        scratch_types=[pltpu.VMEM((TOP_K, SC_WINDOW), jnp.int32),
                       pltpu.VMEM((SC_WINDOW, HALF), U32)],
        name="moe_dispatch_sc")
    def run(dest_hbm, x_hbm, xs_hbm, idx_v, rows_v):
        worker = lax.axis_index("s") * SC_CORES + lax.axis_index("c")

        @pl.loop(0, per_worker)
        def _(c):
            win = worker * per_worker + c
            pltpu.sync_copy(dest_hbm.at[win], idx_v)
            pltpu.sync_copy(x_hbm.at[pl.ds(pl.multiple_of(win * SC_WINDOW, SC_WINDOW),
                                           SC_WINDOW)], rows_v)
            for k in range(TOP_K):
                pltpu.sync_copy(rows_v, xs_hbm.at[idx_v.at[k]])

    return run(dest_win, x1p)


def _sc_gather(dest_win, ys, n, part):
    n_part = n // MOE_PARTS
    per_worker = _sc_windows(n_part)
    first = part * (n_part // SC_WINDOW)

    @functools.partial(
        pl.kernel, mesh=_sc_mesh(),
        out_type=jax.ShapeDtypeStruct((TOP_K, n_part, HALF), U32),
        scratch_types=[pltpu.VMEM((TOP_K, SC_WINDOW), jnp.int32),
                       pltpu.VMEM((SC_WINDOW, HALF), U32)],
        name="moe_gather_sc")
    def run(dest_hbm, ys_hbm, yt_hbm, idx_v, rows_v):
        worker = lax.axis_index("s") * SC_CORES + lax.axis_index("c")

        @pl.loop(0, per_worker)
        def _(c):
            win = worker * per_worker + c
            start = pl.multiple_of(win * SC_WINDOW, SC_WINDOW)
            pltpu.sync_copy(dest_hbm.at[first + win], idx_v)
            for k in range(TOP_K):
                pltpu.sync_copy(ys_hbm.at[idx_v.at[k]], rows_v)
                pltpu.sync_copy(rows_v, yt_hbm.at[k, pl.ds(start, SC_WINDOW)])

    return run(dest_win, ys)


def _expert_kernel(tile_ref, exp_ref, lo_ref, hi_ref, xs_ref, wg_ref, wu_ref, wd_ref, ys_ref,
                   *, bm):
    w = pl.program_id(0)
    lo = lo_ref[w]
    hi = hi_ref[w]

    @pl.when(hi > lo)
    def _():
        x_lo, x_hi = _unpack_halves(xs_ref[...])
        x = jnp.concatenate([x_lo, x_hi], axis=1).astype(BF16)
        g = jnp.dot(x, wg_ref[0, 0].astype(BF16), preferred_element_type=F32)
        u = jnp.dot(x, wu_ref[0, 0].astype(BF16), preferred_element_type=F32)
        h = (g * jax.nn.sigmoid(g) * u).astype(BF16)
        y = _pack_halves(jnp.dot(h, wd_ref[0, 0].astype(BF16), preferred_element_type=F32))

        @pl.when(lo == 0)
        def _():
            ys_ref[...] = y

        @pl.when(lo > 0)
        def _():
            rows = lax.broadcasted_iota(jnp.int32, (bm, 1), 0)
            ys_ref[...] = jnp.where(rows >= lo, y, ys_ref[...])


def _experts(layer, item_tile, item_exp, item_lo, item_hi, xs, wg, wu, wd, bm):
    r = xs.shape[0]
    n_items = item_tile.shape[0]
    grid_spec = pltpu.PrefetchScalarGridSpec(
        num_scalar_prefetch=4,
        grid=(n_items,),
        in_specs=[pl.BlockSpec((bm, HALF), lambda w, t, e, lo, hi: (t[w], 0)),
                  pl.BlockSpec((1, 1, D_MODEL, EXPERT_FF),
                               lambda w, t, e, lo, hi: (layer, e[w], 0, 0)),
                  pl.BlockSpec((1, 1, D_MODEL, EXPERT_FF),
                               lambda w, t, e, lo, hi: (layer, e[w], 0, 0)),
                  pl.BlockSpec((1, 1, EXPERT_FF, D_MODEL),
                               lambda w, t, e, lo, hi: (layer, e[w], 0, 0))],
        out_specs=pl.BlockSpec((bm, HALF), lambda w, t, e, lo, hi: (t[w], 0)),
    )
    return pl.pallas_call(
        functools.partial(_expert_kernel, bm=bm),
        grid_spec=grid_spec,
        out_shape=jax.ShapeDtypeStruct((r, HALF), U32),
        compiler_params=_cp(("arbitrary",)),
        name="moe_experts",
    )(item_tile, item_exp, item_lo, item_hi, xs, wg, wu, wd)


def _expert_items(counts, n_rows, bm):
    n_tiles = n_rows // bm
    ends = jnp.cumsum(counts)
    starts = ends - counts
    bounds = jnp.sort(jnp.concatenate([jnp.arange(n_tiles, dtype=jnp.int32) * bm,
                                       starts.astype(jnp.int32)]))
    nxt = jnp.concatenate([bounds[1:], jnp.array([n_rows], jnp.int32)])
    tile = jnp.minimum(bounds // bm, n_tiles - 1)
    exp = jnp.sum((ends[None, :] <= bounds[:, None]).astype(jnp.int32), axis=1)
    exp = jnp.minimum(exp, N_EXPERTS - 1)
    lo = bounds - tile * bm
    hi = nxt - tile * bm
    return tile.astype(jnp.int32), exp, lo.astype(jnp.int32), hi.astype(jnp.int32), starts


def _combine_rows_kernel(yt_ref, tw_ref, x1_ref, x1b_ref, wsg_ref, wsu_ref, wsd_ref, lg_ref, lb_ref,
                         *rest):
    o_ref, ob_ref = rest[-2:]
    tw = tw_ref[...]
    r_lo = jnp.zeros((tw.shape[0], HALF), F32)
    r_hi = jnp.zeros((tw.shape[0], HALF), F32)
    for k in range(TOP_K):
        y_lo, y_hi = _unpack_halves(yt_ref[k])
        r_lo = r_lo + tw[:, k:k + 1] * y_lo
        r_hi = r_hi + tw[:, k:k + 1] * y_hi
    routed = jnp.concatenate([r_lo, r_hi], axis=1)
    xb = x1b_ref[...]
    g = jnp.dot(xb, wsg_ref[...], preferred_element_type=F32)
    u = jnp.dot(xb, wsu_ref[...], preferred_element_type=F32)
    h = (g * jax.nn.sigmoid(g) * u).astype(BF16)
    shared = jnp.dot(h, wsd_ref[...], preferred_element_type=F32)
    y = _layer_norm(DN_ALPHA * x1_ref[...] + (shared + routed), lg_ref[...], lb_ref[...])
    o_ref[...] = y
    ob_ref[...] = y.astype(BF16)


def _combine_rows(yt, tw, x1, x1b, wsg, wsu, wsd, lg, lb, part, prev, tc=256):
    n = x1.shape[0]
    n_part = n // MOE_PARTS
    tc = min(tc, n_part)
    off = part * (n_part // tc)
    full = lambda a: pl.BlockSpec(a.shape, lambda i: (0,) * a.ndim)
    rows = lambda w: pl.BlockSpec((tc, w), lambda i: (off + i, 0))
    carried = [] if prev is None else list(prev)
    return pl.pallas_call(
        _combine_rows_kernel,
        grid=(n_part // tc,),
        in_specs=[pl.BlockSpec((TOP_K, tc, HALF), lambda i: (0, i, 0)),
                  rows(TOP_K), rows(D_MODEL), rows(D_MODEL),
                  full(wsg), full(wsu), full(wsd), full(lg), full(lb)]
                 + [pl.BlockSpec(memory_space=pl.ANY) for _ in carried],
        out_specs=[rows(D_MODEL), rows(D_MODEL)],
        out_shape=[jax.ShapeDtypeStruct((n, D_MODEL), F32),
                   jax.ShapeDtypeStruct((n, D_MODEL), BF16)],
        input_output_aliases={9 + j: j for j in range(len(carried))},
        compiler_params=_cp(("arbitrary",)),
        name="moe_combine_ln",
    )(yt, tw, x1, x1b, wsg, wsu, wsd, lg, lb, *carried)


def _head_cols(w, heads, width, lo, hi):
    return w.reshape(w.shape[0], heads, width)[:, :, lo:hi]


def _prep_layer(w_in, b_gate, b_forget, mla_q_norm, w_uq, mla_kv_norm, w_ukv):
    f = lambda a: a.astype(BF16)
    half = MLA_ROPE_DIM // 2
    d = w_in.shape[0]
    qscale = HEAD_DIM ** -0.5
    w_qkv = jnp.concatenate([
        w_in[:, OFF_SB:OFF_SB + SB_W] * qscale, w_in[:, OFF_SB + SB_W:OFF_FOX],
        w_in[:, OFF_FOX:OFF_FOX + FOX_W] * (qscale * LOG2E), w_in[:, OFF_FOX + FOX_W:OFF_FGATE]],
        axis=1)
    w_kr = w_in[:, OFF_KR:OFF_GATE]
    w_kr_rot = jnp.concatenate([-w_kr[:, half:], w_kr[:, :half]], axis=1)
    z = lambda c: jnp.zeros((d, c), F32)
    pad_rope = LANES - MLA_QK_DIM
    w_small = jnp.concatenate([
        w_in[:, OFF_DQ:OFF_DKV], w_in[:, OFF_DKV:OFF_KR],
        z(MLA_NOPE_DIM), w_kr, z(pad_rope),
        z(MLA_NOPE_DIM), w_kr_rot, z(pad_rope),
        w_in[:, OFF_FGATE:OFF_DQ], z(LANES - FOX_HEADS)], axis=1)
    w_gate = w_in[:, OFF_GATE:]

    r = w_uq.shape[0]
    q_nope = _head_cols(w_uq, MLA_HEADS, MLA_QK_DIM, 0, MLA_NOPE_DIM)
    q_rope = _head_cols(w_uq, MLA_HEADS, MLA_QK_DIM, MLA_NOPE_DIM, MLA_QK_DIM)
    q_rope_rot = jnp.concatenate([-q_rope[:, :, half:], q_rope[:, :, :half]], axis=2)
    zq = lambda c: jnp.zeros((r, MLA_HEADS, c), F32)
    wqa = jnp.concatenate([q_nope, q_rope, zq(pad_rope)], axis=2).reshape(r, MLA_HEADS * LANES)
    wqb = jnp.concatenate([zq(MLA_NOPE_DIM), q_rope_rot, zq(pad_rope)], axis=2).reshape(
        r, MLA_HEADS * LANES)
    rk = w_ukv.shape[0]
    kvw = MLA_NOPE_DIM + MLA_V_DIM
    k_nope = _head_cols(w_ukv, MLA_HEADS, kvw, 0, MLA_NOPE_DIM)
    wk = jnp.concatenate([k_nope, jnp.zeros((rk, MLA_HEADS, LANES - MLA_NOPE_DIM), F32)],
                         axis=2).reshape(rk, MLA_HEADS * LANES)
    wv = _head_cols(w_ukv, MLA_HEADS, kvw, MLA_NOPE_DIM, kvw).reshape(rk, MLA_HEADS * MLA_V_DIM)
    bf = jnp.concatenate([b_forget, jnp.zeros((LANES - FOX_HEADS,), F32)]).reshape(1, LANES)
    return dict(w_qkv=f(w_qkv), w_small=f(w_small), w_gate=f(w_gate),
                b_gate=b_gate.reshape(1, -1), wqa=f(wqa), wqb=f(wqb), wk=f(wk), wv=f(wv),
                qn=mla_q_norm.reshape(1, -1), kvn=mla_kv_norm.reshape(1, -1), bf=bf)


def _rope_tables(seq):
    half = MLA_ROPE_DIM // 2
    inv_freq = jnp.power(ROPE_BASE, -jnp.arange(half, dtype=F32) / half)
    ang = jnp.arange(seq).astype(F32)[:, None] * inv_freq[None, :]
    cos = jnp.concatenate([jnp.cos(ang), jnp.cos(ang)], axis=1)
    sin = jnp.concatenate([jnp.sin(ang), jnp.sin(ang)], axis=1)
    pad = lambda t: jnp.concatenate([jnp.zeros((seq, MLA_NOPE_DIM), F32), t,
                                     jnp.zeros((seq, LANES - MLA_QK_DIM), F32)], axis=1)
    return pad(cos), pad(sin)


EXPERT_BM = 512


def kernel(x, ln1_g, ln1_b, ln2_g, ln2_b, w_in, b_gate, b_forget, mla_q_norm, w_uq, mla_kv_norm,
           w_ukv, w_proj_sb, w_proj_fox, w_proj_mla, w_out, w_router, router_bias,
           w_exp_gate, w_exp_up, w_exp_down, w_sh_gate, w_sh_up, w_sh_down):
    b, seq, d = x.shape
    n = b * seq
    depth = w_in.shape[0]
    cos_t, sin_t = _rope_tables(seq)
    xf = x.reshape(n, d)
    xb = xf.astype(BF16)
    f = lambda a: a.astype(BF16)
    bm = min(EXPERT_BM, n * TOP_K)
    for l in range(depth):
        p = _prep_layer(w_in[l], b_gate[l], b_forget[l], mla_q_norm[l], w_uq[l], mla_kv_norm[l],
                        w_ukv[l])
        qkv = _matmul(xb, p["w_qkv"], None, BF16)
        gates = _matmul(xb, p["w_gate"], p["b_gate"], BF16, act="sigmoid")
        small = _matmul(xb, p["w_small"], None, F32, tn=SMALL_W)
        q_mla, k_mla, v_mla, lf = _mla_prep(small, p["qn"], p["kvn"], p["wqa"], p["wqb"], p["wk"],
                                            p["wv"], cos_t, sin_t, p["bf"], seq)
        cum, cumt = _cumsum(lf.reshape(b, seq, LANES))
        cumt4 = cumt[:, :FOX_HEADS, :].reshape(b, FOX_HEADS, 1, seq)
        qkv3 = qkv.reshape(b, seq, QKV_W)
        o_sb = _sb_attention(qkv3).reshape(n, SB_W)
        o_fox = _fox_attention(qkv3, cum, cumt4).reshape(n, FOX_W)
        o_mla = _mla_attention(q_mla.reshape(b, seq, -1), k_mla.reshape(b, seq, -1),
                               v_mla.reshape(b, seq, -1)).reshape(n, -1)
        x1, x1b, x1p = _merge(o_sb, o_fox, o_mla, gates, xf, f(w_proj_sb[l]), f(w_proj_fox[l]),
                              f(w_proj_mla[l]), f(w_out[l]), ln1_g[l].reshape(1, d),
                              ln1_b[l].reshape(1, d))

        idx_t, tw_t, rank_t, cnt = _route(x1b, f(w_router[l].T), router_bias[l].reshape(-1, 1))
        counts = cnt[:, 0].astype(jnp.int32)
        item_tile, item_exp, item_lo, item_hi, starts = _expert_items(counts, n * TOP_K, bm)
        dest_t = _dest(idx_t, rank_t, starts)
        dest_win = dest_t.reshape(TOP_K, n // SC_WINDOW, SC_WINDOW).transpose(1, 0, 2)
        xs = _sc_dispatch(dest_win, x1p)
        ys = _experts(l, item_tile, item_exp, item_lo, item_hi, xs, w_exp_gate, w_exp_up,
                      w_exp_down, bm)
        yts = [_sc_gather(dest_win, ys, n, part) for part in range(MOE_PARTS)]
        out = None
        for part in range(MOE_PARTS):
            out = _combine_rows(yts[part], tw_t.T, x1, x1b, f(w_sh_gate[l]), f(w_sh_up[l]),
                                f(w_sh_down[l]), ln2_g[l].reshape(1, d), ln2_b[l].reshape(1, d),
                                part, out)
        xf, xb = out
    return xf.reshape(b, seq, d)
```

```python
import functools

import jax
import jax.numpy as jnp
from jax import lax
from jax.experimental import pallas as pl
from jax.experimental.pallas import tpu as pltpu
from jax.experimental.pallas import tpu_sc as plsc

F32 = jnp.float32
BF16 = jnp.bfloat16

D_MODEL = 1024
HEAD_DIM = 64
SB_HEADS = 4
FOX_HEADS = 4
MLA_HEADS = 8
MLA_Q_RANK = 256
MLA_KV_RANK = 128
MLA_NOPE_DIM = 64
MLA_ROPE_DIM = 32
MLA_V_DIM = 64
ROPE_BASE = 10000.0
N_BRANCHES = 3
N_EXPERTS = 256
TOP_K = 8
N_GROUPS = 8
TOPK_GROUPS = 4
GROUP_SIZE = N_EXPERTS // N_GROUPS
EXPERT_FF = 256
SHARED_FF = 256
ROUTED_SCALE = 2.5
CHUNK = 64
LN_EPS = 1e-5
RMS_EPS = 1e-6
DEPTH = 2
DN_ALPHA = (2 * DEPTH) ** 0.25

SB_W = SB_HEADS * HEAD_DIM
FOX_W = FOX_HEADS * HEAD_DIM
MLA_QK_DIM = MLA_NOPE_DIM + MLA_ROPE_DIM
OFF_SB = 0
OFF_FOX = OFF_SB + 3 * SB_W
OFF_FGATE = OFF_FOX + 3 * FOX_W
OFF_DQ = OFF_FGATE + FOX_HEADS
OFF_DKV = OFF_DQ + MLA_Q_RANK
OFF_KR = OFF_DKV + MLA_KV_RANK
OFF_GATE = OFF_KR + MLA_ROPE_DIM

LANES = 128
SUBLANES = 8
QKV_W = 3 * SB_W + 3 * FOX_W
SMALL_W = MLA_Q_RANK + MLA_KV_RANK + 3 * LANES
SB_CUTOFF = -104.0
LOG2E = 1.4426950408889634

VMEM_LIMIT = 48 * 1024 * 1024


def _cp(sem, vmem=VMEM_LIMIT):
    return pltpu.CompilerParams(dimension_semantics=sem, vmem_limit_bytes=vmem)


def _mm_kernel(x_ref, w_ref, b_ref, o_ref, *, act):
    acc = jnp.dot(x_ref[...], w_ref[...], preferred_element_type=F32)
    if act == "sigmoid":
        acc = jax.nn.sigmoid(acc + b_ref[...])
    o_ref[...] = acc.astype(o_ref.dtype)


def _matmul(x, w, bias, out_dtype, act=None, tm=1024, tn=512):
    n, k = x.shape
    c = w.shape[1]
    tm = min(tm, n)
    tn = min(tn, c)
    if bias is None:
        bias = jnp.zeros((1, c), F32)
    return pl.pallas_call(
        functools.partial(_mm_kernel, act=act),
        grid=(n // tm, c // tn),
        in_specs=[pl.BlockSpec((tm, k), lambda i, j: (i, 0)),
                  pl.BlockSpec((k, tn), lambda i, j: (0, j)),
                  pl.BlockSpec((1, tn), lambda i, j: (0, j))],
        out_specs=pl.BlockSpec((tm, tn), lambda i, j: (i, j)),
        out_shape=jax.ShapeDtypeStruct((n, c), out_dtype),
        compiler_params=_cp(("arbitrary", "arbitrary")),
        name="proj_matmul",
    )(x, w, bias)


def _split_bf16(x, parts):
    out = []
    for _ in range(parts):
        h = x.astype(BF16)
        out.append(h)
        x = x - h.astype(F32)
    return out


def _mla_prep_kernel(sm_ref, qn_ref, kvn_ref, wqa_ref, wqb_ref, wk_ref, wv_ref,
                     cos_ref, sin_ref, bf_ref, q_ref, k_ref, v_ref, lf_ref):
    sm = sm_ref[...]
    dq = sm[:, :MLA_Q_RANK]
    dkv = sm[:, MLA_Q_RANK:MLA_Q_RANK + MLA_KV_RANK]
    o = MLA_Q_RANK + MLA_KV_RANK
    kr = sm[:, o:o + LANES]
    kr_rot = sm[:, o + LANES:o + 2 * LANES]
    fg = sm[:, o + 2 * LANES:o + 3 * LANES]

    cq = dq * lax.rsqrt(jnp.mean(dq * dq, axis=-1, keepdims=True) + RMS_EPS) * qn_ref[...]
    ckv = dkv * lax.rsqrt(jnp.mean(dkv * dkv, axis=-1, keepdims=True) + RMS_EPS) * kvn_ref[...]
    cq = cq.astype(BF16)
    ckv = ckv.astype(BF16)

    cosk = cos_ref[...]
    sink = sin_ref[...]
    lane = lax.broadcasted_iota(jnp.int32, (1, LANES), 1)
    nope = (lane < MLA_NOPE_DIM).astype(F32)
    scale = MLA_QK_DIM ** -0.5 * LOG2E
    cq_tab = jnp.concatenate([(cosk + nope) * scale] * MLA_HEADS, axis=1)
    sq_tab = jnp.concatenate([sink * scale] * MLA_HEADS, axis=1)

    qa = jnp.dot(cq, wqa_ref[...], preferred_element_type=F32)
    qb = jnp.dot(cq, wqb_ref[...], preferred_element_type=F32)
    q_ref[...] = (qa * cq_tab + qb * sq_tab).astype(q_ref.dtype)

    k_rope = kr * cosk + kr_rot * sink
    ka = jnp.dot(ckv, wk_ref[...], preferred_element_type=F32)
    k_ref[...] = (ka + jnp.concatenate([k_rope] * MLA_HEADS, axis=1)).astype(k_ref.dtype)
    v_ref[...] = jnp.dot(ckv, wv_ref[...], preferred_element_type=F32).astype(v_ref.dtype)
    lf_ref[...] = jax.nn.log_sigmoid(fg + bf_ref[...])


def _mla_prep(small, qn, kvn, wqa, wqb, wk, wv, cos_t, sin_t, bf, seq, tm=512):
    n = small.shape[0]
    tm = min(tm, seq)
    sblocks = seq // tm
    hw = MLA_HEADS * LANES
    full = lambda a: pl.BlockSpec(a.shape, lambda i: (0,) * a.ndim)
    return pl.pallas_call(
        _mla_prep_kernel,
        grid=(n // tm,),
        in_specs=[pl.BlockSpec((tm, SMALL_W), lambda i: (i, 0)),
                  full(qn), full(kvn), full(wqa), full(wqb), full(wk), full(wv),
                  pl.BlockSpec((tm, LANES), lambda i: (i % sblocks, 0)),
                  pl.BlockSpec((tm, LANES), lambda i: (i % sblocks, 0)),
                  full(bf)],
        out_specs=[pl.BlockSpec((tm, hw), lambda i: (i, 0)),
                   pl.BlockSpec((tm, hw), lambda i: (i, 0)),
                   pl.BlockSpec((tm, MLA_HEADS * MLA_V_DIM), lambda i: (i, 0)),
                   pl.BlockSpec((tm, LANES), lambda i: (i, 0))],
        out_shape=[jax.ShapeDtypeStruct((n, hw), BF16),
                   jax.ShapeDtypeStruct((n, hw), BF16),
                   jax.ShapeDtypeStruct((n, MLA_HEADS * MLA_V_DIM), BF16),
                   jax.ShapeDtypeStruct((n, LANES), F32)],
        compiler_params=_cp(("arbitrary",)),
        name="mla_prep",
    )(small, qn, kvn, wqa, wqb, wk, wv, cos_t, sin_t, bf)


def _cumsum_kernel(lf_ref, cum_ref, cumt_ref, *, seq):
    r = lax.broadcasted_iota(jnp.int32, (LANES, LANES), 0)
    c = lax.broadcasted_iota(jnp.int32, (LANES, LANES), 1)
    lower = (c <= r).astype(BF16)

    def body(j, carry):
        start = pl.multiple_of(j * LANES, LANES)
        blk = lf_ref[0, pl.ds(start, LANES), :]
        acc = carry
        for part in _split_bf16(blk, 3):
            acc = acc + jnp.dot(lower, part, preferred_element_type=F32)
        scaled = acc * LOG2E
        cum_ref[0, pl.ds(start, LANES), :] = scaled
        cumt_ref[0, :, pl.ds(start, LANES)] = scaled.T[:SUBLANES, :]
        return jnp.broadcast_to(acc[LANES - 1:LANES, :], (LANES, LANES))

    lax.fori_loop(0, seq // LANES, body, jnp.zeros((LANES, LANES), F32))


def _cumsum(lf3):
    b, seq, _ = lf3.shape
    return pl.pallas_call(
        functools.partial(_cumsum_kernel, seq=seq),
        grid=(b,),
        in_specs=[pl.BlockSpec((1, seq, LANES), lambda i: (i, 0, 0))],
        out_specs=[pl.BlockSpec((1, seq, LANES), lambda i: (i, 0, 0)),
                   pl.BlockSpec((1, SUBLANES, seq), lambda i: (i, 0, 0))],
        out_shape=[jax.ShapeDtypeStruct((b, seq, LANES), F32),
                   jax.ShapeDtypeStruct((b, SUBLANES, seq), F32)],
        compiler_params=_cp(("arbitrary",)),
        name="forget_cumsum",
    )(lf3)


_NT = (((1,), (1,)), ((), ()))
NEG_INIT = -1e30
_TN = (((0,), (0,)), ((), ()))
ATTN_TK = 256


def _head_halves(q):
    lane = lax.broadcasted_iota(jnp.int32, (1, LANES), 1)
    zero = jnp.zeros_like(q)
    return [jnp.where(lane < HEAD_DIM, q, zero), jnp.where(lane >= HEAD_DIM, q, zero)]


def _mask_first_block(x, keep, fill, tk):
    head = jnp.where(keep, x[:, :tk], fill)
    return head if x.shape[1] == tk else jnp.concatenate([head, x[:, tk:]], axis=1)


def _fox_kernel(q_ref, k_ref, v_ref, cc_ref, cr_ref, o_ref, *, seq, tk):
    hp = pl.program_id(1)
    lane = lax.broadcasted_iota(jnp.int32, (1, LANES), 1)
    qs = _head_halves(q_ref[0])
    key = lax.broadcasted_iota(jnp.int32, (tk, tk), 0)
    qry = lax.broadcasted_iota(jnp.int32, (tk, tk), 1)
    causal = key <= qry
    carry = [(jnp.full((1, seq), NEG_INIT, F32), jnp.zeros((1, seq), F32),
              jnp.zeros((LANES, seq), F32)) for _ in range(2)]
    for j in range(seq // tk):
        q0 = j * tk
        k = k_ref[0, q0:q0 + tk, :]
        v = v_ref[0, q0:q0 + tk, :]
        cc = cc_ref[0, q0:q0 + tk, :]
        for hh in range(2):
            m, l, acc = carry[hh]
            cs = jnp.sum(jnp.where(lane == 2 * hp + hh, cc, 0.0), axis=1, keepdims=True)
            ct = cr_ref[0, hh, :, q0:]
            st = lax.dot_general(k, qs[hh][q0:, :], _NT, preferred_element_type=F32)
            st = _mask_first_block((st + ct) - cs, causal, -jnp.inf, tk)
            mn, ln, an = _softmax_step_t(st, m[:, q0:], l[:, q0:], acc[:, q0:], v)
            if q0:
                mn = jnp.concatenate([m[:, :q0], mn], axis=1)
                ln = jnp.concatenate([l[:, :q0], ln], axis=1)
                an = jnp.concatenate([acc[:, :q0], an], axis=1)
            carry[hh] = (mn, ln, an)
    _pair_out_t([carry[0][2], carry[1][2]], [carry[0][1], carry[1][1]], o_ref)


def _fox_attention(qkv3, cum, cumt4):
    b, seq, _ = qkv3.shape
    tk = min(ATTN_TK, seq)
    hp = FOX_HEADS // 2
    qoff = OFF_FOX // LANES
    return pl.pallas_call(
        functools.partial(_fox_kernel, seq=seq, tk=tk),
        grid=(b, hp),
        in_specs=[pl.BlockSpec((1, seq, LANES), lambda bi, h: (bi, 0, qoff + h)),
                  pl.BlockSpec((1, seq, LANES), lambda bi, h: (bi, 0, qoff + hp + h)),
                  pl.BlockSpec((1, seq, LANES), lambda bi, h: (bi, 0, qoff + 2 * hp + h)),
                  pl.BlockSpec((1, seq, LANES), lambda bi, h: (bi, 0, 0)),
                  pl.BlockSpec((1, 2, 1, seq), lambda bi, h: (bi, h, 0, 0))],
        out_specs=pl.BlockSpec((1, seq, LANES), lambda bi, h: (bi, 0, h)),
        out_shape=jax.ShapeDtypeStruct((b, seq, FOX_W), BF16),
        compiler_params=_cp(("arbitrary", "arbitrary")),
        name="fox_attention",
    )(qkv3, qkv3, qkv3, cum, cumt4)


def _softmax_step_t(st, m, l, acc, v):
    m_new = jnp.maximum(m, jnp.max(st, axis=0, keepdims=True))
    alpha = jnp.exp2(m - m_new)
    p = jnp.exp2(st - m_new)
    l = alpha * l + jnp.sum(p, axis=0, keepdims=True)
    acc = alpha * acc + lax.dot_general(v, p.astype(BF16), _TN, preferred_element_type=F32)
    return m_new, l, acc


def _pair_out_t(accs, ls, o_ref):
    sub = lax.broadcasted_iota(jnp.int32, (LANES, 1), 0)
    ot = jnp.where(sub < HEAD_DIM, accs[0] / ls[0], accs[1] / ls[1])
    o_ref[0] = ot.T.astype(o_ref.dtype)


def _mla_kernel(q_ref, k_ref, v_ref, o_ref, *, seq, tk):
    q = q_ref[0]
    qs = [q[:, :LANES], q[:, LANES:]]
    key = lax.broadcasted_iota(jnp.int32, (tk, tk), 0)
    qry = lax.broadcasted_iota(jnp.int32, (tk, tk), 1)
    shift = CHUNK.bit_length() - 1
    chunk_causal = (key >> shift) <= (qry >> shift)
    carry = [(jnp.full((1, seq), NEG_INIT, F32), jnp.zeros((1, seq), F32),
              jnp.zeros((LANES, seq), F32)) for _ in range(2)]
    for j in range(seq // tk):
        q0 = j * tk
        k = k_ref[0, q0:q0 + tk, :]
        v = v_ref[0, q0:q0 + tk, :]
        for hh in range(2):
            m, l, acc = carry[hh]
            st = lax.dot_general(k[:, hh * LANES:(hh + 1) * LANES], qs[hh][q0:, :], _NT,
                                 preferred_element_type=F32)
            st = _mask_first_block(st, chunk_causal, -jnp.inf, tk)
            mn, ln, an = _softmax_step_t(st, m[:, q0:], l[:, q0:], acc[:, q0:], v)
            if q0:
                mn = jnp.concatenate([m[:, :q0], mn], axis=1)
                ln = jnp.concatenate([l[:, :q0], ln], axis=1)
                an = jnp.concatenate([acc[:, :q0], an], axis=1)
            carry[hh] = (mn, ln, an)
    _pair_out_t([carry[0][2], carry[1][2]], [carry[0][1], carry[1][1]], o_ref)


def _mla_attention(q3, k3, v3):
    b, seq, _ = q3.shape
    tk = min(ATTN_TK, seq)
    hp = MLA_HEADS // 2
    return pl.pallas_call(
        functools.partial(_mla_kernel, seq=seq, tk=tk),
        grid=(b, hp),
        in_specs=[pl.BlockSpec((1, seq, 2 * LANES), lambda bi, h: (bi, 0, h)),
                  pl.BlockSpec((1, seq, 2 * LANES), lambda bi, h: (bi, 0, h)),
                  pl.BlockSpec((1, seq, LANES), lambda bi, h: (bi, 0, h))],
        out_specs=pl.BlockSpec((1, seq, LANES), lambda bi, h: (bi, 0, h)),
        out_shape=jax.ShapeDtypeStruct((b, seq, MLA_HEADS * MLA_V_DIM), BF16),
        compiler_params=_cp(("arbitrary", "arbitrary")),
        name="mla_attention",
    )(q3, k3, v3)


def _softplus(z):
    return jnp.maximum(z, 0.0) + jnp.log(1.0 + jnp.exp(-jnp.abs(z)))


def _sb_kernel(q_ref, k_ref, v_ref, o_ref, run_ref, acc_ref, *, seq, tk):
    n = seq // tk
    qs = _head_halves(q_ref[0])
    key = lax.broadcasted_iota(jnp.int32, (tk, tk), 0)
    qry = lax.broadcasted_iota(jnp.int32, (tk, tk), 1)
    strict = key < qry
    later = (qry > key).astype(BF16)

    def unit(j, qlo, qhi, diag):
        k = k_ref[0, j * tk:(j + 1) * tk, :]
        v = v_ref[0, j * tk:(j + 1) * tk, :]
        for hh in range(2):
            z = lax.dot_general(k, qs[hh][qlo:qhi, :], _NT, preferred_element_type=F32)
            sp = _softplus(z)
            log_keep = -sp
            if diag:
                log_keep = _mask_first_block(log_keep, strict, 0.0, tk)
            between = run_ref[hh, :, qlo:qhi]
            for part in _split_bf16(log_keep, 2):
                between = between + jnp.dot(later, part, preferred_element_type=F32)
            w = jnp.exp((z - sp) + between)
            if diag:
                w = _mask_first_block(w, strict, 0.0, tk)
            acc_ref[hh, :, qlo:qhi] += lax.dot_general(v, w.astype(BF16), _TN,
                                                       preferred_element_type=F32)
            run_ref[hh, :, qlo:qhi] += jnp.sum(log_keep, axis=0, keepdims=True)

    run_ref[...] = jnp.zeros_like(run_ref)
    acc_ref[...] = jnp.zeros_like(acc_ref)
    for j in range(n - 1, -1, -1):
        unit(j, j * tk, min((j + 2) * tk, seq), True)

    for j in range(n - 3, -1, -1):
        qlo = (j + 2) * tk

        @pl.when(jnp.max(run_ref[:, :, qlo:]) >= SB_CUTOFF)
        def _():
            unit(j, qlo, seq, False)

    sub = lax.broadcasted_iota(jnp.int32, (LANES, 1), 0)
    o_ref[0] = jnp.where(sub < HEAD_DIM, acc_ref[0], acc_ref[1]).T.astype(o_ref.dtype)


def _sb_attention(qkv3):
    b, seq, _ = qkv3.shape
    tk = min(ATTN_TK, seq)
    hp = SB_HEADS // 2
    qoff = OFF_SB // LANES
    return pl.pallas_call(
        functools.partial(_sb_kernel, seq=seq, tk=tk),
        grid=(b, hp),
        in_specs=[pl.BlockSpec((1, seq, LANES), lambda bi, h: (bi, 0, qoff + h)),
                  pl.BlockSpec((1, seq, LANES), lambda bi, h: (bi, 0, qoff + hp + h)),
                  pl.BlockSpec((1, seq, LANES), lambda bi, h: (bi, 0, qoff + 2 * hp + h))],
        out_specs=pl.BlockSpec((1, seq, LANES), lambda bi, h: (bi, 0, h)),
        out_shape=jax.ShapeDtypeStruct((b, seq, SB_W), BF16),
        scratch_shapes=[pltpu.VMEM((2, 1, seq), F32), pltpu.VMEM((2, LANES, seq), F32)],
        compiler_params=_cp(("arbitrary", "arbitrary")),
        name="sb_attention",
    )(qkv3, qkv3, qkv3)


def _layer_norm(y, g, b):
    yc = y - jnp.mean(y, axis=-1, keepdims=True)
    var = jnp.mean(yc * yc, axis=-1, keepdims=True)
    return yc * lax.rsqrt(var + LN_EPS) * g + b


HALF = D_MODEL // 2
U32 = jnp.uint32


def _pack_halves(y):
    bits = lax.bitcast_convert_type(y.astype(BF16).astype(F32), U32)
    return (bits[:, :HALF] >> 16) | (bits[:, HALF:] & U32(0xFFFF0000))


def _unpack_halves(w):
    lo = lax.bitcast_convert_type(w << 16, F32)
    hi = lax.bitcast_convert_type(w & U32(0xFFFF0000), F32)
    return lo, hi


def _merge_kernel(osb_ref, ofox_ref, omla_ref, g_ref, x_ref, wsb_ref, wfox_ref, wmla_ref,
                  wout_ref, lg_ref, lb_ref, x1_ref, x1b_ref, x1p_ref):
    g = g_ref[...].astype(F32)
    a = jnp.dot(osb_ref[...], wsb_ref[...], preferred_element_type=F32)
    b = jnp.dot(ofox_ref[...], wfox_ref[...], preferred_element_type=F32)
    c = jnp.dot(omla_ref[...], wmla_ref[...], preferred_element_type=F32)
    mixed = (g[:, :D_MODEL] * a + g[:, D_MODEL:2 * D_MODEL] * b + g[:, 2 * D_MODEL:] * c)
    mix = jnp.dot(mixed.astype(BF16), wout_ref[...], preferred_element_type=F32)
    y = _layer_norm(DN_ALPHA * x_ref[...] + mix, lg_ref[...], lb_ref[...])
    x1_ref[...] = y
    x1b_ref[...] = y.astype(BF16)
    x1p_ref[...] = _pack_halves(y)


def _merge(o_sb, o_fox, o_mla, gates, x, wsb, wfox, wmla, wout, lg, lb, tm=512):
    n = x.shape[0]
    tm = min(tm, n)
    full = lambda a: pl.BlockSpec(a.shape, lambda i: (0,) * a.ndim)
    rows = lambda w: pl.BlockSpec((tm, w), lambda i: (i, 0))
    return pl.pallas_call(
        _merge_kernel,
        grid=(n // tm,),
        in_specs=[rows(SB_W), rows(FOX_W), rows(MLA_HEADS * MLA_V_DIM), rows(N_BRANCHES * D_MODEL),
                  rows(D_MODEL), full(wsb), full(wfox), full(wmla), full(wout), full(lg), full(lb)],
        out_specs=[rows(D_MODEL), rows(D_MODEL), rows(HALF)],
        out_shape=[jax.ShapeDtypeStruct((n, D_MODEL), F32),
                   jax.ShapeDtypeStruct((n, D_MODEL), BF16),
                   jax.ShapeDtypeStruct((n, HALF), U32)],
        compiler_params=_cp(("arbitrary",)),
        name="merge_outproj_ln",
    )(o_sb, o_fox, o_mla, gates, x, wsb, wfox, wmla, wout, lg, lb)


def _route_kernel(x_ref, wr_ref, rb_ref, idx_ref, w_ref, rank_ref, cnt_ref, run_ref, *, tm):
    step = pl.program_id(0)

    @pl.when(step == 0)
    def _():
        run_ref[...] = jnp.zeros_like(run_ref)

    logits = lax.dot_general(wr_ref[...], x_ref[...], _NT, preferred_element_type=F32)
    scores = jax.nn.sigmoid(logits)
    biased = scores + rb_ref[...]
    e_iota = lax.broadcasted_iota(jnp.int32, (N_EXPERTS, tm), 0)
    big = jnp.int32(1 << 20)

    g_iota = lax.broadcasted_iota(jnp.int32, (GROUP_SIZE, tm), 0)
    gs_rows = []
    for g in range(N_GROUPS):
        blk = biased[g * GROUP_SIZE:(g + 1) * GROUP_SIZE, :]
        m1 = jnp.max(blk, axis=0, keepdims=True)
        first = jnp.min(jnp.where(blk == m1, g_iota, big), axis=0, keepdims=True)
        m2 = jnp.max(jnp.where(g_iota == first, -jnp.inf, blk), axis=0, keepdims=True)
        gs_rows.append(m1 + m2)
    gs = jnp.concatenate(gs_rows, axis=0)
    n_iota = lax.broadcasted_iota(jnp.int32, (N_GROUPS, tm), 0)
    keep = jnp.zeros((N_GROUPS, tm), jnp.bool_)
    for _ in range(TOPK_GROUPS):
        m = jnp.max(gs, axis=0, keepdims=True)
        first = jnp.min(jnp.where(gs == m, n_iota, big), axis=0, keepdims=True)
        hit = n_iota == first
        keep = jnp.logical_or(keep, hit)
        gs = jnp.where(hit, -jnp.inf, gs)
    keep_f = keep.astype(F32)
    expert_keep = jnp.concatenate(
        [jnp.broadcast_to(keep_f[g:g + 1, :], (GROUP_SIZE, tm)) for g in range(N_GROUPS)], axis=0)
    masked = jnp.where(expert_keep > 0.5, biased, -jnp.inf)

    idx_rows, w_rows, hits = [], [], []
    sel = jnp.zeros((N_EXPERTS, tm), F32)
    for _ in range(TOP_K):
        m = jnp.max(masked, axis=0, keepdims=True)
        first = jnp.min(jnp.where(masked == m, e_iota, big), axis=0, keepdims=True)
        hit = e_iota == first
        idx_rows.append(first)
        w_rows.append(jnp.sum(jnp.where(hit, scores, 0.0), axis=0, keepdims=True))
        hits.append(hit)
        sel = sel + hit.astype(F32)
        masked = jnp.where(hit, -jnp.inf, masked)
    w = jnp.concatenate(w_rows, axis=0)
    w = w / jnp.sum(w, axis=0, keepdims=True) * ROUTED_SCALE
    idx_ref[...] = jnp.concatenate(idx_rows, axis=0)
    w_ref[...] = w

    r = lax.broadcasted_iota(jnp.int32, (tm, tm), 0)
    c = lax.broadcasted_iota(jnp.int32, (tm, tm), 1)
    earlier = (r < c).astype(BF16)
    prefix = jnp.dot(sel.astype(BF16), earlier, preferred_element_type=F32) + run_ref[...]
    rank_rows = [jnp.sum(jnp.where(h, prefix, 0.0), axis=0, keepdims=True) for h in hits]
    rank_ref[...] = jnp.concatenate(rank_rows, axis=0).astype(jnp.int32)
    total = run_ref[...] + jnp.sum(sel, axis=1, keepdims=True)
    run_ref[...] = total
    cnt_ref[...] = jnp.broadcast_to(total, (N_EXPERTS, LANES))


def _route(x1b, wr_t, rb, tm=256):
    n = x1b.shape[0]
    tm = min(tm, n)
    return pl.pallas_call(
        functools.partial(_route_kernel, tm=tm),
        grid=(n // tm,),
        in_specs=[pl.BlockSpec((tm, D_MODEL), lambda i: (i, 0)),
                  pl.BlockSpec((N_EXPERTS, D_MODEL), lambda i: (0, 0)),
                  pl.BlockSpec((N_EXPERTS, 1), lambda i: (0, 0))],
        out_specs=[pl.BlockSpec((TOP_K, tm), lambda i: (0, i)),
                   pl.BlockSpec((TOP_K, tm), lambda i: (0, i)),
                   pl.BlockSpec((TOP_K, tm), lambda i: (0, i)),
                   pl.BlockSpec((N_EXPERTS, LANES), lambda i: (0, 0))],
        out_shape=[jax.ShapeDtypeStruct((TOP_K, n), jnp.int32),
                   jax.ShapeDtypeStruct((TOP_K, n), F32),
                   jax.ShapeDtypeStruct((TOP_K, n), jnp.int32),
                   jax.ShapeDtypeStruct((N_EXPERTS, LANES), F32)],
        scratch_shapes=[pltpu.VMEM((N_EXPERTS, 1), F32)],
        compiler_params=_cp(("arbitrary",)),
        name="router_topk",
    )(x1b, wr_t, rb)


def _dest_kernel(idx_ref, rank_ref, st_ref, dest_ref, *, tm):
    e_iota = lax.broadcasted_iota(jnp.int32, (N_EXPERTS, tm), 0)
    starts = st_ref[...]
    rows = []
    for k in range(TOP_K):
        hit = e_iota == idx_ref[k:k + 1, :]
        rows.append(jnp.sum(jnp.where(hit, starts, 0.0), axis=0, keepdims=True))
    dest_ref[...] = jnp.concatenate(rows, axis=0).astype(jnp.int32) + rank_ref[...]


def _dest(idx_t, rank_t, starts, tm=512):
    n = idx_t.shape[1]
    tm = min(tm, n)
    blk = lambda: pl.BlockSpec((TOP_K, tm), lambda i: (0, i))
    return pl.pallas_call(
        functools.partial(_dest_kernel, tm=tm),
        grid=(n // tm,),
        in_specs=[blk(), blk(), pl.BlockSpec((N_EXPERTS, 1), lambda i: (0, 0))],
        out_specs=blk(),
        out_shape=jax.ShapeDtypeStruct((TOP_K, n), jnp.int32),
        compiler_params=_cp(("arbitrary",)),
        name="moe_dest",
    )(idx_t, rank_t, starts.astype(F32).reshape(N_EXPERTS, 1))


SC_CORES = 2
SC_SUBCORES = 16
SC_WINDOW = 128

def _sc_mesh():
    return plsc.VectorSubcoreMesh(core_axis_name="c", subcore_axis_name="s",
                                  num_cores=SC_CORES, num_subcores=SC_SUBCORES)


def _sc_windows(n):
    workers = SC_CORES * SC_SUBCORES
    assert n % (SC_WINDOW * workers) == 0, n
    return n // SC_WINDOW // workers


def _sc_dispatch(dest_win, x1p, n_rows):
    n = x1p.shape[0]
    per_worker = _sc_windows(n)

    @functools.partial(
        pl.kernel, mesh=_sc_mesh(),
        out_type=jax.ShapeDtypeStruct((n_rows, HALF), U32),
        scratch_types=[pltpu.VMEM((TOP_K, SC_WINDOW), jnp.int32),
                       pltpu.VMEM((SC_WINDOW, HALF), U32)],
        name="moe_dispatch_sc")
    def run(dest_hbm, x_hbm, xs_hbm, idx_v, rows_v):
        worker = lax.axis_index("s") * SC_CORES + lax.axis_index("c")

        @pl.loop(0, per_worker)
        def _(c):
            win = worker * per_worker + c
            pltpu.sync_copy(dest_hbm.at[win], idx_v)
            pltpu.sync_copy(x_hbm.at[pl.ds(pl.multiple_of(win * SC_WINDOW, SC_WINDOW),
                                           SC_WINDOW)], rows_v)
            for k in range(TOP_K):
                pltpu.sync_copy(rows_v, xs_hbm.at[idx_v.at[k]])

    return run(dest_win, x1p)


def _sc_gather(dest_win, ys, n):
    per_worker = _sc_windows(n)

    @functools.partial(
        pl.kernel, mesh=_sc_mesh(),
        out_type=jax.ShapeDtypeStruct((TOP_K, n, HALF), U32),
        scratch_types=[pltpu.VMEM((TOP_K, SC_WINDOW), jnp.int32),
                       pltpu.VMEM((SC_WINDOW, HALF), U32)],
        name="moe_gather_sc")
    def run(dest_hbm, ys_hbm, yt_hbm, idx_v, rows_v):
        worker = lax.axis_index("s") * SC_CORES + lax.axis_index("c")

        @pl.loop(0, per_worker)
        def _(c):
            win = worker * per_worker + c
            start = pl.multiple_of(win * SC_WINDOW, SC_WINDOW)
            pltpu.sync_copy(dest_hbm.at[win], idx_v)
            for k in range(TOP_K):
                pltpu.sync_copy(ys_hbm.at[idx_v.at[k]], rows_v)
                pltpu.sync_copy(rows_v, yt_hbm.at[k, pl.ds(start, SC_WINDOW)])

    return run(dest_win, ys)


def _expert_kernel(used_ref, exp_ref, valid_ref, xs_ref, wg_ref, wu_ref, wd_ref, ys_ref, *, bm):
    w = pl.program_id(0)

    @pl.when(w < used_ref[0])
    def _():
        rows = lax.broadcasted_iota(jnp.int32, (bm, 1), 0)
        x_lo, x_hi = _unpack_halves(jnp.where(rows < valid_ref[w], xs_ref[...], U32(0)))
        x = jnp.concatenate([x_lo, x_hi], axis=1).astype(BF16)
        g = jnp.dot(x, wg_ref[0, 0].astype(BF16), preferred_element_type=F32)
        u = jnp.dot(x, wu_ref[0, 0].astype(BF16), preferred_element_type=F32)
        h = (g * jax.nn.sigmoid(g) * u).astype(BF16)
        ys_ref[...] = _pack_halves(jnp.dot(h, wd_ref[0, 0].astype(BF16),
                                           preferred_element_type=F32))


def _experts(layer, n_used, tile_exp, tile_valid, xs, wg, wu, wd, bm):
    r = xs.shape[0]
    last = lambda w, used: jnp.minimum(w, used[0] - 1)
    grid_spec = pltpu.PrefetchScalarGridSpec(
        num_scalar_prefetch=3,
        grid=(r // bm,),
        in_specs=[pl.BlockSpec((bm, HALF), lambda w, used, e, v: (last(w, used), 0)),
                  pl.BlockSpec((1, 1, D_MODEL, EXPERT_FF),
                               lambda w, used, e, v: (layer, e[last(w, used)], 0, 0)),
                  pl.BlockSpec((1, 1, D_MODEL, EXPERT_FF),
                               lambda w, used, e, v: (layer, e[last(w, used)], 0, 0)),
                  pl.BlockSpec((1, 1, EXPERT_FF, D_MODEL),
                               lambda w, used, e, v: (layer, e[last(w, used)], 0, 0))],
        out_specs=pl.BlockSpec((bm, HALF), lambda w, used, e, v: (last(w, used), 0)),
    )
    return pl.pallas_call(
        functools.partial(_expert_kernel, bm=bm),
        grid_spec=grid_spec,
        out_shape=jax.ShapeDtypeStruct((r, HALF), U32),
        compiler_params=_cp(("arbitrary",)),
        name="moe_experts",
    )(n_used, tile_exp, tile_valid, xs, wg, wu, wd)


def _expert_tiles(counts, n_tiles, bm):
    padded = (counts + bm - 1) // bm * bm
    pad_end = jnp.cumsum(padded)
    pad_start = pad_end - padded
    row0 = jnp.arange(n_tiles, dtype=jnp.int32) * bm
    exp = jnp.minimum(jnp.sum((pad_end[None, :] <= row0[:, None]).astype(jnp.int32), axis=1),
                      N_EXPERTS - 1)
    valid = jnp.clip(pad_start[exp] + counts[exp] - row0, 0, bm)
    n_used = (pad_end[-1:] // bm).astype(jnp.int32)
    return pad_start.astype(jnp.int32), exp, valid.astype(jnp.int32), n_used


def _combine_rows_kernel(yt_ref, tw_ref, x1_ref, x1b_ref, wsg_ref, wsu_ref, wsd_ref, lg_ref, lb_ref,
                         o_ref, ob_ref):
    tw = tw_ref[...]
    r_lo = jnp.zeros((tw.shape[0], HALF), F32)
    r_hi = jnp.zeros((tw.shape[0], HALF), F32)
    for k in range(TOP_K):
        y_lo, y_hi = _unpack_halves(yt_ref[k])
        r_lo = r_lo + tw[:, k:k + 1] * y_lo
        r_hi = r_hi + tw[:, k:k + 1] * y_hi
    routed = jnp.concatenate([r_lo, r_hi], axis=1)
    xb = x1b_ref[...]
    g = jnp.dot(xb, wsg_ref[...], preferred_element_type=F32)
    u = jnp.dot(xb, wsu_ref[...], preferred_element_type=F32)
    h = (g * jax.nn.sigmoid(g) * u).astype(BF16)
    shared = jnp.dot(h, wsd_ref[...], preferred_element_type=F32)
    y = _layer_norm(DN_ALPHA * x1_ref[...] + (shared + routed), lg_ref[...], lb_ref[...])
    o_ref[...] = y
    ob_ref[...] = y.astype(BF16)


def _combine_rows(yt, tw, x1, x1b, wsg, wsu, wsd, lg, lb, tc=256):
    n = x1.shape[0]
    tc = min(tc, n)
    full = lambda a: pl.BlockSpec(a.shape, lambda i: (0,) * a.ndim)
    rows = lambda w: pl.BlockSpec((tc, w), lambda i: (i, 0))
    return pl.pallas_call(
        _combine_rows_kernel,
        grid=(n // tc,),
        in_specs=[pl.BlockSpec((TOP_K, tc, HALF), lambda i: (0, i, 0)),
                  rows(TOP_K), rows(D_MODEL), rows(D_MODEL),
                  full(wsg), full(wsu), full(wsd), full(lg), full(lb)],
        out_specs=[rows(D_MODEL), rows(D_MODEL)],
        out_shape=[jax.ShapeDtypeStruct((n, D_MODEL), F32),
                   jax.ShapeDtypeStruct((n, D_MODEL), BF16)],
        compiler_params=_cp(("arbitrary",)),
        name="moe_combine_ln",
    )(yt, tw, x1, x1b, wsg, wsu, wsd, lg, lb)


def _head_cols(w, heads, width, lo, hi):
    return w.reshape(w.shape[0], heads, width)[:, :, lo:hi]


def _prep_layer(w_in, b_gate, b_forget, mla_q_norm, w_uq, mla_kv_norm, w_ukv):
    f = lambda a: a.astype(BF16)
    half = MLA_ROPE_DIM // 2
    d = w_in.shape[0]
    qscale = HEAD_DIM ** -0.5
    w_qkv = jnp.concatenate([
        w_in[:, OFF_SB:OFF_SB + SB_W] * qscale, w_in[:, OFF_SB + SB_W:OFF_FOX],
        w_in[:, OFF_FOX:OFF_FOX + FOX_W] * (qscale * LOG2E), w_in[:, OFF_FOX + FOX_W:OFF_FGATE]],
        axis=1)
    w_kr = w_in[:, OFF_KR:OFF_GATE]
    w_kr_rot = jnp.concatenate([-w_kr[:, half:], w_kr[:, :half]], axis=1)
    z = lambda c: jnp.zeros((d, c), F32)
    pad_rope = LANES - MLA_QK_DIM
    w_small = jnp.concatenate([
        w_in[:, OFF_DQ:OFF_DKV], w_in[:, OFF_DKV:OFF_KR],
        z(MLA_NOPE_DIM), w_kr, z(pad_rope),
        z(MLA_NOPE_DIM), w_kr_rot, z(pad_rope),
        w_in[:, OFF_FGATE:OFF_DQ], z(LANES - FOX_HEADS)], axis=1)
    w_gate = w_in[:, OFF_GATE:]

    r = w_uq.shape[0]
    q_nope = _head_cols(w_uq, MLA_HEADS, MLA_QK_DIM, 0, MLA_NOPE_DIM)
    q_rope = _head_cols(w_uq, MLA_HEADS, MLA_QK_DIM, MLA_NOPE_DIM, MLA_QK_DIM)
    q_rope_rot = jnp.concatenate([-q_rope[:, :, half:], q_rope[:, :, :half]], axis=2)
    zq = lambda c: jnp.zeros((r, MLA_HEADS, c), F32)
    wqa = jnp.concatenate([q_nope, q_rope, zq(pad_rope)], axis=2).reshape(r, MLA_HEADS * LANES)
    wqb = jnp.concatenate([zq(MLA_NOPE_DIM), q_rope_rot, zq(pad_rope)], axis=2).reshape(
        r, MLA_HEADS * LANES)
    rk = w_ukv.shape[0]
    kvw = MLA_NOPE_DIM + MLA_V_DIM
    k_nope = _head_cols(w_ukv, MLA_HEADS, kvw, 0, MLA_NOPE_DIM)
    wk = jnp.concatenate([k_nope, jnp.zeros((rk, MLA_HEADS, LANES - MLA_NOPE_DIM), F32)],
                         axis=2).reshape(rk, MLA_HEADS * LANES)
    wv = _head_cols(w_ukv, MLA_HEADS, kvw, MLA_NOPE_DIM, kvw).reshape(rk, MLA_HEADS * MLA_V_DIM)
    bf = jnp.concatenate([b_forget, jnp.zeros((LANES - FOX_HEADS,), F32)]).reshape(1, LANES)
    return dict(w_qkv=f(w_qkv), w_small=f(w_small), w_gate=f(w_gate),
                b_gate=b_gate.reshape(1, -1), wqa=f(wqa), wqb=f(wqb), wk=f(wk), wv=f(wv),
                qn=mla_q_norm.reshape(1, -1), kvn=mla_kv_norm.reshape(1, -1), bf=bf)


def _rope_tables(seq):
    half = MLA_ROPE_DIM // 2
    inv_freq = jnp.power(ROPE_BASE, -jnp.arange(half, dtype=F32) / half)
    ang = jnp.arange(seq).astype(F32)[:, None] * inv_freq[None, :]
    cos = jnp.concatenate([jnp.cos(ang), jnp.cos(ang)], axis=1)
    sin = jnp.concatenate([jnp.sin(ang), jnp.sin(ang)], axis=1)
    pad = lambda t: jnp.concatenate([jnp.zeros((seq, MLA_NOPE_DIM), F32), t,
                                     jnp.zeros((seq, LANES - MLA_QK_DIM), F32)], axis=1)
    return pad(cos), pad(sin)


EXPERT_BM = 512


def kernel(x, ln1_g, ln1_b, ln2_g, ln2_b, w_in, b_gate, b_forget, mla_q_norm, w_uq, mla_kv_norm,
           w_ukv, w_proj_sb, w_proj_fox, w_proj_mla, w_out, w_router, router_bias,
           w_exp_gate, w_exp_up, w_exp_down, w_sh_gate, w_sh_up, w_sh_down):
    b, seq, d = x.shape
    n = b * seq
    depth = w_in.shape[0]
    cos_t, sin_t = _rope_tables(seq)
    xf = x.reshape(n, d)
    xb = xf.astype(BF16)
    f = lambda a: a.astype(BF16)
    bm = min(EXPERT_BM, n * TOP_K)
    n_tiles = n * TOP_K // bm + N_EXPERTS
    for l in range(depth):
        p = _prep_layer(w_in[l], b_gate[l], b_forget[l], mla_q_norm[l], w_uq[l], mla_kv_norm[l],
                        w_ukv[l])
        qkv = _matmul(xb, p["w_qkv"], None, BF16)
        gates = _matmul(xb, p["w_gate"], p["b_gate"], BF16, act="sigmoid")
        small = _matmul(xb, p["w_small"], None, F32, tn=SMALL_W)
        q_mla, k_mla, v_mla, lf = _mla_prep(small, p["qn"], p["kvn"], p["wqa"], p["wqb"], p["wk"],
                                            p["wv"], cos_t, sin_t, p["bf"], seq)
        cum, cumt = _cumsum(lf.reshape(b, seq, LANES))
        cumt4 = cumt[:, :FOX_HEADS, :].reshape(b, FOX_HEADS, 1, seq)
        qkv3 = qkv.reshape(b, seq, QKV_W)
        o_sb = _sb_attention(qkv3).reshape(n, SB_W)
        o_fox = _fox_attention(qkv3, cum, cumt4).reshape(n, FOX_W)
        o_mla = _mla_attention(q_mla.reshape(b, seq, -1), k_mla.reshape(b, seq, -1),
                               v_mla.reshape(b, seq, -1)).reshape(n, -1)
        x1, x1b, x1p = _merge(o_sb, o_fox, o_mla, gates, xf, f(w_proj_sb[l]), f(w_proj_fox[l]),
                              f(w_proj_mla[l]), f(w_out[l]), ln1_g[l].reshape(1, d),
                              ln1_b[l].reshape(1, d))

        idx_t, tw_t, rank_t, cnt = _route(x1b, f(w_router[l].T), router_bias[l].reshape(-1, 1))
        counts = cnt[:, 0].astype(jnp.int32)
        starts, tile_exp, tile_valid, n_used = _expert_tiles(counts, n_tiles, bm)
        dest_t = _dest(idx_t, rank_t, starts)
        dest_win = dest_t.reshape(TOP_K, n // SC_WINDOW, SC_WINDOW).transpose(1, 0, 2)
        xs = _sc_dispatch(dest_win, x1p, n_tiles * bm)
        ys = _experts(l, n_used, tile_exp, tile_valid, xs, w_exp_gate, w_exp_up, w_exp_down, bm)
        yt = _sc_gather(dest_win, ys, n)
        xf, xb = _combine_rows(yt, tw_t.T, x1, x1b, f(w_sh_gate[l]), f(w_sh_up[l]),
                               f(w_sh_down[l]), ln2_g[l].reshape(1, d), ln2_b[l].reshape(1, d))
    return xf.reshape(b, seq, d)
```

```python
import functools

import jax
import jax.numpy as jnp
from jax import lax
from jax.experimental import pallas as pl
from jax.experimental.pallas import tpu as pltpu
from jax.experimental.pallas import tpu_sc as plsc

F32 = jnp.float32
BF16 = jnp.bfloat16

D_MODEL = 1024
HEAD_DIM = 64
SB_HEADS = 4
FOX_HEADS = 4
MLA_HEADS = 8
MLA_Q_RANK = 256
MLA_KV_RANK = 128
MLA_NOPE_DIM = 64
MLA_ROPE_DIM = 32
MLA_V_DIM = 64
ROPE_BASE = 10000.0
N_BRANCHES = 3
N_EXPERTS = 256
TOP_K = 8
N_GROUPS = 8
TOPK_GROUPS = 4
GROUP_SIZE = N_EXPERTS // N_GROUPS
EXPERT_FF = 256
SHARED_FF = 256
ROUTED_SCALE = 2.5
CHUNK = 64
LN_EPS = 1e-5
RMS_EPS = 1e-6
DEPTH = 2
DN_ALPHA = (2 * DEPTH) ** 0.25

SB_W = SB_HEADS * HEAD_DIM
FOX_W = FOX_HEADS * HEAD_DIM
MLA_QK_DIM = MLA_NOPE_DIM + MLA_ROPE_DIM
OFF_SB = 0
OFF_FOX = OFF_SB + 3 * SB_W
OFF_FGATE = OFF_FOX + 3 * FOX_W
OFF_DQ = OFF_FGATE + FOX_HEADS
OFF_DKV = OFF_DQ + MLA_Q_RANK
OFF_KR = OFF_DKV + MLA_KV_RANK
OFF_GATE = OFF_KR + MLA_ROPE_DIM

LANES = 128
SUBLANES = 8
QKV_W = 3 * SB_W + 3 * FOX_W
SMALL_W = MLA_Q_RANK + MLA_KV_RANK + 3 * LANES
SB_CUTOFF = -104.0
LOG2E = 1.4426950408889634

VMEM_LIMIT = 48 * 1024 * 1024


def _cp(sem, vmem=VMEM_LIMIT):
    return pltpu.CompilerParams(dimension_semantics=sem, vmem_limit_bytes=vmem)


def _mm_kernel(x_ref, w_ref, o_ref, *, sub):
    x = x_ref[...]
    for c in range(0, o_ref.shape[1], sub):
        acc = jnp.dot(x, w_ref[:, c:c + sub], preferred_element_type=F32)
        o_ref[:, c:c + sub] = acc.astype(o_ref.dtype)


def _matmul(x, w, out_dtype, tn, tm=1024, sub=256):
    n, k = x.shape
    c = w.shape[1]
    tm = min(tm, n)
    return pl.pallas_call(
        functools.partial(_mm_kernel, sub=sub),
        grid=(n // tm, c // tn),
        in_specs=[pl.BlockSpec((tm, k), lambda i, j: (i, 0)),
                  pl.BlockSpec((k, tn), lambda i, j: (0, j))],
        out_specs=pl.BlockSpec((tm, tn), lambda i, j: (i, j)),
        out_shape=jax.ShapeDtypeStruct((n, c), out_dtype),
        compiler_params=_cp(("arbitrary", "arbitrary")),
        name="proj_matmul",
    )(x, w)


def _split_bf16(x, parts):
    out = []
    for _ in range(parts):
        h = x.astype(BF16)
        out.append(h)
        x = x - h.astype(F32)
    return out


def _mla_prep_kernel(sm_ref, qn_ref, kvn_ref, wqa_ref, wqb_ref, wk_ref, wv_ref,
                     cos_ref, sin_ref, bf_ref, q_ref, k_ref, v_ref, lf_ref):
    sm = sm_ref[...]
    dq = sm[:, :MLA_Q_RANK]
    dkv = sm[:, MLA_Q_RANK:MLA_Q_RANK + MLA_KV_RANK]
    o = MLA_Q_RANK + MLA_KV_RANK
    kr = sm[:, o:o + LANES]
    kr_rot = sm[:, o + LANES:o + 2 * LANES]
    fg = sm[:, o + 2 * LANES:o + 3 * LANES]

    cq = dq * lax.rsqrt(jnp.mean(dq * dq, axis=-1, keepdims=True) + RMS_EPS) * qn_ref[...]
    ckv = dkv * lax.rsqrt(jnp.mean(dkv * dkv, axis=-1, keepdims=True) + RMS_EPS) * kvn_ref[...]
    cq = cq.astype(BF16)
    ckv = ckv.astype(BF16)

    cosk = cos_ref[...]
    sink = sin_ref[...]
    lane = lax.broadcasted_iota(jnp.int32, (1, LANES), 1)
    nope = (lane < MLA_NOPE_DIM).astype(F32)
    scale = MLA_QK_DIM ** -0.5 * LOG2E
    cq_tab = jnp.concatenate([(cosk + nope) * scale] * MLA_HEADS, axis=1)
    sq_tab = jnp.concatenate([sink * scale] * MLA_HEADS, axis=1)

    qa = jnp.dot(cq, wqa_ref[...], preferred_element_type=F32)
    qb = jnp.dot(cq, wqb_ref[...], preferred_element_type=F32)
    q_ref[...] = (qa * cq_tab + qb * sq_tab).astype(q_ref.dtype)

    k_rope = kr * cosk + kr_rot * sink
    ka = jnp.dot(ckv, wk_ref[...], preferred_element_type=F32)
    k_ref[...] = (ka + jnp.concatenate([k_rope] * MLA_HEADS, axis=1)).astype(k_ref.dtype)
    v_ref[...] = jnp.dot(ckv, wv_ref[...], preferred_element_type=F32).astype(v_ref.dtype)
    lf_ref[...] = jax.nn.log_sigmoid(fg + bf_ref[...])


def _mla_prep(small, qn, kvn, wqa, wqb, wk, wv, cos_t, sin_t, bf, seq, tm=512):
    n = small.shape[0]
    tm = min(tm, seq)
    sblocks = seq // tm
    hw = MLA_HEADS * LANES
    full = lambda a: pl.BlockSpec(a.shape, lambda i: (0,) * a.ndim)
    return pl.pallas_call(
        _mla_prep_kernel,
        grid=(n // tm,),
        in_specs=[pl.BlockSpec((tm, SMALL_W), lambda i: (i, 0)),
                  full(qn), full(kvn), full(wqa), full(wqb), full(wk), full(wv),
                  pl.BlockSpec((tm, LANES), lambda i: (i % sblocks, 0)),
                  pl.BlockSpec((tm, LANES), lambda i: (i % sblocks, 0)),
                  full(bf)],
        out_specs=[pl.BlockSpec((tm, hw), lambda i: (i, 0)),
                   pl.BlockSpec((tm, hw), lambda i: (i, 0)),
                   pl.BlockSpec((tm, MLA_HEADS * MLA_V_DIM), lambda i: (i, 0)),
                   pl.BlockSpec((tm, LANES), lambda i: (i, 0))],
        out_shape=[jax.ShapeDtypeStruct((n, hw), BF16),
                   jax.ShapeDtypeStruct((n, hw), BF16),
                   jax.ShapeDtypeStruct((n, MLA_HEADS * MLA_V_DIM), BF16),
                   jax.ShapeDtypeStruct((n, LANES), F32)],
        compiler_params=_cp(("arbitrary",)),
        name="mla_prep",
    )(small, qn, kvn, wqa, wqb, wk, wv, cos_t, sin_t, bf)


def _cumsum_kernel(lf_ref, cum_ref, cumt_ref, *, seq):
    r = lax.broadcasted_iota(jnp.int32, (LANES, LANES), 0)
    c = lax.broadcasted_iota(jnp.int32, (LANES, LANES), 1)
    lower = (c <= r).astype(BF16)

    def body(j, carry):
        start = pl.multiple_of(j * LANES, LANES)
        blk = lf_ref[0, pl.ds(start, LANES), :]
        acc = carry
        for part in _split_bf16(blk, 3):
            acc = acc + jnp.dot(lower, part, preferred_element_type=F32)
        scaled = acc * LOG2E
        cum_ref[0, pl.ds(start, LANES), :] = scaled
        cumt_ref[0, :, pl.ds(start, LANES)] = scaled.T[:SUBLANES, :]
        return jnp.broadcast_to(acc[LANES - 1:LANES, :], (LANES, LANES))

    lax.fori_loop(0, seq // LANES, body, jnp.zeros((LANES, LANES), F32))


def _cumsum(lf3):
    b, seq, _ = lf3.shape
    return pl.pallas_call(
        functools.partial(_cumsum_kernel, seq=seq),
        grid=(b,),
        in_specs=[pl.BlockSpec((1, seq, LANES), lambda i: (i, 0, 0))],
        out_specs=[pl.BlockSpec((1, seq, LANES), lambda i: (i, 0, 0)),
                   pl.BlockSpec((1, SUBLANES, seq), lambda i: (i, 0, 0))],
        out_shape=[jax.ShapeDtypeStruct((b, seq, LANES), F32),
                   jax.ShapeDtypeStruct((b, SUBLANES, seq), F32)],
        compiler_params=_cp(("arbitrary",)),
        name="forget_cumsum",
    )(lf3)


_NT = (((1,), (1,)), ((), ()))
NEG_INIT = -1e30
_TN = (((0,), (0,)), ((), ()))
ATTN_TK = 256


def _head_halves(q):
    lane = lax.broadcasted_iota(jnp.int32, (1, LANES), 1)
    zero = jnp.zeros_like(q)
    return [jnp.where(lane < HEAD_DIM, q, zero), jnp.where(lane >= HEAD_DIM, q, zero)]


def _mask_first_block(x, keep, fill, tk):
    head = jnp.where(keep, x[:, :tk], fill)
    return head if x.shape[1] == tk else jnp.concatenate([head, x[:, tk:]], axis=1)


def _fox_kernel(q_ref, k_ref, v_ref, cc_ref, cr_ref, o_ref, *, seq, tk):
    hp = pl.program_id(1)
    lane = lax.broadcasted_iota(jnp.int32, (1, LANES), 1)
    qs = _head_halves(q_ref[0])
    key = lax.broadcasted_iota(jnp.int32, (tk, tk), 0)
    qry = lax.broadcasted_iota(jnp.int32, (tk, tk), 1)
    causal = key <= qry
    carry = [(jnp.full((1, seq), NEG_INIT, F32), jnp.zeros((1, seq), F32),
              jnp.zeros((LANES, seq), F32)) for _ in range(2)]
    for j in range(seq // tk):
        q0 = j * tk
        k = k_ref[0, q0:q0 + tk, :]
        v = v_ref[0, q0:q0 + tk, :]
        cc = cc_ref[0, q0:q0 + tk, :]
        for hh in range(2):
            m, l, acc = carry[hh]
            cs = jnp.sum(jnp.where(lane == 2 * hp + hh, cc, 0.0), axis=1, keepdims=True)
            ct = cr_ref[0, hh, :, q0:]
            st = lax.dot_general(k, qs[hh][q0:, :], _NT, preferred_element_type=F32)
            st = _mask_first_block((st + ct) - cs, causal, -jnp.inf, tk)
            mn, ln, an = _softmax_step_t(st, m[:, q0:], l[:, q0:], acc[:, q0:], v)
            if q0:
                mn = jnp.concatenate([m[:, :q0], mn], axis=1)
                ln = jnp.concatenate([l[:, :q0], ln], axis=1)
                an = jnp.concatenate([acc[:, :q0], an], axis=1)
            carry[hh] = (mn, ln, an)
    _pair_out_t([carry[0][2], carry[1][2]], [carry[0][1], carry[1][1]], o_ref)


def _fox_attention(qkv3, cum, cumt4):
    b, seq, _ = qkv3.shape
    tk = min(ATTN_TK, seq)
    hp = FOX_HEADS // 2
    qoff = OFF_FOX // LANES
    return pl.pallas_call(
        functools.partial(_fox_kernel, seq=seq, tk=tk),
        grid=(b, hp),
        in_specs=[pl.BlockSpec((1, seq, LANES), lambda bi, h: (bi, 0, qoff + h)),
                  pl.BlockSpec((1, seq, LANES), lambda bi, h: (bi, 0, qoff + hp + h)),
                  pl.BlockSpec((1, seq, LANES), lambda bi, h: (bi, 0, qoff + 2 * hp + h)),
                  pl.BlockSpec((1, seq, LANES), lambda bi, h: (bi, 0, 0)),
                  pl.BlockSpec((1, 2, 1, seq), lambda bi, h: (bi, h, 0, 0))],
        out_specs=pl.BlockSpec((1, seq, LANES), lambda bi, h: (bi, 0, h)),
        out_shape=jax.ShapeDtypeStruct((b, seq, FOX_W), BF16),
        compiler_params=_cp(("arbitrary", "arbitrary")),
        name="fox_attention",
    )(qkv3, qkv3, qkv3, cum, cumt4)


def _softmax_step_t(st, m, l, acc, v):
    m_new = jnp.maximum(m, jnp.max(st, axis=0, keepdims=True))
    alpha = jnp.exp2(m - m_new)
    p = jnp.exp2(st - m_new)
    l = alpha * l + jnp.sum(p, axis=0, keepdims=True)
    acc = alpha * acc + lax.dot_general(v, p.astype(BF16), _TN, preferred_element_type=F32)
    return m_new, l, acc


def _pair_out_t(accs, ls, o_ref):
    sub = lax.broadcasted_iota(jnp.int32, (LANES, 1), 0)
    ot = jnp.where(sub < HEAD_DIM, accs[0] / ls[0], accs[1] / ls[1])
    o_ref[0] = ot.T.astype(o_ref.dtype)


def _mla_kernel(q_ref, k_ref, v_ref, o_ref, *, seq, tk):
    q = q_ref[0]
    qs = [q[:, :LANES], q[:, LANES:]]
    key = lax.broadcasted_iota(jnp.int32, (tk, tk), 0)
    qry = lax.broadcasted_iota(jnp.int32, (tk, tk), 1)
    shift = CHUNK.bit_length() - 1
    chunk_causal = (key >> shift) <= (qry >> shift)
    carry = [(jnp.full((1, seq), NEG_INIT, F32), jnp.zeros((1, seq), F32),
              jnp.zeros((LANES, seq), F32)) for _ in range(2)]
    for j in range(seq // tk):
        q0 = j * tk
        k = k_ref[0, q0:q0 + tk, :]
        v = v_ref[0, q0:q0 + tk, :]
        for hh in range(2):
            m, l, acc = carry[hh]
            st = lax.dot_general(k[:, hh * LANES:(hh + 1) * LANES], qs[hh][q0:, :], _NT,
                                 preferred_element_type=F32)
            st = _mask_first_block(st, chunk_causal, -jnp.inf, tk)
            mn, ln, an = _softmax_step_t(st, m[:, q0:], l[:, q0:], acc[:, q0:], v)
            if q0:
                mn = jnp.concatenate([m[:, :q0], mn], axis=1)
                ln = jnp.concatenate([l[:, :q0], ln], axis=1)
                an = jnp.concatenate([acc[:, :q0], an], axis=1)
            carry[hh] = (mn, ln, an)
    _pair_out_t([carry[0][2], carry[1][2]], [carry[0][1], carry[1][1]], o_ref)


def _mla_attention(q3, k3, v3):
    b, seq, _ = q3.shape
    tk = min(ATTN_TK, seq)
    hp = MLA_HEADS // 2
    return pl.pallas_call(
        functools.partial(_mla_kernel, seq=seq, tk=tk),
        grid=(b, hp),
        in_specs=[pl.BlockSpec((1, seq, 2 * LANES), lambda bi, h: (bi, 0, h)),
                  pl.BlockSpec((1, seq, 2 * LANES), lambda bi, h: (bi, 0, h)),
                  pl.BlockSpec((1, seq, LANES), lambda bi, h: (bi, 0, h))],
        out_specs=pl.BlockSpec((1, seq, LANES), lambda bi, h: (bi, 0, h)),
        out_shape=jax.ShapeDtypeStruct((b, seq, MLA_HEADS * MLA_V_DIM), BF16),
        compiler_params=_cp(("arbitrary", "arbitrary")),
        name="mla_attention",
    )(q3, k3, v3)


def _softplus(z):
    return jnp.maximum(z, 0.0) + jnp.log(1.0 + jnp.exp(-jnp.abs(z)))


def _sb_kernel(q_ref, k_ref, v_ref, o_ref, run_ref, acc_ref, *, seq, tk):
    n = seq // tk
    qs = _head_halves(q_ref[0])
    key = lax.broadcasted_iota(jnp.int32, (tk, tk), 0)
    qry = lax.broadcasted_iota(jnp.int32, (tk, tk), 1)
    strict = key < qry
    later = (qry > key).astype(BF16)

    def unit(j, qlo, qhi, diag):
        k = k_ref[0, j * tk:(j + 1) * tk, :]
        v = v_ref[0, j * tk:(j + 1) * tk, :]
        for hh in range(2):
            z = lax.dot_general(k, qs[hh][qlo:qhi, :], _NT, preferred_element_type=F32)
            sp = _softplus(z)
            log_keep = -sp
            if diag:
                log_keep = _mask_first_block(log_keep, strict, 0.0, tk)
            between = run_ref[hh, :, qlo:qhi]
            for part in _split_bf16(log_keep, 2):
                between = between + jnp.dot(later, part, preferred_element_type=F32)
            w = jnp.exp((z - sp) + between)
            if diag:
                w = _mask_first_block(w, strict, 0.0, tk)
            acc_ref[hh, :, qlo:qhi] += lax.dot_general(v, w.astype(BF16), _TN,
                                                       preferred_element_type=F32)
            run_ref[hh, :, qlo:qhi] += jnp.sum(log_keep, axis=0, keepdims=True)

    run_ref[...] = jnp.zeros_like(run_ref)
    acc_ref[...] = jnp.zeros_like(acc_ref)
    for j in range(n - 1, -1, -1):
        unit(j, j * tk, min((j + 2) * tk, seq), True)

    for j in range(n - 3, -1, -1):
        qlo = (j + 2) * tk

        @pl.when(jnp.max(run_ref[:, :, qlo:]) >= SB_CUTOFF)
        def _():
            unit(j, qlo, seq, False)

    sub = lax.broadcasted_iota(jnp.int32, (LANES, 1), 0)
    o_ref[0] = jnp.where(sub < HEAD_DIM, acc_ref[0], acc_ref[1]).T.astype(o_ref.dtype)


def _sb_attention(qkv3):
    b, seq, _ = qkv3.shape
    tk = min(ATTN_TK, seq)
    hp = SB_HEADS // 2
    qoff = OFF_SB // LANES
    return pl.pallas_call(
        functools.partial(_sb_kernel, seq=seq, tk=tk),
        grid=(b, hp),
        in_specs=[pl.BlockSpec((1, seq, LANES), lambda bi, h: (bi, 0, qoff + h)),
                  pl.BlockSpec((1, seq, LANES), lambda bi, h: (bi, 0, qoff + hp + h)),
                  pl.BlockSpec((1, seq, LANES), lambda bi, h: (bi, 0, qoff + 2 * hp + h))],
        out_specs=pl.BlockSpec((1, seq, LANES), lambda bi, h: (bi, 0, h)),
        out_shape=jax.ShapeDtypeStruct((b, seq, SB_W), BF16),
        scratch_shapes=[pltpu.VMEM((2, 1, seq), F32), pltpu.VMEM((2, LANES, seq), F32)],
        compiler_params=_cp(("arbitrary", "arbitrary")),
        name="sb_attention",
    )(qkv3, qkv3, qkv3)


def _layer_norm(y, g, b):
    yc = y - jnp.mean(y, axis=-1, keepdims=True)
    var = jnp.mean(yc * yc, axis=-1, keepdims=True)
    return yc * lax.rsqrt(var + LN_EPS) * g + b


HALF = D_MODEL // 2
U32 = jnp.uint32


def _pack_halves(y):
    bits = lax.bitcast_convert_type(y.astype(BF16).astype(F32), U32)
    return (bits[:, :HALF] >> 16) | (bits[:, HALF:] & U32(0xFFFF0000))


def _unpack_halves(w):
    lo = lax.bitcast_convert_type(w << 16, F32)
    hi = lax.bitcast_convert_type(w & U32(0xFFFF0000), F32)
    return lo, hi


def _merge_kernel(osb_ref, ofox_ref, omla_ref, xb_ref, x_ref, wg_ref, bg_ref, wsb_ref, wfox_ref,
                  wmla_ref, wout_ref, lg_ref, lb_ref, x1_ref, x1b_ref, x1p_ref):
    xb = xb_ref[...]
    mixed = None
    branches = ((osb_ref, wsb_ref), (ofox_ref, wfox_ref), (omla_ref, wmla_ref))
    for i, (o_ref, w_ref) in enumerate(branches):
        cols = slice(i * D_MODEL, (i + 1) * D_MODEL)
        gate = jax.nn.sigmoid(jnp.dot(xb, wg_ref[:, cols], preferred_element_type=F32)
                              + bg_ref[:, cols])
        term = gate * jnp.dot(o_ref[...], w_ref[...], preferred_element_type=F32)
        mixed = term if mixed is None else mixed + term
    mix = jnp.dot(mixed.astype(BF16), wout_ref[...], preferred_element_type=F32)
    y = _layer_norm(DN_ALPHA * x_ref[...] + mix, lg_ref[...], lb_ref[...])
    x1_ref[...] = y
    x1b_ref[...] = y.astype(BF16)
    x1p_ref[...] = _pack_halves(y)


def _merge(o_sb, o_fox, o_mla, xb, x, wg, bg, wsb, wfox, wmla, wout, lg, lb, tm=512):
    n = x.shape[0]
    tm = min(tm, n)
    full = lambda a: pl.BlockSpec(a.shape, lambda i: (0,) * a.ndim)
    rows = lambda w: pl.BlockSpec((tm, w), lambda i: (i, 0))
    return pl.pallas_call(
        _merge_kernel,
        grid=(n // tm,),
        in_specs=[rows(SB_W), rows(FOX_W), rows(MLA_HEADS * MLA_V_DIM), rows(D_MODEL),
                  rows(D_MODEL), full(wg), full(bg), full(wsb), full(wfox), full(wmla),
                  full(wout), full(lg), full(lb)],
        out_specs=[rows(D_MODEL), rows(D_MODEL), rows(HALF)],
        out_shape=[jax.ShapeDtypeStruct((n, D_MODEL), F32),
                   jax.ShapeDtypeStruct((n, D_MODEL), BF16),
                   jax.ShapeDtypeStruct((n, HALF), U32)],
        compiler_params=_cp(("arbitrary",)),
        name="merge_outproj_ln",
    )(o_sb, o_fox, o_mla, xb, x, wg, bg, wsb, wfox, wmla, wout, lg, lb)


def _route_kernel(x_ref, wr_ref, rb_ref, idx_ref, w_ref, rank_ref, cnt_ref, run_ref, *, tm):
    step = pl.program_id(0)

    @pl.when(step == 0)
    def _():
        run_ref[...] = jnp.zeros_like(run_ref)

    logits = lax.dot_general(wr_ref[...], x_ref[...], _NT, preferred_element_type=F32)
    scores = jax.nn.sigmoid(logits)
    biased = scores + rb_ref[...]
    e_iota = lax.broadcasted_iota(jnp.int32, (N_EXPERTS, tm), 0)
    big = jnp.int32(1 << 20)

    g_iota = lax.broadcasted_iota(jnp.int32, (GROUP_SIZE, tm), 0)
    gs_rows = []
    for g in range(N_GROUPS):
        blk = biased[g * GROUP_SIZE:(g + 1) * GROUP_SIZE, :]
        m1 = jnp.max(blk, axis=0, keepdims=True)
        first = jnp.min(jnp.where(blk == m1, g_iota, big), axis=0, keepdims=True)
        m2 = jnp.max(jnp.where(g_iota == first, -jnp.inf, blk), axis=0, keepdims=True)
        gs_rows.append(m1 + m2)
    gs = jnp.concatenate(gs_rows, axis=0)
    n_iota = lax.broadcasted_iota(jnp.int32, (N_GROUPS, tm), 0)
    keep = jnp.zeros((N_GROUPS, tm), jnp.bool_)
    for _ in range(TOPK_GROUPS):
        m = jnp.max(gs, axis=0, keepdims=True)
        first = jnp.min(jnp.where(gs == m, n_iota, big), axis=0, keepdims=True)
        hit = n_iota == first
        keep = jnp.logical_or(keep, hit)
        gs = jnp.where(hit, -jnp.inf, gs)
    keep_f = keep.astype(F32)
    expert_keep = jnp.concatenate(
        [jnp.broadcast_to(keep_f[g:g + 1, :], (GROUP_SIZE, tm)) for g in range(N_GROUPS)], axis=0)
    masked = jnp.where(expert_keep > 0.5, biased, -jnp.inf)

    idx_rows, w_rows, hits = [], [], []
    sel = jnp.zeros((N_EXPERTS, tm), F32)
    for _ in range(TOP_K):
        m = jnp.max(masked, axis=0, keepdims=True)
        first = jnp.min(jnp.where(masked == m, e_iota, big), axis=0, keepdims=True)
        hit = e_iota == first
        idx_rows.append(first)
        w_rows.append(jnp.sum(jnp.where(hit, scores, 0.0), axis=0, keepdims=True))
        hits.append(hit)
        sel = sel + hit.astype(F32)
        masked = jnp.where(hit, -jnp.inf, masked)
    w = jnp.concatenate(w_rows, axis=0)
    w = w / jnp.sum(w, axis=0, keepdims=True) * ROUTED_SCALE
    idx_ref[...] = jnp.concatenate(idx_rows, axis=0)
    w_ref[...] = w

    r = lax.broadcasted_iota(jnp.int32, (tm, tm), 0)
    c = lax.broadcasted_iota(jnp.int32, (tm, tm), 1)
    earlier = (r < c).astype(BF16)
    prefix = jnp.dot(sel.astype(BF16), earlier, preferred_element_type=F32) + run_ref[...]
    rank_rows = [jnp.sum(jnp.where(h, prefix, 0.0), axis=0, keepdims=True) for h in hits]
    rank_ref[...] = jnp.concatenate(rank_rows, axis=0).astype(jnp.int32)
    total = run_ref[...] + jnp.sum(sel, axis=1, keepdims=True)
    run_ref[...] = total
    cnt_ref[...] = jnp.broadcast_to(total, (N_EXPERTS, LANES))


def _route(x1b, wr_t, rb, tm=256):
    n = x1b.shape[0]
    tm = min(tm, n)
    return pl.pallas_call(
        functools.partial(_route_kernel, tm=tm),
        grid=(n // tm,),
        in_specs=[pl.BlockSpec((tm, D_MODEL), lambda i: (i, 0)),
                  pl.BlockSpec((N_EXPERTS, D_MODEL), lambda i: (0, 0)),
                  pl.BlockSpec((N_EXPERTS, 1), lambda i: (0, 0))],
        out_specs=[pl.BlockSpec((TOP_K, tm), lambda i: (0, i)),
                   pl.BlockSpec((TOP_K, tm), lambda i: (0, i)),
                   pl.BlockSpec((TOP_K, tm), lambda i: (0, i)),
                   pl.BlockSpec((N_EXPERTS, LANES), lambda i: (0, 0))],
        out_shape=[jax.ShapeDtypeStruct((TOP_K, n), jnp.int32),
                   jax.ShapeDtypeStruct((TOP_K, n), F32),
                   jax.ShapeDtypeStruct((TOP_K, n), jnp.int32),
                   jax.ShapeDtypeStruct((N_EXPERTS, LANES), F32)],
        scratch_shapes=[pltpu.VMEM((N_EXPERTS, 1), F32)],
        compiler_params=_cp(("arbitrary",)),
        name="router_topk",
    )(x1b, wr_t, rb)


def _dest_kernel(idx_ref, rank_ref, st_ref, dest_ref, *, tm):
    e_iota = lax.broadcasted_iota(jnp.int32, (N_EXPERTS, tm), 0)
    starts = st_ref[...]
    rows = []
    for k in range(TOP_K):
        hit = e_iota == idx_ref[k:k + 1, :]
        rows.append(jnp.sum(jnp.where(hit, starts, 0.0), axis=0, keepdims=True))
    dest_ref[...] = jnp.concatenate(rows, axis=0).astype(jnp.int32) + rank_ref[...]


def _dest(idx_t, rank_t, starts, tm=512):
    n = idx_t.shape[1]
    tm = min(tm, n)
    blk = lambda: pl.BlockSpec((TOP_K, tm), lambda i: (0, i))
    return pl.pallas_call(
        functools.partial(_dest_kernel, tm=tm),
        grid=(n // tm,),
        in_specs=[blk(), blk(), pl.BlockSpec((N_EXPERTS, 1), lambda i: (0, 0))],
        out_specs=blk(),
        out_shape=jax.ShapeDtypeStruct((TOP_K, n), jnp.int32),
        compiler_params=_cp(("arbitrary",)),
        name="moe_dest",
    )(idx_t, rank_t, starts.astype(F32).reshape(N_EXPERTS, 1))


SC_CORES = 2
SC_SUBCORES = 16
SC_WINDOW = 128

def _sc_mesh():
    return plsc.VectorSubcoreMesh(core_axis_name="c", subcore_axis_name="s",
                                  num_cores=SC_CORES, num_subcores=SC_SUBCORES)


def _sc_windows(n):
    workers = SC_CORES * SC_SUBCORES
    assert n % (SC_WINDOW * workers) == 0, n
    return n // SC_WINDOW // workers


def _sc_dispatch(dest_win, x1p, n_rows):
    n = x1p.shape[0]
    per_worker = _sc_windows(n)

    @functools.partial(
        pl.kernel, mesh=_sc_mesh(),
        out_type=jax.ShapeDtypeStruct((n_rows, HALF), U32),
        scratch_types=[pltpu.VMEM((TOP_K, SC_WINDOW), jnp.int32),
                       pltpu.VMEM((SC_WINDOW, HALF), U32)],
        name="moe_dispatch_sc")
    def run(dest_hbm, x_hbm, xs_hbm, idx_v, rows_v):
        worker = lax.axis_index("s") * SC_CORES + lax.axis_index("c")

        @pl.loop(0, per_worker)
        def _(c):
            win = worker * per_worker + c
            pltpu.sync_copy(dest_hbm.at[win], idx_v)
            pltpu.sync_copy(x_hbm.at[pl.ds(pl.multiple_of(win * SC_WINDOW, SC_WINDOW),
                                           SC_WINDOW)], rows_v)
            for k in range(TOP_K):
                pltpu.sync_copy(rows_v, xs_hbm.at[idx_v.at[k]])

    return run(dest_win, x1p)


def _sc_gather(dest_win, ys, n):
    per_worker = _sc_windows(n)

    @functools.partial(
        pl.kernel, mesh=_sc_mesh(),
        out_type=jax.ShapeDtypeStruct((TOP_K, n, HALF), U32),
        scratch_types=[pltpu.VMEM((TOP_K, SC_WINDOW), jnp.int32),
                       pltpu.VMEM((SC_WINDOW, HALF), U32)],
        name="moe_gather_sc")
    def run(dest_hbm, ys_hbm, yt_hbm, idx_v, rows_v):
        worker = lax.axis_index("s") * SC_CORES + lax.axis_index("c")

        @pl.loop(0, per_worker)
        def _(c):
            win = worker * per_worker + c
            start = pl.multiple_of(win * SC_WINDOW, SC_WINDOW)
            pltpu.sync_copy(dest_hbm.at[win], idx_v)
            for k in range(TOP_K):
                pltpu.sync_copy(ys_hbm.at[idx_v.at[k]], rows_v)
                pltpu.sync_copy(rows_v, yt_hbm.at[k, pl.ds(start, SC_WINDOW)])

    return run(dest_win, ys)


def _expert_kernel(used_ref, exp_ref, valid_ref, xs_ref, wg_ref, wu_ref, wd_ref, ys_ref, *, bm):
    w = pl.program_id(0)

    @pl.when(w < used_ref[0])
    def _():
        rows = lax.broadcasted_iota(jnp.int32, (bm, 1), 0)
        x_lo, x_hi = _unpack_halves(jnp.where(rows < valid_ref[w], xs_ref[...], U32(0)))
        x = jnp.concatenate([x_lo, x_hi], axis=1).astype(BF16)
        g = jnp.dot(x, wg_ref[0, 0].astype(BF16), preferred_element_type=F32)
        u = jnp.dot(x, wu_ref[0, 0].astype(BF16), preferred_element_type=F32)
        h = (g * jax.nn.sigmoid(g) * u).astype(BF16)
        ys_ref[...] = _pack_halves(jnp.dot(h, wd_ref[0, 0].astype(BF16),
                                           preferred_element_type=F32))


def _experts(layer, n_used, tile_exp, tile_valid, xs, wg, wu, wd, bm):
    r = xs.shape[0]
    last = lambda w, used: jnp.minimum(w, used[0] - 1)
    weights = lambda shape: pl.BlockSpec(
        (1, 1) + shape, lambda w, used, e, v: (layer, e[last(w, used)], 0, 0))
    grid_spec = pltpu.PrefetchScalarGridSpec(
        num_scalar_prefetch=3,
        grid=(r // bm,),
        in_specs=[pl.BlockSpec((bm, HALF), lambda w, used, e, v: (last(w, used), 0)),
                  weights((D_MODEL, EXPERT_FF)), weights((D_MODEL, EXPERT_FF)),
                  weights((EXPERT_FF, D_MODEL))],
        out_specs=pl.BlockSpec((bm, HALF), lambda w, used, e, v: (last(w, used), 0)),
    )
    return pl.pallas_call(
        functools.partial(_expert_kernel, bm=bm),
        grid_spec=grid_spec,
        out_shape=jax.ShapeDtypeStruct((r, HALF), U32),
        compiler_params=_cp(("arbitrary",)),
        name="moe_experts",
    )(n_used, tile_exp, tile_valid, xs, wg, wu, wd)


def _expert_tiles(counts, n_tiles, bm):
    padded = (counts + bm - 1) // bm * bm
    pad_end = jnp.cumsum(padded)
    pad_start = pad_end - padded
    row0 = jnp.arange(n_tiles, dtype=jnp.int32) * bm
    exp = jnp.minimum(jnp.sum((pad_end[None, :] <= row0[:, None]).astype(jnp.int32), axis=1),
                      N_EXPERTS - 1)
    valid = jnp.clip(pad_start[exp] + counts[exp] - row0, 0, bm)
    n_used = (pad_end[-1:] // bm).astype(jnp.int32)
    return pad_start.astype(jnp.int32), exp, valid.astype(jnp.int32), n_used


def _combine_rows_kernel(yt_ref, tw_ref, x1_ref, x1b_ref, wsg_ref, wsu_ref, wsd_ref, lg_ref, lb_ref,
                         o_ref, ob_ref):
    tw = tw_ref[...]
    r_lo = jnp.zeros((tw.shape[0], HALF), F32)
    r_hi = jnp.zeros((tw.shape[0], HALF), F32)
    for k in range(TOP_K):
        y_lo, y_hi = _unpack_halves(yt_ref[k])
        r_lo = r_lo + tw[:, k:k + 1] * y_lo
        r_hi = r_hi + tw[:, k:k + 1] * y_hi
    routed = jnp.concatenate([r_lo, r_hi], axis=1)
    xb = x1b_ref[...]
    g = jnp.dot(xb, wsg_ref[...], preferred_element_type=F32)
    u = jnp.dot(xb, wsu_ref[...], preferred_element_type=F32)
    h = (g * jax.nn.sigmoid(g) * u).astype(BF16)
    shared = jnp.dot(h, wsd_ref[...], preferred_element_type=F32)
    y = _layer_norm(DN_ALPHA * x1_ref[...] + (shared + routed), lg_ref[...], lb_ref[...])
    o_ref[...] = y
    ob_ref[...] = y.astype(BF16)


def _combine_rows(yt, tw, x1, x1b, wsg, wsu, wsd, lg, lb, tc=256):
    n = x1.shape[0]
    tc = min(tc, n)
    full = lambda a: pl.BlockSpec(a.shape, lambda i: (0,) * a.ndim)
    rows = lambda w: pl.BlockSpec((tc, w), lambda i: (i, 0))
    return pl.pallas_call(
        _combine_rows_kernel,
        grid=(n // tc,),
        in_specs=[pl.BlockSpec((TOP_K, tc, HALF), lambda i: (0, i, 0)),
                  rows(TOP_K), rows(D_MODEL), rows(D_MODEL),
                  full(wsg), full(wsu), full(wsd), full(lg), full(lb)],
        out_specs=[rows(D_MODEL), rows(D_MODEL)],
        out_shape=[jax.ShapeDtypeStruct((n, D_MODEL), F32),
                   jax.ShapeDtypeStruct((n, D_MODEL), BF16)],
        compiler_params=_cp(("arbitrary",)),
        name="moe_combine_ln",
    )(yt, tw, x1, x1b, wsg, wsu, wsd, lg, lb)


def _head_cols(w, heads, width, lo, hi):
    return w.reshape(w.shape[0], heads, width)[:, :, lo:hi]


def _prep_layer(w_in, b_gate, b_forget, mla_q_norm, w_uq, mla_kv_norm, w_ukv):
    f = lambda a: a.astype(BF16)
    half = MLA_ROPE_DIM // 2
    d = w_in.shape[0]
    qscale = HEAD_DIM ** -0.5
    w_qkv = jnp.concatenate([
        w_in[:, OFF_SB:OFF_SB + SB_W] * qscale, w_in[:, OFF_SB + SB_W:OFF_FOX],
        w_in[:, OFF_FOX:OFF_FOX + FOX_W] * (qscale * LOG2E), w_in[:, OFF_FOX + FOX_W:OFF_FGATE]],
        axis=1)
    w_kr = w_in[:, OFF_KR:OFF_GATE]
    w_kr_rot = jnp.concatenate([-w_kr[:, half:], w_kr[:, :half]], axis=1)
    z = lambda c: jnp.zeros((d, c), F32)
    pad_rope = LANES - MLA_QK_DIM
    w_small = jnp.concatenate([
        w_in[:, OFF_DQ:OFF_DKV], w_in[:, OFF_DKV:OFF_KR],
        z(MLA_NOPE_DIM), w_kr, z(pad_rope),
        z(MLA_NOPE_DIM), w_kr_rot, z(pad_rope),
        w_in[:, OFF_FGATE:OFF_DQ], z(LANES - FOX_HEADS)], axis=1)
    w_gate = w_in[:, OFF_GATE:]

    r = w_uq.shape[0]
    q_nope = _head_cols(w_uq, MLA_HEADS, MLA_QK_DIM, 0, MLA_NOPE_DIM)
    q_rope = _head_cols(w_uq, MLA_HEADS, MLA_QK_DIM, MLA_NOPE_DIM, MLA_QK_DIM)
    q_rope_rot = jnp.concatenate([-q_rope[:, :, half:], q_rope[:, :, :half]], axis=2)
    zq = lambda c: jnp.zeros((r, MLA_HEADS, c), F32)
    wqa = jnp.concatenate([q_nope, q_rope, zq(pad_rope)], axis=2).reshape(r, MLA_HEADS * LANES)
    wqb = jnp.concatenate([zq(MLA_NOPE_DIM), q_rope_rot, zq(pad_rope)], axis=2).reshape(
        r, MLA_HEADS * LANES)
    rk = w_ukv.shape[0]
    kvw = MLA_NOPE_DIM + MLA_V_DIM
    k_nope = _head_cols(w_ukv, MLA_HEADS, kvw, 0, MLA_NOPE_DIM)
    wk = jnp.concatenate([k_nope, jnp.zeros((rk, MLA_HEADS, LANES - MLA_NOPE_DIM), F32)],
                         axis=2).reshape(rk, MLA_HEADS * LANES)
    wv = _head_cols(w_ukv, MLA_HEADS, kvw, MLA_NOPE_DIM, kvw).reshape(rk, MLA_HEADS * MLA_V_DIM)
    bf = jnp.concatenate([b_forget, jnp.zeros((LANES - FOX_HEADS,), F32)]).reshape(1, LANES)
    return dict(w_qkv=f(w_qkv), w_small=f(w_small), w_gate=f(w_gate),
                b_gate=b_gate.reshape(1, -1), wqa=f(wqa), wqb=f(wqb), wk=f(wk), wv=f(wv),
                qn=mla_q_norm.reshape(1, -1), kvn=mla_kv_norm.reshape(1, -1), bf=bf)


def _rope_tables(seq):
    half = MLA_ROPE_DIM // 2
    inv_freq = jnp.power(ROPE_BASE, -jnp.arange(half, dtype=F32) / half)
    ang = jnp.arange(seq).astype(F32)[:, None] * inv_freq[None, :]
    cos = jnp.concatenate([jnp.cos(ang), jnp.cos(ang)], axis=1)
    sin = jnp.concatenate([jnp.sin(ang), jnp.sin(ang)], axis=1)
    pad = lambda t: jnp.concatenate([jnp.zeros((seq, MLA_NOPE_DIM), F32), t,
                                     jnp.zeros((seq, LANES - MLA_QK_DIM), F32)], axis=1)
    return pad(cos), pad(sin)


EXPERT_BM = 512


def kernel(x, ln1_g, ln1_b, ln2_g, ln2_b, w_in, b_gate, b_forget, mla_q_norm, w_uq, mla_kv_norm,
           w_ukv, w_proj_sb, w_proj_fox, w_proj_mla, w_out, w_router, router_bias,
           w_exp_gate, w_exp_up, w_exp_down, w_sh_gate, w_sh_up, w_sh_down):
    b, seq, d = x.shape
    n = b * seq
    depth = w_in.shape[0]
    cos_t, sin_t = _rope_tables(seq)
    xf = x.reshape(n, d)
    xb = xf.astype(BF16)
    f = lambda a: a.astype(BF16)
    bm = min(EXPERT_BM, n * TOP_K)
    n_tiles = n * TOP_K // bm + N_EXPERTS
    for l in range(depth):
        p = _prep_layer(w_in[l], b_gate[l], b_forget[l], mla_q_norm[l], w_uq[l], mla_kv_norm[l],
                        w_ukv[l])
        qkv = _matmul(xb, p["w_qkv"], BF16, tn=QKV_W // 2)
        small = _matmul(xb, p["w_small"], F32, tn=SMALL_W)
        q_mla, k_mla, v_mla, lf = _mla_prep(small, p["qn"], p["kvn"], p["wqa"], p["wqb"], p["wk"],
                                            p["wv"], cos_t, sin_t, p["bf"], seq)
        cum, cumt = _cumsum(lf.reshape(b, seq, LANES))
        cumt4 = cumt[:, :FOX_HEADS, :].reshape(b, FOX_HEADS, 1, seq)
        qkv3 = qkv.reshape(b, seq, QKV_W)
        o_sb = _sb_attention(qkv3).reshape(n, SB_W)
        o_fox = _fox_attention(qkv3, cum, cumt4).reshape(n, FOX_W)
        o_mla = _mla_attention(q_mla.reshape(b, seq, -1), k_mla.reshape(b, seq, -1),
                               v_mla.reshape(b, seq, -1)).reshape(n, -1)
        x1, x1b, x1p = _merge(o_sb, o_fox, o_mla, xb, xf, p["w_gate"], p["b_gate"],
                              f(w_proj_sb[l]), f(w_proj_fox[l]), f(w_proj_mla[l]), f(w_out[l]),
                              ln1_g[l].reshape(1, d), ln1_b[l].reshape(1, d))

        idx_t, tw_t, rank_t, cnt = _route(x1b, f(w_router[l].T), router_bias[l].reshape(-1, 1))
        counts = cnt[:, 0].astype(jnp.int32)
        starts, tile_exp, tile_valid, n_used = _expert_tiles(counts, n_tiles, bm)
        dest_t = _dest(idx_t, rank_t, starts)
        dest_win = dest_t.reshape(TOP_K, n // SC_WINDOW, SC_WINDOW).transpose(1, 0, 2)
        xs = _sc_dispatch(dest_win, x1p, n_tiles * bm)
        ys = _experts(l, n_used, tile_exp, tile_valid, xs, w_exp_gate, w_exp_up, w_exp_down, bm)
        yt = _sc_gather(dest_win, ys, n)
        xf, xb = _combine_rows(yt, tw_t.T, x1, x1b, f(w_sh_gate[l]), f(w_sh_up[l]),
                               f(w_sh_down[l]), ln2_g[l].reshape(1, d), ln2_b[l].reshape(1, d))
    return xf.reshape(b, seq, d)
```

```python
import functools

import jax
import jax.numpy as jnp
from jax import lax
from jax.experimental import pallas as pl
from jax.experimental.pallas import tpu as pltpu
from jax.experimental.pallas import tpu_sc as plsc

F32 = jnp.float32
BF16 = jnp.bfloat16

D_MODEL = 1024
HEAD_DIM = 64
SB_HEADS = 4
FOX_HEADS = 4
MLA_HEADS = 8
MLA_Q_RANK = 256
MLA_KV_RANK = 128
MLA_NOPE_DIM = 64
MLA_ROPE_DIM = 32
MLA_V_DIM = 64
ROPE_BASE = 10000.0
N_BRANCHES = 3
N_EXPERTS = 256
TOP_K = 8
N_GROUPS = 8
TOPK_GROUPS = 4
GROUP_SIZE = N_EXPERTS // N_GROUPS
EXPERT_FF = 256
SHARED_FF = 256
ROUTED_SCALE = 2.5
CHUNK = 64
LN_EPS = 1e-5
RMS_EPS = 1e-6
DEPTH = 2
DN_ALPHA = (2 * DEPTH) ** 0.25

SB_W = SB_HEADS * HEAD_DIM
FOX_W = FOX_HEADS * HEAD_DIM
MLA_QK_DIM = MLA_NOPE_DIM + MLA_ROPE_DIM
OFF_SB = 0
OFF_FOX = OFF_SB + 3 * SB_W
OFF_FGATE = OFF_FOX + 3 * FOX_W
OFF_DQ = OFF_FGATE + FOX_HEADS
OFF_DKV = OFF_DQ + MLA_Q_RANK
OFF_KR = OFF_DKV + MLA_KV_RANK
OFF_GATE = OFF_KR + MLA_ROPE_DIM

LANES = 128
SUBLANES = 8
QKV_W = 3 * SB_W + 3 * FOX_W
SMALL_W = MLA_Q_RANK + MLA_KV_RANK + 3 * LANES
SB_CUTOFF = -104.0
LOG2E = 1.4426950408889634

VMEM_LIMIT = 48 * 1024 * 1024


def _cp(sem, vmem=VMEM_LIMIT):
    return pltpu.CompilerParams(dimension_semantics=sem, vmem_limit_bytes=vmem)


def _mm_kernel(x_ref, w_ref, o_ref, *, sub):
    x = x_ref[...]
    for c in range(0, o_ref.shape[1], sub):
        acc = jnp.dot(x, w_ref[:, c:c + sub], preferred_element_type=F32)
        o_ref[:, c:c + sub] = acc.astype(o_ref.dtype)


def _matmul(x, w, out_dtype, tn, tm=1024, sub=256):
    n, k = x.shape
    c = w.shape[1]
    tm = min(tm, n)
    return pl.pallas_call(
        functools.partial(_mm_kernel, sub=sub),
        grid=(n // tm, c // tn),
        in_specs=[pl.BlockSpec((tm, k), lambda i, j: (i, 0)),
                  pl.BlockSpec((k, tn), lambda i, j: (0, j))],
        out_specs=pl.BlockSpec((tm, tn), lambda i, j: (i, j)),
        out_shape=jax.ShapeDtypeStruct((n, c), out_dtype),
        compiler_params=_cp(("arbitrary", "arbitrary")),
        name="proj_matmul",
    )(x, w)


def _split_bf16(x, parts):
    out = []
    for _ in range(parts):
        h = x.astype(BF16)
        out.append(h)
        x = x - h.astype(F32)
    return out


def _mla_prep_kernel(x_ref, ws_ref, qn_ref, kvn_ref, wqa_ref, wqb_ref, wk_ref, wv_ref,
                     cos_ref, sin_ref, bf_ref, q_ref, k_ref, v_ref, lf_ref):
    sm = jnp.dot(x_ref[...], ws_ref[...], preferred_element_type=F32)
    dq = sm[:, :MLA_Q_RANK]
    dkv = sm[:, MLA_Q_RANK:MLA_Q_RANK + MLA_KV_RANK]
    o = MLA_Q_RANK + MLA_KV_RANK
    kr = sm[:, o:o + LANES]
    kr_rot = sm[:, o + LANES:o + 2 * LANES]
    fg = sm[:, o + 2 * LANES:o + 3 * LANES]

    cq = dq * lax.rsqrt(jnp.mean(dq * dq, axis=-1, keepdims=True) + RMS_EPS) * qn_ref[...]
    ckv = dkv * lax.rsqrt(jnp.mean(dkv * dkv, axis=-1, keepdims=True) + RMS_EPS) * kvn_ref[...]
    cq = cq.astype(BF16)
    ckv = ckv.astype(BF16)

    cosk = cos_ref[...]
    sink = sin_ref[...]
    lane = lax.broadcasted_iota(jnp.int32, (1, LANES), 1)
    nope = (lane < MLA_NOPE_DIM).astype(F32)
    scale = MLA_QK_DIM ** -0.5 * LOG2E
    cq_tab = jnp.concatenate([(cosk + nope) * scale] * MLA_HEADS, axis=1)
    sq_tab = jnp.concatenate([sink * scale] * MLA_HEADS, axis=1)

    qa = jnp.dot(cq, wqa_ref[...], preferred_element_type=F32)
    qb = jnp.dot(cq, wqb_ref[...], preferred_element_type=F32)
    q_ref[...] = (qa * cq_tab + qb * sq_tab).astype(q_ref.dtype)

    k_rope = kr * cosk + kr_rot * sink
    ka = jnp.dot(ckv, wk_ref[...], preferred_element_type=F32)
    k_ref[...] = (ka + jnp.concatenate([k_rope] * MLA_HEADS, axis=1)).astype(k_ref.dtype)
    v_ref[...] = jnp.dot(ckv, wv_ref[...], preferred_element_type=F32).astype(v_ref.dtype)
    lf_ref[...] = jax.nn.log_sigmoid(fg + bf_ref[...])


def _mla_prep(xb, w_small, qn, kvn, wqa, wqb, wk, wv, cos_t, sin_t, bf, seq, tm=512):
    n, d = xb.shape
    tm = min(tm, seq)
    sblocks = seq // tm
    hw = MLA_HEADS * LANES
    full = lambda a: pl.BlockSpec(a.shape, lambda i: (0,) * a.ndim)
    return pl.pallas_call(
        _mla_prep_kernel,
        grid=(n // tm,),
        in_specs=[pl.BlockSpec((tm, d), lambda i: (i, 0)), full(w_small),
                  full(qn), full(kvn), full(wqa), full(wqb), full(wk), full(wv),
                  pl.BlockSpec((tm, LANES), lambda i: (i % sblocks, 0)),
                  pl.BlockSpec((tm, LANES), lambda i: (i % sblocks, 0)),
                  full(bf)],
        out_specs=[pl.BlockSpec((tm, hw), lambda i: (i, 0)),
                   pl.BlockSpec((tm, hw), lambda i: (i, 0)),
                   pl.BlockSpec((tm, MLA_HEADS * MLA_V_DIM), lambda i: (i, 0)),
                   pl.BlockSpec((tm, LANES), lambda i: (i, 0))],
        out_shape=[jax.ShapeDtypeStruct((n, hw), BF16),
                   jax.ShapeDtypeStruct((n, hw), BF16),
                   jax.ShapeDtypeStruct((n, MLA_HEADS * MLA_V_DIM), BF16),
                   jax.ShapeDtypeStruct((n, LANES), F32)],
        compiler_params=_cp(("arbitrary",)),
        name="mla_prep",
    )(xb, w_small, qn, kvn, wqa, wqb, wk, wv, cos_t, sin_t, bf)


def _cumsum_kernel(lf_ref, cum_ref, cumt_ref, *, seq):
    r = lax.broadcasted_iota(jnp.int32, (LANES, LANES), 0)
    c = lax.broadcasted_iota(jnp.int32, (LANES, LANES), 1)
    lower = (c <= r).astype(BF16)

    def body(j, carry):
        start = pl.multiple_of(j * LANES, LANES)
        blk = lf_ref[0, pl.ds(start, LANES), :]
        acc = carry
        for part in _split_bf16(blk, 3):
            acc = acc + jnp.dot(lower, part, preferred_element_type=F32)
        scaled = acc * LOG2E
        cum_ref[0, pl.ds(start, LANES), :] = scaled
        cumt_ref[0, :, pl.ds(start, LANES)] = scaled.T[:SUBLANES, :]
        return jnp.broadcast_to(acc[LANES - 1:LANES, :], (LANES, LANES))

    lax.fori_loop(0, seq // LANES, body, jnp.zeros((LANES, LANES), F32))


def _cumsum(lf3):
    b, seq, _ = lf3.shape
    return pl.pallas_call(
        functools.partial(_cumsum_kernel, seq=seq),
        grid=(b,),
        in_specs=[pl.BlockSpec((1, seq, LANES), lambda i: (i, 0, 0))],
        out_specs=[pl.BlockSpec((1, seq, LANES), lambda i: (i, 0, 0)),
                   pl.BlockSpec((1, SUBLANES, seq), lambda i: (i, 0, 0))],
        out_shape=[jax.ShapeDtypeStruct((b, seq, LANES), F32),
                   jax.ShapeDtypeStruct((b, SUBLANES, seq), F32)],
        compiler_params=_cp(("arbitrary",)),
        name="forget_cumsum",
    )(lf3)


_NT = (((1,), (1,)), ((), ()))
NEG_INIT = -1e30
_TN = (((0,), (0,)), ((), ()))
ATTN_TK = 256


def _head_halves(q):
    lane = lax.broadcasted_iota(jnp.int32, (1, LANES), 1)
    zero = jnp.zeros_like(q)
    return [jnp.where(lane < HEAD_DIM, q, zero), jnp.where(lane >= HEAD_DIM, q, zero)]


def _mask_first_block(x, keep, fill, tk):
    head = jnp.where(keep, x[:, :tk], fill)
    return head if x.shape[1] == tk else jnp.concatenate([head, x[:, tk:]], axis=1)


def _fox_kernel(q_ref, k_ref, v_ref, cc_ref, cr_ref, o_ref, *, seq, tk):
    hp = pl.program_id(1)
    lane = lax.broadcasted_iota(jnp.int32, (1, LANES), 1)
    qs = _head_halves(q_ref[0])
    key = lax.broadcasted_iota(jnp.int32, (tk, tk), 0)
    qry = lax.broadcasted_iota(jnp.int32, (tk, tk), 1)
    causal = key <= qry
    carry = [(jnp.full((1, seq), NEG_INIT, F32), jnp.zeros((1, seq), F32),
              jnp.zeros((HEAD_DIM, seq), F32)) for _ in range(2)]
    for j in range(seq // tk):
        q0 = j * tk
        k = k_ref[0, q0:q0 + tk, :]
        cc = cc_ref[0, q0:q0 + tk, :]
        for hh in range(2):
            m, l, acc = carry[hh]
            v = v_ref[0, q0:q0 + tk, hh * HEAD_DIM:(hh + 1) * HEAD_DIM]
            cs = jnp.sum(jnp.where(lane == 2 * hp + hh, cc, 0.0), axis=1, keepdims=True)
            ct = cr_ref[0, hh, :, q0:]
            st = lax.dot_general(k, qs[hh][q0:, :], _NT, preferred_element_type=F32)
            st = _mask_first_block((st + ct) - cs, causal, -jnp.inf, tk)
            mn, ln, an = _softmax_step_t(st, m[:, q0:], l[:, q0:], acc[:, q0:], v)
            if q0:
                mn = jnp.concatenate([m[:, :q0], mn], axis=1)
                ln = jnp.concatenate([l[:, :q0], ln], axis=1)
                an = jnp.concatenate([acc[:, :q0], an], axis=1)
            carry[hh] = (mn, ln, an)
    _pair_out_t([carry[0][2], carry[1][2]], [carry[0][1], carry[1][1]], o_ref)


def _fox_attention(qkv3, cum, cumt4):
    b, seq, _ = qkv3.shape
    tk = min(ATTN_TK, seq)
    hp = FOX_HEADS // 2
    qoff = OFF_FOX // LANES
    return pl.pallas_call(
        functools.partial(_fox_kernel, seq=seq, tk=tk),
        grid=(b, hp),
        in_specs=[pl.BlockSpec((1, seq, LANES), lambda bi, h: (bi, 0, qoff + h)),
                  pl.BlockSpec((1, seq, LANES), lambda bi, h: (bi, 0, qoff + hp + h)),
                  pl.BlockSpec((1, seq, LANES), lambda bi, h: (bi, 0, qoff + 2 * hp + h)),
                  pl.BlockSpec((1, seq, LANES), lambda bi, h: (bi, 0, 0)),
                  pl.BlockSpec((1, 2, 1, seq), lambda bi, h: (bi, h, 0, 0))],
        out_specs=pl.BlockSpec((1, seq, LANES), lambda bi, h: (bi, 0, h)),
        out_shape=jax.ShapeDtypeStruct((b, seq, FOX_W), BF16),
        compiler_params=_cp(("arbitrary", "arbitrary")),
        name="fox_attention",
    )(qkv3, qkv3, qkv3, cum, cumt4)


def _softmax_step_t(st, m, l, acc, v):
    m_new = jnp.maximum(m, jnp.max(st, axis=0, keepdims=True))
    alpha = jnp.exp2(m - m_new)
    p = jnp.exp2(st - m_new)
    l = alpha * l + jnp.sum(p, axis=0, keepdims=True)
    acc = alpha * acc + lax.dot_general(v, p.astype(BF16), _TN, preferred_element_type=F32)
    return m_new, l, acc


def _pair_out_t(accs, ls, o_ref, pair=0):
    ot = jnp.concatenate([accs[0] / ls[0], accs[1] / ls[1]], axis=0)
    o_ref[0, :, pair * LANES:(pair + 1) * LANES] = ot.T.astype(o_ref.dtype)


MLA_STEP_HEADS = 2


def _mla_kernel(q_ref, k_ref, v_ref, o_ref, *, seq, tk):
    nh = MLA_STEP_HEADS
    key = lax.broadcasted_iota(jnp.int32, (tk, tk), 0)
    qry = lax.broadcasted_iota(jnp.int32, (tk, tk), 1)
    shift = CHUNK.bit_length() - 1
    chunk_causal = (key >> shift) <= (qry >> shift)
    carry = [(jnp.full((1, seq), NEG_INIT, F32), jnp.zeros((1, seq), F32),
              jnp.zeros((MLA_V_DIM, seq), F32)) for _ in range(nh)]
    for j in range(seq // tk):
        q0 = j * tk
        for hh in range(nh):
            m, l, acc = carry[hh]
            k = k_ref[0, q0:q0 + tk, hh * LANES:(hh + 1) * LANES]
            v = v_ref[0, q0:q0 + tk, hh * MLA_V_DIM:(hh + 1) * MLA_V_DIM]
            st = lax.dot_general(k, q_ref[0, q0:, hh * LANES:(hh + 1) * LANES], _NT,
                                 preferred_element_type=F32)
            st = _mask_first_block(st, chunk_causal, -jnp.inf, tk)
            mn, ln, an = _softmax_step_t(st, m[:, q0:], l[:, q0:], acc[:, q0:], v)
            if q0:
                mn = jnp.concatenate([m[:, :q0], mn], axis=1)
                ln = jnp.concatenate([l[:, :q0], ln], axis=1)
                an = jnp.concatenate([acc[:, :q0], an], axis=1)
            carry[hh] = (mn, ln, an)
    for pair in range(nh // 2):
        a, b = carry[2 * pair], carry[2 * pair + 1]
        _pair_out_t([a[2], b[2]], [a[1], b[1]], o_ref, pair)


def _mla_attention(q3, k3, v3):
    b, seq, _ = q3.shape
    tk = min(ATTN_TK, seq)
    nh = MLA_STEP_HEADS
    return pl.pallas_call(
        functools.partial(_mla_kernel, seq=seq, tk=tk),
        grid=(b, MLA_HEADS // nh),
        in_specs=[pl.BlockSpec((1, seq, nh * LANES), lambda bi, h: (bi, 0, h)),
                  pl.BlockSpec((1, seq, nh * LANES), lambda bi, h: (bi, 0, h)),
                  pl.BlockSpec((1, seq, nh // 2 * LANES), lambda bi, h: (bi, 0, h))],
        out_specs=pl.BlockSpec((1, seq, nh // 2 * LANES), lambda bi, h: (bi, 0, h)),
        out_shape=jax.ShapeDtypeStruct((b, seq, MLA_HEADS * MLA_V_DIM), BF16),
        compiler_params=_cp(("arbitrary", "arbitrary")),
        name="mla_attention",
    )(q3, k3, v3)


def _softplus(z):
    return jnp.maximum(z, 0.0) + jnp.log(1.0 + jnp.exp(-jnp.abs(z)))


def _sb_kernel(q_ref, k_ref, v_ref, o_ref, run_ref, acc_ref, *, seq, tk):
    n = seq // tk
    qs = _head_halves(q_ref[0])
    key = lax.broadcasted_iota(jnp.int32, (tk, tk), 0)
    qry = lax.broadcasted_iota(jnp.int32, (tk, tk), 1)
    strict = key < qry
    later = (qry > key).astype(BF16)

    def unit(j, qlo, qhi, diag):
        k = k_ref[0, j * tk:(j + 1) * tk, :]
        for hh in range(2):
            v = v_ref[0, j * tk:(j + 1) * tk, hh * HEAD_DIM:(hh + 1) * HEAD_DIM]
            z = lax.dot_general(k, qs[hh][qlo:qhi, :], _NT, preferred_element_type=F32)
            sp = _softplus(z)
            log_keep = -sp
            if diag:
                log_keep = _mask_first_block(log_keep, strict, 0.0, tk)
            between = run_ref[hh, :, qlo:qhi]
            for part in _split_bf16(log_keep, 2):
                between = between + jnp.dot(later, part, preferred_element_type=F32)
            w = jnp.exp((z - sp) + between)
            if diag:
                w = _mask_first_block(w, strict, 0.0, tk)
            acc_ref[hh, :, qlo:qhi] += lax.dot_general(v, w.astype(BF16), _TN,
                                                       preferred_element_type=F32)
            run_ref[hh, :, qlo:qhi] += jnp.sum(log_keep, axis=0, keepdims=True)

    run_ref[...] = jnp.zeros_like(run_ref)
    acc_ref[...] = jnp.zeros_like(acc_ref)
    for j in range(n - 1, -1, -1):
        unit(j, j * tk, min((j + 2) * tk, seq), True)

    for j in range(n - 3, -1, -1):
        qlo = (j + 2) * tk

        @pl.when(jnp.max(run_ref[:, :, qlo:]) >= SB_CUTOFF)
        def _():
            unit(j, qlo, seq, False)

    o_ref[0] = jnp.concatenate([acc_ref[0], acc_ref[1]], axis=0).T.astype(o_ref.dtype)


def _sb_attention(qkv3):
    b, seq, _ = qkv3.shape
    tk = min(ATTN_TK, seq)
    hp = SB_HEADS // 2
    qoff = OFF_SB // LANES
    return pl.pallas_call(
        functools.partial(_sb_kernel, seq=seq, tk=tk),
        grid=(b, hp),
        in_specs=[pl.BlockSpec((1, seq, LANES), lambda bi, h: (bi, 0, qoff + h)),
                  pl.BlockSpec((1, seq, LANES), lambda bi, h: (bi, 0, qoff + hp + h)),
                  pl.BlockSpec((1, seq, LANES), lambda bi, h: (bi, 0, qoff + 2 * hp + h))],
        out_specs=pl.BlockSpec((1, seq, LANES), lambda bi, h: (bi, 0, h)),
        out_shape=jax.ShapeDtypeStruct((b, seq, SB_W), BF16),
        scratch_shapes=[pltpu.VMEM((2, 1, seq), F32), pltpu.VMEM((2, HEAD_DIM, seq), F32)],
        compiler_params=_cp(("arbitrary", "arbitrary")),
        name="sb_attention",
    )(qkv3, qkv3, qkv3)


def _layer_norm(y, g, b):
    yc = y - jnp.mean(y, axis=-1, keepdims=True)
    var = jnp.mean(yc * yc, axis=-1, keepdims=True)
    return yc * lax.rsqrt(var + LN_EPS) * g + b


HALF = D_MODEL // 2
U32 = jnp.uint32


def _pack_halves(y):
    bits = lax.bitcast_convert_type(y.astype(BF16).astype(F32), U32)
    return (bits[:, :HALF] >> 16) | (bits[:, HALF:] & U32(0xFFFF0000))


def _unpack_halves(w):
    lo = lax.bitcast_convert_type(w << 16, F32)
    hi = lax.bitcast_convert_type(w & U32(0xFFFF0000), F32)
    return lo, hi


def _merge_kernel(osb_ref, ofox_ref, omla_ref, xb_ref, x_ref, wg_ref, bg_ref, wsb_ref, wfox_ref,
                  wmla_ref, wout_ref, lg_ref, lb_ref, x1_ref, x1b_ref, x1p_ref):
    xb = xb_ref[...]
    mixed = None
    branches = ((osb_ref, wsb_ref), (ofox_ref, wfox_ref), (omla_ref, wmla_ref))
    for i, (o_ref, w_ref) in enumerate(branches):
        cols = slice(i * D_MODEL, (i + 1) * D_MODEL)
        gate = jax.nn.sigmoid(jnp.dot(xb, wg_ref[:, cols], preferred_element_type=F32)
                              + bg_ref[:, cols])
        term = gate * jnp.dot(o_ref[...], w_ref[...], preferred_element_type=F32)
        mixed = term if mixed is None else mixed + term
    mix = jnp.dot(mixed.astype(BF16), wout_ref[...], preferred_element_type=F32)
    y = _layer_norm(DN_ALPHA * x_ref[...] + mix, lg_ref[...], lb_ref[...])
    x1_ref[...] = y
    x1b_ref[...] = y.astype(BF16)
    x1p_ref[...] = _pack_halves(y)


def _merge(o_sb, o_fox, o_mla, xb, x, wg, bg, wsb, wfox, wmla, wout, lg, lb, tm=512):
    n = x.shape[0]
    tm = min(tm, n)
    full = lambda a: pl.BlockSpec(a.shape, lambda i: (0,) * a.ndim)
    rows = lambda w: pl.BlockSpec((tm, w), lambda i: (i, 0))
    return pl.pallas_call(
        _merge_kernel,
        grid=(n // tm,),
        in_specs=[rows(SB_W), rows(FOX_W), rows(MLA_HEADS * MLA_V_DIM), rows(D_MODEL),
                  rows(D_MODEL), full(wg), full(bg), full(wsb), full(wfox), full(wmla),
                  full(wout), full(lg), full(lb)],
        out_specs=[rows(D_MODEL), rows(D_MODEL), rows(HALF)],
        out_shape=[jax.ShapeDtypeStruct((n, D_MODEL), F32),
                   jax.ShapeDtypeStruct((n, D_MODEL), BF16),
                   jax.ShapeDtypeStruct((n, HALF), U32)],
        compiler_params=_cp(("arbitrary",)),
        name="merge_outproj_ln",
    )(o_sb, o_fox, o_mla, xb, x, wg, bg, wsb, wfox, wmla, wout, lg, lb)


def _route_kernel(x_ref, wr_ref, rb_ref, idx_ref, w_ref, rank_ref, cnt_ref, run_ref, *, tm):
    step = pl.program_id(0)

    @pl.when(step == 0)
    def _():
        run_ref[...] = jnp.zeros_like(run_ref)

    logits = lax.dot_general(wr_ref[...], x_ref[...], _NT, preferred_element_type=F32)
    scores = jax.nn.sigmoid(logits)
    biased = scores + rb_ref[...]
    e_iota = lax.broadcasted_iota(jnp.int32, (N_EXPERTS, tm), 0)
    big = jnp.int32(1 << 20)

    g_iota = lax.broadcasted_iota(jnp.int32, (GROUP_SIZE, tm), 0)
    gs_rows = []
    for g in range(N_GROUPS):
        blk = biased[g * GROUP_SIZE:(g + 1) * GROUP_SIZE, :]
        m1 = jnp.max(blk, axis=0, keepdims=True)
        first = jnp.min(jnp.where(blk == m1, g_iota, big), axis=0, keepdims=True)
        m2 = jnp.max(jnp.where(g_iota == first, -jnp.inf, blk), axis=0, keepdims=True)
        gs_rows.append(m1 + m2)
    gs = jnp.concatenate(gs_rows, axis=0)
    n_iota = lax.broadcasted_iota(jnp.int32, (N_GROUPS, tm), 0)
    keep = jnp.zeros((N_GROUPS, tm), jnp.bool_)
    for _ in range(TOPK_GROUPS):
        m = jnp.max(gs, axis=0, keepdims=True)
        first = jnp.min(jnp.where(gs == m, n_iota, big), axis=0, keepdims=True)
        hit = n_iota == first
        keep = jnp.logical_or(keep, hit)
        gs = jnp.where(hit, -jnp.inf, gs)
    keep_f = keep.astype(F32)
    expert_keep = jnp.concatenate(
        [jnp.broadcast_to(keep_f[g:g + 1, :], (GROUP_SIZE, tm)) for g in range(N_GROUPS)], axis=0)
    masked = jnp.where(expert_keep > 0.5, biased, -jnp.inf)

    idx_rows, w_rows, hits = [], [], []
    sel = jnp.zeros((N_EXPERTS, tm), F32)
    for _ in range(TOP_K):
        m = jnp.max(masked, axis=0, keepdims=True)
        first = jnp.min(jnp.where(masked == m, e_iota, big), axis=0, keepdims=True)
        hit = e_iota == first
        idx_rows.append(first)
        w_rows.append(jnp.sum(jnp.where(hit, scores, 0.0), axis=0, keepdims=True))
        hits.append(hit)
        sel = sel + hit.astype(F32)
        masked = jnp.where(hit, -jnp.inf, masked)
    w = jnp.concatenate(w_rows, axis=0)
    w = w / jnp.sum(w, axis=0, keepdims=True) * ROUTED_SCALE
    idx_ref[...] = jnp.concatenate(idx_rows, axis=0)
    w_ref[...] = w

    r = lax.broadcasted_iota(jnp.int32, (tm, tm), 0)
    c = lax.broadcasted_iota(jnp.int32, (tm, tm), 1)
    earlier = (r < c).astype(BF16)
    prefix = jnp.dot(sel.astype(BF16), earlier, preferred_element_type=F32) + run_ref[...]
    rank_rows = [jnp.sum(jnp.where(h, prefix, 0.0), axis=0, keepdims=True) for h in hits]
    rank_ref[...] = jnp.concatenate(rank_rows, axis=0).astype(jnp.int32)
    total = run_ref[...] + jnp.sum(sel, axis=1, keepdims=True)
    run_ref[...] = total
    cnt_ref[...] = jnp.broadcast_to(total, (N_EXPERTS, LANES))


def _route(x1b, wr_t, rb, tm=256):
    n = x1b.shape[0]
    tm = min(tm, n)
    return pl.pallas_call(
        functools.partial(_route_kernel, tm=tm),
        grid=(n // tm,),
        in_specs=[pl.BlockSpec((tm, D_MODEL), lambda i: (i, 0)),
                  pl.BlockSpec((N_EXPERTS, D_MODEL), lambda i: (0, 0)),
                  pl.BlockSpec((N_EXPERTS, 1), lambda i: (0, 0))],
        out_specs=[pl.BlockSpec((TOP_K, tm), lambda i: (0, i)),
                   pl.BlockSpec((TOP_K, tm), lambda i: (0, i)),
                   pl.BlockSpec((TOP_K, tm), lambda i: (0, i)),
                   pl.BlockSpec((N_EXPERTS, LANES), lambda i: (0, 0))],
        out_shape=[jax.ShapeDtypeStruct((TOP_K, n), jnp.int32),
                   jax.ShapeDtypeStruct((TOP_K, n), F32),
                   jax.ShapeDtypeStruct((TOP_K, n), jnp.int32),
                   jax.ShapeDtypeStruct((N_EXPERTS, LANES), F32)],
        scratch_shapes=[pltpu.VMEM((N_EXPERTS, 1), F32)],
        compiler_params=_cp(("arbitrary",)),
        name="router_topk",
    )(x1b, wr_t, rb)


def _dest_kernel(idx_ref, rank_ref, st_ref, dest_ref, *, tm):
    e_iota = lax.broadcasted_iota(jnp.int32, (N_EXPERTS, tm), 0)
    starts = st_ref[...]
    rows = []
    for k in range(TOP_K):
        hit = e_iota == idx_ref[k:k + 1, :]
        rows.append(jnp.sum(jnp.where(hit, starts, 0.0), axis=0, keepdims=True))
    dest_ref[...] = jnp.concatenate(rows, axis=0).astype(jnp.int32) + rank_ref[...]


def _dest(idx_t, rank_t, starts, tm=512):
    n = idx_t.shape[1]
    tm = min(tm, n)
    blk = lambda: pl.BlockSpec((TOP_K, tm), lambda i: (0, i))
    return pl.pallas_call(
        functools.partial(_dest_kernel, tm=tm),
        grid=(n // tm,),
        in_specs=[blk(), blk(), pl.BlockSpec((N_EXPERTS, 1), lambda i: (0, 0))],
        out_specs=blk(),
        out_shape=jax.ShapeDtypeStruct((TOP_K, n), jnp.int32),
        compiler_params=_cp(("arbitrary",)),
        name="moe_dest",
    )(idx_t, rank_t, starts.astype(F32).reshape(N_EXPERTS, 1))


SC_CORES = 2
SC_SUBCORES = 16
SC_WINDOW = 128

def _sc_mesh():
    return plsc.VectorSubcoreMesh(core_axis_name="c", subcore_axis_name="s",
                                  num_cores=SC_CORES, num_subcores=SC_SUBCORES)


def _sc_windows(n):
    workers = SC_CORES * SC_SUBCORES
    assert n % (SC_WINDOW * workers) == 0, n
    return n // SC_WINDOW // workers


def _sc_dispatch(dest_win, x1p, n_rows):
    n = x1p.shape[0]
    per_worker = _sc_windows(n)

    @functools.partial(
        pl.kernel, mesh=_sc_mesh(),
        out_type=jax.ShapeDtypeStruct((n_rows, HALF), U32),
        scratch_types=[pltpu.VMEM((TOP_K, SC_WINDOW), jnp.int32),
                       pltpu.VMEM((SC_WINDOW, HALF), U32)],
        name="moe_dispatch_sc")
    def run(dest_hbm, x_hbm, xs_hbm, idx_v, rows_v):
        worker = lax.axis_index("s") * SC_CORES + lax.axis_index("c")

        @pl.loop(0, per_worker)
        def _(c):
            win = worker * per_worker + c
            pltpu.sync_copy(dest_hbm.at[win], idx_v)
            pltpu.sync_copy(x_hbm.at[pl.ds(pl.multiple_of(win * SC_WINDOW, SC_WINDOW),
                                           SC_WINDOW)], rows_v)
            for k in range(TOP_K):
                pltpu.sync_copy(rows_v, xs_hbm.at[idx_v.at[k]])

    return run(dest_win, x1p)


def _sc_gather(dest_win, ys, n):
    per_worker = _sc_windows(n)

    @functools.partial(
        pl.kernel, mesh=_sc_mesh(),
        out_type=jax.ShapeDtypeStruct((TOP_K, n, HALF), U32),
        scratch_types=[pltpu.VMEM((TOP_K, SC_WINDOW), jnp.int32),
                       pltpu.VMEM((SC_WINDOW, HALF), U32)],
        name="moe_gather_sc")
    def run(dest_hbm, ys_hbm, yt_hbm, idx_v, rows_v):
        worker = lax.axis_index("s") * SC_CORES + lax.axis_index("c")

        @pl.loop(0, per_worker)
        def _(c):
            win = worker * per_worker + c
            start = pl.multiple_of(win * SC_WINDOW, SC_WINDOW)
            pltpu.sync_copy(dest_hbm.at[win], idx_v)
            for k in range(TOP_K):
                pltpu.sync_copy(ys_hbm.at[idx_v.at[k]], rows_v)
                pltpu.sync_copy(rows_v, yt_hbm.at[k, pl.ds(start, SC_WINDOW)])

    return run(dest_win, ys)


def _expert_kernel(used_ref, exp_ref, valid_ref, xs_ref, wg_ref, wu_ref, wd_ref, ys_ref, *, bm):
    w = pl.program_id(0)

    @pl.when(w < used_ref[0])
    def _():
        rows = lax.broadcasted_iota(jnp.int32, (bm, 1), 0)
        x_lo, x_hi = _unpack_halves(jnp.where(rows < valid_ref[w], xs_ref[...], U32(0)))
        x = jnp.concatenate([x_lo, x_hi], axis=1).astype(BF16)
        g = jnp.dot(x, wg_ref[0, 0].astype(BF16), preferred_element_type=F32)
        u = jnp.dot(x, wu_ref[0, 0].astype(BF16), preferred_element_type=F32)
        h = (g * jax.nn.sigmoid(g) * u).astype(BF16)
        ys_ref[...] = _pack_halves(jnp.dot(h, wd_ref[0, 0].astype(BF16),
                                           preferred_element_type=F32))


def _experts(layer, n_used, tile_exp, tile_valid, xs, wg, wu, wd, bm):
    r = xs.shape[0]
    last = lambda w, used: jnp.minimum(w, used[0] - 1)
    weights = lambda shape: pl.BlockSpec(
        (1, 1) + shape, lambda w, used, e, v: (layer, e[last(w, used)], 0, 0))
    grid_spec = pltpu.PrefetchScalarGridSpec(
        num_scalar_prefetch=3,
        grid=(r // bm,),
        in_specs=[pl.BlockSpec((bm, HALF), lambda w, used, e, v: (last(w, used), 0)),
                  weights((D_MODEL, EXPERT_FF)), weights((D_MODEL, EXPERT_FF)),
                  weights((EXPERT_FF, D_MODEL))],
        out_specs=pl.BlockSpec((bm, HALF), lambda w, used, e, v: (last(w, used), 0)),
    )
    return pl.pallas_call(
        functools.partial(_expert_kernel, bm=bm),
        grid_spec=grid_spec,
        out_shape=jax.ShapeDtypeStruct((r, HALF), U32),
        compiler_params=_cp(("arbitrary",)),
        name="moe_experts",
    )(n_used, tile_exp, tile_valid, xs, wg, wu, wd)


def _expert_tiles(counts, n_tiles, bm):
    padded = (counts + bm - 1) // bm * bm
    pad_end = jnp.cumsum(padded)
    pad_start = pad_end - padded
    row0 = jnp.arange(n_tiles, dtype=jnp.int32) * bm
    exp = jnp.minimum(jnp.sum((pad_end[None, :] <= row0[:, None]).astype(jnp.int32), axis=1),
                      N_EXPERTS - 1)
    valid = jnp.clip(pad_start[exp] + counts[exp] - row0, 0, bm)
    n_used = (pad_end[-1:] // bm).astype(jnp.int32)
    return pad_start.astype(jnp.int32), exp, valid.astype(jnp.int32), n_used


def _combine_rows_kernel(yt_ref, tw_ref, x1_ref, x1b_ref, wsg_ref, wsu_ref, wsd_ref, lg_ref, lb_ref,
                         o_ref, ob_ref):
    tw = tw_ref[...]
    r_lo = jnp.zeros((tw.shape[0], HALF), F32)
    r_hi = jnp.zeros((tw.shape[0], HALF), F32)
    for k in range(TOP_K):
        y_lo, y_hi = _unpack_halves(yt_ref[k])
        r_lo = r_lo + tw[:, k:k + 1] * y_lo
        r_hi = r_hi + tw[:, k:k + 1] * y_hi
    routed = jnp.concatenate([r_lo, r_hi], axis=1)
    xb = x1b_ref[...]
    g = jnp.dot(xb, wsg_ref[...], preferred_element_type=F32)
    u = jnp.dot(xb, wsu_ref[...], preferred_element_type=F32)
    h = (g * jax.nn.sigmoid(g) * u).astype(BF16)
    shared = jnp.dot(h, wsd_ref[...], preferred_element_type=F32)
    y = _layer_norm(DN_ALPHA * x1_ref[...] + (shared + routed), lg_ref[...], lb_ref[...])
    o_ref[...] = y
    ob_ref[...] = y.astype(BF16)


def _combine_rows(yt, tw, x1, x1b, wsg, wsu, wsd, lg, lb, tc=256):
    n = x1.shape[0]
    tc = min(tc, n)
    full = lambda a: pl.BlockSpec(a.shape, lambda i: (0,) * a.ndim)
    rows = lambda w: pl.BlockSpec((tc, w), lambda i: (i, 0))
    return pl.pallas_call(
        _combine_rows_kernel,
        grid=(n // tc,),
        in_specs=[pl.BlockSpec((TOP_K, tc, HALF), lambda i: (0, i, 0)),
                  rows(TOP_K), rows(D_MODEL), rows(D_MODEL),
                  full(wsg), full(wsu), full(wsd), full(lg), full(lb)],
        out_specs=[rows(D_MODEL), rows(D_MODEL)],
        out_shape=[jax.ShapeDtypeStruct((n, D_MODEL), F32),
                   jax.ShapeDtypeStruct((n, D_MODEL), BF16)],
        compiler_params=_cp(("arbitrary",)),
        name="moe_combine_ln",
    )(yt, tw, x1, x1b, wsg, wsu, wsd, lg, lb)


def _head_cols(w, heads, width, lo, hi):
    return w.reshape(w.shape[0], heads, width)[:, :, lo:hi]


def _prep_layer(w_in, b_gate, b_forget, mla_q_norm, w_uq, mla_kv_norm, w_ukv):
    f = lambda a: a.astype(BF16)
    half = MLA_ROPE_DIM // 2
    d = w_in.shape[0]
    qscale = HEAD_DIM ** -0.5
    w_qkv = jnp.concatenate([
        w_in[:, OFF_SB:OFF_SB + SB_W] * qscale, w_in[:, OFF_SB + SB_W:OFF_FOX],
        w_in[:, OFF_FOX:OFF_FOX + FOX_W] * (qscale * LOG2E), w_in[:, OFF_FOX + FOX_W:OFF_FGATE]],
        axis=1)
    w_kr = w_in[:, OFF_KR:OFF_GATE]
    w_kr_rot = jnp.concatenate([-w_kr[:, half:], w_kr[:, :half]], axis=1)
    z = lambda c: jnp.zeros((d, c), F32)
    pad_rope = LANES - MLA_QK_DIM
    w_small = jnp.concatenate([
        w_in[:, OFF_DQ:OFF_DKV], w_in[:, OFF_DKV:OFF_KR],
        z(MLA_NOPE_DIM), w_kr, z(pad_rope),
        z(MLA_NOPE_DIM), w_kr_rot, z(pad_rope),
        w_in[:, OFF_FGATE:OFF_DQ], z(LANES - FOX_HEADS)], axis=1)
    w_gate = w_in[:, OFF_GATE:]

    r = w_uq.shape[0]
    q_nope = _head_cols(w_uq, MLA_HEADS, MLA_QK_DIM, 0, MLA_NOPE_DIM)
    q_rope = _head_cols(w_uq, MLA_HEADS, MLA_QK_DIM, MLA_NOPE_DIM, MLA_QK_DIM)
    q_rope_rot = jnp.concatenate([-q_rope[:, :, half:], q_rope[:, :, :half]], axis=2)
    zq = lambda c: jnp.zeros((r, MLA_HEADS, c), F32)
    wqa = jnp.concatenate([q_nope, q_rope, zq(pad_rope)], axis=2).reshape(r, MLA_HEADS * LANES)
    wqb = jnp.concatenate([zq(MLA_NOPE_DIM), q_rope_rot, zq(pad_rope)], axis=2).reshape(
        r, MLA_HEADS * LANES)
    rk = w_ukv.shape[0]
    kvw = MLA_NOPE_DIM + MLA_V_DIM
    k_nope = _head_cols(w_ukv, MLA_HEADS, kvw, 0, MLA_NOPE_DIM)
    wk = jnp.concatenate([k_nope, jnp.zeros((rk, MLA_HEADS, LANES - MLA_NOPE_DIM), F32)],
                         axis=2).reshape(rk, MLA_HEADS * LANES)
    wv = _head_cols(w_ukv, MLA_HEADS, kvw, MLA_NOPE_DIM, kvw).reshape(rk, MLA_HEADS * MLA_V_DIM)
    bf = jnp.concatenate([b_forget, jnp.zeros((LANES - FOX_HEADS,), F32)]).reshape(1, LANES)
    return dict(w_qkv=f(w_qkv), w_small=f(w_small), w_gate=f(w_gate),
                b_gate=b_gate.reshape(1, -1), wqa=f(wqa), wqb=f(wqb), wk=f(wk), wv=f(wv),
                qn=mla_q_norm.reshape(1, -1), kvn=mla_kv_norm.reshape(1, -1), bf=bf)


def _rope_tables(seq):
    half = MLA_ROPE_DIM // 2
    inv_freq = jnp.power(ROPE_BASE, -jnp.arange(half, dtype=F32) / half)
    ang = jnp.arange(seq).astype(F32)[:, None] * inv_freq[None, :]
    cos = jnp.concatenate([jnp.cos(ang), jnp.cos(ang)], axis=1)
    sin = jnp.concatenate([jnp.sin(ang), jnp.sin(ang)], axis=1)
    pad = lambda t: jnp.concatenate([jnp.zeros((seq, MLA_NOPE_DIM), F32), t,
                                     jnp.zeros((seq, LANES - MLA_QK_DIM), F32)], axis=1)
    return pad(cos), pad(sin)


EXPERT_BM = 512


def kernel(x, ln1_g, ln1_b, ln2_g, ln2_b, w_in, b_gate, b_forget, mla_q_norm, w_uq, mla_kv_norm,
           w_ukv, w_proj_sb, w_proj_fox, w_proj_mla, w_out, w_router, router_bias,
           w_exp_gate, w_exp_up, w_exp_down, w_sh_gate, w_sh_up, w_sh_down):
    b, seq, d = x.shape
    n = b * seq
    depth = w_in.shape[0]
    cos_t, sin_t = _rope_tables(seq)
    xf = x.reshape(n, d)
    xb = xf.astype(BF16)
    f = lambda a: a.astype(BF16)
    bm = min(EXPERT_BM, n * TOP_K)
    n_tiles = n * TOP_K // bm + N_EXPERTS
    for l in range(depth):
        p = _prep_layer(w_in[l], b_gate[l], b_forget[l], mla_q_norm[l], w_uq[l], mla_kv_norm[l],
                        w_ukv[l])
        qkv = _matmul(xb, p["w_qkv"], BF16, tn=QKV_W // 2)
        q_mla, k_mla, v_mla, lf = _mla_prep(xb, p["w_small"], p["qn"], p["kvn"], p["wqa"],
                                            p["wqb"], p["wk"], p["wv"], cos_t, sin_t, p["bf"], seq)
        cum, cumt = _cumsum(lf.reshape(b, seq, LANES))
        cumt4 = cumt[:, :FOX_HEADS, :].reshape(b, FOX_HEADS, 1, seq)
        qkv3 = qkv.reshape(b, seq, QKV_W)
        o_sb = _sb_attention(qkv3).reshape(n, SB_W)
        o_fox = _fox_attention(qkv3, cum, cumt4).reshape(n, FOX_W)
        o_mla = _mla_attention(q_mla.reshape(b, seq, -1), k_mla.reshape(b, seq, -1),
                               v_mla.reshape(b, seq, -1)).reshape(n, -1)
        x1, x1b, x1p = _merge(o_sb, o_fox, o_mla, xb, xf, p["w_gate"], p["b_gate"],
                              f(w_proj_sb[l]), f(w_proj_fox[l]), f(w_proj_mla[l]), f(w_out[l]),
                              ln1_g[l].reshape(1, d), ln1_b[l].reshape(1, d))

        idx_t, tw_t, rank_t, cnt = _route(x1b, f(w_router[l].T), router_bias[l].reshape(-1, 1))
        counts = cnt[:, 0].astype(jnp.int32)
        starts, tile_exp, tile_valid, n_used = _expert_tiles(counts, n_tiles, bm)
        dest_t = _dest(idx_t, rank_t, starts)
        dest_win = dest_t.reshape(TOP_K, n // SC_WINDOW, SC_WINDOW).transpose(1, 0, 2)
        xs = _sc_dispatch(dest_win, x1p, n_tiles * bm)
        ys = _experts(l, n_used, tile_exp, tile_valid, xs, w_exp_gate, w_exp_up, w_exp_down, bm)
        yt = _sc_gather(dest_win, ys, n)
        xf, xb = _combine_rows(yt, tw_t.T, x1, x1b, f(w_sh_gate[l]), f(w_sh_up[l]),
                               f(w_sh_down[l]), ln2_g[l].reshape(1, d), ln2_b[l].reshape(1, d))
    return xf.reshape(b, seq, d)
```

```python
import functools

import jax
import jax.numpy as jnp
from jax import lax
from jax.experimental import pallas as pl
from jax.experimental.pallas import tpu as pltpu
from jax.experimental.pallas import tpu_sc as plsc

F32 = jnp.float32
BF16 = jnp.bfloat16

D_MODEL = 1024
HEAD_DIM = 64
SB_HEADS = 4
FOX_HEADS = 4
MLA_HEADS = 8
MLA_Q_RANK = 256
MLA_KV_RANK = 128
MLA_NOPE_DIM = 64
MLA_ROPE_DIM = 32
MLA_V_DIM = 64
ROPE_BASE = 10000.0
N_BRANCHES = 3
N_EXPERTS = 256
TOP_K = 8
N_GROUPS = 8
TOPK_GROUPS = 4
GROUP_SIZE = N_EXPERTS // N_GROUPS
EXPERT_FF = 256
SHARED_FF = 256
ROUTED_SCALE = 2.5
CHUNK = 64
LN_EPS = 1e-5
RMS_EPS = 1e-6
DEPTH = 2
DN_ALPHA = (2 * DEPTH) ** 0.25

SB_W = SB_HEADS * HEAD_DIM
FOX_W = FOX_HEADS * HEAD_DIM
MLA_QK_DIM = MLA_NOPE_DIM + MLA_ROPE_DIM
OFF_SB = 0
OFF_FOX = OFF_SB + 3 * SB_W
OFF_FGATE = OFF_FOX + 3 * FOX_W
OFF_DQ = OFF_FGATE + FOX_HEADS
OFF_DKV = OFF_DQ + MLA_Q_RANK
OFF_KR = OFF_DKV + MLA_KV_RANK
OFF_GATE = OFF_KR + MLA_ROPE_DIM

LANES = 128
SUBLANES = 8
QKV_W = 3 * SB_W + 3 * FOX_W
SMALL_W = MLA_Q_RANK + MLA_KV_RANK + 3 * LANES
SB_CUTOFF = -104.0
LOG2E = 1.4426950408889634

VMEM_LIMIT = 48 * 1024 * 1024


def _cp(sem, vmem=VMEM_LIMIT):
    return pltpu.CompilerParams(dimension_semantics=sem, vmem_limit_bytes=vmem)


def _mm_kernel(x_ref, w_ref, o_ref, *, sub):
    x = x_ref[...]
    for c in range(0, o_ref.shape[1], sub):
        acc = jnp.dot(x, w_ref[:, c:c + sub], preferred_element_type=F32)
        o_ref[:, c:c + sub] = acc.astype(o_ref.dtype)


def _matmul(x, w, out_dtype, tn, tm=1024, sub=256):
    n, k = x.shape
    c = w.shape[1]
    tm = min(tm, n)
    return pl.pallas_call(
        functools.partial(_mm_kernel, sub=sub),
        grid=(n // tm, c // tn),
        in_specs=[pl.BlockSpec((tm, k), lambda i, j: (i, 0)),
                  pl.BlockSpec((k, tn), lambda i, j: (0, j))],
        out_specs=pl.BlockSpec((tm, tn), lambda i, j: (i, j)),
        out_shape=jax.ShapeDtypeStruct((n, c), out_dtype),
        compiler_params=_cp(("arbitrary", "arbitrary")),
        name="proj_matmul",
    )(x, w)


def _split_bf16(x, parts):
    out = []
    for _ in range(parts):
        h = x.astype(BF16)
        out.append(h)
        x = x - h.astype(F32)
    return out


def _mla_prep_kernel(x_ref, ws_ref, qn_ref, kvn_ref, wqa_ref, wqb_ref, wk_ref, wv_ref,
                     cos_ref, sin_ref, bf_ref, q_ref, k_ref, v_ref, lf_ref):
    sm = jnp.dot(x_ref[...], ws_ref[...], preferred_element_type=F32)
    dq = sm[:, :MLA_Q_RANK]
    dkv = sm[:, MLA_Q_RANK:MLA_Q_RANK + MLA_KV_RANK]
    o = MLA_Q_RANK + MLA_KV_RANK
    kr = sm[:, o:o + LANES]
    kr_rot = sm[:, o + LANES:o + 2 * LANES]
    fg = sm[:, o + 2 * LANES:o + 3 * LANES]

    cq = dq * lax.rsqrt(jnp.mean(dq * dq, axis=-1, keepdims=True) + RMS_EPS) * qn_ref[...]
    ckv = dkv * lax.rsqrt(jnp.mean(dkv * dkv, axis=-1, keepdims=True) + RMS_EPS) * kvn_ref[...]
    cq = cq.astype(BF16)
    ckv = ckv.astype(BF16)

    cosk = cos_ref[...]
    sink = sin_ref[...]
    lane = lax.broadcasted_iota(jnp.int32, (1, LANES), 1)
    nope = (lane < MLA_NOPE_DIM).astype(F32)
    scale = MLA_QK_DIM ** -0.5 * LOG2E
    cq_tab = jnp.concatenate([(cosk + nope) * scale] * MLA_HEADS, axis=1)
    sq_tab = jnp.concatenate([sink * scale] * MLA_HEADS, axis=1)

    qa = jnp.dot(cq, wqa_ref[...], preferred_element_type=F32)
    qb = jnp.dot(cq, wqb_ref[...], preferred_element_type=F32)
    q_ref[...] = (qa * cq_tab + qb * sq_tab).astype(q_ref.dtype)

    k_rope = kr * cosk + kr_rot * sink
    ka = jnp.dot(ckv, wk_ref[...], preferred_element_type=F32)
    k_ref[...] = (ka + jnp.concatenate([k_rope] * MLA_HEADS, axis=1)).astype(k_ref.dtype)
    v_ref[...] = jnp.dot(ckv, wv_ref[...], preferred_element_type=F32).astype(v_ref.dtype)
    lf_ref[...] = jax.nn.log_sigmoid(fg + bf_ref[...])


def _mla_prep(xb, w_small, qn, kvn, wqa, wqb, wk, wv, cos_t, sin_t, bf, seq, tm=512):
    n, d = xb.shape
    tm = min(tm, seq)
    sblocks = seq // tm
    hw = MLA_HEADS * LANES
    full = lambda a: pl.BlockSpec(a.shape, lambda i: (0,) * a.ndim)
    return pl.pallas_call(
        _mla_prep_kernel,
        grid=(n // tm,),
        in_specs=[pl.BlockSpec((tm, d), lambda i: (i, 0)), full(w_small),
                  full(qn), full(kvn), full(wqa), full(wqb), full(wk), full(wv),
                  pl.BlockSpec((tm, LANES), lambda i: (i % sblocks, 0)),
                  pl.BlockSpec((tm, LANES), lambda i: (i % sblocks, 0)),
                  full(bf)],
        out_specs=[pl.BlockSpec((tm, hw), lambda i: (i, 0)),
                   pl.BlockSpec((tm, hw), lambda i: (i, 0)),
                   pl.BlockSpec((tm, MLA_HEADS * MLA_V_DIM), lambda i: (i, 0)),
                   pl.BlockSpec((tm, LANES), lambda i: (i, 0))],
        out_shape=[jax.ShapeDtypeStruct((n, hw), BF16),
                   jax.ShapeDtypeStruct((n, hw), BF16),
                   jax.ShapeDtypeStruct((n, MLA_HEADS * MLA_V_DIM), BF16),
                   jax.ShapeDtypeStruct((n, LANES), F32)],
        compiler_params=_cp(("arbitrary",)),
        name="mla_prep",
    )(xb, w_small, qn, kvn, wqa, wqb, wk, wv, cos_t, sin_t, bf)


def _cumsum_kernel(lf_ref, cum_ref, cumt_ref, *, seq):
    r = lax.broadcasted_iota(jnp.int32, (LANES, LANES), 0)
    c = lax.broadcasted_iota(jnp.int32, (LANES, LANES), 1)
    lower = (c <= r).astype(BF16)

    def body(j, carry):
        start = pl.multiple_of(j * LANES, LANES)
        blk = lf_ref[0, pl.ds(start, LANES), :]
        acc = carry
        for part in _split_bf16(blk, 3):
            acc = acc + jnp.dot(lower, part, preferred_element_type=F32)
        scaled = acc * LOG2E
        cum_ref[0, pl.ds(start, LANES), :] = scaled
        cumt_ref[0, :, pl.ds(start, LANES)] = scaled.T[:SUBLANES, :]
        return jnp.broadcast_to(acc[LANES - 1:LANES, :], (LANES, LANES))

    lax.fori_loop(0, seq // LANES, body, jnp.zeros((LANES, LANES), F32))


def _cumsum(lf3):
    b, seq, _ = lf3.shape
    return pl.pallas_call(
        functools.partial(_cumsum_kernel, seq=seq),
        grid=(b,),
        in_specs=[pl.BlockSpec((1, seq, LANES), lambda i: (i, 0, 0))],
        out_specs=[pl.BlockSpec((1, seq, LANES), lambda i: (i, 0, 0)),
                   pl.BlockSpec((1, SUBLANES, seq), lambda i: (i, 0, 0))],
        out_shape=[jax.ShapeDtypeStruct((b, seq, LANES), F32),
                   jax.ShapeDtypeStruct((b, SUBLANES, seq), F32)],
        compiler_params=_cp(("arbitrary",)),
        name="forget_cumsum",
    )(lf3)


_NT = (((1,), (1,)), ((), ()))
NEG_INIT = -1e30
_TN = (((0,), (0,)), ((), ()))
ATTN_TK = 256


def _head_halves(q):
    lane = lax.broadcasted_iota(jnp.int32, (1, LANES), 1)
    zero = jnp.zeros_like(q)
    return [jnp.where(lane < HEAD_DIM, q, zero), jnp.where(lane >= HEAD_DIM, q, zero)]


def _mask_first_block(x, keep, fill, tk):
    head = jnp.where(keep, x[:, :tk], fill)
    return head if x.shape[1] == tk else jnp.concatenate([head, x[:, tk:]], axis=1)


def _fox_kernel(q_ref, k_ref, v_ref, cc_ref, cr_ref, o_ref, *, seq, tk):
    hp = pl.program_id(1)
    lane = lax.broadcasted_iota(jnp.int32, (1, LANES), 1)
    qs = _head_halves(q_ref[0])
    key = lax.broadcasted_iota(jnp.int32, (tk, tk), 0)
    qry = lax.broadcasted_iota(jnp.int32, (tk, tk), 1)
    causal = key <= qry
    carry = [(jnp.full((1, seq), NEG_INIT, F32), jnp.zeros((1, seq), F32),
              jnp.zeros((HEAD_DIM, seq), F32)) for _ in range(2)]
    for j in range(seq // tk):
        q0 = j * tk
        k = k_ref[0, q0:q0 + tk, :]
        cc = cc_ref[0, q0:q0 + tk, :]
        for hh in range(2):
            m, l, acc = carry[hh]
            v = v_ref[0, q0:q0 + tk, hh * HEAD_DIM:(hh + 1) * HEAD_DIM]
            cs = jnp.sum(jnp.where(lane == 2 * hp + hh, cc, 0.0), axis=1, keepdims=True)
            ct = cr_ref[0, hh, :, q0:]
            st = lax.dot_general(k, qs[hh][q0:, :], _NT, preferred_element_type=F32)
            st = _mask_first_block((st + ct) - cs, causal, -jnp.inf, tk)
            mn, ln, an = _softmax_step_t(st, m[:, q0:], l[:, q0:], acc[:, q0:], v)
            if q0:
                mn = jnp.concatenate([m[:, :q0], mn], axis=1)
                ln = jnp.concatenate([l[:, :q0], ln], axis=1)
                an = jnp.concatenate([acc[:, :q0], an], axis=1)
            carry[hh] = (mn, ln, an)
    _pair_out_t([carry[0][2], carry[1][2]], [carry[0][1], carry[1][1]], o_ref)


def _fox_attention(qkv3, cum, cumt4):
    b, seq, _ = qkv3.shape
    tk = min(ATTN_TK, seq)
    hp = FOX_HEADS // 2
    qoff = OFF_FOX // LANES
    return pl.pallas_call(
        functools.partial(_fox_kernel, seq=seq, tk=tk),
        grid=(b, hp),
        in_specs=[pl.BlockSpec((1, seq, LANES), lambda bi, h: (bi, 0, qoff + h)),
                  pl.BlockSpec((1, seq, LANES), lambda bi, h: (bi, 0, qoff + hp + h)),
                  pl.BlockSpec((1, seq, LANES), lambda bi, h: (bi, 0, qoff + 2 * hp + h)),
                  pl.BlockSpec((1, seq, LANES), lambda bi, h: (bi, 0, 0)),
                  pl.BlockSpec((1, 2, 1, seq), lambda bi, h: (bi, h, 0, 0))],
        out_specs=pl.BlockSpec((1, seq, LANES), lambda bi, h: (bi, 0, h)),
        out_shape=jax.ShapeDtypeStruct((b, seq, FOX_W), BF16),
        compiler_params=_cp(("arbitrary", "arbitrary")),
        name="fox_attention",
    )(qkv3, qkv3, qkv3, cum, cumt4)


def _softmax_step_t(st, m, l, acc, v):
    m_new = jnp.maximum(m, jnp.max(st, axis=0, keepdims=True))
    alpha = jnp.exp2(m - m_new)
    p = jnp.exp2(st - m_new)
    l = alpha * l + jnp.sum(p, axis=0, keepdims=True)
    acc = alpha * acc + lax.dot_general(v, p.astype(BF16), _TN, preferred_element_type=F32)
    return m_new, l, acc


def _pair_out_t(accs, ls, o_ref, pair=0):
    ot = jnp.concatenate([accs[0] / ls[0], accs[1] / ls[1]], axis=0)
    o_ref[0, :, pair * LANES:(pair + 1) * LANES] = ot.T.astype(o_ref.dtype)


MLA_STEP_HEADS = 2


def _mla_kernel(q_ref, k_ref, v_ref, o_ref, *, seq, tk):
    nh = MLA_STEP_HEADS
    key = lax.broadcasted_iota(jnp.int32, (tk, tk), 0)
    qry = lax.broadcasted_iota(jnp.int32, (tk, tk), 1)
    shift = CHUNK.bit_length() - 1
    chunk_causal = (key >> shift) <= (qry >> shift)
    carry = [(jnp.full((1, seq), NEG_INIT, F32), jnp.zeros((1, seq), F32),
              jnp.zeros((MLA_V_DIM, seq), F32)) for _ in range(nh)]
    for j in range(seq // tk):
        q0 = j * tk
        for hh in range(nh):
            m, l, acc = carry[hh]
            k = k_ref[0, q0:q0 + tk, hh * LANES:(hh + 1) * LANES]
            v = v_ref[0, q0:q0 + tk, hh * MLA_V_DIM:(hh + 1) * MLA_V_DIM]
            st = lax.dot_general(k, q_ref[0, q0:, hh * LANES:(hh + 1) * LANES], _NT,
                                 preferred_element_type=F32)
            st = _mask_first_block(st, chunk_causal, -jnp.inf, tk)
            mn, ln, an = _softmax_step_t(st, m[:, q0:], l[:, q0:], acc[:, q0:], v)
            if q0:
                mn = jnp.concatenate([m[:, :q0], mn], axis=1)
                ln = jnp.concatenate([l[:, :q0], ln], axis=1)
                an = jnp.concatenate([acc[:, :q0], an], axis=1)
            carry[hh] = (mn, ln, an)
    for pair in range(nh // 2):
        a, b = carry[2 * pair], carry[2 * pair + 1]
        _pair_out_t([a[2], b[2]], [a[1], b[1]], o_ref, pair)


def _mla_attention(q3, k3, v3):
    b, seq, _ = q3.shape
    tk = min(ATTN_TK, seq)
    nh = MLA_STEP_HEADS
    return pl.pallas_call(
        functools.partial(_mla_kernel, seq=seq, tk=tk),
        grid=(b, MLA_HEADS // nh),
        in_specs=[pl.BlockSpec((1, seq, nh * LANES), lambda bi, h: (bi, 0, h)),
                  pl.BlockSpec((1, seq, nh * LANES), lambda bi, h: (bi, 0, h)),
                  pl.BlockSpec((1, seq, nh // 2 * LANES), lambda bi, h: (bi, 0, h))],
        out_specs=pl.BlockSpec((1, seq, nh // 2 * LANES), lambda bi, h: (bi, 0, h)),
        out_shape=jax.ShapeDtypeStruct((b, seq, MLA_HEADS * MLA_V_DIM), BF16),
        compiler_params=_cp(("arbitrary", "arbitrary")),
        name="mla_attention",
    )(q3, k3, v3)


def _softplus(z):
    return jnp.maximum(z, 0.0) + jnp.log(1.0 + jnp.exp(-jnp.abs(z)))


def _sb_kernel(q_ref, k_ref, v_ref, o_ref, run_ref, acc_ref, *, seq, tk):
    n = seq // tk
    qs = _head_halves(q_ref[0])
    key = lax.broadcasted_iota(jnp.int32, (tk, tk), 0)
    qry = lax.broadcasted_iota(jnp.int32, (tk, tk), 1)
    strict = key < qry
    later = (qry > key).astype(BF16)

    def unit(j, qlo, qhi, diag):
        k = k_ref[0, j * tk:(j + 1) * tk, :]
        for hh in range(2):
            v = v_ref[0, j * tk:(j + 1) * tk, hh * HEAD_DIM:(hh + 1) * HEAD_DIM]
            z = lax.dot_general(k, qs[hh][qlo:qhi, :], _NT, preferred_element_type=F32)
            sp = _softplus(z)
            log_keep = -sp
            if diag:
                log_keep = _mask_first_block(log_keep, strict, 0.0, tk)
            between = run_ref[hh, :, qlo:qhi]
            for part in _split_bf16(log_keep, 2):
                between = between + jnp.dot(later, part, preferred_element_type=F32)
            w = jnp.exp((z - sp) + between)
            if diag:
                w = _mask_first_block(w, strict, 0.0, tk)
            acc_ref[hh, :, qlo:qhi] += lax.dot_general(v, w.astype(BF16), _TN,
                                                       preferred_element_type=F32)
            run_ref[hh, :, qlo:qhi] += jnp.sum(log_keep, axis=0, keepdims=True)

    run_ref[...] = jnp.zeros_like(run_ref)
    acc_ref[...] = jnp.zeros_like(acc_ref)
    for j in range(n - 1, -1, -1):
        unit(j, j * tk, min((j + 2) * tk, seq), True)

    for j in range(n - 3, -1, -1):
        qlo = (j + 2) * tk

        @pl.when(jnp.max(run_ref[:, :, qlo:]) >= SB_CUTOFF)
        def _():
            unit(j, qlo, seq, False)

    o_ref[0] = jnp.concatenate([acc_ref[0], acc_ref[1]], axis=0).T.astype(o_ref.dtype)


def _sb_attention(qkv3):
    b, seq, _ = qkv3.shape
    tk = min(ATTN_TK, seq)
    hp = SB_HEADS // 2
    qoff = OFF_SB // LANES
    return pl.pallas_call(
        functools.partial(_sb_kernel, seq=seq, tk=tk),
        grid=(b, hp),
        in_specs=[pl.BlockSpec((1, seq, LANES), lambda bi, h: (bi, 0, qoff + h)),
                  pl.BlockSpec((1, seq, LANES), lambda bi, h: (bi, 0, qoff + hp + h)),
                  pl.BlockSpec((1, seq, LANES), lambda bi, h: (bi, 0, qoff + 2 * hp + h))],
        out_specs=pl.BlockSpec((1, seq, LANES), lambda bi, h: (bi, 0, h)),
        out_shape=jax.ShapeDtypeStruct((b, seq, SB_W), BF16),
        scratch_shapes=[pltpu.VMEM((2, 1, seq), F32), pltpu.VMEM((2, HEAD_DIM, seq), F32)],
        compiler_params=_cp(("arbitrary", "arbitrary")),
        name="sb_attention",
    )(qkv3, qkv3, qkv3)


def _layer_norm(y, g, b):
    yc = y - jnp.mean(y, axis=-1, keepdims=True)
    var = jnp.mean(yc * yc, axis=-1, keepdims=True)
    return yc * lax.rsqrt(var + LN_EPS) * g + b


HALF = D_MODEL // 2
U32 = jnp.uint32


def _pack_halves(y):
    bits = lax.bitcast_convert_type(y.astype(BF16).astype(F32), U32)
    return (bits[:, :HALF] >> 16) | (bits[:, HALF:] & U32(0xFFFF0000))


def _unpack_halves(w):
    lo = lax.bitcast_convert_type(w << 16, F32)
    hi = lax.bitcast_convert_type(w & U32(0xFFFF0000), F32)
    return lo, hi


def _merge_kernel(osb_ref, ofox_ref, omla_ref, xb_ref, x_ref, wg_ref, bg_ref, wsb_ref, wfox_ref,
                  wmla_ref, wout_ref, lg_ref, lb_ref, x1_ref, x1b_ref, x1p_ref):
    xb = xb_ref[...]
    mixed = None
    branches = ((osb_ref, wsb_ref), (ofox_ref, wfox_ref), (omla_ref, wmla_ref))
    for i, (o_ref, w_ref) in enumerate(branches):
        cols = slice(i * D_MODEL, (i + 1) * D_MODEL)
        gate = jax.nn.sigmoid(jnp.dot(xb, wg_ref[:, cols], preferred_element_type=F32)
                              + bg_ref[:, cols])
        term = gate * jnp.dot(o_ref[...], w_ref[...], preferred_element_type=F32)
        mixed = term if mixed is None else mixed + term
    mix = jnp.dot(mixed.astype(BF16), wout_ref[...], preferred_element_type=F32)
    y = _layer_norm(DN_ALPHA * x_ref[...] + mix, lg_ref[...], lb_ref[...])
    x1_ref[...] = y
    x1b_ref[...] = y.astype(BF16)
    x1p_ref[...] = _pack_halves(y)


def _merge(o_sb, o_fox, o_mla, xb, x, wg, bg, wsb, wfox, wmla, wout, lg, lb, tm=512):
    n = x.shape[0]
    tm = min(tm, n)
    full = lambda a: pl.BlockSpec(a.shape, lambda i: (0,) * a.ndim)
    rows = lambda w: pl.BlockSpec((tm, w), lambda i: (i, 0))
    return pl.pallas_call(
        _merge_kernel,
        grid=(n // tm,),
        in_specs=[rows(SB_W), rows(FOX_W), rows(MLA_HEADS * MLA_V_DIM), rows(D_MODEL),
                  rows(D_MODEL), full(wg), full(bg), full(wsb), full(wfox), full(wmla),
                  full(wout), full(lg), full(lb)],
        out_specs=[rows(D_MODEL), rows(D_MODEL), rows(HALF)],
        out_shape=[jax.ShapeDtypeStruct((n, D_MODEL), F32),
                   jax.ShapeDtypeStruct((n, D_MODEL), BF16),
                   jax.ShapeDtypeStruct((n, HALF), U32)],
        compiler_params=_cp(("arbitrary",)),
        name="merge_outproj_ln",
    )(o_sb, o_fox, o_mla, xb, x, wg, bg, wsb, wfox, wmla, wout, lg, lb)


def _route_kernel(x_ref, wr_ref, rb_ref, idx_ref, w_ref, rank_ref, cnt_ref, run_ref, *, tm):
    step = pl.program_id(0)

    @pl.when(step == 0)
    def _():
        run_ref[...] = jnp.zeros_like(run_ref)

    logits = lax.dot_general(wr_ref[...], x_ref[...], _NT, preferred_element_type=F32)
    scores = jax.nn.sigmoid(logits)
    biased = scores + rb_ref[...]
    e_iota = lax.broadcasted_iota(jnp.int32, (N_EXPERTS, tm), 0)
    big = jnp.int32(1 << 20)

    g_iota = lax.broadcasted_iota(jnp.int32, (GROUP_SIZE, tm), 0)
    gs_rows = []
    for g in range(N_GROUPS):
        blk = biased[g * GROUP_SIZE:(g + 1) * GROUP_SIZE, :]
        m1 = jnp.max(blk, axis=0, keepdims=True)
        first = jnp.min(jnp.where(blk == m1, g_iota, big), axis=0, keepdims=True)
        m2 = jnp.max(jnp.where(g_iota == first, -jnp.inf, blk), axis=0, keepdims=True)
        gs_rows.append(m1 + m2)
    gs = jnp.concatenate(gs_rows, axis=0)
    n_iota = lax.broadcasted_iota(jnp.int32, (N_GROUPS, tm), 0)
    keep = jnp.zeros((N_GROUPS, tm), jnp.bool_)
    for _ in range(TOPK_GROUPS):
        m = jnp.max(gs, axis=0, keepdims=True)
        first = jnp.min(jnp.where(gs == m, n_iota, big), axis=0, keepdims=True)
        hit = n_iota == first
        keep = jnp.logical_or(keep, hit)
        gs = jnp.where(hit, -jnp.inf, gs)
    keep_f = keep.astype(F32)
    expert_keep = jnp.concatenate(
        [jnp.broadcast_to(keep_f[g:g + 1, :], (GROUP_SIZE, tm)) for g in range(N_GROUPS)], axis=0)
    masked = jnp.where(expert_keep > 0.5, biased, -jnp.inf)

    idx_rows, w_rows, hits = [], [], []
    sel = jnp.zeros((N_EXPERTS, tm), F32)
    for _ in range(TOP_K):
        m = jnp.max(masked, axis=0, keepdims=True)
        first = jnp.min(jnp.where(masked == m, e_iota, big), axis=0, keepdims=True)
        hit = e_iota == first
        idx_rows.append(first)
        w_rows.append(jnp.sum(jnp.where(hit, scores, 0.0), axis=0, keepdims=True))
        hits.append(hit)
        sel = sel + hit.astype(F32)
        masked = jnp.where(hit, -jnp.inf, masked)
    w = jnp.concatenate(w_rows, axis=0)
    w = w / jnp.sum(w, axis=0, keepdims=True) * ROUTED_SCALE
    idx_ref[...] = jnp.concatenate(idx_rows, axis=0)
    w_ref[...] = w

    r = lax.broadcasted_iota(jnp.int32, (tm, tm), 0)
    c = lax.broadcasted_iota(jnp.int32, (tm, tm), 1)
    earlier = (r < c).astype(BF16)
    prefix = jnp.dot(sel.astype(BF16), earlier, preferred_element_type=F32) + run_ref[...]
    rank_rows = [jnp.sum(jnp.where(h, prefix, 0.0), axis=0, keepdims=True) for h in hits]
    rank_ref[...] = jnp.concatenate(rank_rows, axis=0).astype(jnp.int32)
    total = run_ref[...] + jnp.sum(sel, axis=1, keepdims=True)
    run_ref[...] = total
    cnt_ref[...] = jnp.broadcast_to(total, (N_EXPERTS, LANES))


def _route(x1b, wr_t, rb, tm=256):
    n = x1b.shape[0]
    tm = min(tm, n)
    return pl.pallas_call(
        functools.partial(_route_kernel, tm=tm),
        grid=(n // tm,),
        in_specs=[pl.BlockSpec((tm, D_MODEL), lambda i: (i, 0)),
                  pl.BlockSpec((N_EXPERTS, D_MODEL), lambda i: (0, 0)),
                  pl.BlockSpec((N_EXPERTS, 1), lambda i: (0, 0))],
        out_specs=[pl.BlockSpec((TOP_K, tm), lambda i: (0, i)),
                   pl.BlockSpec((TOP_K, tm), lambda i: (0, i)),
                   pl.BlockSpec((TOP_K, tm), lambda i: (0, i)),
                   pl.BlockSpec((N_EXPERTS, LANES), lambda i: (0, 0))],
        out_shape=[jax.ShapeDtypeStruct((TOP_K, n), jnp.int32),
                   jax.ShapeDtypeStruct((TOP_K, n), F32),
                   jax.ShapeDtypeStruct((TOP_K, n), jnp.int32),
                   jax.ShapeDtypeStruct((N_EXPERTS, LANES), F32)],
        scratch_shapes=[pltpu.VMEM((N_EXPERTS, 1), F32)],
        compiler_params=_cp(("arbitrary",)),
        name="router_topk",
    )(x1b, wr_t, rb)


def _dest_kernel(idx_ref, rank_ref, st_ref, dest_ref, *, tm):
    e_iota = lax.broadcasted_iota(jnp.int32, (N_EXPERTS, tm), 0)
    starts = st_ref[...]
    rows = []
    for k in range(TOP_K):
        hit = e_iota == idx_ref[k:k + 1, :]
        rows.append(jnp.sum(jnp.where(hit, starts, 0.0), axis=0, keepdims=True))
    dest_ref[...] = jnp.concatenate(rows, axis=0).astype(jnp.int32) + rank_ref[...]


def _dest(idx_t, rank_t, starts, tm=512):
    n = idx_t.shape[1]
    tm = min(tm, n)
    blk = lambda: pl.BlockSpec((TOP_K, tm), lambda i: (0, i))
    return pl.pallas_call(
        functools.partial(_dest_kernel, tm=tm),
        grid=(n // tm,),
        in_specs=[blk(), blk(), pl.BlockSpec((N_EXPERTS, 1), lambda i: (0, 0))],
        out_specs=blk(),
        out_shape=jax.ShapeDtypeStruct((TOP_K, n), jnp.int32),
        compiler_params=_cp(("arbitrary",)),
        name="moe_dest",
    )(idx_t, rank_t, starts.astype(F32).reshape(N_EXPERTS, 1))


SC_CORES = 2
SC_SUBCORES = 16
SC_WINDOW = 128

def _sc_mesh():
    return plsc.VectorSubcoreMesh(core_axis_name="c", subcore_axis_name="s",
                                  num_cores=SC_CORES, num_subcores=SC_SUBCORES)


def _sc_windows(n):
    workers = SC_CORES * SC_SUBCORES
    assert n % (SC_WINDOW * workers) == 0, n
    return n // SC_WINDOW // workers


def _sc_dispatch(dest_win, x1p, n_rows):
    n = x1p.shape[0]
    per_worker = _sc_windows(n)

    @functools.partial(
        pl.kernel, mesh=_sc_mesh(),
        out_type=jax.ShapeDtypeStruct((n_rows, HALF), U32),
        scratch_types=[pltpu.VMEM((TOP_K, SC_WINDOW), jnp.int32),
                       pltpu.VMEM((SC_WINDOW, HALF), U32)],
        name="moe_dispatch_sc")
    def run(dest_hbm, x_hbm, xs_hbm, idx_v, rows_v):
        worker = lax.axis_index("s") * SC_CORES + lax.axis_index("c")

        @pl.loop(0, per_worker)
        def _(c):
            win = worker * per_worker + c
            pltpu.sync_copy(dest_hbm.at[win], idx_v)
            pltpu.sync_copy(x_hbm.at[pl.ds(pl.multiple_of(win * SC_WINDOW, SC_WINDOW),
                                           SC_WINDOW)], rows_v)
            for k in range(TOP_K):
                pltpu.sync_copy(rows_v, xs_hbm.at[idx_v.at[k]])

    return run(dest_win, x1p)


def _sc_gather(dest_win, ys, n):
    per_worker = _sc_windows(n)

    @functools.partial(
        pl.kernel, mesh=_sc_mesh(),
        out_type=jax.ShapeDtypeStruct((TOP_K, n, HALF), U32),
        scratch_types=[pltpu.VMEM((TOP_K, SC_WINDOW), jnp.int32),
                       pltpu.VMEM((SC_WINDOW, HALF), U32)],
        name="moe_gather_sc")
    def run(dest_hbm, ys_hbm, yt_hbm, idx_v, rows_v):
        worker = lax.axis_index("s") * SC_CORES + lax.axis_index("c")

        @pl.loop(0, per_worker)
        def _(c):
            win = worker * per_worker + c
            start = pl.multiple_of(win * SC_WINDOW, SC_WINDOW)
            pltpu.sync_copy(dest_hbm.at[win], idx_v)
            for k in range(TOP_K):
                pltpu.sync_copy(ys_hbm.at[idx_v.at[k]], rows_v)
                pltpu.sync_copy(rows_v, yt_hbm.at[k, pl.ds(start, SC_WINDOW)])

    return run(dest_win, ys)


def _expert_kernel(used_ref, exp_ref, valid_ref, first_ref, slot_ref, next_ref, xs_ref, wg_hbm,
                   wu_hbm, wd_hbm, ys_ref, wg_buf, wu_buf, wd_buf, sem, *, bm, layer):
    w = pl.program_id(0)

    def weight_copies(expert, s):
        return [pltpu.make_async_copy(hbm.at[layer, expert], buf.at[s], sem.at[s, i])
                for i, (hbm, buf) in enumerate(((wg_hbm, wg_buf), (wu_hbm, wu_buf),
                                                (wd_hbm, wd_buf)))]

    @pl.when(w < used_ref[0])
    def _():
        s = slot_ref[w]

        @pl.when(w == 0)
        def _():
            for c in weight_copies(exp_ref[0], 0):
                c.start()

        @pl.when(first_ref[w] == 1)
        def _():
            for c in weight_copies(exp_ref[w], s):
                c.wait()

            @pl.when(next_ref[w] >= 0)
            def _():
                for c in weight_copies(next_ref[w], 1 - s):
                    c.start()

        rows = lax.broadcasted_iota(jnp.int32, (bm, 1), 0)
        x_lo, x_hi = _unpack_halves(jnp.where(rows < valid_ref[w], xs_ref[...], U32(0)))
        x = jnp.concatenate([x_lo, x_hi], axis=1).astype(BF16)
        g = jnp.dot(x, wg_buf[s].astype(BF16), preferred_element_type=F32)
        u = jnp.dot(x, wu_buf[s].astype(BF16), preferred_element_type=F32)
        h = (g * jax.nn.sigmoid(g) * u).astype(BF16)
        ys_ref[...] = _pack_halves(jnp.dot(h, wd_buf[s].astype(BF16),
                                           preferred_element_type=F32))


def _experts(layer, tiles, xs, wg, wu, wd, bm):
    r = xs.shape[0]
    last = lambda w, used, *_: (jnp.minimum(w, used[0] - 1), 0)
    grid_spec = pltpu.PrefetchScalarGridSpec(
        num_scalar_prefetch=len(tiles),
        grid=(r // bm,),
        in_specs=[pl.BlockSpec((bm, HALF), last)] + [pl.BlockSpec(memory_space=pl.ANY)] * 3,
        out_specs=pl.BlockSpec((bm, HALF), last),
        scratch_shapes=[pltpu.VMEM((2, D_MODEL, EXPERT_FF), F32),
                        pltpu.VMEM((2, D_MODEL, EXPERT_FF), F32),
                        pltpu.VMEM((2, EXPERT_FF, D_MODEL), F32),
                        pltpu.SemaphoreType.DMA((2, 3))],
    )
    return pl.pallas_call(
        functools.partial(_expert_kernel, bm=bm, layer=layer),
        grid_spec=grid_spec,
        out_shape=jax.ShapeDtypeStruct((r, HALF), U32),
        compiler_params=_cp(("arbitrary",)),
        name="moe_experts",
    )(*tiles, xs, wg, wu, wd)


def _expert_tiles(counts, n_tiles, bm):
    padded = (counts + bm - 1) // bm * bm
    pad_end = jnp.cumsum(padded)
    pad_start = pad_end - padded
    tile = jnp.arange(n_tiles, dtype=jnp.int32)
    row0 = tile * bm
    exp = jnp.minimum(jnp.sum((pad_end[None, :] <= row0[:, None]).astype(jnp.int32), axis=1),
                      N_EXPERTS - 1)
    valid = jnp.clip(pad_start[exp] + counts[exp] - row0, 0, bm)
    n_used = (pad_end[-1:] // bm).astype(jnp.int32)
    first = jnp.concatenate([jnp.ones((1,), jnp.int32),
                             (exp[1:] != exp[:-1]).astype(jnp.int32)])
    slot = (jnp.cumsum(first) - 1) % 2
    next_tile = pad_end[exp] // bm
    nxt = jnp.where(next_tile < n_used[0], exp[jnp.minimum(next_tile, n_tiles - 1)], -1)
    i32 = lambda a: a.astype(jnp.int32)
    return i32(pad_start), (n_used, i32(exp), i32(valid), first, i32(slot), i32(nxt))


def _combine_rows_kernel(yt_ref, tw_ref, x1_ref, x1b_ref, wsg_ref, wsu_ref, wsd_ref, lg_ref, lb_ref,
                         o_ref, ob_ref):
    tw = tw_ref[...]
    r_lo = jnp.zeros((tw.shape[0], HALF), F32)
    r_hi = jnp.zeros((tw.shape[0], HALF), F32)
    for k in range(TOP_K):
        y_lo, y_hi = _unpack_halves(yt_ref[k])
        r_lo = r_lo + tw[:, k:k + 1] * y_lo
        r_hi = r_hi + tw[:, k:k + 1] * y_hi
    routed = jnp.concatenate([r_lo, r_hi], axis=1)
    xb = x1b_ref[...]
    g = jnp.dot(xb, wsg_ref[...], preferred_element_type=F32)
    u = jnp.dot(xb, wsu_ref[...], preferred_element_type=F32)
    h = (g * jax.nn.sigmoid(g) * u).astype(BF16)
    shared = jnp.dot(h, wsd_ref[...], preferred_element_type=F32)
    y = _layer_norm(DN_ALPHA * x1_ref[...] + (shared + routed), lg_ref[...], lb_ref[...])
    o_ref[...] = y
    ob_ref[...] = y.astype(BF16)


def _combine_rows(yt, tw, x1, x1b, wsg, wsu, wsd, lg, lb, tc=256):
    n = x1.shape[0]
    tc = min(tc, n)
    full = lambda a: pl.BlockSpec(a.shape, lambda i: (0,) * a.ndim)
    rows = lambda w: pl.BlockSpec((tc, w), lambda i: (i, 0))
    return pl.pallas_call(
        _combine_rows_kernel,
        grid=(n // tc,),
        in_specs=[pl.BlockSpec((TOP_K, tc, HALF), lambda i: (0, i, 0)),
                  rows(TOP_K), rows(D_MODEL), rows(D_MODEL),
                  full(wsg), full(wsu), full(wsd), full(lg), full(lb)],
        out_specs=[rows(D_MODEL), rows(D_MODEL)],
        out_shape=[jax.ShapeDtypeStruct((n, D_MODEL), F32),
                   jax.ShapeDtypeStruct((n, D_MODEL), BF16)],
        compiler_params=_cp(("arbitrary",)),
        name="moe_combine_ln",
    )(yt, tw, x1, x1b, wsg, wsu, wsd, lg, lb)


def _head_cols(w, heads, width, lo, hi):
    return w.reshape(w.shape[0], heads, width)[:, :, lo:hi]


def _prep_layer(w_in, b_gate, b_forget, mla_q_norm, w_uq, mla_kv_norm, w_ukv):
    f = lambda a: a.astype(BF16)
    half = MLA_ROPE_DIM // 2
    d = w_in.shape[0]
    qscale = HEAD_DIM ** -0.5
    w_qkv = jnp.concatenate([
        w_in[:, OFF_SB:OFF_SB + SB_W] * qscale, w_in[:, OFF_SB + SB_W:OFF_FOX],
        w_in[:, OFF_FOX:OFF_FOX + FOX_W] * (qscale * LOG2E), w_in[:, OFF_FOX + FOX_W:OFF_FGATE]],
        axis=1)
    w_kr = w_in[:, OFF_KR:OFF_GATE]
    w_kr_rot = jnp.concatenate([-w_kr[:, half:], w_kr[:, :half]], axis=1)
    z = lambda c: jnp.zeros((d, c), F32)
    pad_rope = LANES - MLA_QK_DIM
    w_small = jnp.concatenate([
        w_in[:, OFF_DQ:OFF_DKV], w_in[:, OFF_DKV:OFF_KR],
        z(MLA_NOPE_DIM), w_kr, z(pad_rope),
        z(MLA_NOPE_DIM), w_kr_rot, z(pad_rope),
        w_in[:, OFF_FGATE:OFF_DQ], z(LANES - FOX_HEADS)], axis=1)
    w_gate = w_in[:, OFF_GATE:]

    r = w_uq.shape[0]
    q_nope = _head_cols(w_uq, MLA_HEADS, MLA_QK_DIM, 0, MLA_NOPE_DIM)
    q_rope = _head_cols(w_uq, MLA_HEADS, MLA_QK_DIM, MLA_NOPE_DIM, MLA_QK_DIM)
    q_rope_rot = jnp.concatenate([-q_rope[:, :, half:], q_rope[:, :, :half]], axis=2)
    zq = lambda c: jnp.zeros((r, MLA_HEADS, c), F32)
    wqa = jnp.concatenate([q_nope, q_rope, zq(pad_rope)], axis=2).reshape(r, MLA_HEADS * LANES)
    wqb = jnp.concatenate([zq(MLA_NOPE_DIM), q_rope_rot, zq(pad_rope)], axis=2).reshape(
        r, MLA_HEADS * LANES)
    rk = w_ukv.shape[0]
    kvw = MLA_NOPE_DIM + MLA_V_DIM
    k_nope = _head_cols(w_ukv, MLA_HEADS, kvw, 0, MLA_NOPE_DIM)
    wk = jnp.concatenate([k_nope, jnp.zeros((rk, MLA_HEADS, LANES - MLA_NOPE_DIM), F32)],
                         axis=2).reshape(rk, MLA_HEADS * LANES)
    wv = _head_cols(w_ukv, MLA_HEADS, kvw, MLA_NOPE_DIM, kvw).reshape(rk, MLA_HEADS * MLA_V_DIM)
    bf = jnp.concatenate([b_forget, jnp.zeros((LANES - FOX_HEADS,), F32)]).reshape(1, LANES)
    return dict(w_qkv=f(w_qkv), w_small=f(w_small), w_gate=f(w_gate),
                b_gate=b_gate.reshape(1, -1), wqa=f(wqa), wqb=f(wqb), wk=f(wk), wv=f(wv),
                qn=mla_q_norm.reshape(1, -1), kvn=mla_kv_norm.reshape(1, -1), bf=bf)


def _rope_tables(seq):
    half = MLA_ROPE_DIM // 2
    inv_freq = jnp.power(ROPE_BASE, -jnp.arange(half, dtype=F32) / half)
    ang = jnp.arange(seq).astype(F32)[:, None] * inv_freq[None, :]
    cos = jnp.concatenate([jnp.cos(ang), jnp.cos(ang)], axis=1)
    sin = jnp.concatenate([jnp.sin(ang), jnp.sin(ang)], axis=1)
    pad = lambda t: jnp.concatenate([jnp.zeros((seq, MLA_NOPE_DIM), F32), t,
                                     jnp.zeros((seq, LANES - MLA_QK_DIM), F32)], axis=1)
    return pad(cos), pad(sin)


EXPERT_BM = 512


def kernel(x, ln1_g, ln1_b, ln2_g, ln2_b, w_in, b_gate, b_forget, mla_q_norm, w_uq, mla_kv_norm,
           w_ukv, w_proj_sb, w_proj_fox, w_proj_mla, w_out, w_router, router_bias,
           w_exp_gate, w_exp_up, w_exp_down, w_sh_gate, w_sh_up, w_sh_down):
    b, seq, d = x.shape
    n = b * seq
    depth = w_in.shape[0]
    cos_t, sin_t = _rope_tables(seq)
    xf = x.reshape(n, d)
    xb = xf.astype(BF16)
    f = lambda a: a.astype(BF16)
    bm = min(EXPERT_BM, n * TOP_K)
    n_tiles = n * TOP_K // bm + N_EXPERTS
    for l in range(depth):
        p = _prep_layer(w_in[l], b_gate[l], b_forget[l], mla_q_norm[l], w_uq[l], mla_kv_norm[l],
                        w_ukv[l])
        qkv = _matmul(xb, p["w_qkv"], BF16, tn=QKV_W // 2)
        q_mla, k_mla, v_mla, lf = _mla_prep(xb, p["w_small"], p["qn"], p["kvn"], p["wqa"],
                                            p["wqb"], p["wk"], p["wv"], cos_t, sin_t, p["bf"], seq)
        cum, cumt = _cumsum(lf.reshape(b, seq, LANES))
        cumt4 = cumt[:, :FOX_HEADS, :].reshape(b, FOX_HEADS, 1, seq)
        qkv3 = qkv.reshape(b, seq, QKV_W)
        o_sb = _sb_attention(qkv3).reshape(n, SB_W)
        o_fox = _fox_attention(qkv3, cum, cumt4).reshape(n, FOX_W)
        o_mla = _mla_attention(q_mla.reshape(b, seq, -1), k_mla.reshape(b, seq, -1),
                               v_mla.reshape(b, seq, -1)).reshape(n, -1)
        x1, x1b, x1p = _merge(o_sb, o_fox, o_mla, xb, xf, p["w_gate"], p["b_gate"],
                              f(w_proj_sb[l]), f(w_proj_fox[l]), f(w_proj_mla[l]), f(w_out[l]),
                              ln1_g[l].reshape(1, d), ln1_b[l].reshape(1, d))

        idx_t, tw_t, rank_t, cnt = _route(x1b, f(w_router[l].T), router_bias[l].reshape(-1, 1))
        counts = cnt[:, 0].astype(jnp.int32)
        starts, tiles = _expert_tiles(counts, n_tiles, bm)
        dest_t = _dest(idx_t, rank_t, starts)
        dest_win = dest_t.reshape(TOP_K, n // SC_WINDOW, SC_WINDOW).transpose(1, 0, 2)
        xs = _sc_dispatch(dest_win, x1p, n_tiles * bm)
        ys = _experts(l, tiles, xs, w_exp_gate, w_exp_up, w_exp_down, bm)
        yt = _sc_gather(dest_win, ys, n)
        xf, xb = _combine_rows(yt, tw_t.T, x1, x1b, f(w_sh_gate[l]), f(w_sh_up[l]),
                               f(w_sh_down[l]), ln2_g[l].reshape(1, d), ln2_b[l].reshape(1, d))
    return xf.reshape(b, seq, d)
```

```python
import functools

import jax
import jax.numpy as jnp
from jax import lax
from jax.experimental import pallas as pl
from jax.experimental.pallas import tpu as pltpu
from jax.experimental.pallas import tpu_sc as plsc

F32 = jnp.float32
BF16 = jnp.bfloat16

D_MODEL = 1024
HEAD_DIM = 64
SB_HEADS = 4
FOX_HEADS = 4
MLA_HEADS = 8
MLA_Q_RANK = 256
MLA_KV_RANK = 128
MLA_NOPE_DIM = 64
MLA_ROPE_DIM = 32
MLA_V_DIM = 64
ROPE_BASE = 10000.0
N_BRANCHES = 3
N_EXPERTS = 256
TOP_K = 8
N_GROUPS = 8
TOPK_GROUPS = 4
GROUP_SIZE = N_EXPERTS // N_GROUPS
EXPERT_FF = 256
SHARED_FF = 256
ROUTED_SCALE = 2.5
CHUNK = 64
LN_EPS = 1e-5
RMS_EPS = 1e-6
DEPTH = 2
DN_ALPHA = (2 * DEPTH) ** 0.25

SB_W = SB_HEADS * HEAD_DIM
FOX_W = FOX_HEADS * HEAD_DIM
MLA_QK_DIM = MLA_NOPE_DIM + MLA_ROPE_DIM
OFF_SB = 0
OFF_FOX = OFF_SB + 3 * SB_W
OFF_FGATE = OFF_FOX + 3 * FOX_W
OFF_DQ = OFF_FGATE + FOX_HEADS
OFF_DKV = OFF_DQ + MLA_Q_RANK
OFF_KR = OFF_DKV + MLA_KV_RANK
OFF_GATE = OFF_KR + MLA_ROPE_DIM

LANES = 128
SUBLANES = 8
QKV_W = 3 * SB_W + 3 * FOX_W
SMALL_W = MLA_Q_RANK + MLA_KV_RANK + 3 * LANES
SB_CUTOFF = -104.0
LOG2E = 1.4426950408889634

VMEM_LIMIT = 48 * 1024 * 1024


def _cp(sem, vmem=VMEM_LIMIT):
    return pltpu.CompilerParams(dimension_semantics=sem, vmem_limit_bytes=vmem)


def _mm_kernel(x_ref, w_ref, o_ref, *, sub):
    x = x_ref[...]
    for c in range(0, o_ref.shape[1], sub):
        acc = jnp.dot(x, w_ref[:, c:c + sub], preferred_element_type=F32)
        o_ref[:, c:c + sub] = acc.astype(o_ref.dtype)


def _matmul(x, w, out_dtype, tn, tm=1024, sub=256):
    n, k = x.shape
    c = w.shape[1]
    tm = min(tm, n)
    return pl.pallas_call(
        functools.partial(_mm_kernel, sub=sub),
        grid=(n // tm, c // tn),
        in_specs=[pl.BlockSpec((tm, k), lambda i, j: (i, 0)),
                  pl.BlockSpec((k, tn), lambda i, j: (0, j))],
        out_specs=pl.BlockSpec((tm, tn), lambda i, j: (i, j)),
        out_shape=jax.ShapeDtypeStruct((n, c), out_dtype),
        compiler_params=_cp(("arbitrary", "arbitrary")),
        name="proj_matmul",
    )(x, w)


def _split_bf16(x, parts):
    out = []
    for _ in range(parts):
        h = x.astype(BF16)
        out.append(h)
        x = x - h.astype(F32)
    return out


def _mla_prep_kernel(x_ref, ws_ref, qn_ref, kvn_ref, wqa_ref, wqb_ref, wk_ref, wv_ref,
                     cos_ref, sin_ref, bf_ref, q_ref, k_ref, v_ref, lf_ref):
    sm = jnp.dot(x_ref[...], ws_ref[...], preferred_element_type=F32)
    dq = sm[:, :MLA_Q_RANK]
    dkv = sm[:, MLA_Q_RANK:MLA_Q_RANK + MLA_KV_RANK]
    o = MLA_Q_RANK + MLA_KV_RANK
    kr = sm[:, o:o + LANES]
    kr_rot = sm[:, o + LANES:o + 2 * LANES]
    fg = sm[:, o + 2 * LANES:o + 3 * LANES]

    cq = dq * lax.rsqrt(jnp.mean(dq * dq, axis=-1, keepdims=True) + RMS_EPS) * qn_ref[...]
    ckv = dkv * lax.rsqrt(jnp.mean(dkv * dkv, axis=-1, keepdims=True) + RMS_EPS) * kvn_ref[...]
    cq = cq.astype(BF16)
    ckv = ckv.astype(BF16)

    cosk = cos_ref[...]
    sink = sin_ref[...]
    lane = lax.broadcasted_iota(jnp.int32, (1, LANES), 1)
    nope = (lane < MLA_NOPE_DIM).astype(F32)
    scale = MLA_QK_DIM ** -0.5 * LOG2E
    cq_tab = jnp.concatenate([(cosk + nope) * scale] * MLA_HEADS, axis=1)
    sq_tab = jnp.concatenate([sink * scale] * MLA_HEADS, axis=1)

    qa = jnp.dot(cq, wqa_ref[...], preferred_element_type=F32)
    qb = jnp.dot(cq, wqb_ref[...], preferred_element_type=F32)
    q_ref[...] = (qa * cq_tab + qb * sq_tab).astype(q_ref.dtype)

    k_rope = kr * cosk + kr_rot * sink
    ka = jnp.dot(ckv, wk_ref[...], preferred_element_type=F32)
    k_ref[...] = (ka + jnp.concatenate([k_rope] * MLA_HEADS, axis=1)).astype(k_ref.dtype)
    v_ref[...] = jnp.dot(ckv, wv_ref[...], preferred_element_type=F32).astype(v_ref.dtype)
    lf_ref[...] = jax.nn.log_sigmoid(fg + bf_ref[...])


def _mla_prep(xb, w_small, qn, kvn, wqa, wqb, wk, wv, cos_t, sin_t, bf, seq, tm=512):
    n, d = xb.shape
    tm = min(tm, seq)
    sblocks = seq // tm
    hw = MLA_HEADS * LANES
    full = lambda a: pl.BlockSpec(a.shape, lambda i: (0,) * a.ndim)
    return pl.pallas_call(
        _mla_prep_kernel,
        grid=(n // tm,),
        in_specs=[pl.BlockSpec((tm, d), lambda i: (i, 0)), full(w_small),
                  full(qn), full(kvn), full(wqa), full(wqb), full(wk), full(wv),
                  pl.BlockSpec((tm, LANES), lambda i: (i % sblocks, 0)),
                  pl.BlockSpec((tm, LANES), lambda i: (i % sblocks, 0)),
                  full(bf)],
        out_specs=[pl.BlockSpec((tm, hw), lambda i: (i, 0)),
                   pl.BlockSpec((tm, hw), lambda i: (i, 0)),
                   pl.BlockSpec((tm, MLA_HEADS * MLA_V_DIM), lambda i: (i, 0)),
                   pl.BlockSpec((tm, LANES), lambda i: (i, 0))],
        out_shape=[jax.ShapeDtypeStruct((n, hw), BF16),
                   jax.ShapeDtypeStruct((n, hw), BF16),
                   jax.ShapeDtypeStruct((n, MLA_HEADS * MLA_V_DIM), BF16),
                   jax.ShapeDtypeStruct((n, LANES), F32)],
        compiler_params=_cp(("arbitrary",)),
        name="mla_prep",
    )(xb, w_small, qn, kvn, wqa, wqb, wk, wv, cos_t, sin_t, bf)


def _cumsum_kernel(lf_ref, cum_ref, cumt_ref, *, seq):
    r = lax.broadcasted_iota(jnp.int32, (LANES, LANES), 0)
    c = lax.broadcasted_iota(jnp.int32, (LANES, LANES), 1)
    lower = (c <= r).astype(BF16)

    def body(j, carry):
        start = pl.multiple_of(j * LANES, LANES)
        blk = lf_ref[0, pl.ds(start, LANES), :]
        acc = carry
        for part in _split_bf16(blk, 3):
            acc = acc + jnp.dot(lower, part, preferred_element_type=F32)
        scaled = acc * LOG2E
        cum_ref[0, pl.ds(start, LANES), :] = scaled
        cumt_ref[0, :, pl.ds(start, LANES)] = scaled.T[:SUBLANES, :]
        return jnp.broadcast_to(acc[LANES - 1:LANES, :], (LANES, LANES))

    lax.fori_loop(0, seq // LANES, body, jnp.zeros((LANES, LANES), F32))


def _cumsum(lf3):
    b, seq, _ = lf3.shape
    return pl.pallas_call(
        functools.partial(_cumsum_kernel, seq=seq),
        grid=(b,),
        in_specs=[pl.BlockSpec((1, seq, LANES), lambda i: (i, 0, 0))],
        out_specs=[pl.BlockSpec((1, seq, LANES), lambda i: (i, 0, 0)),
                   pl.BlockSpec((1, SUBLANES, seq), lambda i: (i, 0, 0))],
        out_shape=[jax.ShapeDtypeStruct((b, seq, LANES), F32),
                   jax.ShapeDtypeStruct((b, SUBLANES, seq), F32)],
        compiler_params=_cp(("arbitrary",)),
        name="forget_cumsum",
    )(lf3)


_NT = (((1,), (1,)), ((), ()))
NEG_INIT = -1e30
_TN = (((0,), (0,)), ((), ()))
ATTN_TK = 256


def _head_halves(q):
    lane = lax.broadcasted_iota(jnp.int32, (1, LANES), 1)
    zero = jnp.zeros_like(q)
    return [jnp.where(lane < HEAD_DIM, q, zero), jnp.where(lane >= HEAD_DIM, q, zero)]


def _mask_first_block(x, keep, fill, tk):
    head = jnp.where(keep, x[:, :tk], fill)
    return head if x.shape[1] == tk else jnp.concatenate([head, x[:, tk:]], axis=1)


def _fox_kernel(q_ref, k_ref, v_ref, cc_ref, cr_ref, o_ref, *, seq, tk):
    hp = pl.program_id(1)
    lane = lax.broadcasted_iota(jnp.int32, (1, LANES), 1)
    qs = _head_halves(q_ref[0])
    key = lax.broadcasted_iota(jnp.int32, (tk, tk), 0)
    qry = lax.broadcasted_iota(jnp.int32, (tk, tk), 1)
    causal = key <= qry
    carry = [(jnp.full((1, seq), NEG_INIT, F32), jnp.zeros((1, seq), F32),
              jnp.zeros((HEAD_DIM, seq), F32)) for _ in range(2)]
    for j in range(seq // tk):
        q0 = j * tk
        k = k_ref[0, q0:q0 + tk, :]
        cc = cc_ref[0, q0:q0 + tk, :]
        for hh in range(2):
            m, l, acc = carry[hh]
            v = v_ref[0, q0:q0 + tk, hh * HEAD_DIM:(hh + 1) * HEAD_DIM]
            cs = jnp.sum(jnp.where(lane == 2 * hp + hh, cc, 0.0), axis=1, keepdims=True)
            ct = cr_ref[0, hh, :, q0:]
            st = lax.dot_general(k, qs[hh][q0:, :], _NT, preferred_element_type=F32)
            st = _mask_first_block((st + ct) - cs, causal, -jnp.inf, tk)
            mn, ln, an = _softmax_step_t(st, m[:, q0:], l[:, q0:], acc[:, q0:], v)
            if q0:
                mn = jnp.concatenate([m[:, :q0], mn], axis=1)
                ln = jnp.concatenate([l[:, :q0], ln], axis=1)
                an = jnp.concatenate([acc[:, :q0], an], axis=1)
            carry[hh] = (mn, ln, an)
    _pair_out_t([carry[0][2], carry[1][2]], [carry[0][1], carry[1][1]], o_ref)


def _fox_attention(qkv3, cum, cumt4):
    b, seq, _ = qkv3.shape
    tk = min(ATTN_TK, seq)
    hp = FOX_HEADS // 2
    qoff = OFF_FOX // LANES
    return pl.pallas_call(
        functools.partial(_fox_kernel, seq=seq, tk=tk),
        grid=(b, hp),
        in_specs=[pl.BlockSpec((1, seq, LANES), lambda bi, h: (bi, 0, qoff + h)),
                  pl.BlockSpec((1, seq, LANES), lambda bi, h: (bi, 0, qoff + hp + h)),
                  pl.BlockSpec((1, seq, LANES), lambda bi, h: (bi, 0, qoff + 2 * hp + h)),
                  pl.BlockSpec((1, seq, LANES), lambda bi, h: (bi, 0, 0)),
                  pl.BlockSpec((1, 2, 1, seq), lambda bi, h: (bi, h, 0, 0))],
        out_specs=pl.BlockSpec((1, seq, LANES), lambda bi, h: (bi, 0, h)),
        out_shape=jax.ShapeDtypeStruct((b, seq, FOX_W), BF16),
        compiler_params=_cp(("arbitrary", "arbitrary")),
        name="fox_attention",
    )(qkv3, qkv3, qkv3, cum, cumt4)


def _softmax_step_t(st, m, l, acc, v):
    m_new = jnp.maximum(m, jnp.max(st, axis=0, keepdims=True))
    alpha = jnp.exp2(m - m_new)
    p = jnp.exp2(st - m_new)
    l = alpha * l + jnp.sum(p, axis=0, keepdims=True)
    acc = alpha * acc + lax.dot_general(v, p.astype(BF16), _TN, preferred_element_type=F32)
    return m_new, l, acc


def _pair_out_t(accs, ls, o_ref, pair=0):
    ot = jnp.concatenate([accs[0] / ls[0], accs[1] / ls[1]], axis=0)
    o_ref[0, :, pair * LANES:(pair + 1) * LANES] = ot.T.astype(o_ref.dtype)


MLA_STEP_HEADS = 2


def _mla_kernel(q_ref, k_ref, v_ref, o_ref, *, seq, tk):
    nh = MLA_STEP_HEADS
    key = lax.broadcasted_iota(jnp.int32, (tk, tk), 0)
    qry = lax.broadcasted_iota(jnp.int32, (tk, tk), 1)
    shift = CHUNK.bit_length() - 1
    chunk_causal = (key >> shift) <= (qry >> shift)
    carry = [(jnp.full((1, seq), NEG_INIT, F32), jnp.zeros((1, seq), F32),
              jnp.zeros((MLA_V_DIM, seq), F32)) for _ in range(nh)]
    for j in range(seq // tk):
        q0 = j * tk
        for hh in range(nh):
            m, l, acc = carry[hh]
            k = k_ref[0, q0:q0 + tk, hh * LANES:(hh + 1) * LANES]
            v = v_ref[0, q0:q0 + tk, hh * MLA_V_DIM:(hh + 1) * MLA_V_DIM]
            st = lax.dot_general(k, q_ref[0, q0:, hh * LANES:(hh + 1) * LANES], _NT,
                                 preferred_element_type=F32)
            st = _mask_first_block(st, chunk_causal, -jnp.inf, tk)
            mn, ln, an = _softmax_step_t(st, m[:, q0:], l[:, q0:], acc[:, q0:], v)
            if q0:
                mn = jnp.concatenate([m[:, :q0], mn], axis=1)
                ln = jnp.concatenate([l[:, :q0], ln], axis=1)
                an = jnp.concatenate([acc[:, :q0], an], axis=1)
            carry[hh] = (mn, ln, an)
    for pair in range(nh // 2):
        a, b = carry[2 * pair], carry[2 * pair + 1]
        _pair_out_t([a[2], b[2]], [a[1], b[1]], o_ref, pair)


def _mla_attention(q3, k3, v3):
    b, seq, _ = q3.shape
    tk = min(ATTN_TK, seq)
    nh = MLA_STEP_HEADS
    return pl.pallas_call(
        functools.partial(_mla_kernel, seq=seq, tk=tk),
        grid=(b, MLA_HEADS // nh),
        in_specs=[pl.BlockSpec((1, seq, nh * LANES), lambda bi, h: (bi, 0, h)),
                  pl.BlockSpec((1, seq, nh * LANES), lambda bi, h: (bi, 0, h)),
                  pl.BlockSpec((1, seq, nh // 2 * LANES), lambda bi, h: (bi, 0, h))],
        out_specs=pl.BlockSpec((1, seq, nh // 2 * LANES), lambda bi, h: (bi, 0, h)),
        out_shape=jax.ShapeDtypeStruct((b, seq, MLA_HEADS * MLA_V_DIM), BF16),
        compiler_params=_cp(("arbitrary", "arbitrary")),
        name="mla_attention",
    )(q3, k3, v3)


def _softplus(z):
    return jnp.maximum(z, 0.0) + jnp.log(1.0 + jnp.exp(-jnp.abs(z)))


def _sb_kernel(q_ref, k_ref, v_ref, o_ref, run_ref, acc_ref, *, seq, tk):
    n = seq // tk
    qs = _head_halves(q_ref[0])
    key = lax.broadcasted_iota(jnp.int32, (tk, tk), 0)
    qry = lax.broadcasted_iota(jnp.int32, (tk, tk), 1)
    strict = key < qry
    later = (qry > key).astype(BF16)

    def unit(j, qlo, qhi, diag):
        k = k_ref[0, j * tk:(j + 1) * tk, :]
        for hh in range(2):
            v = v_ref[0, j * tk:(j + 1) * tk, hh * HEAD_DIM:(hh + 1) * HEAD_DIM]
            z = lax.dot_general(k, qs[hh][qlo:qhi, :], _NT, preferred_element_type=F32)
            sp = _softplus(z)
            log_keep = -sp
            if diag:
                log_keep = _mask_first_block(log_keep, strict, 0.0, tk)
            between = run_ref[hh, :, qlo:qhi]
            for part in _split_bf16(log_keep, 2):
                between = between + jnp.dot(later, part, preferred_element_type=F32)
            w = jnp.exp((z - sp) + between)
            if diag:
                w = _mask_first_block(w, strict, 0.0, tk)
            acc_ref[hh, :, qlo:qhi] += lax.dot_general(v, w.astype(BF16), _TN,
                                                       preferred_element_type=F32)
            run_ref[hh, :, qlo:qhi] += jnp.sum(log_keep, axis=0, keepdims=True)

    run_ref[...] = jnp.zeros_like(run_ref)
    acc_ref[...] = jnp.zeros_like(acc_ref)
    for j in range(n - 1, -1, -1):
        unit(j, j * tk, min((j + 2) * tk, seq), True)

    for j in range(n - 3, -1, -1):
        qlo = (j + 2) * tk

        @pl.when(jnp.max(run_ref[:, :, qlo:]) >= SB_CUTOFF)
        def _():
            unit(j, qlo, seq, False)

    o_ref[0] = jnp.concatenate([acc_ref[0], acc_ref[1]], axis=0).T.astype(o_ref.dtype)


def _sb_attention(qkv3):
    b, seq, _ = qkv3.shape
    tk = min(ATTN_TK, seq)
    hp = SB_HEADS // 2
    qoff = OFF_SB // LANES
    return pl.pallas_call(
        functools.partial(_sb_kernel, seq=seq, tk=tk),
        grid=(b, hp),
        in_specs=[pl.BlockSpec((1, seq, LANES), lambda bi, h: (bi, 0, qoff + h)),
                  pl.BlockSpec((1, seq, LANES), lambda bi, h: (bi, 0, qoff + hp + h)),
                  pl.BlockSpec((1, seq, LANES), lambda bi, h: (bi, 0, qoff + 2 * hp + h))],
        out_specs=pl.BlockSpec((1, seq, LANES), lambda bi, h: (bi, 0, h)),
        out_shape=jax.ShapeDtypeStruct((b, seq, SB_W), BF16),
        scratch_shapes=[pltpu.VMEM((2, 1, seq), F32), pltpu.VMEM((2, HEAD_DIM, seq), F32)],
        compiler_params=_cp(("arbitrary", "arbitrary")),
        name="sb_attention",
    )(qkv3, qkv3, qkv3)


def _layer_norm(y, g, b):
    yc = y - jnp.mean(y, axis=-1, keepdims=True)
    var = jnp.mean(yc * yc, axis=-1, keepdims=True)
    return yc * lax.rsqrt(var + LN_EPS) * g + b


HALF = D_MODEL // 2
U32 = jnp.uint32


def _pack_halves(y):
    bits = lax.bitcast_convert_type(y.astype(BF16).astype(F32), U32)
    return (bits[:, :HALF] >> 16) | (bits[:, HALF:] & U32(0xFFFF0000))


def _unpack_halves(w):
    lo = lax.bitcast_convert_type(w << 16, F32)
    hi = lax.bitcast_convert_type(w & U32(0xFFFF0000), F32)
    return lo, hi


def _merge_kernel(osb_ref, ofox_ref, omla_ref, xb_ref, x_ref, wg_ref, bg_ref, wsb_ref, wfox_ref,
                  wmla_ref, wout_ref, lg_ref, lb_ref, x1_ref, x1b_ref, x1p_ref):
    xb = xb_ref[...]
    mixed = None
    branches = ((osb_ref, wsb_ref), (ofox_ref, wfox_ref), (omla_ref, wmla_ref))
    for i, (o_ref, w_ref) in enumerate(branches):
        cols = slice(i * D_MODEL, (i + 1) * D_MODEL)
        gate = jax.nn.sigmoid(jnp.dot(xb, wg_ref[:, cols], preferred_element_type=F32)
                              + bg_ref[:, cols])
        term = gate * jnp.dot(o_ref[...], w_ref[...], preferred_element_type=F32)
        mixed = term if mixed is None else mixed + term
    mix = jnp.dot(mixed.astype(BF16), wout_ref[...], preferred_element_type=F32)
    y = _layer_norm(DN_ALPHA * x_ref[...] + mix, lg_ref[...], lb_ref[...])
    x1_ref[...] = y
    x1b_ref[...] = y.astype(BF16)
    x1p_ref[...] = _pack_halves(y)


def _merge(o_sb, o_fox, o_mla, xb, x, wg, bg, wsb, wfox, wmla, wout, lg, lb, tm=512):
    n = x.shape[0]
    tm = min(tm, n)
    full = lambda a: pl.BlockSpec(a.shape, lambda i: (0,) * a.ndim)
    rows = lambda w: pl.BlockSpec((tm, w), lambda i: (i, 0))
    return pl.pallas_call(
        _merge_kernel,
        grid=(n // tm,),
        in_specs=[rows(SB_W), rows(FOX_W), rows(MLA_HEADS * MLA_V_DIM), rows(D_MODEL),
                  rows(D_MODEL), full(wg), full(bg), full(wsb), full(wfox), full(wmla),
                  full(wout), full(lg), full(lb)],
        out_specs=[rows(D_MODEL), rows(D_MODEL), rows(HALF)],
        out_shape=[jax.ShapeDtypeStruct((n, D_MODEL), F32),
                   jax.ShapeDtypeStruct((n, D_MODEL), BF16),
                   jax.ShapeDtypeStruct((n, HALF), U32)],
        compiler_params=_cp(("arbitrary",)),
        name="merge_outproj_ln",
    )(o_sb, o_fox, o_mla, xb, x, wg, bg, wsb, wfox, wmla, wout, lg, lb)


def _route_kernel(x_ref, wr_ref, rb_ref, idx_ref, w_ref, rank_ref, cnt_ref, run_ref, *, tm):
    step = pl.program_id(0)

    @pl.when(step == 0)
    def _():
        run_ref[...] = jnp.zeros_like(run_ref)

    logits = lax.dot_general(wr_ref[...], x_ref[...], _NT, preferred_element_type=F32)
    scores = jax.nn.sigmoid(logits)
    biased = scores + rb_ref[...]
    e_iota = lax.broadcasted_iota(jnp.int32, (N_EXPERTS, tm), 0)
    big = jnp.int32(1 << 20)

    g_iota = lax.broadcasted_iota(jnp.int32, (GROUP_SIZE, tm), 0)
    gs_rows = []
    for g in range(N_GROUPS):
        blk = biased[g * GROUP_SIZE:(g + 1) * GROUP_SIZE, :]
        m1 = jnp.max(blk, axis=0, keepdims=True)
        first = jnp.min(jnp.where(blk == m1, g_iota, big), axis=0, keepdims=True)
        m2 = jnp.max(jnp.where(g_iota == first, -jnp.inf, blk), axis=0, keepdims=True)
        gs_rows.append(m1 + m2)
    gs = jnp.concatenate(gs_rows, axis=0)
    n_iota = lax.broadcasted_iota(jnp.int32, (N_GROUPS, tm), 0)
    keep = jnp.zeros((N_GROUPS, tm), jnp.bool_)
    for _ in range(TOPK_GROUPS):
        m = jnp.max(gs, axis=0, keepdims=True)
        first = jnp.min(jnp.where(gs == m, n_iota, big), axis=0, keepdims=True)
        hit = n_iota == first
        keep = jnp.logical_or(keep, hit)
        gs = jnp.where(hit, -jnp.inf, gs)
    keep_f = keep.astype(F32)
    expert_keep = jnp.concatenate(
        [jnp.broadcast_to(keep_f[g:g + 1, :], (GROUP_SIZE, tm)) for g in range(N_GROUPS)], axis=0)
    masked = jnp.where(expert_keep > 0.5, biased, -jnp.inf)

    idx_rows, w_rows, hits = [], [], []
    sel = jnp.zeros((N_EXPERTS, tm), F32)
    for _ in range(TOP_K):
        m = jnp.max(masked, axis=0, keepdims=True)
        first = jnp.min(jnp.where(masked == m, e_iota, big), axis=0, keepdims=True)
        hit = e_iota == first
        idx_rows.append(first)
        w_rows.append(jnp.sum(jnp.where(hit, scores, 0.0), axis=0, keepdims=True))
        hits.append(hit)
        sel = sel + hit.astype(F32)
        masked = jnp.where(hit, -jnp.inf, masked)
    w = jnp.concatenate(w_rows, axis=0)
    w = w / jnp.sum(w, axis=0, keepdims=True) * ROUTED_SCALE
    idx_ref[...] = jnp.concatenate(idx_rows, axis=0)
    w_ref[...] = w

    r = lax.broadcasted_iota(jnp.int32, (tm, tm), 0)
    c = lax.broadcasted_iota(jnp.int32, (tm, tm), 1)
    earlier = (r < c).astype(BF16)
    prefix = jnp.dot(sel.astype(BF16), earlier, preferred_element_type=F32) + run_ref[...]
    rank_rows = [jnp.sum(jnp.where(h, prefix, 0.0), axis=0, keepdims=True) for h in hits]
    rank_ref[...] = jnp.concatenate(rank_rows, axis=0).astype(jnp.int32)
    total = run_ref[...] + jnp.sum(sel, axis=1, keepdims=True)
    run_ref[...] = total
    cnt_ref[...] = jnp.broadcast_to(total, (N_EXPERTS, LANES))


def _route(x1b, wr_t, rb, tm=256):
    n = x1b.shape[0]
    tm = min(tm, n)
    return pl.pallas_call(
        functools.partial(_route_kernel, tm=tm),
        grid=(n // tm,),
        in_specs=[pl.BlockSpec((tm, D_MODEL), lambda i: (i, 0)),
                  pl.BlockSpec((N_EXPERTS, D_MODEL), lambda i: (0, 0)),
                  pl.BlockSpec((N_EXPERTS, 1), lambda i: (0, 0))],
        out_specs=[pl.BlockSpec((TOP_K, tm), lambda i: (0, i)),
                   pl.BlockSpec((TOP_K, tm), lambda i: (0, i)),
                   pl.BlockSpec((TOP_K, tm), lambda i: (0, i)),
                   pl.BlockSpec((N_EXPERTS, LANES), lambda i: (0, 0))],
        out_shape=[jax.ShapeDtypeStruct((TOP_K, n), jnp.int32),
                   jax.ShapeDtypeStruct((TOP_K, n), F32),
                   jax.ShapeDtypeStruct((TOP_K, n), jnp.int32),
                   jax.ShapeDtypeStruct((N_EXPERTS, LANES), F32)],
        scratch_shapes=[pltpu.VMEM((N_EXPERTS, 1), F32)],
        compiler_params=_cp(("arbitrary",)),
        name="router_topk",
    )(x1b, wr_t, rb)


def _dest_kernel(idx_ref, rank_ref, st_ref, dest_ref, *, tm):
    e_iota = lax.broadcasted_iota(jnp.int32, (N_EXPERTS, tm), 0)
    starts = st_ref[...]
    rows = []
    for k in range(TOP_K):
        hit = e_iota == idx_ref[k:k + 1, :]
        rows.append(jnp.sum(jnp.where(hit, starts, 0.0), axis=0, keepdims=True))
    dest_ref[...] = jnp.concatenate(rows, axis=0).astype(jnp.int32) + rank_ref[...]


def _dest(idx_t, rank_t, starts, tm=512):
    n = idx_t.shape[1]
    tm = min(tm, n)
    blk = lambda: pl.BlockSpec((TOP_K, tm), lambda i: (0, i))
    return pl.pallas_call(
        functools.partial(_dest_kernel, tm=tm),
        grid=(n // tm,),
        in_specs=[blk(), blk(), pl.BlockSpec((N_EXPERTS, 1), lambda i: (0, 0))],
        out_specs=blk(),
        out_shape=jax.ShapeDtypeStruct((TOP_K, n), jnp.int32),
        compiler_params=_cp(("arbitrary",)),
        name="moe_dest",
    )(idx_t, rank_t, starts.astype(F32).reshape(N_EXPERTS, 1))


SC_CORES = 2
SC_SUBCORES = 16
SC_WINDOW = 128

def _sc_mesh():
    return plsc.VectorSubcoreMesh(core_axis_name="c", subcore_axis_name="s",
                                  num_cores=SC_CORES, num_subcores=SC_SUBCORES)


def _sc_windows(n):
    workers = SC_CORES * SC_SUBCORES
    assert n % (SC_WINDOW * workers) == 0, n
    return n // SC_WINDOW // workers


def _sc_dispatch(dest_win, x1p, n_rows):
    n = x1p.shape[0]
    per_worker = _sc_windows(n)

    @functools.partial(
        pl.kernel, mesh=_sc_mesh(),
        out_type=jax.ShapeDtypeStruct((n_rows, HALF), U32),
        scratch_types=[pltpu.VMEM((TOP_K, SC_WINDOW), jnp.int32),
                       pltpu.VMEM((SC_WINDOW, HALF), U32)],
        name="moe_dispatch_sc")
    def run(dest_hbm, x_hbm, xs_hbm, idx_v, rows_v):
        worker = lax.axis_index("s") * SC_CORES + lax.axis_index("c")

        @pl.loop(0, per_worker)
        def _(c):
            win = worker * per_worker + c
            pltpu.sync_copy(dest_hbm.at[win], idx_v)
            pltpu.sync_copy(x_hbm.at[pl.ds(pl.multiple_of(win * SC_WINDOW, SC_WINDOW),
                                           SC_WINDOW)], rows_v)
            for k in range(TOP_K):
                pltpu.sync_copy(rows_v, xs_hbm.at[idx_v.at[k]])

    return run(dest_win, x1p)


def _sc_gather(dest_win, ys, n):
    per_worker = _sc_windows(n)

    @functools.partial(
        pl.kernel, mesh=_sc_mesh(),
        out_type=jax.ShapeDtypeStruct((TOP_K, n, HALF), U32),
        scratch_types=[pltpu.VMEM((TOP_K, SC_WINDOW), jnp.int32),
                       pltpu.VMEM((SC_WINDOW, HALF), U32)],
        name="moe_gather_sc")
    def run(dest_hbm, ys_hbm, yt_hbm, idx_v, rows_v):
        worker = lax.axis_index("s") * SC_CORES + lax.axis_index("c")

        @pl.loop(0, per_worker)
        def _(c):
            win = worker * per_worker + c
            start = pl.multiple_of(win * SC_WINDOW, SC_WINDOW)
            pltpu.sync_copy(dest_hbm.at[win], idx_v)
            for k in range(TOP_K):
                pltpu.sync_copy(ys_hbm.at[idx_v.at[k]], rows_v)
                pltpu.sync_copy(rows_v, yt_hbm.at[k, pl.ds(start, SC_WINDOW)])

    return run(dest_win, ys)


def _expert_kernel(start_ref, ntile_ref, count_ref, xs_hbm, wg_hbm, wu_hbm, wd_hbm, ys_hbm,
                   wg_buf, wu_buf, wd_buf, wgb, wub, wdb, x_buf, y_buf, wsem, xsem, ysem,
                   *, bm, layer):
    e = pl.program_id(0)
    s = lax.rem(e, 2)
    n_t = ntile_ref[e]
    tile0 = start_ref[e]
    count = count_ref[e]

    def weight_copies(expert, slot):
        return [pltpu.make_async_copy(hbm.at[layer, expert], buf.at[slot], wsem.at[slot, i])
                for i, (hbm, buf) in enumerate(((wg_hbm, wg_buf), (wu_hbm, wu_buf),
                                                (wd_hbm, wd_buf)))]

    def tile_rows(t):
        return pl.ds(pl.multiple_of((tile0 + t) * bm, bm), bm)

    def x_copy(t, slot):
        return pltpu.make_async_copy(xs_hbm.at[tile_rows(t)], x_buf.at[slot], xsem.at[slot])

    def y_copy(t, slot):
        return pltpu.make_async_copy(y_buf.at[slot], ys_hbm.at[tile_rows(t)], ysem.at[slot])

    @pl.when(e == 0)
    def _():
        for c in weight_copies(0, 0):
            c.start()

    for c in weight_copies(e, s):
        c.wait()

    @pl.when(e + 1 < N_EXPERTS)
    def _():
        for c in weight_copies(e + 1, 1 - s):
            c.start()

    @pl.when(n_t > 0)
    def _():
        wgb[...] = wg_buf[s].astype(BF16)
        wub[...] = wu_buf[s].astype(BF16)
        wdb[...] = wd_buf[s].astype(BF16)
        x_copy(0, 0).start()

        def tile_step(t, carry):
            slot = lax.rem(t, 2)
            x_copy(t, slot).wait()

            @pl.when(t + 1 < n_t)
            def _():
                x_copy(t + 1, 1 - slot).start()

            @pl.when(t >= 2)
            def _():
                y_copy(t - 2, slot).wait()

            rows = lax.broadcasted_iota(jnp.int32, (bm, 1), 0)
            x_lo, x_hi = _unpack_halves(jnp.where(rows < count - t * bm, x_buf[slot], U32(0)))
            x = jnp.concatenate([x_lo, x_hi], axis=1).astype(BF16)
            g = jnp.dot(x, wgb[...], preferred_element_type=F32)
            u = jnp.dot(x, wub[...], preferred_element_type=F32)
            h = (g * jax.nn.sigmoid(g) * u).astype(BF16)
            y_buf[slot] = _pack_halves(jnp.dot(h, wdb[...], preferred_element_type=F32))
            y_copy(t, slot).start()
            return carry

        lax.fori_loop(0, n_t, tile_step, 0)

        @pl.when(n_t >= 2)
        def _():
            y_copy(n_t - 2, lax.rem(n_t - 2, 2)).wait()

        y_copy(n_t - 1, lax.rem(n_t - 1, 2)).wait()


def _experts(layer, tiles, xs, wg, wu, wd, bm):
    grid_spec = pltpu.PrefetchScalarGridSpec(
        num_scalar_prefetch=len(tiles),
        grid=(N_EXPERTS,),
        in_specs=[pl.BlockSpec(memory_space=pl.ANY)] * 4,
        out_specs=pl.BlockSpec(memory_space=pl.ANY),
        scratch_shapes=[pltpu.VMEM((2, D_MODEL, EXPERT_FF), F32),
                        pltpu.VMEM((2, D_MODEL, EXPERT_FF), F32),
                        pltpu.VMEM((2, EXPERT_FF, D_MODEL), F32),
                        pltpu.VMEM((D_MODEL, EXPERT_FF), BF16),
                        pltpu.VMEM((D_MODEL, EXPERT_FF), BF16),
                        pltpu.VMEM((EXPERT_FF, D_MODEL), BF16),
                        pltpu.VMEM((2, bm, HALF), U32),
                        pltpu.VMEM((2, bm, HALF), U32),
                        pltpu.SemaphoreType.DMA((2, 3)),
                        pltpu.SemaphoreType.DMA((2,)),
                        pltpu.SemaphoreType.DMA((2,))],
    )
    return pl.pallas_call(
        functools.partial(_expert_kernel, bm=bm, layer=layer),
        grid_spec=grid_spec,
        out_shape=jax.ShapeDtypeStruct(xs.shape, U32),
        compiler_params=_cp(("arbitrary",)),
        name="moe_experts",
    )(*tiles, xs, wg, wu, wd)


def _expert_tiles(counts, bm):
    padded = (counts + bm - 1) // bm * bm
    pad_start = jnp.cumsum(padded) - padded
    i32 = lambda a: a.astype(jnp.int32)
    return i32(pad_start), (i32(pad_start // bm), i32(padded // bm), i32(counts))


def _combine_rows_kernel(yt_ref, tw_ref, x1_ref, x1b_ref, wsg_ref, wsu_ref, wsd_ref, lg_ref, lb_ref,
                         o_ref, ob_ref):
    tw = tw_ref[...]
    r_lo = jnp.zeros((tw.shape[0], HALF), F32)
    r_hi = jnp.zeros((tw.shape[0], HALF), F32)
    for k in range(TOP_K):
        y_lo, y_hi = _unpack_halves(yt_ref[k])
        r_lo = r_lo + tw[:, k:k + 1] * y_lo
        r_hi = r_hi + tw[:, k:k + 1] * y_hi
    routed = jnp.concatenate([r_lo, r_hi], axis=1)
    xb = x1b_ref[...]
    g = jnp.dot(xb, wsg_ref[...], preferred_element_type=F32)
    u = jnp.dot(xb, wsu_ref[...], preferred_element_type=F32)
    h = (g * jax.nn.sigmoid(g) * u).astype(BF16)
    shared = jnp.dot(h, wsd_ref[...], preferred_element_type=F32)
    y = _layer_norm(DN_ALPHA * x1_ref[...] + (shared + routed), lg_ref[...], lb_ref[...])
    o_ref[...] = y
    ob_ref[...] = y.astype(BF16)


def _combine_rows(yt, tw, x1, x1b, wsg, wsu, wsd, lg, lb, tc=256):
    n = x1.shape[0]
    tc = min(tc, n)
    full = lambda a: pl.BlockSpec(a.shape, lambda i: (0,) * a.ndim)
    rows = lambda w: pl.BlockSpec((tc, w), lambda i: (i, 0))
    return pl.pallas_call(
        _combine_rows_kernel,
        grid=(n // tc,),
        in_specs=[pl.BlockSpec((TOP_K, tc, HALF), lambda i: (0, i, 0)),
                  rows(TOP_K), rows(D_MODEL), rows(D_MODEL),
                  full(wsg), full(wsu), full(wsd), full(lg), full(lb)],
        out_specs=[rows(D_MODEL), rows(D_MODEL)],
        out_shape=[jax.ShapeDtypeStruct((n, D_MODEL), F32),
                   jax.ShapeDtypeStruct((n, D_MODEL), BF16)],
        compiler_params=_cp(("arbitrary",)),
        name="moe_combine_ln",
    )(yt, tw, x1, x1b, wsg, wsu, wsd, lg, lb)


def _head_cols(w, heads, width, lo, hi):
    return w.reshape(w.shape[0], heads, width)[:, :, lo:hi]


def _prep_layer(w_in, b_gate, b_forget, mla_q_norm, w_uq, mla_kv_norm, w_ukv):
    f = lambda a: a.astype(BF16)
    half = MLA_ROPE_DIM // 2
    d = w_in.shape[0]
    qscale = HEAD_DIM ** -0.5
    w_qkv = jnp.concatenate([
        w_in[:, OFF_SB:OFF_SB + SB_W] * qscale, w_in[:, OFF_SB + SB_W:OFF_FOX],
        w_in[:, OFF_FOX:OFF_FOX + FOX_W] * (qscale * LOG2E), w_in[:, OFF_FOX + FOX_W:OFF_FGATE]],
        axis=1)
    w_kr = w_in[:, OFF_KR:OFF_GATE]
    w_kr_rot = jnp.concatenate([-w_kr[:, half:], w_kr[:, :half]], axis=1)
    z = lambda c: jnp.zeros((d, c), F32)
    pad_rope = LANES - MLA_QK_DIM
    w_small = jnp.concatenate([
        w_in[:, OFF_DQ:OFF_DKV], w_in[:, OFF_DKV:OFF_KR],
        z(MLA_NOPE_DIM), w_kr, z(pad_rope),
        z(MLA_NOPE_DIM), w_kr_rot, z(pad_rope),
        w_in[:, OFF_FGATE:OFF_DQ], z(LANES - FOX_HEADS)], axis=1)
    w_gate = w_in[:, OFF_GATE:]

    r = w_uq.shape[0]
    q_nope = _head_cols(w_uq, MLA_HEADS, MLA_QK_DIM, 0, MLA_NOPE_DIM)
    q_rope = _head_cols(w_uq, MLA_HEADS, MLA_QK_DIM, MLA_NOPE_DIM, MLA_QK_DIM)
    q_rope_rot = jnp.concatenate([-q_rope[:, :, half:], q_rope[:, :, :half]], axis=2)
    zq = lambda c: jnp.zeros((r, MLA_HEADS, c), F32)
    wqa = jnp.concatenate([q_nope, q_rope, zq(pad_rope)], axis=2).reshape(r, MLA_HEADS * LANES)
    wqb = jnp.concatenate([zq(MLA_NOPE_DIM), q_rope_rot, zq(pad_rope)], axis=2).reshape(
        r, MLA_HEADS * LANES)
    rk = w_ukv.shape[0]
    kvw = MLA_NOPE_DIM + MLA_V_DIM
    k_nope = _head_cols(w_ukv, MLA_HEADS, kvw, 0, MLA_NOPE_DIM)
    wk = jnp.concatenate([k_nope, jnp.zeros((rk, MLA_HEADS, LANES - MLA_NOPE_DIM), F32)],
                         axis=2).reshape(rk, MLA_HEADS * LANES)
    wv = _head_cols(w_ukv, MLA_HEADS, kvw, MLA_NOPE_DIM, kvw).reshape(rk, MLA_HEADS * MLA_V_DIM)
    bf = jnp.concatenate([b_forget, jnp.zeros((LANES - FOX_HEADS,), F32)]).reshape(1, LANES)
    return dict(w_qkv=f(w_qkv), w_small=f(w_small), w_gate=f(w_gate),
                b_gate=b_gate.reshape(1, -1), wqa=f(wqa), wqb=f(wqb), wk=f(wk), wv=f(wv),
                qn=mla_q_norm.reshape(1, -1), kvn=mla_kv_norm.reshape(1, -1), bf=bf)


def _rope_tables(seq):
    half = MLA_ROPE_DIM // 2
    inv_freq = jnp.power(ROPE_BASE, -jnp.arange(half, dtype=F32) / half)
    ang = jnp.arange(seq).astype(F32)[:, None] * inv_freq[None, :]
    cos = jnp.concatenate([jnp.cos(ang), jnp.cos(ang)], axis=1)
    sin = jnp.concatenate([jnp.sin(ang), jnp.sin(ang)], axis=1)
    pad = lambda t: jnp.concatenate([jnp.zeros((seq, MLA_NOPE_DIM), F32), t,
                                     jnp.zeros((seq, LANES - MLA_QK_DIM), F32)], axis=1)
    return pad(cos), pad(sin)


EXPERT_BM = 512


def kernel(x, ln1_g, ln1_b, ln2_g, ln2_b, w_in, b_gate, b_forget, mla_q_norm, w_uq, mla_kv_norm,
           w_ukv, w_proj_sb, w_proj_fox, w_proj_mla, w_out, w_router, router_bias,
           w_exp_gate, w_exp_up, w_exp_down, w_sh_gate, w_sh_up, w_sh_down):
    b, seq, d = x.shape
    n = b * seq
    depth = w_in.shape[0]
    cos_t, sin_t = _rope_tables(seq)
    xf = x.reshape(n, d)
    xb = xf.astype(BF16)
    f = lambda a: a.astype(BF16)
    bm = min(EXPERT_BM, n * TOP_K)
    n_tiles = n * TOP_K // bm + N_EXPERTS
    for l in range(depth):
        p = _prep_layer(w_in[l], b_gate[l], b_forget[l], mla_q_norm[l], w_uq[l], mla_kv_norm[l],
                        w_ukv[l])
        qkv = _matmul(xb, p["w_qkv"], BF16, tn=QKV_W // 2)
        q_mla, k_mla, v_mla, lf = _mla_prep(xb, p["w_small"], p["qn"], p["kvn"], p["wqa"],
                                            p["wqb"], p["wk"], p["wv"], cos_t, sin_t, p["bf"], seq)
        cum, cumt = _cumsum(lf.reshape(b, seq, LANES))
        cumt4 = cumt[:, :FOX_HEADS, :].reshape(b, FOX_HEADS, 1, seq)
        qkv3 = qkv.reshape(b, seq, QKV_W)
        o_sb = _sb_attention(qkv3).reshape(n, SB_W)
        o_fox = _fox_attention(qkv3, cum, cumt4).reshape(n, FOX_W)
        o_mla = _mla_attention(q_mla.reshape(b, seq, -1), k_mla.reshape(b, seq, -1),
                               v_mla.reshape(b, seq, -1)).reshape(n, -1)
        x1, x1b, x1p = _merge(o_sb, o_fox, o_mla, xb, xf, p["w_gate"], p["b_gate"],
                              f(w_proj_sb[l]), f(w_proj_fox[l]), f(w_proj_mla[l]), f(w_out[l]),
                              ln1_g[l].reshape(1, d), ln1_b[l].reshape(1, d))

        idx_t, tw_t, rank_t, cnt = _route(x1b, f(w_router[l].T), router_bias[l].reshape(-1, 1))
        counts = cnt[:, 0].astype(jnp.int32)
        starts, tiles = _expert_tiles(counts, bm)
        dest_t = _dest(idx_t, rank_t, starts)
        dest_win = dest_t.reshape(TOP_K, n // SC_WINDOW, SC_WINDOW).transpose(1, 0, 2)
        xs = _sc_dispatch(dest_win, x1p, n_tiles * bm)
        ys = _experts(l, tiles, xs, w_exp_gate, w_exp_up, w_exp_down, bm)
        yt = _sc_gather(dest_win, ys, n)
        xf, xb = _combine_rows(yt, tw_t.T, x1, x1b, f(w_sh_gate[l]), f(w_sh_up[l]),
                               f(w_sh_down[l]), ln2_g[l].reshape(1, d), ln2_b[l].reshape(1, d))
    return xf.reshape(b, seq, d)
```

```python
import functools

import jax
import jax.numpy as jnp
from jax import lax
from jax.experimental import pallas as pl
from jax.experimental.pallas import tpu as pltpu
from jax.experimental.pallas import tpu_sc as plsc

F32 = jnp.float32
BF16 = jnp.bfloat16

D_MODEL = 1024
HEAD_DIM = 64
SB_HEADS = 4
FOX_HEADS = 4
MLA_HEADS = 8
MLA_Q_RANK = 256
MLA_KV_RANK = 128
MLA_NOPE_DIM = 64
MLA_ROPE_DIM = 32
MLA_V_DIM = 64
ROPE_BASE = 10000.0
N_BRANCHES = 3
N_EXPERTS = 256
TOP_K = 8
N_GROUPS = 8
TOPK_GROUPS = 4
GROUP_SIZE = N_EXPERTS // N_GROUPS
EXPERT_FF = 256
SHARED_FF = 256
ROUTED_SCALE = 2.5
CHUNK = 64
LN_EPS = 1e-5
RMS_EPS = 1e-6
DEPTH = 2
DN_ALPHA = (2 * DEPTH) ** 0.25

SB_W = SB_HEADS * HEAD_DIM
FOX_W = FOX_HEADS * HEAD_DIM
MLA_QK_DIM = MLA_NOPE_DIM + MLA_ROPE_DIM
OFF_SB = 0
OFF_FOX = OFF_SB + 3 * SB_W
OFF_FGATE = OFF_FOX + 3 * FOX_W
OFF_DQ = OFF_FGATE + FOX_HEADS
OFF_DKV = OFF_DQ + MLA_Q_RANK
OFF_KR = OFF_DKV + MLA_KV_RANK
OFF_GATE = OFF_KR + MLA_ROPE_DIM

LANES = 128
SUBLANES = 8
QKV_W = 3 * SB_W + 3 * FOX_W
SMALL_W = MLA_Q_RANK + MLA_KV_RANK + 3 * LANES
SB_CUTOFF = -104.0
LOG2E = 1.4426950408889634

VMEM_LIMIT = 48 * 1024 * 1024


def _cp(sem, vmem=VMEM_LIMIT):
    return pltpu.CompilerParams(dimension_semantics=sem, vmem_limit_bytes=vmem)


def _mm_kernel(x_ref, w_ref, o_ref, *, sub):
    x = x_ref[...]
    for c in range(0, o_ref.shape[1], sub):
        acc = jnp.dot(x, w_ref[:, c:c + sub], preferred_element_type=F32)
        o_ref[:, c:c + sub] = acc.astype(o_ref.dtype)


def _matmul(x, w, out_dtype, tn, tm=1024, sub=256):
    n, k = x.shape
    c = w.shape[1]
    tm = min(tm, n)
    return pl.pallas_call(
        functools.partial(_mm_kernel, sub=sub),
        grid=(n // tm, c // tn),
        in_specs=[pl.BlockSpec((tm, k), lambda i, j: (i, 0)),
                  pl.BlockSpec((k, tn), lambda i, j: (0, j))],
        out_specs=pl.BlockSpec((tm, tn), lambda i, j: (i, j)),
        out_shape=jax.ShapeDtypeStruct((n, c), out_dtype),
        compiler_params=_cp(("arbitrary", "arbitrary")),
        name="proj_matmul",
    )(x, w)


def _split_bf16(x, parts):
    out = []
    for _ in range(parts):
        h = x.astype(BF16)
        out.append(h)
        x = x - h.astype(F32)
    return out


def _mla_prep_kernel(x_ref, ws_ref, qn_ref, kvn_ref, wqa_ref, wqb_ref, wk_ref, wv_ref,
                     cos_ref, sin_ref, bf_ref, q_ref, k_ref, v_ref, lf_ref):
    sm = jnp.dot(x_ref[...], ws_ref[...], preferred_element_type=F32)
    dq = sm[:, :MLA_Q_RANK]
    dkv = sm[:, MLA_Q_RANK:MLA_Q_RANK + MLA_KV_RANK]
    o = MLA_Q_RANK + MLA_KV_RANK
    kr = sm[:, o:o + LANES]
    kr_rot = sm[:, o + LANES:o + 2 * LANES]
    fg = sm[:, o + 2 * LANES:o + 3 * LANES]

    cq = dq * lax.rsqrt(jnp.mean(dq * dq, axis=-1, keepdims=True) + RMS_EPS) * qn_ref[...]
    ckv = dkv * lax.rsqrt(jnp.mean(dkv * dkv, axis=-1, keepdims=True) + RMS_EPS) * kvn_ref[...]
    cq = cq.astype(BF16)
    ckv = ckv.astype(BF16)

    cosk = cos_ref[...]
    sink = sin_ref[...]
    lane = lax.broadcasted_iota(jnp.int32, (1, LANES), 1)
    nope = (lane < MLA_NOPE_DIM).astype(F32)
    scale = MLA_QK_DIM ** -0.5 * LOG2E
    cq_tab = jnp.concatenate([(cosk + nope) * scale] * MLA_HEADS, axis=1)
    sq_tab = jnp.concatenate([sink * scale] * MLA_HEADS, axis=1)

    qa = jnp.dot(cq, wqa_ref[...], preferred_element_type=F32)
    qb = jnp.dot(cq, wqb_ref[...], preferred_element_type=F32)
    q_ref[...] = (qa * cq_tab + qb * sq_tab).astype(q_ref.dtype)

    k_rope = kr * cosk + kr_rot * sink
    ka = jnp.dot(ckv, wk_ref[...], preferred_element_type=F32)
    k_ref[...] = (ka + jnp.concatenate([k_rope] * MLA_HEADS, axis=1)).astype(k_ref.dtype)
    v_ref[...] = jnp.dot(ckv, wv_ref[...], preferred_element_type=F32).astype(v_ref.dtype)
    lf_ref[...] = jax.nn.log_sigmoid(fg + bf_ref[...])


def _mla_prep(xb, w_small, qn, kvn, wqa, wqb, wk, wv, cos_t, sin_t, bf, seq, tm=512):
    n, d = xb.shape
    tm = min(tm, seq)
    sblocks = seq // tm
    hw = MLA_HEADS * LANES
    full = lambda a: pl.BlockSpec(a.shape, lambda i: (0,) * a.ndim)
    return pl.pallas_call(
        _mla_prep_kernel,
        grid=(n // tm,),
        in_specs=[pl.BlockSpec((tm, d), lambda i: (i, 0)), full(w_small),
                  full(qn), full(kvn), full(wqa), full(wqb), full(wk), full(wv),
                  pl.BlockSpec((tm, LANES), lambda i: (i % sblocks, 0)),
                  pl.BlockSpec((tm, LANES), lambda i: (i % sblocks, 0)),
                  full(bf)],
        out_specs=[pl.BlockSpec((tm, hw), lambda i: (i, 0)),
                   pl.BlockSpec((tm, hw), lambda i: (i, 0)),
                   pl.BlockSpec((tm, MLA_HEADS * MLA_V_DIM), lambda i: (i, 0)),
                   pl.BlockSpec((tm, LANES), lambda i: (i, 0))],
        out_shape=[jax.ShapeDtypeStruct((n, hw), BF16),
                   jax.ShapeDtypeStruct((n, hw), BF16),
                   jax.ShapeDtypeStruct((n, MLA_HEADS * MLA_V_DIM), BF16),
                   jax.ShapeDtypeStruct((n, LANES), F32)],
        compiler_params=_cp(("arbitrary",)),
        name="mla_prep",
    )(xb, w_small, qn, kvn, wqa, wqb, wk, wv, cos_t, sin_t, bf)


def _cumsum_kernel(lf_ref, cum_ref, cumt_ref, *, seq):
    r = lax.broadcasted_iota(jnp.int32, (LANES, LANES), 0)
    c = lax.broadcasted_iota(jnp.int32, (LANES, LANES), 1)
    lower = (c <= r).astype(BF16)

    def body(j, carry):
        start = pl.multiple_of(j * LANES, LANES)
        blk = lf_ref[0, pl.ds(start, LANES), :]
        acc = carry
        for part in _split_bf16(blk, 3):
            acc = acc + jnp.dot(lower, part, preferred_element_type=F32)
        scaled = acc * LOG2E
        cum_ref[0, pl.ds(start, LANES), :] = scaled
        cumt_ref[0, :, pl.ds(start, LANES)] = scaled.T[:SUBLANES, :]
        return jnp.broadcast_to(acc[LANES - 1:LANES, :], (LANES, LANES))

    lax.fori_loop(0, seq // LANES, body, jnp.zeros((LANES, LANES), F32))


def _cumsum(lf3):
    b, seq, _ = lf3.shape
    return pl.pallas_call(
        functools.partial(_cumsum_kernel, seq=seq),
        grid=(b,),
        in_specs=[pl.BlockSpec((1, seq, LANES), lambda i: (i, 0, 0))],
        out_specs=[pl.BlockSpec((1, seq, LANES), lambda i: (i, 0, 0)),
                   pl.BlockSpec((1, SUBLANES, seq), lambda i: (i, 0, 0))],
        out_shape=[jax.ShapeDtypeStruct((b, seq, LANES), F32),
                   jax.ShapeDtypeStruct((b, SUBLANES, seq), F32)],
        compiler_params=_cp(("arbitrary",)),
        name="forget_cumsum",
    )(lf3)


_NT = (((1,), (1,)), ((), ()))
NEG_INIT = -1e30
_TN = (((0,), (0,)), ((), ()))
ATTN_TK = 256


def _head_halves(q):
    lane = lax.broadcasted_iota(jnp.int32, (1, LANES), 1)
    zero = jnp.zeros_like(q)
    return [jnp.where(lane < HEAD_DIM, q, zero), jnp.where(lane >= HEAD_DIM, q, zero)]


def _mask_first_block(x, keep, fill, tk):
    head = jnp.where(keep, x[:, :tk], fill)
    return head if x.shape[1] == tk else jnp.concatenate([head, x[:, tk:]], axis=1)


def _fox_kernel(q_ref, k_ref, v_ref, cc_ref, cr_ref, o_ref, *, seq, tk):
    hp = pl.program_id(1)
    lane = lax.broadcasted_iota(jnp.int32, (1, LANES), 1)
    qs = _head_halves(q_ref[0])
    key = lax.broadcasted_iota(jnp.int32, (tk, tk), 0)
    qry = lax.broadcasted_iota(jnp.int32, (tk, tk), 1)
    causal = key <= qry
    carry = [(jnp.full((1, seq), NEG_INIT, F32), jnp.zeros((1, seq), F32),
              jnp.zeros((HEAD_DIM, seq), F32)) for _ in range(2)]
    for j in range(seq // tk):
        q0 = j * tk
        k = k_ref[0, q0:q0 + tk, :]
        cc = cc_ref[0, q0:q0 + tk, :]
        for hh in range(2):
            m, l, acc = carry[hh]
            v = v_ref[0, q0:q0 + tk, hh * HEAD_DIM:(hh + 1) * HEAD_DIM]
            cs = jnp.sum(jnp.where(lane == 2 * hp + hh, cc, 0.0), axis=1, keepdims=True)
            ct = cr_ref[0, hh, :, q0:]
            st = lax.dot_general(k, qs[hh][q0:, :], _NT, preferred_element_type=F32)
            st = _mask_first_block((st + ct) - cs, causal, -jnp.inf, tk)
            mn, ln, an = _softmax_step_t(st, m[:, q0:], l[:, q0:], acc[:, q0:], v)
            if q0:
                mn = jnp.concatenate([m[:, :q0], mn], axis=1)
                ln = jnp.concatenate([l[:, :q0], ln], axis=1)
                an = jnp.concatenate([acc[:, :q0], an], axis=1)
            carry[hh] = (mn, ln, an)
    _pair_out_t([carry[0][2], carry[1][2]], [carry[0][1], carry[1][1]], o_ref)


def _fox_attention(qkv3, cum, cumt4):
    b, seq, _ = qkv3.shape
    tk = min(ATTN_TK, seq)
    hp = FOX_HEADS // 2
    qoff = OFF_FOX // LANES
    return pl.pallas_call(
        functools.partial(_fox_kernel, seq=seq, tk=tk),
        grid=(b, hp),
        in_specs=[pl.BlockSpec((1, seq, LANES), lambda bi, h: (bi, 0, qoff + h)),
                  pl.BlockSpec((1, seq, LANES), lambda bi, h: (bi, 0, qoff + hp + h)),
                  pl.BlockSpec((1, seq, LANES), lambda bi, h: (bi, 0, qoff + 2 * hp + h)),
                  pl.BlockSpec((1, seq, LANES), lambda bi, h: (bi, 0, 0)),
                  pl.BlockSpec((1, 2, 1, seq), lambda bi, h: (bi, h, 0, 0))],
        out_specs=pl.BlockSpec((1, seq, LANES), lambda bi, h: (bi, 0, h)),
        out_shape=jax.ShapeDtypeStruct((b, seq, FOX_W), BF16),
        compiler_params=_cp(("arbitrary", "arbitrary")),
        name="fox_attention",
    )(qkv3, qkv3, qkv3, cum, cumt4)


def _softmax_step_t(st, m, l, acc, v):
    m_new = jnp.maximum(m, jnp.max(st, axis=0, keepdims=True))
    alpha = jnp.exp2(m - m_new)
    p = jnp.exp2(st - m_new)
    l = alpha * l + jnp.sum(p, axis=0, keepdims=True)
    acc = alpha * acc + lax.dot_general(v, p.astype(BF16), _TN, preferred_element_type=F32)
    return m_new, l, acc


def _pair_out_t(accs, ls, o_ref, pair=0):
    ot = jnp.concatenate([accs[0] / ls[0], accs[1] / ls[1]], axis=0)
    o_ref[0, :, pair * LANES:(pair + 1) * LANES] = ot.T.astype(o_ref.dtype)


MLA_STEP_HEADS = 2


def _mla_kernel(q_ref, k_ref, v_ref, o_ref, *, seq, tk):
    nh = MLA_STEP_HEADS
    key = lax.broadcasted_iota(jnp.int32, (tk, tk), 0)
    qry = lax.broadcasted_iota(jnp.int32, (tk, tk), 1)
    shift = CHUNK.bit_length() - 1
    chunk_causal = (key >> shift) <= (qry >> shift)
    carry = [(jnp.full((1, seq), NEG_INIT, F32), jnp.zeros((1, seq), F32),
              jnp.zeros((MLA_V_DIM, seq), F32)) for _ in range(nh)]
    for j in range(seq // tk):
        q0 = j * tk
        for hh in range(nh):
            m, l, acc = carry[hh]
            k = k_ref[0, q0:q0 + tk, hh * LANES:(hh + 1) * LANES]
            v = v_ref[0, q0:q0 + tk, hh * MLA_V_DIM:(hh + 1) * MLA_V_DIM]
            st = lax.dot_general(k, q_ref[0, q0:, hh * LANES:(hh + 1) * LANES], _NT,
                                 preferred_element_type=F32)
            st = _mask_first_block(st, chunk_causal, -jnp.inf, tk)
            mn, ln, an = _softmax_step_t(st, m[:, q0:], l[:, q0:], acc[:, q0:], v)
            if q0:
                mn = jnp.concatenate([m[:, :q0], mn], axis=1)
                ln = jnp.concatenate([l[:, :q0], ln], axis=1)
                an = jnp.concatenate([acc[:, :q0], an], axis=1)
            carry[hh] = (mn, ln, an)
    for pair in range(nh // 2):
        a, b = carry[2 * pair], carry[2 * pair + 1]
        _pair_out_t([a[2], b[2]], [a[1], b[1]], o_ref, pair)


def _mla_attention(q3, k3, v3):
    b, seq, _ = q3.shape
    tk = min(ATTN_TK, seq)
    nh = MLA_STEP_HEADS
    return pl.pallas_call(
        functools.partial(_mla_kernel, seq=seq, tk=tk),
        grid=(b, MLA_HEADS // nh),
        in_specs=[pl.BlockSpec((1, seq, nh * LANES), lambda bi, h: (bi, 0, h)),
                  pl.BlockSpec((1, seq, nh * LANES), lambda bi, h: (bi, 0, h)),
                  pl.BlockSpec((1, seq, nh // 2 * LANES), lambda bi, h: (bi, 0, h))],
        out_specs=pl.BlockSpec((1, seq, nh // 2 * LANES), lambda bi, h: (bi, 0, h)),
        out_shape=jax.ShapeDtypeStruct((b, seq, MLA_HEADS * MLA_V_DIM), BF16),
        compiler_params=_cp(("arbitrary", "arbitrary")),
        name="mla_attention",
    )(q3, k3, v3)


def _softplus(z):
    return jnp.maximum(z, 0.0) + jnp.log(1.0 + jnp.exp(-jnp.abs(z)))


def _sb_kernel(q_ref, k_ref, v_ref, o_ref, run_ref, acc_ref, *, seq, tk):
    n = seq // tk
    qs = _head_halves(q_ref[0])
    key = lax.broadcasted_iota(jnp.int32, (tk, tk), 0)
    qry = lax.broadcasted_iota(jnp.int32, (tk, tk), 1)
    strict = key < qry
    later = (qry > key).astype(BF16)

    def unit(j, qlo, qhi, diag):
        k = k_ref[0, j * tk:(j + 1) * tk, :]
        for hh in range(2):
            v = v_ref[0, j * tk:(j + 1) * tk, hh * HEAD_DIM:(hh + 1) * HEAD_DIM]
            z = lax.dot_general(k, qs[hh][qlo:qhi, :], _NT, preferred_element_type=F32)
            sp = _softplus(z)
            log_keep = -sp
            if diag:
                log_keep = _mask_first_block(log_keep, strict, 0.0, tk)
            between = run_ref[hh, :, qlo:qhi]
            for part in _split_bf16(log_keep, 2):
                between = between + jnp.dot(later, part, preferred_element_type=F32)
            w = jnp.exp((z - sp) + between)
            if diag:
                w = _mask_first_block(w, strict, 0.0, tk)
            acc_ref[hh, :, qlo:qhi] += lax.dot_general(v, w.astype(BF16), _TN,
                                                       preferred_element_type=F32)
            run_ref[hh, :, qlo:qhi] += jnp.sum(log_keep, axis=0, keepdims=True)

    run_ref[...] = jnp.zeros_like(run_ref)
    acc_ref[...] = jnp.zeros_like(acc_ref)
    for j in range(n - 1, -1, -1):
        unit(j, j * tk, min((j + 2) * tk, seq), True)

    for j in range(n - 3, -1, -1):
        qlo = (j + 2) * tk

        @pl.when(jnp.max(run_ref[:, :, qlo:]) >= SB_CUTOFF)
        def _():
            unit(j, qlo, seq, False)

    o_ref[0] = jnp.concatenate([acc_ref[0], acc_ref[1]], axis=0).T.astype(o_ref.dtype)


def _sb_attention(qkv3):
    b, seq, _ = qkv3.shape
    tk = min(ATTN_TK, seq)
    hp = SB_HEADS // 2
    qoff = OFF_SB // LANES
    return pl.pallas_call(
        functools.partial(_sb_kernel, seq=seq, tk=tk),
        grid=(b, hp),
        in_specs=[pl.BlockSpec((1, seq, LANES), lambda bi, h: (bi, 0, qoff + h)),
                  pl.BlockSpec((1, seq, LANES), lambda bi, h: (bi, 0, qoff + hp + h)),
                  pl.BlockSpec((1, seq, LANES), lambda bi, h: (bi, 0, qoff + 2 * hp + h))],
        out_specs=pl.BlockSpec((1, seq, LANES), lambda bi, h: (bi, 0, h)),
        out_shape=jax.ShapeDtypeStruct((b, seq, SB_W), BF16),
        scratch_shapes=[pltpu.VMEM((2, 1, seq), F32), pltpu.VMEM((2, HEAD_DIM, seq), F32)],
        compiler_params=_cp(("arbitrary", "arbitrary")),
        name="sb_attention",
    )(qkv3, qkv3, qkv3)


def _layer_norm(y, g, b):
    yc = y - jnp.mean(y, axis=-1, keepdims=True)
    var = jnp.mean(yc * yc, axis=-1, keepdims=True)
    return yc * lax.rsqrt(var + LN_EPS) * g + b


HALF = D_MODEL // 2
U32 = jnp.uint32


def _pack_halves(y):
    bits = lax.bitcast_convert_type(y.astype(BF16).astype(F32), U32)
    return (bits[:, :HALF] >> 16) | (bits[:, HALF:] & U32(0xFFFF0000))


def _unpack_halves(w):
    lo = lax.bitcast_convert_type(w << 16, F32)
    hi = lax.bitcast_convert_type(w & U32(0xFFFF0000), F32)
    return lo, hi


def _merge_kernel(osb_ref, ofox_ref, omla_ref, xb_ref, x_ref, wg_ref, bg_ref, wsb_ref, wfox_ref,
                  wmla_ref, wout_ref, lg_ref, lb_ref, x1_ref, x1b_ref, x1p_ref):
    xb = xb_ref[...]
    mixed = None
    branches = ((osb_ref, wsb_ref), (ofox_ref, wfox_ref), (omla_ref, wmla_ref))
    for i, (o_ref, w_ref) in enumerate(branches):
        cols = slice(i * D_MODEL, (i + 1) * D_MODEL)
        gate = jax.nn.sigmoid(jnp.dot(xb, wg_ref[:, cols], preferred_element_type=F32)
                              + bg_ref[:, cols])
        term = gate * jnp.dot(o_ref[...], w_ref[...], preferred_element_type=F32)
        mixed = term if mixed is None else mixed + term
    mix = jnp.dot(mixed.astype(BF16), wout_ref[...], preferred_element_type=F32)
    y = _layer_norm(DN_ALPHA * x_ref[...] + mix, lg_ref[...], lb_ref[...])
    x1_ref[...] = y
    x1b_ref[...] = y.astype(BF16)
    x1p_ref[...] = _pack_halves(y)


def _merge(o_sb, o_fox, o_mla, xb, x, wg, bg, wsb, wfox, wmla, wout, lg, lb, tm=512):
    n = x.shape[0]
    tm = min(tm, n)
    full = lambda a: pl.BlockSpec(a.shape, lambda i: (0,) * a.ndim)
    rows = lambda w: pl.BlockSpec((tm, w), lambda i: (i, 0))
    return pl.pallas_call(
        _merge_kernel,
        grid=(n // tm,),
        in_specs=[rows(SB_W), rows(FOX_W), rows(MLA_HEADS * MLA_V_DIM), rows(D_MODEL),
                  rows(D_MODEL), full(wg), full(bg), full(wsb), full(wfox), full(wmla),
                  full(wout), full(lg), full(lb)],
        out_specs=[rows(D_MODEL), rows(D_MODEL), rows(HALF)],
        out_shape=[jax.ShapeDtypeStruct((n, D_MODEL), F32),
                   jax.ShapeDtypeStruct((n, D_MODEL), BF16),
                   jax.ShapeDtypeStruct((n, HALF), U32)],
        compiler_params=_cp(("arbitrary",)),
        name="merge_outproj_ln",
    )(o_sb, o_fox, o_mla, xb, x, wg, bg, wsb, wfox, wmla, wout, lg, lb)


def _route_kernel(x_ref, wr_ref, rb_ref, idx_ref, w_ref, rank_ref, cnt_ref, run_ref, *, tm):
    step = pl.program_id(0)

    @pl.when(step == 0)
    def _():
        run_ref[...] = jnp.zeros_like(run_ref)

    logits = lax.dot_general(wr_ref[...], x_ref[...], _NT, preferred_element_type=F32)
    scores = jax.nn.sigmoid(logits)
    biased = scores + rb_ref[...]
    e_iota = lax.broadcasted_iota(jnp.int32, (N_EXPERTS, tm), 0)
    big = jnp.int32(1 << 20)

    g_iota = lax.broadcasted_iota(jnp.int32, (GROUP_SIZE, tm), 0)
    gs_rows = []
    for g in range(N_GROUPS):
        blk = biased[g * GROUP_SIZE:(g + 1) * GROUP_SIZE, :]
        m1 = jnp.max(blk, axis=0, keepdims=True)
        first = jnp.min(jnp.where(blk == m1, g_iota, big), axis=0, keepdims=True)
        m2 = jnp.max(jnp.where(g_iota == first, -jnp.inf, blk), axis=0, keepdims=True)
        gs_rows.append(m1 + m2)
    gs = jnp.concatenate(gs_rows, axis=0)
    n_iota = lax.broadcasted_iota(jnp.int32, (N_GROUPS, tm), 0)
    keep = jnp.zeros((N_GROUPS, tm), jnp.bool_)
    for _ in range(TOPK_GROUPS):
        m = jnp.max(gs, axis=0, keepdims=True)
        first = jnp.min(jnp.where(gs == m, n_iota, big), axis=0, keepdims=True)
        hit = n_iota == first
        keep = jnp.logical_or(keep, hit)
        gs = jnp.where(hit, -jnp.inf, gs)
    keep_f = keep.astype(F32)
    expert_keep = jnp.concatenate(
        [jnp.broadcast_to(keep_f[g:g + 1, :], (GROUP_SIZE, tm)) for g in range(N_GROUPS)], axis=0)
    masked = jnp.where(expert_keep > 0.5, biased, -jnp.inf)

    idx_rows, w_rows, hits = [], [], []
    sel = jnp.zeros((N_EXPERTS, tm), F32)
    for _ in range(TOP_K):
        m = jnp.max(masked, axis=0, keepdims=True)
        first = jnp.min(jnp.where(masked == m, e_iota, big), axis=0, keepdims=True)
        hit = e_iota == first
        idx_rows.append(first)
        w_rows.append(jnp.sum(jnp.where(hit, scores, 0.0), axis=0, keepdims=True))
        hits.append(hit)
        sel = sel + hit.astype(F32)
        masked = jnp.where(hit, -jnp.inf, masked)
    w = jnp.concatenate(w_rows, axis=0)
    w = w / jnp.sum(w, axis=0, keepdims=True) * ROUTED_SCALE
    idx_ref[...] = jnp.concatenate(idx_rows, axis=0)
    w_ref[...] = w

    r = lax.broadcasted_iota(jnp.int32, (tm, tm), 0)
    c = lax.broadcasted_iota(jnp.int32, (tm, tm), 1)
    earlier = (r < c).astype(BF16)
    prefix = jnp.dot(sel.astype(BF16), earlier, preferred_element_type=F32) + run_ref[...]
    rank_rows = [jnp.sum(jnp.where(h, prefix, 0.0), axis=0, keepdims=True) for h in hits]
    rank_ref[...] = jnp.concatenate(rank_rows, axis=0).astype(jnp.int32)
    total = run_ref[...] + jnp.sum(sel, axis=1, keepdims=True)
    run_ref[...] = total
    cnt_ref[...] = jnp.broadcast_to(total, (N_EXPERTS, LANES))


def _route(x1b, wr_t, rb, tm=512):
    n = x1b.shape[0]
    tm = min(tm, n)
    return pl.pallas_call(
        functools.partial(_route_kernel, tm=tm),
        grid=(n // tm,),
        in_specs=[pl.BlockSpec((tm, D_MODEL), lambda i: (i, 0)),
                  pl.BlockSpec((N_EXPERTS, D_MODEL), lambda i: (0, 0)),
                  pl.BlockSpec((N_EXPERTS, 1), lambda i: (0, 0))],
        out_specs=[pl.BlockSpec((TOP_K, tm), lambda i: (0, i)),
                   pl.BlockSpec((TOP_K, tm), lambda i: (0, i)),
                   pl.BlockSpec((TOP_K, tm), lambda i: (0, i)),
                   pl.BlockSpec((N_EXPERTS, LANES), lambda i: (0, 0))],
        out_shape=[jax.ShapeDtypeStruct((TOP_K, n), jnp.int32),
                   jax.ShapeDtypeStruct((TOP_K, n), F32),
                   jax.ShapeDtypeStruct((TOP_K, n), jnp.int32),
                   jax.ShapeDtypeStruct((N_EXPERTS, LANES), F32)],
        scratch_shapes=[pltpu.VMEM((N_EXPERTS, 1), F32)],
        compiler_params=_cp(("arbitrary",)),
        name="router_topk",
    )(x1b, wr_t, rb)


def _dest_kernel(idx_ref, rank_ref, st_ref, dest_ref, *, tm):
    e_iota = lax.broadcasted_iota(jnp.int32, (N_EXPERTS, tm), 0)
    starts = st_ref[...]
    rows = []
    for k in range(TOP_K):
        hit = e_iota == idx_ref[k:k + 1, :]
        rows.append(jnp.sum(jnp.where(hit, starts, 0.0), axis=0, keepdims=True))
    dest_ref[...] = jnp.concatenate(rows, axis=0).astype(jnp.int32) + rank_ref[...]


def _dest(idx_t, rank_t, starts, tm=512):
    n = idx_t.shape[1]
    tm = min(tm, n)
    blk = lambda: pl.BlockSpec((TOP_K, tm), lambda i: (0, i))
    return pl.pallas_call(
        functools.partial(_dest_kernel, tm=tm),
        grid=(n // tm,),
        in_specs=[blk(), blk(), pl.BlockSpec((N_EXPERTS, 1), lambda i: (0, 0))],
        out_specs=blk(),
        out_shape=jax.ShapeDtypeStruct((TOP_K, n), jnp.int32),
        compiler_params=_cp(("arbitrary",)),
        name="moe_dest",
    )(idx_t, rank_t, starts.astype(F32).reshape(N_EXPERTS, 1))


SC_CORES = 2
SC_SUBCORES = 16
SC_WINDOW = 128

def _sc_mesh():
    return plsc.VectorSubcoreMesh(core_axis_name="c", subcore_axis_name="s",
                                  num_cores=SC_CORES, num_subcores=SC_SUBCORES)


def _sc_windows(n):
    workers = SC_CORES * SC_SUBCORES
    assert n % (SC_WINDOW * workers) == 0, n
    return n // SC_WINDOW // workers


def _sc_dispatch(dest_win, x1p, n_rows):
    n = x1p.shape[0]
    per_worker = _sc_windows(n)

    @functools.partial(
        pl.kernel, mesh=_sc_mesh(),
        out_type=jax.ShapeDtypeStruct((n_rows, HALF), U32),
        scratch_types=[pltpu.VMEM((TOP_K, SC_WINDOW), jnp.int32),
                       pltpu.VMEM((SC_WINDOW, HALF), U32)],
        name="moe_dispatch_sc")
    def run(dest_hbm, x_hbm, xs_hbm, idx_v, rows_v):
        worker = lax.axis_index("s") * SC_CORES + lax.axis_index("c")

        @pl.loop(0, per_worker)
        def _(c):
            win = worker * per_worker + c
            pltpu.sync_copy(dest_hbm.at[win], idx_v)
            pltpu.sync_copy(x_hbm.at[pl.ds(pl.multiple_of(win * SC_WINDOW, SC_WINDOW),
                                           SC_WINDOW)], rows_v)
            for k in range(TOP_K):
                pltpu.sync_copy(rows_v, xs_hbm.at[idx_v.at[k]])

    return run(dest_win, x1p)


def _sc_gather(dest_win, ys, n):
    per_worker = _sc_windows(n)

    @functools.partial(
        pl.kernel, mesh=_sc_mesh(),
        out_type=jax.ShapeDtypeStruct((TOP_K, n, HALF), U32),
        scratch_types=[pltpu.VMEM((TOP_K, SC_WINDOW), jnp.int32),
                       pltpu.VMEM((SC_WINDOW, HALF), U32)],
        name="moe_gather_sc")
    def run(dest_hbm, ys_hbm, yt_hbm, idx_v, rows_v):
        worker = lax.axis_index("s") * SC_CORES + lax.axis_index("c")

        @pl.loop(0, per_worker)
        def _(c):
            win = worker * per_worker + c
            start = pl.multiple_of(win * SC_WINDOW, SC_WINDOW)
            pltpu.sync_copy(dest_hbm.at[win], idx_v)
            for k in range(TOP_K):
                pltpu.sync_copy(ys_hbm.at[idx_v.at[k]], rows_v)
                pltpu.sync_copy(rows_v, yt_hbm.at[k, pl.ds(start, SC_WINDOW)])

    return run(dest_win, ys)


def _expert_kernel(used_ref, exp_ref, valid_ref, first_ref, slot_ref, next_ref, xs_ref, wg_hbm,
                   wu_hbm, wd_hbm, ys_ref, wg_buf, wu_buf, wd_buf, sem, *, bm, layer):
    w = pl.program_id(0)

    def weight_copies(expert, s):
        return [pltpu.make_async_copy(hbm.at[layer, expert], buf.at[s], sem.at[s, i])
                for i, (hbm, buf) in enumerate(((wg_hbm, wg_buf), (wu_hbm, wu_buf),
                                                (wd_hbm, wd_buf)))]

    @pl.when(w < used_ref[0])
    def _():
        s = slot_ref[w]

        @pl.when(w == 0)
        def _():
            for c in weight_copies(exp_ref[0], 0):
                c.start()

        @pl.when(first_ref[w] == 1)
        def _():
            for c in weight_copies(exp_ref[w], s):
                c.wait()

            @pl.when(next_ref[w] >= 0)
            def _():
                for c in weight_copies(next_ref[w], 1 - s):
                    c.start()

        rows = lax.broadcasted_iota(jnp.int32, (bm, 1), 0)
        x_lo, x_hi = _unpack_halves(jnp.where(rows < valid_ref[w], xs_ref[...], U32(0)))
        x = jnp.concatenate([x_lo, x_hi], axis=1).astype(BF16)
        g = jnp.dot(x, wg_buf[s].astype(BF16), preferred_element_type=F32)
        u = jnp.dot(x, wu_buf[s].astype(BF16), preferred_element_type=F32)
        h = (g * jax.nn.sigmoid(g) * u).astype(BF16)
        ys_ref[...] = _pack_halves(jnp.dot(h, wd_buf[s].astype(BF16),
                                           preferred_element_type=F32))


def _experts(layer, tiles, xs, wg, wu, wd, bm):
    r = xs.shape[0]
    last = lambda w, used, *_: (jnp.minimum(w, used[0] - 1), 0)
    grid_spec = pltpu.PrefetchScalarGridSpec(
        num_scalar_prefetch=len(tiles),
        grid=(r // bm,),
        in_specs=[pl.BlockSpec((bm, HALF), last)] + [pl.BlockSpec(memory_space=pl.ANY)] * 3,
        out_specs=pl.BlockSpec((bm, HALF), last),
        scratch_shapes=[pltpu.VMEM((2, D_MODEL, EXPERT_FF), F32),
                        pltpu.VMEM((2, D_MODEL, EXPERT_FF), F32),
                        pltpu.VMEM((2, EXPERT_FF, D_MODEL), F32),
                        pltpu.SemaphoreType.DMA((2, 3))],
    )
    return pl.pallas_call(
        functools.partial(_expert_kernel, bm=bm, layer=layer),
        grid_spec=grid_spec,
        out_shape=jax.ShapeDtypeStruct((r, HALF), U32),
        compiler_params=_cp(("arbitrary",)),
        name="moe_experts",
    )(*tiles, xs, wg, wu, wd)


def _expert_tiles(counts, n_tiles, bm):
    padded = (counts + bm - 1) // bm * bm
    pad_end = jnp.cumsum(padded)
    pad_start = pad_end - padded
    tile = jnp.arange(n_tiles, dtype=jnp.int32)
    row0 = tile * bm
    exp = jnp.minimum(jnp.sum((pad_end[None, :] <= row0[:, None]).astype(jnp.int32), axis=1),
                      N_EXPERTS - 1)
    valid = jnp.clip(pad_start[exp] + counts[exp] - row0, 0, bm)
    n_used = (pad_end[-1:] // bm).astype(jnp.int32)
    first = jnp.concatenate([jnp.ones((1,), jnp.int32),
                             (exp[1:] != exp[:-1]).astype(jnp.int32)])
    slot = (jnp.cumsum(first) - 1) % 2
    next_tile = pad_end[exp] // bm
    nxt = jnp.where(next_tile < n_used[0], exp[jnp.minimum(next_tile, n_tiles - 1)], -1)
    i32 = lambda a: a.astype(jnp.int32)
    return i32(pad_start), (n_used, i32(exp), i32(valid), first, i32(slot), i32(nxt))


def _combine_rows_kernel(yt_ref, tw_ref, x1_ref, x1b_ref, wsg_ref, wsu_ref, wsd_ref, lg_ref, lb_ref,
                         o_ref, ob_ref):
    tw = tw_ref[...]
    r_lo = jnp.zeros((tw.shape[0], HALF), F32)
    r_hi = jnp.zeros((tw.shape[0], HALF), F32)
    for k in range(TOP_K):
        y_lo, y_hi = _unpack_halves(yt_ref[k])
        r_lo = r_lo + tw[:, k:k + 1] * y_lo
        r_hi = r_hi + tw[:, k:k + 1] * y_hi
    routed = jnp.concatenate([r_lo, r_hi], axis=1)
    xb = x1b_ref[...]
    g = jnp.dot(xb, wsg_ref[...], preferred_element_type=F32)
    u = jnp.dot(xb, wsu_ref[...], preferred_element_type=F32)
    h = (g * jax.nn.sigmoid(g) * u).astype(BF16)
    shared = jnp.dot(h, wsd_ref[...], preferred_element_type=F32)
    y = _layer_norm(DN_ALPHA * x1_ref[...] + (shared + routed), lg_ref[...], lb_ref[...])
    o_ref[...] = y
    ob_ref[...] = y.astype(BF16)


def _combine_rows(yt, tw, x1, x1b, wsg, wsu, wsd, lg, lb, tc=512):
    n = x1.shape[0]
    tc = min(tc, n)
    full = lambda a: pl.BlockSpec(a.shape, lambda i: (0,) * a.ndim)
    rows = lambda w: pl.BlockSpec((tc, w), lambda i: (i, 0))
    return pl.pallas_call(
        _combine_rows_kernel,
        grid=(n // tc,),
        in_specs=[pl.BlockSpec((TOP_K, tc, HALF), lambda i: (0, i, 0)),
                  rows(TOP_K), rows(D_MODEL), rows(D_MODEL),
                  full(wsg), full(wsu), full(wsd), full(lg), full(lb)],
        out_specs=[rows(D_MODEL), rows(D_MODEL)],
        out_shape=[jax.ShapeDtypeStruct((n, D_MODEL), F32),
                   jax.ShapeDtypeStruct((n, D_MODEL), BF16)],
        compiler_params=_cp(("arbitrary",)),
        name="moe_combine_ln",
    )(yt, tw, x1, x1b, wsg, wsu, wsd, lg, lb)


def _head_cols(w, heads, width, lo, hi):
    return w.reshape(w.shape[0], heads, width)[:, :, lo:hi]


def _prep_layer(w_in, b_gate, b_forget, mla_q_norm, w_uq, mla_kv_norm, w_ukv):
    f = lambda a: a.astype(BF16)
    half = MLA_ROPE_DIM // 2
    d = w_in.shape[0]
    qscale = HEAD_DIM ** -0.5
    w_qkv = jnp.concatenate([
        w_in[:, OFF_SB:OFF_SB + SB_W] * qscale, w_in[:, OFF_SB + SB_W:OFF_FOX],
        w_in[:, OFF_FOX:OFF_FOX + FOX_W] * (qscale * LOG2E), w_in[:, OFF_FOX + FOX_W:OFF_FGATE]],
        axis=1)
    w_kr = w_in[:, OFF_KR:OFF_GATE]
    w_kr_rot = jnp.concatenate([-w_kr[:, half:], w_kr[:, :half]], axis=1)
    z = lambda c: jnp.zeros((d, c), F32)
    pad_rope = LANES - MLA_QK_DIM
    w_small = jnp.concatenate([
        w_in[:, OFF_DQ:OFF_DKV], w_in[:, OFF_DKV:OFF_KR],
        z(MLA_NOPE_DIM), w_kr, z(pad_rope),
        z(MLA_NOPE_DIM), w_kr_rot, z(pad_rope),
        w_in[:, OFF_FGATE:OFF_DQ], z(LANES - FOX_HEADS)], axis=1)
    w_gate = w_in[:, OFF_GATE:]

    r = w_uq.shape[0]
    q_nope = _head_cols(w_uq, MLA_HEADS, MLA_QK_DIM, 0, MLA_NOPE_DIM)
    q_rope = _head_cols(w_uq, MLA_HEADS, MLA_QK_DIM, MLA_NOPE_DIM, MLA_QK_DIM)
    q_rope_rot = jnp.concatenate([-q_rope[:, :, half:], q_rope[:, :, :half]], axis=2)
    zq = lambda c: jnp.zeros((r, MLA_HEADS, c), F32)
    wqa = jnp.concatenate([q_nope, q_rope, zq(pad_rope)], axis=2).reshape(r, MLA_HEADS * LANES)
    wqb = jnp.concatenate([zq(MLA_NOPE_DIM), q_rope_rot, zq(pad_rope)], axis=2).reshape(
        r, MLA_HEADS * LANES)
    rk = w_ukv.shape[0]
    kvw = MLA_NOPE_DIM + MLA_V_DIM
    k_nope = _head_cols(w_ukv, MLA_HEADS, kvw, 0, MLA_NOPE_DIM)
    wk = jnp.concatenate([k_nope, jnp.zeros((rk, MLA_HEADS, LANES - MLA_NOPE_DIM), F32)],
                         axis=2).reshape(rk, MLA_HEADS * LANES)
    wv = _head_cols(w_ukv, MLA_HEADS, kvw, MLA_NOPE_DIM, kvw).reshape(rk, MLA_HEADS * MLA_V_DIM)
    bf = jnp.concatenate([b_forget, jnp.zeros((LANES - FOX_HEADS,), F32)]).reshape(1, LANES)
    return dict(w_qkv=f(w_qkv), w_small=f(w_small), w_gate=f(w_gate),
                b_gate=b_gate.reshape(1, -1), wqa=f(wqa), wqb=f(wqb), wk=f(wk), wv=f(wv),
                qn=mla_q_norm.reshape(1, -1), kvn=mla_kv_norm.reshape(1, -1), bf=bf)


def _rope_tables(seq):
    half = MLA_ROPE_DIM // 2
    inv_freq = jnp.power(ROPE_BASE, -jnp.arange(half, dtype=F32) / half)
    ang = jnp.arange(seq).astype(F32)[:, None] * inv_freq[None, :]
    cos = jnp.concatenate([jnp.cos(ang), jnp.cos(ang)], axis=1)
    sin = jnp.concatenate([jnp.sin(ang), jnp.sin(ang)], axis=1)
    pad = lambda t: jnp.concatenate([jnp.zeros((seq, MLA_NOPE_DIM), F32), t,
                                     jnp.zeros((seq, LANES - MLA_QK_DIM), F32)], axis=1)
    return pad(cos), pad(sin)


EXPERT_BM = 512


def kernel(x, ln1_g, ln1_b, ln2_g, ln2_b, w_in, b_gate, b_forget, mla_q_norm, w_uq, mla_kv_norm,
           w_ukv, w_proj_sb, w_proj_fox, w_proj_mla, w_out, w_router, router_bias,
           w_exp_gate, w_exp_up, w_exp_down, w_sh_gate, w_sh_up, w_sh_down):
    b, seq, d = x.shape
    n = b * seq
    depth = w_in.shape[0]
    cos_t, sin_t = _rope_tables(seq)
    xf = x.reshape(n, d)
    xb = xf.astype(BF16)
    f = lambda a: a.astype(BF16)
    bm = min(EXPERT_BM, n * TOP_K)
    n_tiles = n * TOP_K // bm + N_EXPERTS
    for l in range(depth):
        p = _prep_layer(w_in[l], b_gate[l], b_forget[l], mla_q_norm[l], w_uq[l], mla_kv_norm[l],
                        w_ukv[l])
        qkv = _matmul(xb, p["w_qkv"], BF16, tn=QKV_W // 2)
        q_mla, k_mla, v_mla, lf = _mla_prep(xb, p["w_small"], p["qn"], p["kvn"], p["wqa"],
                                            p["wqb"], p["wk"], p["wv"], cos_t, sin_t, p["bf"], seq)
        cum, cumt = _cumsum(lf.reshape(b, seq, LANES))
        cumt4 = cumt[:, :FOX_HEADS, :].reshape(b, FOX_HEADS, 1, seq)
        qkv3 = qkv.reshape(b, seq, QKV_W)
        o_sb = _sb_attention(qkv3).reshape(n, SB_W)
        o_fox = _fox_attention(qkv3, cum, cumt4).reshape(n, FOX_W)
        o_mla = _mla_attention(q_mla.reshape(b, seq, -1), k_mla.reshape(b, seq, -1),
                               v_mla.reshape(b, seq, -1)).reshape(n, -1)
        x1, x1b, x1p = _merge(o_sb, o_fox, o_mla, xb, xf, p["w_gate"], p["b_gate"],
                              f(w_proj_sb[l]), f(w_proj_fox[l]), f(w_proj_mla[l]), f(w_out[l]),
                              ln1_g[l].reshape(1, d), ln1_b[l].reshape(1, d))

        idx_t, tw_t, rank_t, cnt = _route(x1b, f(w_router[l].T), router_bias[l].reshape(-1, 1))
        counts = cnt[:, 0].astype(jnp.int32)
        starts, tiles = _expert_tiles(counts, n_tiles, bm)
        dest_t = _dest(idx_t, rank_t, starts)
        dest_win = dest_t.reshape(TOP_K, n // SC_WINDOW, SC_WINDOW).transpose(1, 0, 2)
        xs = _sc_dispatch(dest_win, x1p, n_tiles * bm)
        ys = _experts(l, tiles, xs, w_exp_gate, w_exp_up, w_exp_down, bm)
        yt = _sc_gather(dest_win, ys, n)
        xf, xb = _combine_rows(yt, tw_t.T, x1, x1b, f(w_sh_gate[l]), f(w_sh_up[l]),
                               f(w_sh_down[l]), ln2_g[l].reshape(1, d), ln2_b[l].reshape(1, d))
    return xf.reshape(b, seq, d)
```

```python
import functools

import jax
import jax.numpy as jnp
from jax import lax
from jax.experimental import pallas as pl
from jax.experimental.pallas import tpu as pltpu
from jax.experimental.pallas import tpu_sc as plsc

F32 = jnp.float32
BF16 = jnp.bfloat16

D_MODEL = 1024
HEAD_DIM = 64
SB_HEADS = 4
FOX_HEADS = 4
MLA_HEADS = 8
MLA_Q_RANK = 256
MLA_KV_RANK = 128
MLA_NOPE_DIM = 64
MLA_ROPE_DIM = 32
MLA_V_DIM = 64
ROPE_BASE = 10000.0
N_BRANCHES = 3
N_EXPERTS = 256
TOP_K = 8
N_GROUPS = 8
TOPK_GROUPS = 4
GROUP_SIZE = N_EXPERTS // N_GROUPS
EXPERT_FF = 256
SHARED_FF = 256
ROUTED_SCALE = 2.5
CHUNK = 64
LN_EPS = 1e-5
RMS_EPS = 1e-6
DEPTH = 2
DN_ALPHA = (2 * DEPTH) ** 0.25

SB_W = SB_HEADS * HEAD_DIM
FOX_W = FOX_HEADS * HEAD_DIM
MLA_QK_DIM = MLA_NOPE_DIM + MLA_ROPE_DIM
OFF_SB = 0
OFF_FOX = OFF_SB + 3 * SB_W
OFF_FGATE = OFF_FOX + 3 * FOX_W
OFF_DQ = OFF_FGATE + FOX_HEADS
OFF_DKV = OFF_DQ + MLA_Q_RANK
OFF_KR = OFF_DKV + MLA_KV_RANK
OFF_GATE = OFF_KR + MLA_ROPE_DIM

LANES = 128
SUBLANES = 8
QKV_W = 3 * SB_W + 3 * FOX_W
SMALL_W = MLA_Q_RANK + MLA_KV_RANK + 3 * LANES
SB_CUTOFF = -104.0
LOG2E = 1.4426950408889634

VMEM_LIMIT = 48 * 1024 * 1024


def _cp(sem, vmem=VMEM_LIMIT):
    return pltpu.CompilerParams(dimension_semantics=sem, vmem_limit_bytes=vmem)


def _mm_kernel(x_ref, w_ref, o_ref, *, sub):
    x = x_ref[...]
    for c in range(0, o_ref.shape[1], sub):
        acc = jnp.dot(x, w_ref[:, c:c + sub], preferred_element_type=F32)
        o_ref[:, c:c + sub] = acc.astype(o_ref.dtype)


def _matmul(x, w, out_dtype, tn, tm=1024, sub=256):
    n, k = x.shape
    c = w.shape[1]
    tm = min(tm, n)
    return pl.pallas_call(
        functools.partial(_mm_kernel, sub=sub),
        grid=(n // tm, c // tn),
        in_specs=[pl.BlockSpec((tm, k), lambda i, j: (i, 0)),
                  pl.BlockSpec((k, tn), lambda i, j: (0, j))],
        out_specs=pl.BlockSpec((tm, tn), lambda i, j: (i, j)),
        out_shape=jax.ShapeDtypeStruct((n, c), out_dtype),
        compiler_params=_cp(("arbitrary", "arbitrary")),
        name="proj_matmul",
    )(x, w)


def _split_bf16(x, parts):
    out = []
    for _ in range(parts):
        h = x.astype(BF16)
        out.append(h)
        x = x - h.astype(F32)
    return out


def _mla_prep_kernel(x_ref, ws_ref, qn_ref, kvn_ref, wqa_ref, wqb_ref, wk_ref, wv_ref,
                     cos_ref, sin_ref, bf_ref, q_ref, k_ref, v_ref, lf_ref):
    sm = jnp.dot(x_ref[...], ws_ref[...], preferred_element_type=F32)
    dq = sm[:, :MLA_Q_RANK]
    dkv = sm[:, MLA_Q_RANK:MLA_Q_RANK + MLA_KV_RANK]
    o = MLA_Q_RANK + MLA_KV_RANK
    kr = sm[:, o:o + LANES]
    kr_rot = sm[:, o + LANES:o + 2 * LANES]
    fg = sm[:, o + 2 * LANES:o + 3 * LANES]

    cq = dq * lax.rsqrt(jnp.mean(dq * dq, axis=-1, keepdims=True) + RMS_EPS) * qn_ref[...]
    ckv = dkv * lax.rsqrt(jnp.mean(dkv * dkv, axis=-1, keepdims=True) + RMS_EPS) * kvn_ref[...]
    cq = cq.astype(BF16)
    ckv = ckv.astype(BF16)

    cosk = cos_ref[...]
    sink = sin_ref[...]
    lane = lax.broadcasted_iota(jnp.int32, (1, LANES), 1)
    nope = (lane < MLA_NOPE_DIM).astype(F32)
    scale = MLA_QK_DIM ** -0.5 * LOG2E
    cq_tab = jnp.concatenate([(cosk + nope) * scale] * MLA_HEADS, axis=1)
    sq_tab = jnp.concatenate([sink * scale] * MLA_HEADS, axis=1)

    qa = jnp.dot(cq, wqa_ref[...], preferred_element_type=F32)
    qb = jnp.dot(cq, wqb_ref[...], preferred_element_type=F32)
    q_ref[...] = (qa * cq_tab + qb * sq_tab).astype(q_ref.dtype)

    k_rope = kr * cosk + kr_rot * sink
    ka = jnp.dot(ckv, wk_ref[...], preferred_element_type=F32)
    k_ref[...] = (ka + jnp.concatenate([k_rope] * MLA_HEADS, axis=1)).astype(k_ref.dtype)
    v_ref[...] = jnp.dot(ckv, wv_ref[...], preferred_element_type=F32).astype(v_ref.dtype)
    lf_ref[...] = jax.nn.log_sigmoid(fg + bf_ref[...])


def _mla_prep(xb, w_small, qn, kvn, wqa, wqb, wk, wv, cos_t, sin_t, bf, seq, tm=512):
    n, d = xb.shape
    tm = min(tm, seq)
    sblocks = seq // tm
    hw = MLA_HEADS * LANES
    full = lambda a: pl.BlockSpec(a.shape, lambda i: (0,) * a.ndim)
    return pl.pallas_call(
        _mla_prep_kernel,
        grid=(n // tm,),
        in_specs=[pl.BlockSpec((tm, d), lambda i: (i, 0)), full(w_small),
                  full(qn), full(kvn), full(wqa), full(wqb), full(wk), full(wv),
                  pl.BlockSpec((tm, LANES), lambda i: (i % sblocks, 0)),
                  pl.BlockSpec((tm, LANES), lambda i: (i % sblocks, 0)),
                  full(bf)],
        out_specs=[pl.BlockSpec((tm, hw), lambda i: (i, 0)),
                   pl.BlockSpec((tm, hw), lambda i: (i, 0)),
                   pl.BlockSpec((tm, MLA_HEADS * MLA_V_DIM), lambda i: (i, 0)),
                   pl.BlockSpec((tm, LANES), lambda i: (i, 0))],
        out_shape=[jax.ShapeDtypeStruct((n, hw), BF16),
                   jax.ShapeDtypeStruct((n, hw), BF16),
                   jax.ShapeDtypeStruct((n, MLA_HEADS * MLA_V_DIM), BF16),
                   jax.ShapeDtypeStruct((n, LANES), F32)],
        compiler_params=_cp(("arbitrary",)),
        name="mla_prep",
    )(xb, w_small, qn, kvn, wqa, wqb, wk, wv, cos_t, sin_t, bf)


def _cumsum_kernel(lf_ref, cum_ref, cumt_ref, *, seq):
    r = lax.broadcasted_iota(jnp.int32, (LANES, LANES), 0)
    c = lax.broadcasted_iota(jnp.int32, (LANES, LANES), 1)
    lower = (c <= r).astype(BF16)

    def body(j, carry):
        start = pl.multiple_of(j * LANES, LANES)
        blk = lf_ref[0, pl.ds(start, LANES), :]
        acc = carry
        for part in _split_bf16(blk, 3):
            acc = acc + jnp.dot(lower, part, preferred_element_type=F32)
        scaled = acc * LOG2E
        cum_ref[0, pl.ds(start, LANES), :] = scaled
        cumt_ref[0, :, pl.ds(start, LANES)] = scaled.T[:SUBLANES, :]
        return jnp.broadcast_to(acc[LANES - 1:LANES, :], (LANES, LANES))

    lax.fori_loop(0, seq // LANES, body, jnp.zeros((LANES, LANES), F32))


def _cumsum(lf3):
    b, seq, _ = lf3.shape
    return pl.pallas_call(
        functools.partial(_cumsum_kernel, seq=seq),
        grid=(b,),
        in_specs=[pl.BlockSpec((1, seq, LANES), lambda i: (i, 0, 0))],
        out_specs=[pl.BlockSpec((1, seq, LANES), lambda i: (i, 0, 0)),
                   pl.BlockSpec((1, SUBLANES, seq), lambda i: (i, 0, 0))],
        out_shape=[jax.ShapeDtypeStruct((b, seq, LANES), F32),
                   jax.ShapeDtypeStruct((b, SUBLANES, seq), F32)],
        compiler_params=_cp(("arbitrary",)),
        name="forget_cumsum",
    )(lf3)


_NT = (((1,), (1,)), ((), ()))
NEG_INIT = -1e30
_TN = (((0,), (0,)), ((), ()))
ATTN_TK = 512
SB_TK = 256


def _head_halves(q):
    lane = lax.broadcasted_iota(jnp.int32, (1, LANES), 1)
    zero = jnp.zeros_like(q)
    return [jnp.where(lane < HEAD_DIM, q, zero), jnp.where(lane >= HEAD_DIM, q, zero)]


def _mask_first_block(x, keep, fill, tk):
    head = jnp.where(keep, x[:, :tk], fill)
    return head if x.shape[1] == tk else jnp.concatenate([head, x[:, tk:]], axis=1)


def _fox_kernel(q_ref, k_ref, v_ref, cc_ref, cr_ref, o_ref, *, seq, tk):
    hp = pl.program_id(1)
    lane = lax.broadcasted_iota(jnp.int32, (1, LANES), 1)
    qs = _head_halves(q_ref[0])
    key = lax.broadcasted_iota(jnp.int32, (tk, tk), 0)
    qry = lax.broadcasted_iota(jnp.int32, (tk, tk), 1)
    causal = key <= qry
    carry = [(jnp.full((1, seq), NEG_INIT, F32), jnp.zeros((1, seq), F32),
              jnp.zeros((HEAD_DIM, seq), F32)) for _ in range(2)]
    for j in range(seq // tk):
        q0 = j * tk
        k = k_ref[0, q0:q0 + tk, :]
        cc = cc_ref[0, q0:q0 + tk, :]
        for hh in range(2):
            m, l, acc = carry[hh]
            v = v_ref[0, q0:q0 + tk, hh * HEAD_DIM:(hh + 1) * HEAD_DIM]
            cs = jnp.sum(jnp.where(lane == 2 * hp + hh, cc, 0.0), axis=1, keepdims=True)
            ct = cr_ref[0, hh, :, q0:]
            st = lax.dot_general(k, qs[hh][q0:, :], _NT, preferred_element_type=F32)
            st = _mask_first_block((st + ct) - cs, causal, -jnp.inf, tk)
            mn, ln, an = _softmax_step_t(st, m[:, q0:], l[:, q0:], acc[:, q0:], v)
            if q0:
                mn = jnp.concatenate([m[:, :q0], mn], axis=1)
                ln = jnp.concatenate([l[:, :q0], ln], axis=1)
                an = jnp.concatenate([acc[:, :q0], an], axis=1)
            carry[hh] = (mn, ln, an)
    _pair_out_t([carry[0][2], carry[1][2]], [carry[0][1], carry[1][1]], o_ref)


def _fox_attention(qkv3, cum, cumt4):
    b, seq, _ = qkv3.shape
    tk = min(ATTN_TK, seq)
    hp = FOX_HEADS // 2
    qoff = OFF_FOX // LANES
    return pl.pallas_call(
        functools.partial(_fox_kernel, seq=seq, tk=tk),
        grid=(b, hp),
        in_specs=[pl.BlockSpec((1, seq, LANES), lambda bi, h: (bi, 0, qoff + h)),
                  pl.BlockSpec((1, seq, LANES), lambda bi, h: (bi, 0, qoff + hp + h)),
                  pl.BlockSpec((1, seq, LANES), lambda bi, h: (bi, 0, qoff + 2 * hp + h)),
                  pl.BlockSpec((1, seq, LANES), lambda bi, h: (bi, 0, 0)),
                  pl.BlockSpec((1, 2, 1, seq), lambda bi, h: (bi, h, 0, 0))],
        out_specs=pl.BlockSpec((1, seq, LANES), lambda bi, h: (bi, 0, h)),
        out_shape=jax.ShapeDtypeStruct((b, seq, FOX_W), BF16),
        compiler_params=_cp(("arbitrary", "arbitrary")),
        name="fox_attention",
    )(qkv3, qkv3, qkv3, cum, cumt4)


def _softmax_step_t(st, m, l, acc, v):
    m_new = jnp.maximum(m, jnp.max(st, axis=0, keepdims=True))
    alpha = jnp.exp2(m - m_new)
    p = jnp.exp2(st - m_new)
    l = alpha * l + jnp.sum(p, axis=0, keepdims=True)
    acc = alpha * acc + lax.dot_general(v, p.astype(BF16), _TN, preferred_element_type=F32)
    return m_new, l, acc


def _pair_out_t(accs, ls, o_ref, pair=0):
    ot = jnp.concatenate([accs[0] / ls[0], accs[1] / ls[1]], axis=0)
    o_ref[0, :, pair * LANES:(pair + 1) * LANES] = ot.T.astype(o_ref.dtype)


MLA_STEP_HEADS = 2


def _mla_kernel(q_ref, k_ref, v_ref, o_ref, *, seq, tk):
    nh = MLA_STEP_HEADS
    key = lax.broadcasted_iota(jnp.int32, (tk, tk), 0)
    qry = lax.broadcasted_iota(jnp.int32, (tk, tk), 1)
    shift = CHUNK.bit_length() - 1
    chunk_causal = (key >> shift) <= (qry >> shift)
    carry = [(jnp.full((1, seq), NEG_INIT, F32), jnp.zeros((1, seq), F32),
              jnp.zeros((MLA_V_DIM, seq), F32)) for _ in range(nh)]
    for j in range(seq // tk):
        q0 = j * tk
        for hh in range(nh):
            m, l, acc = carry[hh]
            k = k_ref[0, q0:q0 + tk, hh * LANES:(hh + 1) * LANES]
            v = v_ref[0, q0:q0 + tk, hh * MLA_V_DIM:(hh + 1) * MLA_V_DIM]
            st = lax.dot_general(k, q_ref[0, q0:, hh * LANES:(hh + 1) * LANES], _NT,
                                 preferred_element_type=F32)
            st = _mask_first_block(st, chunk_causal, -jnp.inf, tk)
            mn, ln, an = _softmax_step_t(st, m[:, q0:], l[:, q0:], acc[:, q0:], v)
            if q0:
                mn = jnp.concatenate([m[:, :q0], mn], axis=1)
                ln = jnp.concatenate([l[:, :q0], ln], axis=1)
                an = jnp.concatenate([acc[:, :q0], an], axis=1)
            carry[hh] = (mn, ln, an)
    for pair in range(nh // 2):
        a, b = carry[2 * pair], carry[2 * pair + 1]
        _pair_out_t([a[2], b[2]], [a[1], b[1]], o_ref, pair)


def _mla_attention(q3, k3, v3):
    b, seq, _ = q3.shape
    tk = min(ATTN_TK, seq)
    nh = MLA_STEP_HEADS
    return pl.pallas_call(
        functools.partial(_mla_kernel, seq=seq, tk=tk),
        grid=(b, MLA_HEADS // nh),
        in_specs=[pl.BlockSpec((1, seq, nh * LANES), lambda bi, h: (bi, 0, h)),
                  pl.BlockSpec((1, seq, nh * LANES), lambda bi, h: (bi, 0, h)),
                  pl.BlockSpec((1, seq, nh // 2 * LANES), lambda bi, h: (bi, 0, h))],
        out_specs=pl.BlockSpec((1, seq, nh // 2 * LANES), lambda bi, h: (bi, 0, h)),
        out_shape=jax.ShapeDtypeStruct((b, seq, MLA_HEADS * MLA_V_DIM), BF16),
        compiler_params=_cp(("arbitrary", "arbitrary")),
        name="mla_attention",
    )(q3, k3, v3)


def _softplus(z):
    return jnp.maximum(z, 0.0) + jnp.log(1.0 + jnp.exp(-jnp.abs(z)))


def _sb_kernel(q_ref, k_ref, v_ref, o_ref, run_ref, acc_ref, *, seq, tk):
    n = seq // tk
    qs = _head_halves(q_ref[0])
    key = lax.broadcasted_iota(jnp.int32, (tk, tk), 0)
    qry = lax.broadcasted_iota(jnp.int32, (tk, tk), 1)
    strict = key < qry
    later = (qry > key).astype(BF16)

    def unit(j, qlo, qhi, diag):
        k = k_ref[0, j * tk:(j + 1) * tk, :]
        for hh in range(2):
            v = v_ref[0, j * tk:(j + 1) * tk, hh * HEAD_DIM:(hh + 1) * HEAD_DIM]
            z = lax.dot_general(k, qs[hh][qlo:qhi, :], _NT, preferred_element_type=F32)
            sp = _softplus(z)
            log_keep = -sp
            if diag:
                log_keep = _mask_first_block(log_keep, strict, 0.0, tk)
            between = run_ref[hh, :, qlo:qhi]
            for part in _split_bf16(log_keep, 2):
                between = between + jnp.dot(later, part, preferred_element_type=F32)
            w = jnp.exp((z - sp) + between)
            if diag:
                w = _mask_first_block(w, strict, 0.0, tk)
            acc_ref[hh, :, qlo:qhi] += lax.dot_general(v, w.astype(BF16), _TN,
                                                       preferred_element_type=F32)
            run_ref[hh, :, qlo:qhi] += jnp.sum(log_keep, axis=0, keepdims=True)

    run_ref[...] = jnp.zeros_like(run_ref)
    acc_ref[...] = jnp.zeros_like(acc_ref)
    for j in range(n - 1, -1, -1):
        unit(j, j * tk, min((j + 2) * tk, seq), True)

    for j in range(n - 3, -1, -1):
        qlo = (j + 2) * tk

        @pl.when(jnp.max(run_ref[:, :, qlo:]) >= SB_CUTOFF)
        def _():
            unit(j, qlo, seq, False)

    o_ref[0] = jnp.concatenate([acc_ref[0], acc_ref[1]], axis=0).T.astype(o_ref.dtype)


def _sb_attention(qkv3):
    b, seq, _ = qkv3.shape
    tk = min(SB_TK, seq)
    hp = SB_HEADS // 2
    qoff = OFF_SB // LANES
    return pl.pallas_call(
        functools.partial(_sb_kernel, seq=seq, tk=tk),
        grid=(b, hp),
        in_specs=[pl.BlockSpec((1, seq, LANES), lambda bi, h: (bi, 0, qoff + h)),
                  pl.BlockSpec((1, seq, LANES), lambda bi, h: (bi, 0, qoff + hp + h)),
                  pl.BlockSpec((1, seq, LANES), lambda bi, h: (bi, 0, qoff + 2 * hp + h))],
        out_specs=pl.BlockSpec((1, seq, LANES), lambda bi, h: (bi, 0, h)),
        out_shape=jax.ShapeDtypeStruct((b, seq, SB_W), BF16),
        scratch_shapes=[pltpu.VMEM((2, 1, seq), F32), pltpu.VMEM((2, HEAD_DIM, seq), F32)],
        compiler_params=_cp(("arbitrary", "arbitrary")),
        name="sb_attention",
    )(qkv3, qkv3, qkv3)


def _layer_norm(y, g, b):
    yc = y - jnp.mean(y, axis=-1, keepdims=True)
    var = jnp.mean(yc * yc, axis=-1, keepdims=True)
    return yc * lax.rsqrt(var + LN_EPS) * g + b


HALF = D_MODEL // 2
U32 = jnp.uint32


def _pack_halves(y):
    bits = lax.bitcast_convert_type(y.astype(BF16).astype(F32), U32)
    return (bits[:, :HALF] >> 16) | (bits[:, HALF:] & U32(0xFFFF0000))


def _unpack_halves(w):
    lo = lax.bitcast_convert_type(w << 16, F32)
    hi = lax.bitcast_convert_type(w & U32(0xFFFF0000), F32)
    return lo, hi


def _merge_kernel(osb_ref, ofox_ref, omla_ref, xb_ref, x_ref, wg_ref, bg_ref, wsb_ref, wfox_ref,
                  wmla_ref, wout_ref, lg_ref, lb_ref, x1_ref, x1b_ref, x1p_ref):
    xb = xb_ref[...]
    mixed = None
    branches = ((osb_ref, wsb_ref), (ofox_ref, wfox_ref), (omla_ref, wmla_ref))
    for i, (o_ref, w_ref) in enumerate(branches):
        cols = slice(i * D_MODEL, (i + 1) * D_MODEL)
        gate = jax.nn.sigmoid(jnp.dot(xb, wg_ref[:, cols], preferred_element_type=F32)
                              + bg_ref[:, cols])
        term = gate * jnp.dot(o_ref[...], w_ref[...], preferred_element_type=F32)
        mixed = term if mixed is None else mixed + term
    mix = jnp.dot(mixed.astype(BF16), wout_ref[...], preferred_element_type=F32)
    y = _layer_norm(DN_ALPHA * x_ref[...] + mix, lg_ref[...], lb_ref[...])
    x1_ref[...] = y
    x1b_ref[...] = y.astype(BF16)
    x1p_ref[...] = _pack_halves(y)


def _merge(o_sb, o_fox, o_mla, xb, x, wg, bg, wsb, wfox, wmla, wout, lg, lb, tm=512):
    n = x.shape[0]
    tm = min(tm, n)
    full = lambda a: pl.BlockSpec(a.shape, lambda i: (0,) * a.ndim)
    rows = lambda w: pl.BlockSpec((tm, w), lambda i: (i, 0))
    return pl.pallas_call(
        _merge_kernel,
        grid=(n // tm,),
        in_specs=[rows(SB_W), rows(FOX_W), rows(MLA_HEADS * MLA_V_DIM), rows(D_MODEL),
                  rows(D_MODEL), full(wg), full(bg), full(wsb), full(wfox), full(wmla),
                  full(wout), full(lg), full(lb)],
        out_specs=[rows(D_MODEL), rows(D_MODEL), rows(HALF)],
        out_shape=[jax.ShapeDtypeStruct((n, D_MODEL), F32),
                   jax.ShapeDtypeStruct((n, D_MODEL), BF16),
                   jax.ShapeDtypeStruct((n, HALF), U32)],
        compiler_params=_cp(("arbitrary",)),
        name="merge_outproj_ln",
    )(o_sb, o_fox, o_mla, xb, x, wg, bg, wsb, wfox, wmla, wout, lg, lb)


def _route_kernel(x_ref, wr_ref, rb_ref, idx_ref, w_ref, rank_ref, cnt_ref, run_ref, *, tm):
    step = pl.program_id(0)

    @pl.when(step == 0)
    def _():
        run_ref[...] = jnp.zeros_like(run_ref)

    logits = lax.dot_general(wr_ref[...], x_ref[...], _NT, preferred_element_type=F32)
    scores = jax.nn.sigmoid(logits)
    biased = scores + rb_ref[...]
    e_iota = lax.broadcasted_iota(jnp.int32, (N_EXPERTS, tm), 0)
    big = jnp.int32(1 << 20)

    g_iota = lax.broadcasted_iota(jnp.int32, (GROUP_SIZE, tm), 0)
    gs_rows = []
    for g in range(N_GROUPS):
        blk = biased[g * GROUP_SIZE:(g + 1) * GROUP_SIZE, :]
        m1 = jnp.max(blk, axis=0, keepdims=True)
        first = jnp.min(jnp.where(blk == m1, g_iota, big), axis=0, keepdims=True)
        m2 = jnp.max(jnp.where(g_iota == first, -jnp.inf, blk), axis=0, keepdims=True)
        gs_rows.append(m1 + m2)
    gs = jnp.concatenate(gs_rows, axis=0)
    n_iota = lax.broadcasted_iota(jnp.int32, (N_GROUPS, tm), 0)
    keep = jnp.zeros((N_GROUPS, tm), jnp.bool_)
    for _ in range(TOPK_GROUPS):
        m = jnp.max(gs, axis=0, keepdims=True)
        first = jnp.min(jnp.where(gs == m, n_iota, big), axis=0, keepdims=True)
        hit = n_iota == first
        keep = jnp.logical_or(keep, hit)
        gs = jnp.where(hit, -jnp.inf, gs)
    keep_f = keep.astype(F32)
    expert_keep = jnp.concatenate(
        [jnp.broadcast_to(keep_f[g:g + 1, :], (GROUP_SIZE, tm)) for g in range(N_GROUPS)], axis=0)
    masked = jnp.where(expert_keep > 0.5, biased, -jnp.inf)

    idx_rows, w_rows, hits = [], [], []
    sel = jnp.zeros((N_EXPERTS, tm), F32)
    for _ in range(TOP_K):
        m = jnp.max(masked, axis=0, keepdims=True)
        first = jnp.min(jnp.where(masked == m, e_iota, big), axis=0, keepdims=True)
        hit = e_iota == first
        idx_rows.append(first)
        w_rows.append(jnp.sum(jnp.where(hit, scores, 0.0), axis=0, keepdims=True))
        hits.append(hit)
        sel = sel + hit.astype(F32)
        masked = jnp.where(hit, -jnp.inf, masked)
    w = jnp.concatenate(w_rows, axis=0)
    w = w / jnp.sum(w, axis=0, keepdims=True) * ROUTED_SCALE
    idx_ref[...] = jnp.concatenate(idx_rows, axis=0)
    w_ref[...] = w

    r = lax.broadcasted_iota(jnp.int32, (tm, tm), 0)
    c = lax.broadcasted_iota(jnp.int32, (tm, tm), 1)
    earlier = (r < c).astype(BF16)
    prefix = jnp.dot(sel.astype(BF16), earlier, preferred_element_type=F32) + run_ref[...]
    rank_rows = [jnp.sum(jnp.where(h, prefix, 0.0), axis=0, keepdims=True) for h in hits]
    rank_ref[...] = jnp.concatenate(rank_rows, axis=0).astype(jnp.int32)
    total = run_ref[...] + jnp.sum(sel, axis=1, keepdims=True)
    run_ref[...] = total
    cnt_ref[...] = jnp.broadcast_to(total, (N_EXPERTS, LANES))


def _route(x1b, wr_t, rb, tm=512):
    n = x1b.shape[0]
    tm = min(tm, n)
    return pl.pallas_call(
        functools.partial(_route_kernel, tm=tm),
        grid=(n // tm,),
        in_specs=[pl.BlockSpec((tm, D_MODEL), lambda i: (i, 0)),
                  pl.BlockSpec((N_EXPERTS, D_MODEL), lambda i: (0, 0)),
                  pl.BlockSpec((N_EXPERTS, 1), lambda i: (0, 0))],
        out_specs=[pl.BlockSpec((TOP_K, tm), lambda i: (0, i)),
                   pl.BlockSpec((TOP_K, tm), lambda i: (0, i)),
                   pl.BlockSpec((TOP_K, tm), lambda i: (0, i)),
                   pl.BlockSpec((N_EXPERTS, LANES), lambda i: (0, 0))],
        out_shape=[jax.ShapeDtypeStruct((TOP_K, n), jnp.int32),
                   jax.ShapeDtypeStruct((TOP_K, n), F32),
                   jax.ShapeDtypeStruct((TOP_K, n), jnp.int32),
                   jax.ShapeDtypeStruct((N_EXPERTS, LANES), F32)],
        scratch_shapes=[pltpu.VMEM((N_EXPERTS, 1), F32)],
        compiler_params=_cp(("arbitrary",)),
        name="router_topk",
    )(x1b, wr_t, rb)


def _dest_kernel(idx_ref, rank_ref, st_ref, dest_ref, *, tm):
    e_iota = lax.broadcasted_iota(jnp.int32, (N_EXPERTS, tm), 0)
    starts = st_ref[...]
    rows = []
    for k in range(TOP_K):
        hit = e_iota == idx_ref[k:k + 1, :]
        rows.append(jnp.sum(jnp.where(hit, starts, 0.0), axis=0, keepdims=True))
    dest_ref[...] = jnp.concatenate(rows, axis=0).astype(jnp.int32) + rank_ref[...]


def _dest(idx_t, rank_t, starts, tm=512):
    n = idx_t.shape[1]
    tm = min(tm, n)
    blk = lambda: pl.BlockSpec((TOP_K, tm), lambda i: (0, i))
    return pl.pallas_call(
        functools.partial(_dest_kernel, tm=tm),
        grid=(n // tm,),
        in_specs=[blk(), blk(), pl.BlockSpec((N_EXPERTS, 1), lambda i: (0, 0))],
        out_specs=blk(),
        out_shape=jax.ShapeDtypeStruct((TOP_K, n), jnp.int32),
        compiler_params=_cp(("arbitrary",)),
        name="moe_dest",
    )(idx_t, rank_t, starts.astype(F32).reshape(N_EXPERTS, 1))


SC_CORES = 2
SC_SUBCORES = 16
SC_WINDOW = 128

def _sc_mesh():
    return plsc.VectorSubcoreMesh(core_axis_name="c", subcore_axis_name="s",
                                  num_cores=SC_CORES, num_subcores=SC_SUBCORES)


def _sc_windows(n):
    workers = SC_CORES * SC_SUBCORES
    assert n % (SC_WINDOW * workers) == 0, n
    return n // SC_WINDOW // workers


def _sc_dispatch(dest_win, x1p, n_rows):
    n = x1p.shape[0]
    per_worker = _sc_windows(n)

    @functools.partial(
        pl.kernel, mesh=_sc_mesh(),
        out_type=jax.ShapeDtypeStruct((n_rows, HALF), U32),
        scratch_types=[pltpu.VMEM((TOP_K, SC_WINDOW), jnp.int32),
                       pltpu.VMEM((SC_WINDOW, HALF), U32)],
        name="moe_dispatch_sc")
    def run(dest_hbm, x_hbm, xs_hbm, idx_v, rows_v):
        worker = lax.axis_index("s") * SC_CORES + lax.axis_index("c")

        @pl.loop(0, per_worker)
        def _(c):
            win = worker * per_worker + c
            pltpu.sync_copy(dest_hbm.at[win], idx_v)
            pltpu.sync_copy(x_hbm.at[pl.ds(pl.multiple_of(win * SC_WINDOW, SC_WINDOW),
                                           SC_WINDOW)], rows_v)
            for k in range(TOP_K):
                pltpu.sync_copy(rows_v, xs_hbm.at[idx_v.at[k]])

    return run(dest_win, x1p)


def _sc_gather(dest_win, ys, n):
    per_worker = _sc_windows(n)

    @functools.partial(
        pl.kernel, mesh=_sc_mesh(),
        out_type=jax.ShapeDtypeStruct((TOP_K, n, HALF), U32),
        scratch_types=[pltpu.VMEM((TOP_K, SC_WINDOW), jnp.int32),
                       pltpu.VMEM((SC_WINDOW, HALF), U32)],
        name="moe_gather_sc")
    def run(dest_hbm, ys_hbm, yt_hbm, idx_v, rows_v):
        worker = lax.axis_index("s") * SC_CORES + lax.axis_index("c")

        @pl.loop(0, per_worker)
        def _(c):
            win = worker * per_worker + c
            start = pl.multiple_of(win * SC_WINDOW, SC_WINDOW)
            pltpu.sync_copy(dest_hbm.at[win], idx_v)
            for k in range(TOP_K):
                pltpu.sync_copy(ys_hbm.at[idx_v.at[k]], rows_v)
                pltpu.sync_copy(rows_v, yt_hbm.at[k, pl.ds(start, SC_WINDOW)])

    return run(dest_win, ys)


def _expert_kernel(used_ref, exp_ref, valid_ref, first_ref, slot_ref, next_ref, xs_ref, wg_hbm,
                   wu_hbm, wd_hbm, ys_ref, wg_buf, wu_buf, wd_buf, sem, *, bm, layer):
    w = pl.program_id(0)

    def weight_copies(expert, s):
        return [pltpu.make_async_copy(hbm.at[layer, expert], buf.at[s], sem.at[s, i])
                for i, (hbm, buf) in enumerate(((wg_hbm, wg_buf), (wu_hbm, wu_buf),
                                                (wd_hbm, wd_buf)))]

    @pl.when(w < used_ref[0])
    def _():
        s = slot_ref[w]

        @pl.when(w == 0)
        def _():
            for c in weight_copies(exp_ref[0], 0):
                c.start()

        @pl.when(first_ref[w] == 1)
        def _():
            for c in weight_copies(exp_ref[w], s):
                c.wait()

            @pl.when(next_ref[w] >= 0)
            def _():
                for c in weight_copies(next_ref[w], 1 - s):
                    c.start()

        rows = lax.broadcasted_iota(jnp.int32, (bm, 1), 0)
        x_lo, x_hi = _unpack_halves(jnp.where(rows < valid_ref[w], xs_ref[...], U32(0)))
        x = jnp.concatenate([x_lo, x_hi], axis=1).astype(BF16)
        g = jnp.dot(x, wg_buf[s].astype(BF16), preferred_element_type=F32)
        u = jnp.dot(x, wu_buf[s].astype(BF16), preferred_element_type=F32)
        h = (g * jax.nn.sigmoid(g) * u).astype(BF16)
        ys_ref[...] = _pack_halves(jnp.dot(h, wd_buf[s].astype(BF16),
                                           preferred_element_type=F32))


def _experts(layer, tiles, xs, wg, wu, wd, bm):
    r = xs.shape[0]
    last = lambda w, used, *_: (jnp.minimum(w, used[0] - 1), 0)
    grid_spec = pltpu.PrefetchScalarGridSpec(
        num_scalar_prefetch=len(tiles),
        grid=(r // bm,),
        in_specs=[pl.BlockSpec((bm, HALF), last)] + [pl.BlockSpec(memory_space=pl.ANY)] * 3,
        out_specs=pl.BlockSpec((bm, HALF), last),
        scratch_shapes=[pltpu.VMEM((2, D_MODEL, EXPERT_FF), F32),
                        pltpu.VMEM((2, D_MODEL, EXPERT_FF), F32),
                        pltpu.VMEM((2, EXPERT_FF, D_MODEL), F32),
                        pltpu.SemaphoreType.DMA((2, 3))],
    )
    return pl.pallas_call(
        functools.partial(_expert_kernel, bm=bm, layer=layer),
        grid_spec=grid_spec,
        out_shape=jax.ShapeDtypeStruct((r, HALF), U32),
        compiler_params=_cp(("arbitrary",)),
        name="moe_experts",
    )(*tiles, xs, wg, wu, wd)


def _expert_tiles(counts, n_tiles, bm):
    padded = (counts + bm - 1) // bm * bm
    pad_end = jnp.cumsum(padded)
    pad_start = pad_end - padded
    tile = jnp.arange(n_tiles, dtype=jnp.int32)
    row0 = tile * bm
    exp = jnp.minimum(jnp.sum((pad_end[None, :] <= row0[:, None]).astype(jnp.int32), axis=1),
                      N_EXPERTS - 1)
    valid = jnp.clip(pad_start[exp] + counts[exp] - row0, 0, bm)
    n_used = (pad_end[-1:] // bm).astype(jnp.int32)
    first = jnp.concatenate([jnp.ones((1,), jnp.int32),
                             (exp[1:] != exp[:-1]).astype(jnp.int32)])
    slot = (jnp.cumsum(first) - 1) % 2
    next_tile = pad_end[exp] // bm
    nxt = jnp.where(next_tile < n_used[0], exp[jnp.minimum(next_tile, n_tiles - 1)], -1)
    i32 = lambda a: a.astype(jnp.int32)
    return i32(pad_start), (n_used, i32(exp), i32(valid), first, i32(slot), i32(nxt))


def _combine_rows_kernel(yt_ref, tw_ref, x1_ref, x1b_ref, wsg_ref, wsu_ref, wsd_ref, lg_ref, lb_ref,
                         o_ref, ob_ref):
    tw = tw_ref[...]
    r_lo = jnp.zeros((tw.shape[0], HALF), F32)
    r_hi = jnp.zeros((tw.shape[0], HALF), F32)
    for k in range(TOP_K):
        y_lo, y_hi = _unpack_halves(yt_ref[k])
        r_lo = r_lo + tw[:, k:k + 1] * y_lo
        r_hi = r_hi + tw[:, k:k + 1] * y_hi
    routed = jnp.concatenate([r_lo, r_hi], axis=1)
    xb = x1b_ref[...]
    g = jnp.dot(xb, wsg_ref[...], preferred_element_type=F32)
    u = jnp.dot(xb, wsu_ref[...], preferred_element_type=F32)
    h = (g * jax.nn.sigmoid(g) * u).astype(BF16)
    shared = jnp.dot(h, wsd_ref[...], preferred_element_type=F32)
    y = _layer_norm(DN_ALPHA * x1_ref[...] + (shared + routed), lg_ref[...], lb_ref[...])
    o_ref[...] = y
    ob_ref[...] = y.astype(BF16)


def _combine_rows(yt, tw, x1, x1b, wsg, wsu, wsd, lg, lb, tc=512):
    n = x1.shape[0]
    tc = min(tc, n)
    full = lambda a: pl.BlockSpec(a.shape, lambda i: (0,) * a.ndim)
    rows = lambda w: pl.BlockSpec((tc, w), lambda i: (i, 0))
    return pl.pallas_call(
        _combine_rows_kernel,
        grid=(n // tc,),
        in_specs=[pl.BlockSpec((TOP_K, tc, HALF), lambda i: (0, i, 0)),
                  rows(TOP_K), rows(D_MODEL), rows(D_MODEL),
                  full(wsg), full(wsu), full(wsd), full(lg), full(lb)],
        out_specs=[rows(D_MODEL), rows(D_MODEL)],
        out_shape=[jax.ShapeDtypeStruct((n, D_MODEL), F32),
                   jax.ShapeDtypeStruct((n, D_MODEL), BF16)],
        compiler_params=_cp(("arbitrary",)),
        name="moe_combine_ln",
    )(yt, tw, x1, x1b, wsg, wsu, wsd, lg, lb)


def _head_cols(w, heads, width, lo, hi):
    return w.reshape(w.shape[0], heads, width)[:, :, lo:hi]


def _prep_layer(w_in, b_gate, b_forget, mla_q_norm, w_uq, mla_kv_norm, w_ukv):
    f = lambda a: a.astype(BF16)
    half = MLA_ROPE_DIM // 2
    d = w_in.shape[0]
    qscale = HEAD_DIM ** -0.5
    w_qkv = jnp.concatenate([
        w_in[:, OFF_SB:OFF_SB + SB_W] * qscale, w_in[:, OFF_SB + SB_W:OFF_FOX],
        w_in[:, OFF_FOX:OFF_FOX + FOX_W] * (qscale * LOG2E), w_in[:, OFF_FOX + FOX_W:OFF_FGATE]],
        axis=1)
    w_kr = w_in[:, OFF_KR:OFF_GATE]
    w_kr_rot = jnp.concatenate([-w_kr[:, half:], w_kr[:, :half]], axis=1)
    z = lambda c: jnp.zeros((d, c), F32)
    pad_rope = LANES - MLA_QK_DIM
    w_small = jnp.concatenate([
        w_in[:, OFF_DQ:OFF_DKV], w_in[:, OFF_DKV:OFF_KR],
        z(MLA_NOPE_DIM), w_kr, z(pad_rope),
        z(MLA_NOPE_DIM), w_kr_rot, z(pad_rope),
        w_in[:, OFF_FGATE:OFF_DQ], z(LANES - FOX_HEADS)], axis=1)
    w_gate = w_in[:, OFF_GATE:]

    r = w_uq.shape[0]
    q_nope = _head_cols(w_uq, MLA_HEADS, MLA_QK_DIM, 0, MLA_NOPE_DIM)
    q_rope = _head_cols(w_uq, MLA_HEADS, MLA_QK_DIM, MLA_NOPE_DIM, MLA_QK_DIM)
    q_rope_rot = jnp.concatenate([-q_rope[:, :, half:], q_rope[:, :, :half]], axis=2)
    zq = lambda c: jnp.zeros((r, MLA_HEADS, c), F32)
    wqa = jnp.concatenate([q_nope, q_rope, zq(pad_rope)], axis=2).reshape(r, MLA_HEADS * LANES)
    wqb = jnp.concatenate([zq(MLA_NOPE_DIM), q_rope_rot, zq(pad_rope)], axis=2).reshape(
        r, MLA_HEADS * LANES)
    rk = w_ukv.shape[0]
    kvw = MLA_NOPE_DIM + MLA_V_DIM
    k_nope = _head_cols(w_ukv, MLA_HEADS, kvw, 0, MLA_NOPE_DIM)
    wk = jnp.concatenate([k_nope, jnp.zeros((rk, MLA_HEADS, LANES - MLA_NOPE_DIM), F32)],
                         axis=2).reshape(rk, MLA_HEADS * LANES)
    wv = _head_cols(w_ukv, MLA_HEADS, kvw, MLA_NOPE_DIM, kvw).reshape(rk, MLA_HEADS * MLA_V_DIM)
    bf = jnp.concatenate([b_forget, jnp.zeros((LANES - FOX_HEADS,), F32)]).reshape(1, LANES)
    return dict(w_qkv=f(w_qkv), w_small=f(w_small), w_gate=f(w_gate),
                b_gate=b_gate.reshape(1, -1), wqa=f(wqa), wqb=f(wqb), wk=f(wk), wv=f(wv),
                qn=mla_q_norm.reshape(1, -1), kvn=mla_kv_norm.reshape(1, -1), bf=bf)


def _rope_tables(seq):
    half = MLA_ROPE_DIM // 2
    inv_freq = jnp.power(ROPE_BASE, -jnp.arange(half, dtype=F32) / half)
    ang = jnp.arange(seq).astype(F32)[:, None] * inv_freq[None, :]
    cos = jnp.concatenate([jnp.cos(ang), jnp.cos(ang)], axis=1)
    sin = jnp.concatenate([jnp.sin(ang), jnp.sin(ang)], axis=1)
    pad = lambda t: jnp.concatenate([jnp.zeros((seq, MLA_NOPE_DIM), F32), t,
                                     jnp.zeros((seq, LANES - MLA_QK_DIM), F32)], axis=1)
    return pad(cos), pad(sin)


EXPERT_BM = 512


def kernel(x, ln1_g, ln1_b, ln2_g, ln2_b, w_in, b_gate, b_forget, mla_q_norm, w_uq, mla_kv_norm,
           w_ukv, w_proj_sb, w_proj_fox, w_proj_mla, w_out, w_router, router_bias,
           w_exp_gate, w_exp_up, w_exp_down, w_sh_gate, w_sh_up, w_sh_down):
    b, seq, d = x.shape
    n = b * seq
    depth = w_in.shape[0]
    cos_t, sin_t = _rope_tables(seq)
    xf = x.reshape(n, d)
    xb = xf.astype(BF16)
    f = lambda a: a.astype(BF16)
    bm = min(EXPERT_BM, n * TOP_K)
    n_tiles = n * TOP_K // bm + N_EXPERTS
    for l in range(depth):
        p = _prep_layer(w_in[l], b_gate[l], b_forget[l], mla_q_norm[l], w_uq[l], mla_kv_norm[l],
                        w_ukv[l])
        qkv = _matmul(xb, p["w_qkv"], BF16, tn=QKV_W // 2)
        q_mla, k_mla, v_mla, lf = _mla_prep(xb, p["w_small"], p["qn"], p["kvn"], p["wqa"],
                                            p["wqb"], p["wk"], p["wv"], cos_t, sin_t, p["bf"], seq)
        cum, cumt = _cumsum(lf.reshape(b, seq, LANES))
        cumt4 = cumt[:, :FOX_HEADS, :].reshape(b, FOX_HEADS, 1, seq)
        qkv3 = qkv.reshape(b, seq, QKV_W)
        o_sb = _sb_attention(qkv3).reshape(n, SB_W)
        o_fox = _fox_attention(qkv3, cum, cumt4).reshape(n, FOX_W)
        o_mla = _mla_attention(q_mla.reshape(b, seq, -1), k_mla.reshape(b, seq, -1),
                               v_mla.reshape(b, seq, -1)).reshape(n, -1)
        x1, x1b, x1p = _merge(o_sb, o_fox, o_mla, xb, xf, p["w_gate"], p["b_gate"],
                              f(w_proj_sb[l]), f(w_proj_fox[l]), f(w_proj_mla[l]), f(w_out[l]),
                              ln1_g[l].reshape(1, d), ln1_b[l].reshape(1, d))

        idx_t, tw_t, rank_t, cnt = _route(x1b, f(w_router[l].T), router_bias[l].reshape(-1, 1))
        counts = cnt[:, 0].astype(jnp.int32)
        starts, tiles = _expert_tiles(counts, n_tiles, bm)
        dest_t = _dest(idx_t, rank_t, starts)
        dest_win = dest_t.reshape(TOP_K, n // SC_WINDOW, SC_WINDOW).transpose(1, 0, 2)
        xs = _sc_dispatch(dest_win, x1p, n_tiles * bm)
        ys = _experts(l, tiles, xs, w_exp_gate, w_exp_up, w_exp_down, bm)
        yt = _sc_gather(dest_win, ys, n)
        xf, xb = _combine_rows(yt, tw_t.T, x1, x1b, f(w_sh_gate[l]), f(w_sh_up[l]),
                               f(w_sh_down[l]), ln2_g[l].reshape(1, d), ln2_b[l].reshape(1, d))
    return xf.reshape(b, seq, d)
```

```python
import functools

import jax
import jax.numpy as jnp
from jax import lax
from jax.experimental import pallas as pl
from jax.experimental.pallas import tpu as pltpu
from jax.experimental.pallas import tpu_sc as plsc

F32 = jnp.float32
BF16 = jnp.bfloat16

D_MODEL = 1024
HEAD_DIM = 64
SB_HEADS = 4
FOX_HEADS = 4
MLA_HEADS = 8
MLA_Q_RANK = 256
MLA_KV_RANK = 128
MLA_NOPE_DIM = 64
MLA_ROPE_DIM = 32
MLA_V_DIM = 64
ROPE_BASE = 10000.0
N_BRANCHES = 3
N_EXPERTS = 256
TOP_K = 8
N_GROUPS = 8
TOPK_GROUPS = 4
GROUP_SIZE = N_EXPERTS // N_GROUPS
EXPERT_FF = 256
SHARED_FF = 256
ROUTED_SCALE = 2.5
CHUNK = 64
LN_EPS = 1e-5
RMS_EPS = 1e-6
DEPTH = 2
DN_ALPHA = (2 * DEPTH) ** 0.25

SB_W = SB_HEADS * HEAD_DIM
FOX_W = FOX_HEADS * HEAD_DIM
MLA_QK_DIM = MLA_NOPE_DIM + MLA_ROPE_DIM
OFF_SB = 0
OFF_FOX = OFF_SB + 3 * SB_W
OFF_FGATE = OFF_FOX + 3 * FOX_W
OFF_DQ = OFF_FGATE + FOX_HEADS
OFF_DKV = OFF_DQ + MLA_Q_RANK
OFF_KR = OFF_DKV + MLA_KV_RANK
OFF_GATE = OFF_KR + MLA_ROPE_DIM

LANES = 128
SUBLANES = 8
QKV_W = 3 * SB_W + 3 * FOX_W
SMALL_W = MLA_Q_RANK + MLA_KV_RANK + 3 * LANES
SB_CUTOFF = -104.0
LOG2E = 1.4426950408889634

VMEM_LIMIT = 48 * 1024 * 1024


def _cp(sem, vmem=VMEM_LIMIT):
    return pltpu.CompilerParams(dimension_semantics=sem, vmem_limit_bytes=vmem)


def _mm_kernel(x_ref, w_ref, o_ref, *, sub):
    x = x_ref[...]
    for c in range(0, o_ref.shape[1], sub):
        acc = jnp.dot(x, w_ref[:, c:c + sub], preferred_element_type=F32)
        o_ref[:, c:c + sub] = acc.astype(o_ref.dtype)


def _matmul(x, w, out_dtype, tn, tm=1024, sub=256):
    n, k = x.shape
    c = w.shape[1]
    tm = min(tm, n)
    return pl.pallas_call(
        functools.partial(_mm_kernel, sub=sub),
        grid=(n // tm, c // tn),
        in_specs=[pl.BlockSpec((tm, k), lambda i, j: (i, 0)),
                  pl.BlockSpec((k, tn), lambda i, j: (0, j))],
        out_specs=pl.BlockSpec((tm, tn), lambda i, j: (i, j)),
        out_shape=jax.ShapeDtypeStruct((n, c), out_dtype),
        compiler_params=_cp(("arbitrary", "arbitrary")),
        name="proj_matmul",
    )(x, w)


def _split_bf16(x, parts):
    out = []
    for _ in range(parts):
        h = x.astype(BF16)
        out.append(h)
        x = x - h.astype(F32)
    return out


def _mla_prep_kernel(x_ref, ws_ref, qn_ref, kvn_ref, wqa_ref, wqb_ref, wk_ref, wv_ref,
                     cos_ref, sin_ref, bf_ref, q_ref, k_ref, v_ref, lf_ref):
    sm = jnp.dot(x_ref[...], ws_ref[...], preferred_element_type=F32)
    dq = sm[:, :MLA_Q_RANK]
    dkv = sm[:, MLA_Q_RANK:MLA_Q_RANK + MLA_KV_RANK]
    o = MLA_Q_RANK + MLA_KV_RANK
    kr = sm[:, o:o + LANES]
    kr_rot = sm[:, o + LANES:o + 2 * LANES]
    fg = sm[:, o + 2 * LANES:o + 3 * LANES]

    cq = dq * lax.rsqrt(jnp.mean(dq * dq, axis=-1, keepdims=True) + RMS_EPS) * qn_ref[...]
    ckv = dkv * lax.rsqrt(jnp.mean(dkv * dkv, axis=-1, keepdims=True) + RMS_EPS) * kvn_ref[...]
    cq = cq.astype(BF16)
    ckv = ckv.astype(BF16)

    cosk = cos_ref[...]
    sink = sin_ref[...]
    lane = lax.broadcasted_iota(jnp.int32, (1, LANES), 1)
    nope = (lane < MLA_NOPE_DIM).astype(F32)
    scale = MLA_QK_DIM ** -0.5 * LOG2E
    cq_tab = jnp.concatenate([(cosk + nope) * scale] * MLA_HEADS, axis=1)
    sq_tab = jnp.concatenate([sink * scale] * MLA_HEADS, axis=1)

    qa = jnp.dot(cq, wqa_ref[...], preferred_element_type=F32)
    qb = jnp.dot(cq, wqb_ref[...], preferred_element_type=F32)
    q_ref[...] = (qa * cq_tab + qb * sq_tab).astype(q_ref.dtype)

    k_rope = kr * cosk + kr_rot * sink
    ka = jnp.dot(ckv, wk_ref[...], preferred_element_type=F32)
    k_ref[...] = (ka + jnp.concatenate([k_rope] * MLA_HEADS, axis=1)).astype(k_ref.dtype)
    v_ref[...] = jnp.dot(ckv, wv_ref[...], preferred_element_type=F32).astype(v_ref.dtype)
    lf_ref[...] = jax.nn.log_sigmoid(fg + bf_ref[...])


def _mla_prep(xb, w_small, qn, kvn, wqa, wqb, wk, wv, cos_t, sin_t, bf, seq, tm=512):
    n, d = xb.shape
    tm = min(tm, seq)
    sblocks = seq // tm
    hw = MLA_HEADS * LANES
    full = lambda a: pl.BlockSpec(a.shape, lambda i: (0,) * a.ndim)
    return pl.pallas_call(
        _mla_prep_kernel,
        grid=(n // tm,),
        in_specs=[pl.BlockSpec((tm, d), lambda i: (i, 0)), full(w_small),
                  full(qn), full(kvn), full(wqa), full(wqb), full(wk), full(wv),
                  pl.BlockSpec((tm, LANES), lambda i: (i % sblocks, 0)),
                  pl.BlockSpec((tm, LANES), lambda i: (i % sblocks, 0)),
                  full(bf)],
        out_specs=[pl.BlockSpec((tm, hw), lambda i: (i, 0)),
                   pl.BlockSpec((tm, hw), lambda i: (i, 0)),
                   pl.BlockSpec((tm, MLA_HEADS * MLA_V_DIM), lambda i: (i, 0)),
                   pl.BlockSpec((tm, LANES), lambda i: (i, 0))],
        out_shape=[jax.ShapeDtypeStruct((n, hw), BF16),
                   jax.ShapeDtypeStruct((n, hw), BF16),
                   jax.ShapeDtypeStruct((n, MLA_HEADS * MLA_V_DIM), BF16),
                   jax.ShapeDtypeStruct((n, LANES), F32)],
        compiler_params=_cp(("arbitrary",)),
        name="mla_prep",
    )(xb, w_small, qn, kvn, wqa, wqb, wk, wv, cos_t, sin_t, bf)


def _cumsum_kernel(lf_ref, cum_ref, cumt_ref, *, seq):
    r = lax.broadcasted_iota(jnp.int32, (LANES, LANES), 0)
    c = lax.broadcasted_iota(jnp.int32, (LANES, LANES), 1)
    lower = (c <= r).astype(BF16)

    def body(j, carry):
        start = pl.multiple_of(j * LANES, LANES)
        blk = lf_ref[0, pl.ds(start, LANES), :]
        acc = carry
        for part in _split_bf16(blk, 3):
            acc = acc + jnp.dot(lower, part, preferred_element_type=F32)
        scaled = acc * LOG2E
        cum_ref[0, pl.ds(start, LANES), :] = scaled
        cumt_ref[0, :, pl.ds(start, LANES)] = scaled.T[:SUBLANES, :]
        return jnp.broadcast_to(acc[LANES - 1:LANES, :], (LANES, LANES))

    lax.fori_loop(0, seq // LANES, body, jnp.zeros((LANES, LANES), F32))


def _cumsum(lf3):
    b, seq, _ = lf3.shape
    return pl.pallas_call(
        functools.partial(_cumsum_kernel, seq=seq),
        grid=(b,),
        in_specs=[pl.BlockSpec((1, seq, LANES), lambda i: (i, 0, 0))],
        out_specs=[pl.BlockSpec((1, seq, LANES), lambda i: (i, 0, 0)),
                   pl.BlockSpec((1, SUBLANES, seq), lambda i: (i, 0, 0))],
        out_shape=[jax.ShapeDtypeStruct((b, seq, LANES), F32),
                   jax.ShapeDtypeStruct((b, SUBLANES, seq), F32)],
        compiler_params=_cp(("arbitrary",)),
        name="forget_cumsum",
    )(lf3)


_NT = (((1,), (1,)), ((), ()))
NEG_INIT = -1e30
_TN = (((0,), (0,)), ((), ()))
ATTN_TK = 512
SB_TK = 256


def _head_halves(q):
    lane = lax.broadcasted_iota(jnp.int32, (1, LANES), 1)
    zero = jnp.zeros_like(q)
    return [jnp.where(lane < HEAD_DIM, q, zero), jnp.where(lane >= HEAD_DIM, q, zero)]


def _mask_first_block(x, keep, fill, tk):
    head = jnp.where(keep, x[:, :tk], fill)
    return head if x.shape[1] == tk else jnp.concatenate([head, x[:, tk:]], axis=1)


def _fox_kernel(q_ref, k_ref, v_ref, cc_ref, cr_ref, o_ref, *, seq, tk):
    hp = pl.program_id(1)
    lane = lax.broadcasted_iota(jnp.int32, (1, LANES), 1)
    qs = _head_halves(q_ref[0])
    key = lax.broadcasted_iota(jnp.int32, (tk, tk), 0)
    qry = lax.broadcasted_iota(jnp.int32, (tk, tk), 1)
    causal = key <= qry
    carry = [(jnp.full((1, seq), NEG_INIT, F32), jnp.zeros((1, seq), F32),
              jnp.zeros((HEAD_DIM, seq), F32)) for _ in range(2)]
    for j in range(seq // tk):
        q0 = j * tk
        k = k_ref[0, q0:q0 + tk, :]
        cc = cc_ref[0, q0:q0 + tk, :]
        for hh in range(2):
            m, l, acc = carry[hh]
            v = v_ref[0, q0:q0 + tk, hh * HEAD_DIM:(hh + 1) * HEAD_DIM]
            cs = jnp.sum(jnp.where(lane == 2 * hp + hh, cc, 0.0), axis=1, keepdims=True)
            ct = cr_ref[0, hh, :, q0:]
            st = lax.dot_general(k, qs[hh][q0:, :], _NT, preferred_element_type=F32)
            st = _mask_first_block((st + ct) - cs, causal, -jnp.inf, tk)
            mn, ln, an = _softmax_step_t(st, m[:, q0:], l[:, q0:], acc[:, q0:], v)
            if q0:
                mn = jnp.concatenate([m[:, :q0], mn], axis=1)
                ln = jnp.concatenate([l[:, :q0], ln], axis=1)
                an = jnp.concatenate([acc[:, :q0], an], axis=1)
            carry[hh] = (mn, ln, an)
    _pair_out_t([carry[0][2], carry[1][2]], [carry[0][1], carry[1][1]], o_ref)


def _fox_attention(qkv3, cum, cumt4):
    b, seq, _ = qkv3.shape
    tk = min(ATTN_TK, seq)
    hp = FOX_HEADS // 2
    qoff = OFF_FOX // LANES
    return pl.pallas_call(
        functools.partial(_fox_kernel, seq=seq, tk=tk),
        grid=(b, hp),
        in_specs=[pl.BlockSpec((1, seq, LANES), lambda bi, h: (bi, 0, qoff + h)),
                  pl.BlockSpec((1, seq, LANES), lambda bi, h: (bi, 0, qoff + hp + h)),
                  pl.BlockSpec((1, seq, LANES), lambda bi, h: (bi, 0, qoff + 2 * hp + h)),
                  pl.BlockSpec((1, seq, LANES), lambda bi, h: (bi, 0, 0)),
                  pl.BlockSpec((1, 2, 1, seq), lambda bi, h: (bi, h, 0, 0))],
        out_specs=pl.BlockSpec((1, seq, LANES), lambda bi, h: (bi, 0, h)),
        out_shape=jax.ShapeDtypeStruct((b, seq, FOX_W), BF16),
        compiler_params=_cp(("arbitrary", "arbitrary")),
        name="fox_attention",
    )(qkv3, qkv3, qkv3, cum, cumt4)


def _softmax_step_t(st, m, l, acc, v):
    m_new = jnp.maximum(m, jnp.max(st, axis=0, keepdims=True))
    alpha = jnp.exp2(m - m_new)
    p = jnp.exp2(st - m_new)
    l = alpha * l + jnp.sum(p, axis=0, keepdims=True)
    acc = alpha * acc + lax.dot_general(v, p.astype(BF16), _TN, preferred_element_type=F32)
    return m_new, l, acc


def _pair_out_t(accs, ls, o_ref, pair=0):
    ot = jnp.concatenate([accs[0] / ls[0], accs[1] / ls[1]], axis=0)
    o_ref[0, :, pair * LANES:(pair + 1) * LANES] = ot.T.astype(o_ref.dtype)


MLA_STEP_HEADS = 2


def _mla_kernel(q_ref, k_ref, v_ref, o_ref, *, seq, tk):
    nh = MLA_STEP_HEADS
    key = lax.broadcasted_iota(jnp.int32, (tk, tk), 0)
    qry = lax.broadcasted_iota(jnp.int32, (tk, tk), 1)
    shift = CHUNK.bit_length() - 1
    chunk_causal = (key >> shift) <= (qry >> shift)
    carry = [(jnp.full((1, seq), NEG_INIT, F32), jnp.zeros((1, seq), F32),
              jnp.zeros((MLA_V_DIM, seq), F32)) for _ in range(nh)]
    for j in range(seq // tk):
        q0 = j * tk
        for hh in range(nh):
            m, l, acc = carry[hh]
            k = k_ref[0, q0:q0 + tk, hh * LANES:(hh + 1) * LANES]
            v = v_ref[0, q0:q0 + tk, hh * MLA_V_DIM:(hh + 1) * MLA_V_DIM]
            st = lax.dot_general(k, q_ref[0, q0:, hh * LANES:(hh + 1) * LANES], _NT,
                                 preferred_element_type=F32)
            st = _mask_first_block(st, chunk_causal, -jnp.inf, tk)
            mn, ln, an = _softmax_step_t(st, m[:, q0:], l[:, q0:], acc[:, q0:], v)
            if q0:
                mn = jnp.concatenate([m[:, :q0], mn], axis=1)
                ln = jnp.concatenate([l[:, :q0], ln], axis=1)
                an = jnp.concatenate([acc[:, :q0], an], axis=1)
            carry[hh] = (mn, ln, an)
    for pair in range(nh // 2):
        a, b = carry[2 * pair], carry[2 * pair + 1]
        _pair_out_t([a[2], b[2]], [a[1], b[1]], o_ref, pair)


def _mla_attention(q3, k3, v3):
    b, seq, _ = q3.shape
    tk = min(ATTN_TK, seq)
    nh = MLA_STEP_HEADS
    return pl.pallas_call(
        functools.partial(_mla_kernel, seq=seq, tk=tk),
        grid=(b, MLA_HEADS // nh),
        in_specs=[pl.BlockSpec((1, seq, nh * LANES), lambda bi, h: (bi, 0, h)),
                  pl.BlockSpec((1, seq, nh * LANES), lambda bi, h: (bi, 0, h)),
                  pl.BlockSpec((1, seq, nh // 2 * LANES), lambda bi, h: (bi, 0, h))],
        out_specs=pl.BlockSpec((1, seq, nh // 2 * LANES), lambda bi, h: (bi, 0, h)),
        out_shape=jax.ShapeDtypeStruct((b, seq, MLA_HEADS * MLA_V_DIM), BF16),
        compiler_params=_cp(("arbitrary", "arbitrary")),
        name="mla_attention",
    )(q3, k3, v3)


def _softplus(z):
    return jnp.maximum(z, 0.0) + jnp.log(1.0 + jnp.exp(-jnp.abs(z)))


def _sb_kernel(q_ref, k_ref, v_ref, o_ref, run_ref, acc_ref, *, seq, tk):
    n = seq // tk
    qs = _head_halves(q_ref[0])
    key = lax.broadcasted_iota(jnp.int32, (tk, tk), 0)
    qry = lax.broadcasted_iota(jnp.int32, (tk, tk), 1)
    strict = key < qry
    later = (qry > key).astype(BF16)

    def unit(j, qlo, qhi, diag):
        k = k_ref[0, j * tk:(j + 1) * tk, :]
        for hh in range(2):
            v = v_ref[0, j * tk:(j + 1) * tk, hh * HEAD_DIM:(hh + 1) * HEAD_DIM]
            z = lax.dot_general(k, qs[hh][qlo:qhi, :], _NT, preferred_element_type=F32)
            sp = _softplus(z)
            log_keep = -sp
            if diag:
                log_keep = _mask_first_block(log_keep, strict, 0.0, tk)
            between = run_ref[hh, :, qlo:qhi]
            for part in _split_bf16(log_keep, 2):
                between = between + jnp.dot(later, part, preferred_element_type=F32)
            w = jnp.exp((z - sp) + between)
            if diag:
                w = _mask_first_block(w, strict, 0.0, tk)
            acc_ref[hh, :, qlo:qhi] += lax.dot_general(v, w.astype(BF16), _TN,
                                                       preferred_element_type=F32)
            run_ref[hh, :, qlo:qhi] += jnp.sum(log_keep, axis=0, keepdims=True)

    run_ref[...] = jnp.zeros_like(run_ref)
    acc_ref[...] = jnp.zeros_like(acc_ref)
    for j in range(n - 1, -1, -1):
        unit(j, j * tk, min((j + 2) * tk, seq), True)

    for j in range(n - 3, -1, -1):
        qlo = (j + 2) * tk

        @pl.when(jnp.max(run_ref[:, :, qlo:]) >= SB_CUTOFF)
        def _():
            unit(j, qlo, seq, False)

    o_ref[0] = jnp.concatenate([acc_ref[0], acc_ref[1]], axis=0).T.astype(o_ref.dtype)


def _sb_attention(qkv3):
    b, seq, _ = qkv3.shape
    tk = min(SB_TK, seq)
    hp = SB_HEADS // 2
    qoff = OFF_SB // LANES
    return pl.pallas_call(
        functools.partial(_sb_kernel, seq=seq, tk=tk),
        grid=(b, hp),
        in_specs=[pl.BlockSpec((1, seq, LANES), lambda bi, h: (bi, 0, qoff + h)),
                  pl.BlockSpec((1, seq, LANES), lambda bi, h: (bi, 0, qoff + hp + h)),
                  pl.BlockSpec((1, seq, LANES), lambda bi, h: (bi, 0, qoff + 2 * hp + h))],
        out_specs=pl.BlockSpec((1, seq, LANES), lambda bi, h: (bi, 0, h)),
        out_shape=jax.ShapeDtypeStruct((b, seq, SB_W), BF16),
        scratch_shapes=[pltpu.VMEM((2, 1, seq), F32), pltpu.VMEM((2, HEAD_DIM, seq), F32)],
        compiler_params=_cp(("arbitrary", "arbitrary")),
        name="sb_attention",
    )(qkv3, qkv3, qkv3)


def _layer_norm(y, g, b):
    yc = y - jnp.mean(y, axis=-1, keepdims=True)
    var = jnp.mean(yc * yc, axis=-1, keepdims=True)
    return yc * lax.rsqrt(var + LN_EPS) * g + b


HALF = D_MODEL // 2
U32 = jnp.uint32


def _pack_halves(y):
    bits = lax.bitcast_convert_type(y.astype(BF16).astype(F32), U32)
    return (bits[:, :HALF] >> 16) | (bits[:, HALF:] & U32(0xFFFF0000))


def _unpack_halves(w):
    lo = lax.bitcast_convert_type(w << 16, F32)
    hi = lax.bitcast_convert_type(w & U32(0xFFFF0000), F32)
    return lo, hi


def _merge_kernel(osb_ref, ofox_ref, omla_ref, xb_ref, x_ref, wg_ref, bg_ref, wsb_ref, wfox_ref,
                  wmla_ref, wout_ref, lg_ref, lb_ref, wr_ref, rb_ref, x1_ref, x1b_ref, x1p_ref,
                  idx_ref, tw_ref, rank_ref, cnt_ref, run_ref):
    xb = xb_ref[...]
    mixed = None
    branches = ((osb_ref, wsb_ref), (ofox_ref, wfox_ref), (omla_ref, wmla_ref))
    for i, (o_ref, w_ref) in enumerate(branches):
        cols = slice(i * D_MODEL, (i + 1) * D_MODEL)
        gate = jax.nn.sigmoid(jnp.dot(xb, wg_ref[:, cols], preferred_element_type=F32)
                              + bg_ref[:, cols])
        term = gate * jnp.dot(o_ref[...], w_ref[...], preferred_element_type=F32)
        mixed = term if mixed is None else mixed + term
    mix = jnp.dot(mixed.astype(BF16), wout_ref[...], preferred_element_type=F32)
    y = _layer_norm(DN_ALPHA * x_ref[...] + mix, lg_ref[...], lb_ref[...])
    x1_ref[...] = y
    x1b_ref[...] = y.astype(BF16)
    x1p_ref[...] = _pack_halves(y)
    _route_kernel(x1b_ref, wr_ref, rb_ref, idx_ref, tw_ref, rank_ref, cnt_ref, run_ref,
                  tm=x1b_ref.shape[0])


def _merge(o_sb, o_fox, o_mla, xb, x, wg, bg, wsb, wfox, wmla, wout, lg, lb, wr_t, rb, tm=512):
    n = x.shape[0]
    tm = min(tm, n)
    full = lambda a: pl.BlockSpec(a.shape, lambda i: (0,) * a.ndim)
    rows = lambda w: pl.BlockSpec((tm, w), lambda i: (i, 0))
    picks = lambda: pl.BlockSpec((TOP_K, tm), lambda i: (0, i))
    return pl.pallas_call(
        _merge_kernel,
        grid=(n // tm,),
        in_specs=[rows(SB_W), rows(FOX_W), rows(MLA_HEADS * MLA_V_DIM), rows(D_MODEL),
                  rows(D_MODEL), full(wg), full(bg), full(wsb), full(wfox), full(wmla),
                  full(wout), full(lg), full(lb), full(wr_t), full(rb)],
        out_specs=[rows(D_MODEL), rows(D_MODEL), rows(HALF), picks(), picks(), picks(),
                   pl.BlockSpec((N_EXPERTS, LANES), lambda i: (0, 0))],
        out_shape=[jax.ShapeDtypeStruct((n, D_MODEL), F32),
                   jax.ShapeDtypeStruct((n, D_MODEL), BF16),
                   jax.ShapeDtypeStruct((n, HALF), U32),
                   jax.ShapeDtypeStruct((TOP_K, n), jnp.int32),
                   jax.ShapeDtypeStruct((TOP_K, n), F32),
                   jax.ShapeDtypeStruct((TOP_K, n), jnp.int32),
                   jax.ShapeDtypeStruct((N_EXPERTS, LANES), F32)],
        scratch_shapes=[pltpu.VMEM((N_EXPERTS, 1), F32)],
        compiler_params=_cp(("arbitrary",)),
        name="merge_outproj_ln_route",
    )(o_sb, o_fox, o_mla, xb, x, wg, bg, wsb, wfox, wmla, wout, lg, lb, wr_t, rb)


def _route_kernel(x_ref, wr_ref, rb_ref, idx_ref, w_ref, rank_ref, cnt_ref, run_ref, *, tm):
    step = pl.program_id(0)

    @pl.when(step == 0)
    def _():
        run_ref[...] = jnp.zeros_like(run_ref)

    logits = lax.dot_general(wr_ref[...], x_ref[...], _NT, preferred_element_type=F32)
    scores = jax.nn.sigmoid(logits)
    biased = scores + rb_ref[...]
    e_iota = lax.broadcasted_iota(jnp.int32, (N_EXPERTS, tm), 0)
    big = jnp.int32(1 << 20)

    g_iota = lax.broadcasted_iota(jnp.int32, (GROUP_SIZE, tm), 0)
    gs_rows = []
    for g in range(N_GROUPS):
        blk = biased[g * GROUP_SIZE:(g + 1) * GROUP_SIZE, :]
        m1 = jnp.max(blk, axis=0, keepdims=True)
        first = jnp.min(jnp.where(blk == m1, g_iota, big), axis=0, keepdims=True)
        m2 = jnp.max(jnp.where(g_iota == first, -jnp.inf, blk), axis=0, keepdims=True)
        gs_rows.append(m1 + m2)
    gs = jnp.concatenate(gs_rows, axis=0)
    n_iota = lax.broadcasted_iota(jnp.int32, (N_GROUPS, tm), 0)
    keep = jnp.zeros((N_GROUPS, tm), jnp.bool_)
    for _ in range(TOPK_GROUPS):
        m = jnp.max(gs, axis=0, keepdims=True)
        first = jnp.min(jnp.where(gs == m, n_iota, big), axis=0, keepdims=True)
        hit = n_iota == first
        keep = jnp.logical_or(keep, hit)
        gs = jnp.where(hit, -jnp.inf, gs)
    keep_f = keep.astype(F32)
    expert_keep = jnp.concatenate(
        [jnp.broadcast_to(keep_f[g:g + 1, :], (GROUP_SIZE, tm)) for g in range(N_GROUPS)], axis=0)
    masked = jnp.where(expert_keep > 0.5, biased, -jnp.inf)

    idx_rows, w_rows, hits = [], [], []
    sel = jnp.zeros((N_EXPERTS, tm), F32)
    for _ in range(TOP_K):
        m = jnp.max(masked, axis=0, keepdims=True)
        first = jnp.min(jnp.where(masked == m, e_iota, big), axis=0, keepdims=True)
        hit = e_iota == first
        idx_rows.append(first)
        w_rows.append(jnp.sum(jnp.where(hit, scores, 0.0), axis=0, keepdims=True))
        hits.append(hit)
        sel = sel + hit.astype(F32)
        masked = jnp.where(hit, -jnp.inf, masked)
    w = jnp.concatenate(w_rows, axis=0)
    w = w / jnp.sum(w, axis=0, keepdims=True) * ROUTED_SCALE
    idx_ref[...] = jnp.concatenate(idx_rows, axis=0)
    w_ref[...] = w

    r = lax.broadcasted_iota(jnp.int32, (tm, tm), 0)
    c = lax.broadcasted_iota(jnp.int32, (tm, tm), 1)
    earlier = (r < c).astype(BF16)
    prefix = jnp.dot(sel.astype(BF16), earlier, preferred_element_type=F32) + run_ref[...]
    rank_rows = [jnp.sum(jnp.where(h, prefix, 0.0), axis=0, keepdims=True) for h in hits]
    rank_ref[...] = jnp.concatenate(rank_rows, axis=0).astype(jnp.int32)
    total = run_ref[...] + jnp.sum(sel, axis=1, keepdims=True)
    run_ref[...] = total
    cnt_ref[...] = jnp.broadcast_to(total, (N_EXPERTS, LANES))


def _dest_kernel(idx_ref, rank_ref, st_ref, dest_ref, *, tm):
    e_iota = lax.broadcasted_iota(jnp.int32, (N_EXPERTS, tm), 0)
    starts = st_ref[...]
    rows = []
    for k in range(TOP_K):
        hit = e_iota == idx_ref[k:k + 1, :]
        rows.append(jnp.sum(jnp.where(hit, starts, 0.0), axis=0, keepdims=True))
    dest_ref[...] = jnp.concatenate(rows, axis=0).astype(jnp.int32) + rank_ref[...]


def _dest(idx_t, rank_t, starts, tm=512):
    n = idx_t.shape[1]
    tm = min(tm, n)
    blk = lambda: pl.BlockSpec((TOP_K, tm), lambda i: (0, i))
    return pl.pallas_call(
        functools.partial(_dest_kernel, tm=tm),
        grid=(n // tm,),
        in_specs=[blk(), blk(), pl.BlockSpec((N_EXPERTS, 1), lambda i: (0, 0))],
        out_specs=blk(),
        out_shape=jax.ShapeDtypeStruct((TOP_K, n), jnp.int32),
        compiler_params=_cp(("arbitrary",)),
        name="moe_dest",
    )(idx_t, rank_t, starts.astype(F32).reshape(N_EXPERTS, 1))


SC_CORES = 2
SC_SUBCORES = 16
SC_WINDOW = 128

def _sc_mesh():
    return plsc.VectorSubcoreMesh(core_axis_name="c", subcore_axis_name="s",
                                  num_cores=SC_CORES, num_subcores=SC_SUBCORES)


def _sc_windows(n):
    workers = SC_CORES * SC_SUBCORES
    assert n % (SC_WINDOW * workers) == 0, n
    return n // SC_WINDOW // workers


def _sc_dispatch(dest_win, x1p, n_rows):
    n = x1p.shape[0]
    per_worker = _sc_windows(n)

    @functools.partial(
        pl.kernel, mesh=_sc_mesh(),
        out_type=jax.ShapeDtypeStruct((n_rows, HALF), U32),
        scratch_types=[pltpu.VMEM((TOP_K, SC_WINDOW), jnp.int32),
                       pltpu.VMEM((SC_WINDOW, HALF), U32)],
        name="moe_dispatch_sc")
    def run(dest_hbm, x_hbm, xs_hbm, idx_v, rows_v):
        worker = lax.axis_index("s") * SC_CORES + lax.axis_index("c")

        @pl.loop(0, per_worker)
        def _(c):
            win = worker * per_worker + c
            pltpu.sync_copy(dest_hbm.at[win], idx_v)
            pltpu.sync_copy(x_hbm.at[pl.ds(pl.multiple_of(win * SC_WINDOW, SC_WINDOW),
                                           SC_WINDOW)], rows_v)
            for k in range(TOP_K):
                pltpu.sync_copy(rows_v, xs_hbm.at[idx_v.at[k]])

    return run(dest_win, x1p)


def _sc_gather(dest_win, ys, n):
    per_worker = _sc_windows(n)

    @functools.partial(
        pl.kernel, mesh=_sc_mesh(),
        out_type=jax.ShapeDtypeStruct((TOP_K, n, HALF), U32),
        scratch_types=[pltpu.VMEM((TOP_K, SC_WINDOW), jnp.int32),
                       pltpu.VMEM((SC_WINDOW, HALF), U32)],
        name="moe_gather_sc")
    def run(dest_hbm, ys_hbm, yt_hbm, idx_v, rows_v):
        worker = lax.axis_index("s") * SC_CORES + lax.axis_index("c")

        @pl.loop(0, per_worker)
        def _(c):
            win = worker * per_worker + c
            start = pl.multiple_of(win * SC_WINDOW, SC_WINDOW)
            pltpu.sync_copy(dest_hbm.at[win], idx_v)
            for k in range(TOP_K):
                pltpu.sync_copy(ys_hbm.at[idx_v.at[k]], rows_v)
                pltpu.sync_copy(rows_v, yt_hbm.at[k, pl.ds(start, SC_WINDOW)])

    return run(dest_win, ys)


def _expert_kernel(used_ref, exp_ref, valid_ref, first_ref, slot_ref, next_ref, xs_ref, wg_hbm,
                   wu_hbm, wd_hbm, ys_ref, wg_buf, wu_buf, wd_buf, sem, *, bm, layer):
    w = pl.program_id(0)

    def weight_copies(expert, s):
        return [pltpu.make_async_copy(hbm.at[layer, expert], buf.at[s], sem.at[s, i])
                for i, (hbm, buf) in enumerate(((wg_hbm, wg_buf), (wu_hbm, wu_buf),
                                                (wd_hbm, wd_buf)))]

    @pl.when(w < used_ref[0])
    def _():
        s = slot_ref[w]

        @pl.when(w == 0)
        def _():
            for c in weight_copies(exp_ref[0], 0):
                c.start()

        @pl.when(first_ref[w] == 1)
        def _():
            for c in weight_copies(exp_ref[w], s):
                c.wait()

            @pl.when(next_ref[w] >= 0)
            def _():
                for c in weight_copies(next_ref[w], 1 - s):
                    c.start()

        rows = lax.broadcasted_iota(jnp.int32, (bm, 1), 0)
        x_lo, x_hi = _unpack_halves(jnp.where(rows < valid_ref[w], xs_ref[...], U32(0)))
        x = jnp.concatenate([x_lo, x_hi], axis=1).astype(BF16)
        g = jnp.dot(x, wg_buf[s].astype(BF16), preferred_element_type=F32)
        u = jnp.dot(x, wu_buf[s].astype(BF16), preferred_element_type=F32)
        h = (g * jax.nn.sigmoid(g) * u).astype(BF16)
        ys_ref[...] = _pack_halves(jnp.dot(h, wd_buf[s].astype(BF16),
                                           preferred_element_type=F32))


def _experts(layer, tiles, xs, wg, wu, wd, bm):
    r = xs.shape[0]
    last = lambda w, used, *_: (jnp.minimum(w, used[0] - 1), 0)
    grid_spec = pltpu.PrefetchScalarGridSpec(
        num_scalar_prefetch=len(tiles),
        grid=(r // bm,),
        in_specs=[pl.BlockSpec((bm, HALF), last)] + [pl.BlockSpec(memory_space=pl.ANY)] * 3,
        out_specs=pl.BlockSpec((bm, HALF), last),
        scratch_shapes=[pltpu.VMEM((2, D_MODEL, EXPERT_FF), F32),
                        pltpu.VMEM((2, D_MODEL, EXPERT_FF), F32),
                        pltpu.VMEM((2, EXPERT_FF, D_MODEL), F32),
                        pltpu.SemaphoreType.DMA((2, 3))],
    )
    return pl.pallas_call(
        functools.partial(_expert_kernel, bm=bm, layer=layer),
        grid_spec=grid_spec,
        out_shape=jax.ShapeDtypeStruct((r, HALF), U32),
        compiler_params=_cp(("arbitrary",)),
        name="moe_experts",
    )(*tiles, xs, wg, wu, wd)


def _expert_tiles(counts, n_tiles, bm):
    padded = (counts + bm - 1) // bm * bm
    pad_end = jnp.cumsum(padded)
    pad_start = pad_end - padded
    tile = jnp.arange(n_tiles, dtype=jnp.int32)
    row0 = tile * bm
    exp = jnp.minimum(jnp.sum((pad_end[None, :] <= row0[:, None]).astype(jnp.int32), axis=1),
                      N_EXPERTS - 1)
    valid = jnp.clip(pad_start[exp] + counts[exp] - row0, 0, bm)
    n_used = (pad_end[-1:] // bm).astype(jnp.int32)
    first = jnp.concatenate([jnp.ones((1,), jnp.int32),
                             (exp[1:] != exp[:-1]).astype(jnp.int32)])
    slot = (jnp.cumsum(first) - 1) % 2
    next_tile = pad_end[exp] // bm
    nxt = jnp.where(next_tile < n_used[0], exp[jnp.minimum(next_tile, n_tiles - 1)], -1)
    i32 = lambda a: a.astype(jnp.int32)
    return i32(pad_start), (n_used, i32(exp), i32(valid), first, i32(slot), i32(nxt))


def _combine_rows_kernel(yt_ref, tw_ref, x1_ref, x1b_ref, wsg_ref, wsu_ref, wsd_ref, lg_ref, lb_ref,
                         o_ref, ob_ref):
    tw = tw_ref[...]
    r_lo = jnp.zeros((tw.shape[0], HALF), F32)
    r_hi = jnp.zeros((tw.shape[0], HALF), F32)
    for k in range(TOP_K):
        y_lo, y_hi = _unpack_halves(yt_ref[k])
        r_lo = r_lo + tw[:, k:k + 1] * y_lo
        r_hi = r_hi + tw[:, k:k + 1] * y_hi
    routed = jnp.concatenate([r_lo, r_hi], axis=1)
    xb = x1b_ref[...]
    g = jnp.dot(xb, wsg_ref[...], preferred_element_type=F32)
    u = jnp.dot(xb, wsu_ref[...], preferred_element_type=F32)
    h = (g * jax.nn.sigmoid(g) * u).astype(BF16)
    shared = jnp.dot(h, wsd_ref[...], preferred_element_type=F32)
    y = _layer_norm(DN_ALPHA * x1_ref[...] + (shared + routed), lg_ref[...], lb_ref[...])
    o_ref[...] = y
    ob_ref[...] = y.astype(BF16)


def _combine_rows(yt, tw, x1, x1b, wsg, wsu, wsd, lg, lb, tc=512):
    n = x1.shape[0]
    tc = min(tc, n)
    full = lambda a: pl.BlockSpec(a.shape, lambda i: (0,) * a.ndim)
    rows = lambda w: pl.BlockSpec((tc, w), lambda i: (i, 0))
    return pl.pallas_call(
        _combine_rows_kernel,
        grid=(n // tc,),
        in_specs=[pl.BlockSpec((TOP_K, tc, HALF), lambda i: (0, i, 0)),
                  rows(TOP_K), rows(D_MODEL), rows(D_MODEL),
                  full(wsg), full(wsu), full(wsd), full(lg), full(lb)],
        out_specs=[rows(D_MODEL), rows(D_MODEL)],
        out_shape=[jax.ShapeDtypeStruct((n, D_MODEL), F32),
                   jax.ShapeDtypeStruct((n, D_MODEL), BF16)],
        compiler_params=_cp(("arbitrary",)),
        name="moe_combine_ln",
    )(yt, tw, x1, x1b, wsg, wsu, wsd, lg, lb)


def _head_cols(w, heads, width, lo, hi):
    return w.reshape(w.shape[0], heads, width)[:, :, lo:hi]


def _prep_layer(w_in, b_gate, b_forget, mla_q_norm, w_uq, mla_kv_norm, w_ukv):
    f = lambda a: a.astype(BF16)
    half = MLA_ROPE_DIM // 2
    d = w_in.shape[0]
    qscale = HEAD_DIM ** -0.5
    w_qkv = jnp.concatenate([
        w_in[:, OFF_SB:OFF_SB + SB_W] * qscale, w_in[:, OFF_SB + SB_W:OFF_FOX],
        w_in[:, OFF_FOX:OFF_FOX + FOX_W] * (qscale * LOG2E), w_in[:, OFF_FOX + FOX_W:OFF_FGATE]],
        axis=1)
    w_kr = w_in[:, OFF_KR:OFF_GATE]
    w_kr_rot = jnp.concatenate([-w_kr[:, half:], w_kr[:, :half]], axis=1)
    z = lambda c: jnp.zeros((d, c), F32)
    pad_rope = LANES - MLA_QK_DIM
    w_small = jnp.concatenate([
        w_in[:, OFF_DQ:OFF_DKV], w_in[:, OFF_DKV:OFF_KR],
        z(MLA_NOPE_DIM), w_kr, z(pad_rope),
        z(MLA_NOPE_DIM), w_kr_rot, z(pad_rope),
        w_in[:, OFF_FGATE:OFF_DQ], z(LANES - FOX_HEADS)], axis=1)
    w_gate = w_in[:, OFF_GATE:]

    r = w_uq.shape[0]
    q_nope = _head_cols(w_uq, MLA_HEADS, MLA_QK_DIM, 0, MLA_NOPE_DIM)
    q_rope = _head_cols(w_uq, MLA_HEADS, MLA_QK_DIM, MLA_NOPE_DIM, MLA_QK_DIM)
    q_rope_rot = jnp.concatenate([-q_rope[:, :, half:], q_rope[:, :, :half]], axis=2)
    zq = lambda c: jnp.zeros((r, MLA_HEADS, c), F32)
    wqa = jnp.concatenate([q_nope, q_rope, zq(pad_rope)], axis=2).reshape(r, MLA_HEADS * LANES)
    wqb = jnp.concatenate([zq(MLA_NOPE_DIM), q_rope_rot, zq(pad_rope)], axis=2).reshape(
        r, MLA_HEADS * LANES)
    rk = w_ukv.shape[0]
    kvw = MLA_NOPE_DIM + MLA_V_DIM
    k_nope = _head_cols(w_ukv, MLA_HEADS, kvw, 0, MLA_NOPE_DIM)
    wk = jnp.concatenate([k_nope, jnp.zeros((rk, MLA_HEADS, LANES - MLA_NOPE_DIM), F32)],
                         axis=2).reshape(rk, MLA_HEADS * LANES)
    wv = _head_cols(w_ukv, MLA_HEADS, kvw, MLA_NOPE_DIM, kvw).reshape(rk, MLA_HEADS * MLA_V_DIM)
    bf = jnp.concatenate([b_forget, jnp.zeros((LANES - FOX_HEADS,), F32)]).reshape(1, LANES)
    return dict(w_qkv=f(w_qkv), w_small=f(w_small), w_gate=f(w_gate),
                b_gate=b_gate.reshape(1, -1), wqa=f(wqa), wqb=f(wqb), wk=f(wk), wv=f(wv),
                qn=mla_q_norm.reshape(1, -1), kvn=mla_kv_norm.reshape(1, -1), bf=bf)


def _rope_tables(seq):
    half = MLA_ROPE_DIM // 2
    inv_freq = jnp.power(ROPE_BASE, -jnp.arange(half, dtype=F32) / half)
    ang = jnp.arange(seq).astype(F32)[:, None] * inv_freq[None, :]
    cos = jnp.concatenate([jnp.cos(ang), jnp.cos(ang)], axis=1)
    sin = jnp.concatenate([jnp.sin(ang), jnp.sin(ang)], axis=1)
    pad = lambda t: jnp.concatenate([jnp.zeros((seq, MLA_NOPE_DIM), F32), t,
                                     jnp.zeros((seq, LANES - MLA_QK_DIM), F32)], axis=1)
    return pad(cos), pad(sin)


EXPERT_BM = 512


def kernel(x, ln1_g, ln1_b, ln2_g, ln2_b, w_in, b_gate, b_forget, mla_q_norm, w_uq, mla_kv_norm,
           w_ukv, w_proj_sb, w_proj_fox, w_proj_mla, w_out, w_router, router_bias,
           w_exp_gate, w_exp_up, w_exp_down, w_sh_gate, w_sh_up, w_sh_down):
    b, seq, d = x.shape
    n = b * seq
    depth = w_in.shape[0]
    cos_t, sin_t = _rope_tables(seq)
    xf = x.reshape(n, d)
    xb = xf.astype(BF16)
    f = lambda a: a.astype(BF16)
    bm = min(EXPERT_BM, n * TOP_K)
    n_tiles = n * TOP_K // bm + N_EXPERTS
    for l in range(depth):
        p = _prep_layer(w_in[l], b_gate[l], b_forget[l], mla_q_norm[l], w_uq[l], mla_kv_norm[l],
                        w_ukv[l])
        qkv = _matmul(xb, p["w_qkv"], BF16, tn=QKV_W // 2)
        q_mla, k_mla, v_mla, lf = _mla_prep(xb, p["w_small"], p["qn"], p["kvn"], p["wqa"],
                                            p["wqb"], p["wk"], p["wv"], cos_t, sin_t, p["bf"], seq)
        cum, cumt = _cumsum(lf.reshape(b, seq, LANES))
        cumt4 = cumt[:, :FOX_HEADS, :].reshape(b, FOX_HEADS, 1, seq)
        qkv3 = qkv.reshape(b, seq, QKV_W)
        o_sb = _sb_attention(qkv3).reshape(n, SB_W)
        o_fox = _fox_attention(qkv3, cum, cumt4).reshape(n, FOX_W)
        o_mla = _mla_attention(q_mla.reshape(b, seq, -1), k_mla.reshape(b, seq, -1),
                               v_mla.reshape(b, seq, -1)).reshape(n, -1)
        x1, x1b, x1p, idx_t, tw_t, rank_t, cnt = _merge(
            o_sb, o_fox, o_mla, xb, xf, p["w_gate"], p["b_gate"], f(w_proj_sb[l]),
            f(w_proj_fox[l]), f(w_proj_mla[l]), f(w_out[l]), ln1_g[l].reshape(1, d),
            ln1_b[l].reshape(1, d), f(w_router[l].T), router_bias[l].reshape(-1, 1))
        counts = cnt[:, 0].astype(jnp.int32)
        starts, tiles = _expert_tiles(counts, n_tiles, bm)
        dest_t = _dest(idx_t, rank_t, starts)
        dest_win = dest_t.reshape(TOP_K, n // SC_WINDOW, SC_WINDOW).transpose(1, 0, 2)
        xs = _sc_dispatch(dest_win, x1p, n_tiles * bm)
        ys = _experts(l, tiles, xs, w_exp_gate, w_exp_up, w_exp_down, bm)
        yt = _sc_gather(dest_win, ys, n)
        xf, xb = _combine_rows(yt, tw_t.T, x1, x1b, f(w_sh_gate[l]), f(w_sh_up[l]),
                               f(w_sh_down[l]), ln2_g[l].reshape(1, d), ln2_b[l].reshape(1, d))
    return xf.reshape(b, seq, d)
```
